```python
import jax
import jax.numpy as jnp
from jax import lax
import numpy as np

D_MODEL = 1024
BATCH = 8
SEQ = 2048
DEPTH = 2

HEAD_DIM = 64
SWA_Q_HEADS = 8
SWA_KV_HEADS = 2
SWA_WINDOW = 128
FOX_HEADS = 4
FOX_Q_BLOCK = 128
MOBA_HEADS = 4
MOBA_BLOCK = 256
MOBA_TOPK = 3
MOBA_Q_CHUNK = 64
ROPE_THETA = 10000.0
RMS_EPS = 1e-5
D_FF = 3584
N_EXPERTS = 8
TOP_K = 2
MOE_BLOCK = 512
N_DENSE = (DEPTH + 1) // 2
N_MOE = DEPTH // 2
NEG_INF = -1e30

SWA_WIDTH = SWA_Q_HEADS * HEAD_DIM
SWA_KV_WIDTH = SWA_KV_HEADS * HEAD_DIM
FOX_WIDTH = FOX_HEADS * HEAD_DIM
MOBA_WIDTH = MOBA_HEADS * HEAD_DIM
MIX_WIDTH = SWA_WIDTH + FOX_WIDTH + MOBA_WIDTH
IN_SPLITS = (SWA_WIDTH, SWA_KV_WIDTH, SWA_KV_WIDTH,
             FOX_WIDTH, FOX_WIDTH, FOX_WIDTH, FOX_HEADS,
             MOBA_WIDTH, MOBA_WIDTH, MOBA_WIDTH)
IN_WIDTH = sum(IN_SPLITS)
SPLIT_POINTS = tuple(int(p) for p in np.cumsum(IN_SPLITS)[:-1])
ATTN_SCALE = HEAD_DIM ** -0.5

kernel_name = 'hymba_style_swa_fox_moba_moe_trunk'


def _rms(x):
    xf = x.astype(jnp.float32)
    return xf * lax.rsqrt(jnp.mean(xf * xf, axis=-1, keepdims=True) + RMS_EPS)


def rmsnorm(x, g):
    return (_rms(x) * g.astype(jnp.float32)).astype(x.dtype)


def rope_tables(seq_len):
    inv = 1.0 / (ROPE_THETA ** (jnp.arange(0, HEAD_DIM, 2, dtype=jnp.float32) / HEAD_DIM))
    ang = jnp.arange(seq_len, dtype=jnp.float32)[:, None] * inv[None, :]
    cos = jnp.concatenate([jnp.cos(ang), jnp.cos(ang)], axis=-1)
    sin = jnp.concatenate([jnp.sin(ang), jnp.sin(ang)], axis=-1)
    return cos, sin


def apply_rope(x, cos, sin):
    xf = x.astype(jnp.float32)
    x1, x2 = jnp.split(xf, 2, axis=-1)
    rot = jnp.concatenate([-x2, x1], axis=-1)
    return (xf * cos[None, :, None, :] + rot * sin[None, :, None, :]).astype(x.dtype)


def swa_sink_attention(q, k, v, sinks):
    B, S, Hq, d = q.shape
    Hkv = k.shape[2]
    G = Hq // Hkv
    W = SWA_WINDOW
    nb = S // W
    qb = q.reshape(B, nb, W, Hkv, G, d)
    pad = ((0, 0), (W, 0), (0, 0), (0, 0))
    kb = jnp.pad(k, pad).reshape(B, nb + 1, W, Hkv, d)
    vb = jnp.pad(v, pad).reshape(B, nb + 1, W, Hkv, d)
    kb2 = jnp.concatenate([kb[:, :-1], kb[:, 1:]], axis=2)
    vb2 = jnp.concatenate([vb[:, :-1], vb[:, 1:]], axis=2)
    s = jnp.einsum('bnqhgd,bnkhd->bnhgqk', qb, kb2).astype(jnp.float32) * ATTN_SCALE
    qi = jnp.arange(W)[:, None]
    kj = jnp.arange(2 * W)[None, :]
    diff = W + qi - kj
    kpos = (jnp.arange(nb)[:, None, None] - 1) * W + kj[None]
    mask = (diff >= 0)[None] & (diff < W)[None] & (kpos >= 0)
    s = jnp.where(mask[None, :, None, None], s, NEG_INF)
    sink = sinks.astype(jnp.float32).reshape(Hkv, G)[None, None, :, :, None, None]
    logits = jnp.concatenate([s, jnp.broadcast_to(sink, s.shape[:-1] + (1,))], axis=-1)
    p = jax.nn.softmax(logits, axis=-1)[..., :-1]
    o = jnp.einsum('bnhgqk,bnkhd->bnqhgd', p.astype(v.dtype), vb2)
    return o.reshape(B, S, Hq * d)


def forgetting_attention(q, k, v, log_f):
    B, S, H, d = q.shape
    Qb = FOX_Q_BLOCK
    nb = S // Qb
    c = jnp.cumsum(log_f, axis=1)
    cT = c.transpose(0, 2, 1)
    kpos = jnp.arange(S)
    qs = q.reshape(B, nb, Qb, H, d).transpose(1, 0, 2, 3, 4)
    cq = c.reshape(B, nb, Qb, H).transpose(1, 0, 3, 2)

    def block(args):
        qx, cx, n = args
        s = jnp.einsum('bqhd,bkhd->bhqk', qx, k).astype(jnp.float32) * ATTN_SCALE
        s = s + cx[..., None] - cT[:, :, None, :]
        qpos = n * Qb + jnp.arange(Qb)
        s = jnp.where(kpos[None, :] <= qpos[:, None], s, NEG_INF)
        p = jax.nn.softmax(s, axis=-1)
        return jnp.einsum('bhqk,bkhd->bqhd', p.astype(v.dtype), v)

    o = lax.map(block, (qs, cq, jnp.arange(nb)))
    return o.transpose(1, 0, 2, 3, 4).reshape(B, S, H * d)


def moba_attention(q, k, v):
    B, S, H, d = q.shape
    Bk = MOBA_BLOCK
    Sp = -(-S // Bk) * Bk
    nkb = Sp // Bk
    pad = ((0, 0), (0, Sp - S), (0, 0), (0, 0))
    qp, kp, vp = jnp.pad(q, pad), jnp.pad(k, pad), jnp.pad(v, pad)
    kb = kp.reshape(B, nkb, Bk, H, d).transpose(0, 3, 1, 2, 4)
    vb = vp.reshape(B, nkb, Bk, H, d).transpose(0, 3, 1, 2, 4)
    kmean = jnp.mean(kb.astype(jnp.float32), axis=3)
    topk = min(MOBA_TOPK, nkb)
    Qc = MOBA_Q_CHUNK
    nqc = Sp // Qc
    qc = qp.reshape(B, nqc, Qc, H, d).transpose(1, 0, 3, 2, 4)
    bi = jnp.arange(B)[:, None, None, None]
    hi = jnp.arange(H)[None, :, None, None]

    def chunk(args):
        qx, n = args
        qpos = n * Qc + jnp.arange(Qc)
        own = (n * Qc) // Bk
        g = jnp.einsum('bhqd,bhnd->bhqn', qx.astype(jnp.float32), kmean)
        g = jnp.where(jnp.arange(nkb) < own, g, NEG_INF)
        _, idx = lax.top_k(g, topk)
        ksel = kb[bi, hi, idx]
        vsel = vb[bi, hi, idx]
        kown = lax.dynamic_index_in_dim(kb, own, axis=2, keepdims=False)
        vown = lax.dynamic_index_in_dim(vb, own, axis=2, keepdims=False)
        s_sel = jnp.einsum('bhqd,bhqtkd->bhqtk', qx, ksel).astype(jnp.float32) * ATTN_SCALE
        valid = (jnp.arange(topk) < own)[:, None]
        s_sel = jnp.where(valid, s_sel, NEG_INF).reshape(B, H, Qc, topk * Bk)
        s_own = jnp.einsum('bhqd,bhkd->bhqk', qx, kown).astype(jnp.float32) * ATTN_SCALE
        kpos = own * Bk + jnp.arange(Bk)
        s_own = jnp.where(kpos[None, :] <= qpos[:, None], s_own, NEG_INF)
        p = jax.nn.softmax(jnp.concatenate([s_sel, s_own], axis=-1), axis=-1).astype(v.dtype)
        p_sel = p[..., :topk * Bk].reshape(B, H, Qc, topk, Bk)
        p_own = p[..., topk * Bk:]
        return (jnp.einsum('bhqtk,bhqtkd->bhqd', p_sel, vsel)
                + jnp.einsum('bhqk,bhkd->bhqd', p_own, vown))

    o = lax.map(chunk, (qc, jnp.arange(nqc)))
    o = o.transpose(1, 0, 3, 2, 4).reshape(B, Sp, H * d)
    return o[:, :S]


def hybrid_mixer(h, norm_g, w_in, forget_bias, sinks, mix_gain, w_out, cos, sin):
    B, S, _ = h.shape
    u = rmsnorm(h, norm_g) @ w_in
    aq, ak, av, bq, bk, bv, bf, cq, ck, cv = jnp.split(u, SPLIT_POINTS, axis=-1)
    heads = lambda t, n: t.reshape(B, S, n, HEAD_DIM)
    oa = swa_sink_attention(apply_rope(heads(aq, SWA_Q_HEADS), cos, sin),
                            apply_rope(heads(ak, SWA_KV_HEADS), cos, sin),
                            heads(av, SWA_KV_HEADS), sinks)
    log_f = jax.nn.log_sigmoid(bf.astype(jnp.float32) + forget_bias.astype(jnp.float32))
    ob = forgetting_attention(heads(bq, FOX_HEADS), heads(bk, FOX_HEADS),
                              heads(bv, FOX_HEADS), log_f)
    oc = moba_attention(apply_rope(heads(cq, MOBA_HEADS), cos, sin),
                        apply_rope(heads(ck, MOBA_HEADS), cos, sin),
                        heads(cv, MOBA_HEADS))
    y = jnp.concatenate([_rms(oa), _rms(ob), _rms(oc)], axis=-1) * mix_gain.astype(jnp.float32)
    return y.astype(h.dtype) @ w_out


def swiglu(x, w_gate, w_up, w_down):
    return (jax.nn.silu(x @ w_gate) * (x @ w_up)) @ w_down


def moe_swiglu(x, w_router, w_gate, w_up, w_down):
    B, S, D = x.shape
    T = B * S
    xt = x.reshape(T, D)
    logits = (xt @ w_router).astype(jnp.float32)
    top_v, top_i = lax.top_k(logits, TOP_K)
    gates = jax.nn.softmax(top_v, axis=-1)
    e_flat = top_i.reshape(-1)
    tok_flat = jnp.repeat(jnp.arange(T), TOP_K)
    g_flat = gates.reshape(-1)
    order = jnp.argsort(e_flat)
    e_s, tok_s, g_s = e_flat[order], tok_flat[order], g_flat[order]
    counts = jax.ops.segment_sum(jnp.ones_like(e_flat), e_flat, num_segments=N_EXPERTS)
    starts = jnp.cumsum(counts) - counts
    padded = (counts + MOE_BLOCK - 1) // MOE_BLOCK * MOE_BLOCK
    pends = jnp.cumsum(padded)
    pstarts = pends - padded
    dest = pstarts[e_s] + (jnp.arange(T * TOP_K) - starts[e_s])
    P = -(-(T * TOP_K) // MOE_BLOCK) * MOE_BLOCK + N_EXPERTS * MOE_BLOCK
    nblk = P // MOE_BLOCK
    xbuf = jnp.zeros((P, D), x.dtype).at[dest].set(xt[tok_s])
    blk_start = jnp.arange(nblk) * MOE_BLOCK
    blk_expert = jnp.minimum(jnp.sum(blk_start[:, None] >= pends[None, :], axis=1), N_EXPERTS - 1)

    def run(args):
        xb, e = args
        return swiglu(xb, w_gate[e], w_up[e], w_down[e])

    ybuf = lax.map(run, (xbuf.reshape(nblk, MOE_BLOCK, D), blk_expert)).reshape(P, D)
    y = jnp.zeros((T, D), x.dtype).at[tok_s].add(ybuf[dest] * g_s[:, None].astype(x.dtype))
    return y.reshape(B, S, D)


def setup_inputs(seed: int = 0) -> dict:
    key = jax.random.key(seed)
    ks = jax.random.split(key, 17)
    nrm = lambda k, shape, scale: jax.random.normal(k, shape, jnp.float32) * scale
    return {
        'x': nrm(ks[0], (BATCH, SEQ, D_MODEL), 1.0),
        'attn_norm': 1.0 + nrm(ks[1], (DEPTH, D_MODEL), 0.02),
        'w_in': nrm(ks[2], (DEPTH, D_MODEL, IN_WIDTH), D_MODEL ** -0.5),
        'fox_forget_bias': 2.0 + nrm(ks[3], (DEPTH, FOX_HEADS), 0.5),
        'swa_sinks': nrm(ks[4], (DEPTH, SWA_Q_HEADS), 0.5),
        'mix_gain': 1.0 + nrm(ks[5], (DEPTH, MIX_WIDTH), 0.02),
        'w_out': nrm(ks[6], (DEPTH, MIX_WIDTH, D_MODEL), MIX_WIDTH ** -0.5),
        'ffn_norm': 1.0 + nrm(ks[7], (DEPTH, D_MODEL), 0.02),
        'dense_w_gate': nrm(ks[8], (N_DENSE, D_MODEL, D_FF), D_MODEL ** -0.5),
        'dense_w_up': nrm(ks[9], (N_DENSE, D_MODEL, D_FF), D_MODEL ** -0.5),
        'dense_w_down': nrm(ks[10], (N_DENSE, D_FF, D_MODEL), D_FF ** -0.5),
        'router_w': nrm(ks[11], (N_MOE, D_MODEL, N_EXPERTS), D_MODEL ** -0.5),
        'moe_w_gate': nrm(ks[12], (N_MOE, N_EXPERTS, D_MODEL, D_FF), D_MODEL ** -0.5),
        'moe_w_up': nrm(ks[13], (N_MOE, N_EXPERTS, D_MODEL, D_FF), D_MODEL ** -0.5),
        'moe_w_down': nrm(ks[14], (N_MOE, N_EXPERTS, D_FF, D_MODEL), D_FF ** -0.5),
        'final_norm': 1.0 + nrm(ks[15], (D_MODEL,), 0.02),
    }


def reference(x, attn_norm, w_in, fox_forget_bias, swa_sinks, mix_gain, w_out, ffn_norm,
              dense_w_gate, dense_w_up, dense_w_down, router_w, moe_w_gate, moe_w_up,
              moe_w_down, final_norm):
    cos, sin = rope_tables(x.shape[1])
    h = x
    for layer in range(DEPTH):
        h = h + hybrid_mixer(h, attn_norm[layer], w_in[layer], fox_forget_bias[layer],
                             swa_sinks[layer], mix_gain[layer], w_out[layer], cos, sin)
        z = rmsnorm(h, ffn_norm[layer])
        j = layer // 2
        if layer % 2 == 0:
            h = h + swiglu(z, dense_w_gate[j], dense_w_up[j], dense_w_down[j])
        else:
            h = h + moe_swiglu(z, router_w[j], moe_w_gate[j], moe_w_up[j], moe_w_down[j])
    return rmsnorm(h, final_norm)
```

```python
import functools

import jax
import jax.numpy as jnp
import numpy as np
from jax import lax
from jax.experimental import pallas as pl
from jax.experimental.pallas import tpu as pltpu

F32 = jnp.float32
BF16 = jnp.bfloat16

D_MODEL = 1024
HEAD_DIM = 64
LANES = 128
SWA_Q_HEADS = 8
SWA_KV_HEADS = 2
SWA_WINDOW = 128
FOX_HEADS = 4
MOBA_HEADS = 4
MOBA_BLOCK = 256
MOBA_TOPK = 3
ROPE_THETA = 10000.0
RMS_EPS = 1e-5
D_FF = 3584
N_EXPERTS = 8
TOP_K = 2
MOE_BLOCK = 512
NEG_INF = -1e30
ATTN_SCALE = HEAD_DIM ** -0.5

SWA_WIDTH = SWA_Q_HEADS * HEAD_DIM
SWA_KV_WIDTH = SWA_KV_HEADS * HEAD_DIM
FOX_WIDTH = FOX_HEADS * HEAD_DIM
MOBA_WIDTH = MOBA_HEADS * HEAD_DIM
MIX_WIDTH = SWA_WIDTH + FOX_WIDTH + MOBA_WIDTH

U_BLOCKS = 18
U_WIDTH = U_BLOCKS * LANES
ROPE_BLOCKS = (0, 1, 2, 3, 4, 12, 13, 14, 15)
Q_BLOCKS = (0, 1, 2, 3, 6, 7, 12, 13)
W_ALL_WIDTH = U_WIDTH + LANES

VMEM_LIMIT = 56 * 1024 * 1024

ROW_TILE = 512
ATT_TILE = 256
FF_TILE = 512
CHUNKS = D_MODEL // LANES


def _params(*sem):
    return pltpu.CompilerParams(dimension_semantics=sem, vmem_limit_bytes=VMEM_LIMIT)


def _split3(x):
    hi = x.astype(BF16)
    r1 = x - hi.astype(F32)
    mid = r1.astype(BF16)
    lo = (r1 - mid.astype(F32)).astype(BF16)
    return hi, mid, lo


def _dot_nt(a, b):
    return lax.dot_general(a, b, (((1,), (1,)), ((), ())), preferred_element_type=F32)


def _dot(a, b):
    return jnp.dot(a, b, preferred_element_type=F32)


def _proj_kernel(x_ref, g_ref, w_ref, cos_ref, sa_ref, sb_ref, fb_ref, u_ref, lf_ref):
    x = x_ref[...]
    inv = lax.rsqrt(jnp.mean(x * x, axis=-1, keepdims=True) + RMS_EPS)
    h = (x * inv * g_ref[...]).astype(BF16)
    cos = cos_ref[...]
    sa = sa_ref[...]
    sb = sb_ref[...]
    for c in range(U_BLOCKS // 2):
        acc = _dot(h, w_ref[:, c * 2 * LANES:(c + 1) * 2 * LANES])
        for half in range(2):
            blk = 2 * c + half
            a = acc[:, half * LANES:(half + 1) * LANES]
            if blk in ROPE_BLOCKS:
                a = a * cos + pltpu.roll(a, 32, 1) * sa + pltpu.roll(a, 96, 1) * sb
            if blk in Q_BLOCKS:
                a = a * ATTN_SCALE
            u_ref[:, blk * LANES:(blk + 1) * LANES] = a.astype(BF16)
    f = _dot(h, w_ref[:, U_WIDTH:W_ALL_WIDTH]) + fb_ref[...]
    lf_ref[...] = jnp.minimum(f, 0.0) - jnp.log(1.0 + jnp.exp(-jnp.abs(f)))


def _project(x2, norm_g, w_all, cos, sa, sb, fbias, seq):
    t = x2.shape[0]
    tm = ROW_TILE
    nseq = seq // tm
    row = lambda i: (i, 0)
    pos = lambda i: (i % nseq, 0)
    fixed = lambda i: (0, 0)
    return pl.pallas_call(
        _proj_kernel,
        grid=(t // tm,),
        in_specs=[
            pl.BlockSpec((tm, D_MODEL), row),
            pl.BlockSpec((1, D_MODEL), fixed),
            pl.BlockSpec((D_MODEL, W_ALL_WIDTH), fixed),
            pl.BlockSpec((tm, LANES), pos),
            pl.BlockSpec((tm, LANES), pos),
            pl.BlockSpec((tm, LANES), pos),
            pl.BlockSpec((1, LANES), fixed),
        ],
        out_specs=[pl.BlockSpec((tm, U_WIDTH), row), pl.BlockSpec((tm, LANES), row)],
        out_shape=[jax.ShapeDtypeStruct((t, U_WIDTH), BF16),
                   jax.ShapeDtypeStruct((t, LANES), F32)],
        compiler_params=_params("parallel"),
        name="proj",
    )(x2, norm_g, w_all, cos, sa, sb, fbias)


def _swa_kernel(sink_ref, q_ref, kc_ref, kp_ref, vc_ref, vp_ref, o_ref):
    n = pl.program_id(1)
    w = SWA_WINDOW
    group = SWA_Q_HEADS // SWA_KV_HEADS
    lane = lax.broadcasted_iota(jnp.int32, (1, LANES), 1)
    k2 = jnp.concatenate([kp_ref[0], kc_ref[0]], axis=0)
    v2 = jnp.concatenate([vp_ref[0], vc_ref[0]], axis=0)
    qi = lax.broadcasted_iota(jnp.int32, (group * w, 2 * w), 0) % w
    kj = lax.broadcasted_iota(jnp.int32, (group * w, 2 * w), 1)
    valid = (kj > qi) & (kj <= qi + w) & ((kj >= w) | (n > 0))
    outs = []
    for g in range(SWA_KV_HEADS):
        keep = (lane < HEAD_DIM) if g == 0 else (lane >= HEAD_DIM)
        qs = jnp.concatenate(
            [jnp.where(keep, q_ref[0, :, j * LANES:(j + 1) * LANES], 0) for j in range(group)],
            axis=0)
        s = jnp.where(valid, _dot_nt(qs, k2), NEG_INF)
        sink = jnp.concatenate(
            [jnp.full((w, 1), sink_ref[g * group + j], F32) for j in range(group)], axis=0)
        m = jnp.maximum(jnp.max(s, axis=1, keepdims=True), sink)
        p = jnp.exp(s - m)
        l = jnp.sum(p, axis=1, keepdims=True) + jnp.exp(sink - m)
        outs.append(_dot(p.astype(BF16), v2) / l)
    for j in range(group):
        oj = jnp.where(lane < HEAD_DIM, outs[0][j * w:(j + 1) * w], outs[1][j * w:(j + 1) * w])
        o_ref[0, :, j * LANES:(j + 1) * LANES] = oj.astype(BF16)


def _swa(u3, sinks):
    b, s, _ = u3.shape
    w = SWA_WINDOW
    cur = lambda blk: (lambda bi, n, sk: (bi, n, blk))
    prev = lambda blk: (lambda bi, n, sk: (bi, jnp.maximum(n - 1, 0), blk))
    grid_spec = pltpu.PrefetchScalarGridSpec(
        num_scalar_prefetch=1,
        grid=(b, s // w),
        in_specs=[
            pl.BlockSpec((1, w, SWA_WIDTH), lambda bi, n, sk: (bi, n, 0)),
            pl.BlockSpec((1, w, LANES), cur(4)),
            pl.BlockSpec((1, w, LANES), prev(4)),
            pl.BlockSpec((1, w, LANES), cur(5)),
            pl.BlockSpec((1, w, LANES), prev(5)),
        ],
        out_specs=pl.BlockSpec((1, w, SWA_WIDTH), lambda bi, n, sk: (bi, n, 0)),
    )
    return pl.pallas_call(
        _swa_kernel,
        grid_spec=grid_spec,
        out_shape=jax.ShapeDtypeStruct((b, s, SWA_WIDTH), BF16),
        compiler_params=_params("parallel", "parallel"),
        name="swa",
    )(sinks, u3, u3, u3, u3, u3)


def _cumsum_kernel(lf_ref, c_ref, ct_ref):
    s = lf_ref.shape[1]
    tc = ATT_TILE
    r = lax.broadcasted_iota(jnp.int32, (tc, tc), 0)
    cc = lax.broadcasted_iota(jnp.int32, (tc, tc), 1)
    tri = jnp.where(cc <= r, 1.0, 0.0).astype(BF16)
    carry = jnp.zeros((1, LANES), F32)
    for i in range(s // tc):
        hi, mid, lo = _split3(lf_ref[0, i * tc:(i + 1) * tc, :])
        ci = (_dot(tri, hi) + _dot(tri, mid)) + _dot(tri, lo) + carry
        c_ref[0, i * tc:(i + 1) * tc, :] = ci
        ct_ref[0, :, i * tc:(i + 1) * tc] = ci.T[:8, :]
        carry = ci[tc - 1:tc, :]


def _forget_cumsum(lf3):
    b, s, _ = lf3.shape
    return pl.pallas_call(
        _cumsum_kernel,
        grid=(b,),
        in_specs=[pl.BlockSpec((1, s, LANES), lambda bi: (bi, 0, 0))],
        out_specs=[pl.BlockSpec((1, s, LANES), lambda bi: (bi, 0, 0)),
                   pl.BlockSpec((1, 8, s), lambda bi: (bi, 0, 0))],
        out_shape=[jax.ShapeDtypeStruct((b, s, LANES), F32),
                   jax.ShapeDtypeStruct((b, 8, s), F32)],
        compiler_params=_params("parallel"),
        name="forget_cumsum",
    )(lf3)


def _softmax_first(s, v, m_scr, l_scr, acc_scr):
    m = jnp.max(s, axis=1, keepdims=True)
    p = jnp.exp(s - m)
    m_scr[...] = m
    l_scr[...] = jnp.sum(p, axis=1, keepdims=True)
    acc_scr[...] = _dot(p.astype(BF16), v)


def _softmax_next(s, v, m_scr, l_scr, acc_scr):
    m_prev = m_scr[...]
    m_new = jnp.maximum(m_prev, jnp.max(s, axis=1, keepdims=True))
    alpha = jnp.exp(m_prev - m_new)
    p = jnp.exp(s - m_new)
    l_scr[...] = alpha * l_scr[...] + jnp.sum(p, axis=1, keepdims=True)
    acc_scr[...] = alpha * acc_scr[...] + _dot(p.astype(BF16), v)
    m_scr[...] = m_new


def _fox_kernel(q_ref, k_ref, v_ref, c_ref, ct_ref, o_ref, m_scr, l_scr, acc_scr):
    qi = pl.program_id(1)
    t = ATT_TILE
    lane = lax.broadcasted_iota(jnp.int32, (1, LANES), 1)
    row = lax.broadcasted_iota(jnp.int32, (t, t), 0)
    col = lax.broadcasted_iota(jnp.int32, (t, t), 1)
    outs = []
    for h in range(FOX_HEADS):
        pb, hh = divmod(h, 2)
        cols = slice(pb * LANES, (pb + 1) * LANES)
        keep = (lane < HEAD_DIM) if hh == 0 else (lane >= HEAD_DIM)
        qm = jnp.where(keep, q_ref[0, :, cols], 0)
        cq = c_ref[0, :, h:h + 1]

        def scores(j):
            kc = k_ref[0, pl.ds(j * t, t), cols]
            ck = ct_ref[0, h:h + 1, pl.ds(j * t, t)]
            return (_dot_nt(qm, kc) + cq) - ck, v_ref[0, pl.ds(j * t, t), cols]

        s, vc = scores(qi)
        _softmax_first(jnp.where(col <= row, s, NEG_INF), vc, m_scr, l_scr, acc_scr)

        def body(j, carry):
            s, vc = scores(j)
            _softmax_next(s, vc, m_scr, l_scr, acc_scr)
            return carry

        lax.fori_loop(0, qi, body, 0)
        outs.append(acc_scr[...] / l_scr[...])
    for pb in range(FOX_HEADS // 2):
        o = jnp.where(lane < HEAD_DIM, outs[2 * pb], outs[2 * pb + 1])
        o_ref[0, :, pb * LANES:(pb + 1) * LANES] = o.astype(BF16)


def _fox(u3, c3, ct3):
    b, s, _ = u3.shape
    t = ATT_TILE
    full = lambda blk: (lambda bi, qi: (bi, 0, blk))
    return pl.pallas_call(
        _fox_kernel,
        grid=(b, s // t),
        in_specs=[
            pl.BlockSpec((1, t, FOX_WIDTH), lambda bi, qi: (bi, qi, 3)),
            pl.BlockSpec((1, s, FOX_WIDTH), full(4)),
            pl.BlockSpec((1, s, FOX_WIDTH), full(5)),
            pl.BlockSpec((1, t, LANES), lambda bi, qi: (bi, qi, 0)),
            pl.BlockSpec((1, 8, s), lambda bi, qi: (bi, 0, 0)),
        ],
        out_specs=pl.BlockSpec((1, t, FOX_WIDTH), lambda bi, qi: (bi, qi, 0)),
        out_shape=jax.ShapeDtypeStruct((b, s, FOX_WIDTH), BF16),
        scratch_shapes=[pltpu.VMEM((t, 1), F32), pltpu.VMEM((t, 1), F32),
                        pltpu.VMEM((t, LANES), F32)],
        compiler_params=_params("parallel", "arbitrary"),
        name="fox",
    )(u3, u3, u3, c3, ct3)


def _moba_kernel(q_ref, k_ref, v_ref, o_ref, km_scr, m_scr, l_scr, acc_scr):
    qi = pl.program_id(1)
    t = MOBA_BLOCK
    nkb = k_ref.shape[1] // t
    lane = lax.broadcasted_iota(jnp.int32, (1, LANES), 1)
    row = lax.broadcasted_iota(jnp.int32, (t, t), 0)
    col = lax.broadcasted_iota(jnp.int32, (t, t), 1)

    @pl.when(qi == 0)
    def _():
        rows = [jnp.mean(k_ref[0, j * t:(j + 1) * t, :].astype(F32), axis=0, keepdims=True)
                for j in range(nkb)]
        km = jnp.concatenate(rows + [jnp.zeros((LANES - nkb, MOBA_WIDTH), F32)], axis=0)
        for i, part in enumerate(_split3(km)):
            km_scr[i] = part

    outs = []
    for h in range(MOBA_HEADS):
        pb, hh = divmod(h, 2)
        cols = slice(pb * LANES, (pb + 1) * LANES)
        keep = (lane < HEAD_DIM) if hh == 0 else (lane >= HEAD_DIM)
        qm = jnp.where(keep, q_ref[0, :, cols], 0)

        g = (_dot_nt(qm, km_scr[0, :, cols]) + _dot_nt(qm, km_scr[1, :, cols])) \
            + _dot_nt(qm, km_scr[2, :, cols])
        past = lane < qi
        g = jnp.where(past, g, NEG_INF)
        rank = jnp.zeros((t, LANES), jnp.int32)
        for i in range(nkb):
            gi = jnp.broadcast_to(g[:, i:i + 1], (t, LANES))
            beats = (gi > g) | ((gi == g) & (i < lane))
            rank = rank + beats.astype(jnp.int32)
        chosen = past & (rank < MOBA_TOPK)
        drop = jnp.where(chosen, 0.0, 1.0)

        def scores(j):
            return (_dot_nt(qm, k_ref[0, pl.ds(j * t, t), cols]),
                    v_ref[0, pl.ds(j * t, t), cols])

        s, vc = scores(qi)
        _softmax_first(jnp.where(col <= row, s, NEG_INF), vc, m_scr, l_scr, acc_scr)

        def body(j, carry):
            s, vc = scores(j)
            dj = jnp.sum(jnp.where(lane == j, drop, 0.0), axis=1, keepdims=True)
            _softmax_next(jnp.where(dj > 0.5, NEG_INF, s), vc, m_scr, l_scr, acc_scr)
            return carry

        lax.fori_loop(0, qi, body, 0)
        outs.append(acc_scr[...] / l_scr[...])
    for pb in range(MOBA_HEADS // 2):
        o = jnp.where(lane < HEAD_DIM, outs[2 * pb], outs[2 * pb + 1])
        o_ref[0, :, pb * LANES:(pb + 1) * LANES] = o.astype(BF16)


def _moba(u3):
    b, s, _ = u3.shape
    t = MOBA_BLOCK
    assert s % t == 0
    full = lambda blk: (lambda bi, qi: (bi, 0, blk))
    return pl.pallas_call(
        _moba_kernel,
        grid=(b, s // t),
        in_specs=[
            pl.BlockSpec((1, t, MOBA_WIDTH), lambda bi, qi: (bi, qi, 6)),
            pl.BlockSpec((1, s, MOBA_WIDTH), full(7)),
            pl.BlockSpec((1, s, MOBA_WIDTH), full(8)),
        ],
        out_specs=pl.BlockSpec((1, t, MOBA_WIDTH), lambda bi, qi: (bi, qi, 0)),
        out_shape=jax.ShapeDtypeStruct((b, s, MOBA_WIDTH), BF16),
        scratch_shapes=[pltpu.VMEM((3, LANES, MOBA_WIDTH), BF16),
                        pltpu.VMEM((t, 1), F32), pltpu.VMEM((t, 1), F32),
                        pltpu.VMEM((t, LANES), F32)],
        compiler_params=_params("parallel", "arbitrary"),
        name="moba",
    )(u3, u3, u3)


def _rms_f32(x):
    x = x.astype(F32)
    return x * lax.rsqrt(jnp.mean(x * x, axis=-1, keepdims=True) + RMS_EPS)


def _outproj_kernel(oa_ref, ob_ref, oc_ref, gain_ref, w_ref, x_ref, h_ref):
    a0, a1, a2 = 0, SWA_WIDTH, SWA_WIDTH + FOX_WIDTH
    ya = (_rms_f32(oa_ref[...]) * gain_ref[:, a0:a1]).astype(BF16)
    yb = (_rms_f32(ob_ref[...]) * gain_ref[:, a1:a2]).astype(BF16)
    yc = (_rms_f32(oc_ref[...]) * gain_ref[:, a2:]).astype(BF16)
    y = (_dot(ya, w_ref[a0:a1, :]) + _dot(yb, w_ref[a1:a2, :])) + _dot(yc, w_ref[a2:, :])
    h_ref[...] = x_ref[...] + y


def _outproj(oa, ob, oc, gain, w_out, x2):
    t = x2.shape[0]
    tm = ROW_TILE
    row = lambda i: (i, 0)
    fixed = lambda i: (0, 0)
    return pl.pallas_call(
        _outproj_kernel,
        grid=(t // tm,),
        in_specs=[
            pl.BlockSpec((tm, SWA_WIDTH), row),
            pl.BlockSpec((tm, FOX_WIDTH), row),
            pl.BlockSpec((tm, MOBA_WIDTH), row),
            pl.BlockSpec((1, MIX_WIDTH), fixed),
            pl.BlockSpec((MIX_WIDTH, D_MODEL), fixed),
            pl.BlockSpec((tm, D_MODEL), row),
        ],
        out_specs=pl.BlockSpec((tm, D_MODEL), row),
        out_shape=jax.ShapeDtypeStruct((t, D_MODEL), F32),
        compiler_params=_params("parallel"),
        name="outproj",
    )(oa, ob, oc, gain, w_out, x2)


def _silu(x):
    return x / (1.0 + jnp.exp(-x))


def _ffn_kernel(h_ref, g_ref, wg_ref, wu_ref, wd_ref, o_ref, z_scr, acc_scr):
    j = pl.program_id(1)

    @pl.when(j == 0)
    def _():
        x = h_ref[...]
        inv = lax.rsqrt(jnp.mean(x * x, axis=-1, keepdims=True) + RMS_EPS)
        z_scr[...] = (x * inv * g_ref[...]).astype(BF16)

    z = z_scr[...]
    a = (_silu(_dot(z, wg_ref[...])) * _dot(z, wu_ref[...])).astype(BF16)
    y = _dot(a, wd_ref[...])

    @pl.when(j == 0)
    def _():
        acc_scr[...] = y

    @pl.when(j > 0)
    def _():
        acc_scr[...] += y

    @pl.when(j == pl.num_programs(1) - 1)
    def _():
        o_ref[...] = h_ref[...] + acc_scr[...]


def _dense_ffn(h2, norm_g, w_gate, w_up, w_down):
    t = h2.shape[0]
    tm, tf = 2 * ROW_TILE, FF_TILE
    return pl.pallas_call(
        _ffn_kernel,
        grid=(t // tm, D_FF // tf),
        in_specs=[
            pl.BlockSpec((tm, D_MODEL), lambda i, j: (i, 0)),
            pl.BlockSpec((1, D_MODEL), lambda i, j: (0, 0)),
            pl.BlockSpec((D_MODEL, tf), lambda i, j: (0, j)),
            pl.BlockSpec((D_MODEL, tf), lambda i, j: (0, j)),
            pl.BlockSpec((tf, D_MODEL), lambda i, j: (j, 0)),
        ],
        out_specs=pl.BlockSpec((tm, D_MODEL), lambda i, j: (i, 0)),
        out_shape=jax.ShapeDtypeStruct((t, D_MODEL), F32),
        scratch_shapes=[pltpu.VMEM((tm, D_MODEL), BF16), pltpu.VMEM((tm, D_MODEL), F32)],
        compiler_params=_params("parallel", "arbitrary"),
        name="dense_ffn",
    )(h2, norm_g, w_gate, w_up, w_down)


def _router_kernel(h_ref, g_ref, wr_ref, z_ref, idx_ref, gate_ref):
    tm = h_ref.shape[0]
    x = h_ref[...]
    inv = lax.rsqrt(jnp.mean(x * x, axis=-1, keepdims=True) + RMS_EPS)
    z = x * inv * g_ref[...]
    for c in range(CHUNKS):
        z_ref[pl.ds(c, tm, stride=CHUNKS), :] = z[:, c * LANES:(c + 1) * LANES]
    zh, zm, zl = _split3(z)
    wh, wm, wl = wr_ref[0], wr_ref[1], wr_ref[2]
    logits = ((_dot(zh, wh) + (_dot(zh, wm) + _dot(zm, wh)))
              + ((_dot(zh, wl) + _dot(zl, wh)) + _dot(zm, wm)))
    lane = lax.broadcasted_iota(jnp.int32, (tm, LANES), 1)
    logits = jnp.where(lane < N_EXPERTS, logits, -jnp.inf)
    m1 = jnp.max(logits, axis=1, keepdims=True)
    i1 = jnp.min(jnp.where(logits == m1, lane, LANES), axis=1, keepdims=True)
    rest = jnp.where(lane == i1, -jnp.inf, logits)
    m2 = jnp.max(rest, axis=1, keepdims=True)
    i2 = jnp.min(jnp.where(rest == m2, lane, LANES), axis=1, keepdims=True)
    e2 = jnp.exp(m2 - m1)
    g1 = 1.0 / (1.0 + e2)
    idx_ref[...] = jnp.where(lane == 0, i1, i2)
    gate_ref[...] = jnp.where(lane == 0, g1, e2 * g1)


def _router(h2, norm_g, wr3):
    t = h2.shape[0]
    tm = ROW_TILE
    row = lambda i: (i, 0)
    return pl.pallas_call(
        _router_kernel,
        grid=(t // tm,),
        in_specs=[
            pl.BlockSpec((tm, D_MODEL), row),
            pl.BlockSpec((1, D_MODEL), lambda i: (0, 0)),
            pl.BlockSpec((3, D_MODEL, LANES), lambda i: (0, 0, 0)),
        ],
        out_specs=[pl.BlockSpec((tm * CHUNKS, LANES), row),
                   pl.BlockSpec((tm, LANES), row),
                   pl.BlockSpec((tm, LANES), row)],
        out_shape=[jax.ShapeDtypeStruct((t * CHUNKS, LANES), F32),
                   jax.ShapeDtypeStruct((t, LANES), jnp.int32),
                   jax.ShapeDtypeStruct((t, LANES), F32)],
        compiler_params=_params("parallel"),
        name="router",
    )(h2, norm_g, wr3)


def _row_copy(src, src_row, dst, dst_row, sem):
    return pltpu.make_async_copy(src.at[pl.ds(src_row * CHUNKS, CHUNKS), :],
                                 dst.at[pl.ds(dst_row * CHUNKS, CHUNKS), :], sem)


def _moe_kernel(be_ref, nv_ref, src_ref, dst_ref,
                z_hbm, gate_ref, wg_ref, wu_ref, wd_ref,
                y_hbm,
                xg, xb, acc, yc, gsem, ssem):
    i = pl.program_id(0)
    j = pl.program_id(1)
    tm = xb.shape[0]
    nv = nv_ref[i]
    base = i * tm

    @pl.when((i == 0) & (j == 0))
    def _():
        xg[...] = jnp.zeros_like(xg)

    @pl.when(nv > 0)
    def _():
        @pl.when(j == 0)
        def _():
            def issue(r, carry):
                _row_copy(z_hbm, src_ref[base + r], xg, r, gsem).start()
                return carry
            lax.fori_loop(0, nv, issue, 0)

            def wait(r, carry):
                _row_copy(z_hbm, 0, xg, 0, gsem).wait()
                return carry
            lax.fori_loop(0, nv, wait, 0)
            live = lax.broadcasted_iota(jnp.int32, (tm, LANES), 0) < nv
            for c in range(CHUNKS):
                xc = xg[pl.ds(c, tm, stride=CHUNKS), :]
                xb[:, c * LANES:(c + 1) * LANES] = jnp.where(live, xc, 0.0).astype(BF16)

        x = xb[...]
        a = (_silu(_dot(x, wg_ref[0])) * _dot(x, wu_ref[0])).astype(BF16)
        y = _dot(a, wd_ref[0])

        @pl.when(j == 0)
        def _():
            acc[...] = y

        @pl.when(j > 0)
        def _():
            acc[...] += y

        @pl.when(j == pl.num_programs(1) - 1)
        def _():
            out = acc[...] * gate_ref[...]
            for c in range(CHUNKS):
                yc[pl.ds(c, tm, stride=CHUNKS), :] = out[:, c * LANES:(c + 1) * LANES]

            def issue(r, carry):
                _row_copy(yc, r, y_hbm, dst_ref[base + r], ssem).start()
                return carry
            lax.fori_loop(0, nv, issue, 0)

            def wait(r, carry):
                _row_copy(yc, 0, y_hbm, 0, ssem).wait()
                return carry
            lax.fori_loop(0, nv, wait, 0)


def _moe_experts(z8, gates_p, blk_e, blk_nv, src_tok, dst_slot, w_gate, w_up, w_down, t):
    tm, tf = MOE_BLOCK, FF_TILE
    nblk = src_tok.shape[0] // tm
    nff = D_FF // tf
    ff = lambda i, j, nv: jnp.where(nv[i] > 0, j, nff - 1)
    grid_spec = pltpu.PrefetchScalarGridSpec(
        num_scalar_prefetch=4,
        grid=(nblk, nff),
        in_specs=[
            pl.BlockSpec(memory_space=pl.ANY),
            pl.BlockSpec((tm, 1), lambda i, j, be, nv, s, d: (i, 0)),
            pl.BlockSpec((1, D_MODEL, tf), lambda i, j, be, nv, s, d: (be[i], 0, ff(i, j, nv))),
            pl.BlockSpec((1, D_MODEL, tf), lambda i, j, be, nv, s, d: (be[i], 0, ff(i, j, nv))),
            pl.BlockSpec((1, tf, D_MODEL), lambda i, j, be, nv, s, d: (be[i], ff(i, j, nv), 0)),
        ],
        out_specs=pl.BlockSpec(memory_space=pl.ANY),
        scratch_shapes=[
            pltpu.VMEM((tm * CHUNKS, LANES), F32),
            pltpu.VMEM((tm, D_MODEL), BF16),
            pltpu.VMEM((tm, D_MODEL), F32),
            pltpu.VMEM((tm * CHUNKS, LANES), F32),
            pltpu.SemaphoreType.DMA(()),
            pltpu.SemaphoreType.DMA(()),
        ],
    )
    return pl.pallas_call(
        _moe_kernel,
        grid_spec=grid_spec,
        out_shape=jax.ShapeDtypeStruct((t * TOP_K * CHUNKS, LANES), F32),
        compiler_params=_params("arbitrary", "arbitrary"),
        name="moe_experts",
    )(blk_e, blk_nv, src_tok, dst_slot, z8, gates_p, w_gate, w_up, w_down)


def _combine_kernel(h_ref, y_ref, g_ref, o_ref, *, final_norm):
    tm = h_ref.shape[0]
    stride = TOP_K * CHUNKS
    parts = []
    ss = jnp.zeros((tm, 1), F32)
    for c in range(CHUNKS):
        y = y_ref[pl.ds(c, tm, stride=stride), :]
        for k in range(1, TOP_K):
            y = y + y_ref[pl.ds(k * CHUNKS + c, tm, stride=stride), :]
        hc = h_ref[:, c * LANES:(c + 1) * LANES] + y
        parts.append(hc)
        ss = ss + jnp.sum(hc * hc, axis=1, keepdims=True)
    inv = lax.rsqrt(ss / D_MODEL + RMS_EPS)
    for c in range(CHUNKS):
        cols = slice(c * LANES, (c + 1) * LANES)
        o_ref[:, cols] = parts[c] * inv * g_ref[:, cols] if final_norm else parts[c]


def _combine(h2, y8, norm_g, final_norm):
    t = h2.shape[0]
    tm = ROW_TILE
    row = lambda i: (i, 0)
    return pl.pallas_call(
        functools.partial(_combine_kernel, final_norm=final_norm),
        grid=(t // tm,),
        in_specs=[
            pl.BlockSpec((tm, D_MODEL), row),
            pl.BlockSpec((tm * TOP_K * CHUNKS, LANES), row),
            pl.BlockSpec((1, D_MODEL), lambda i: (0, 0)),
        ],
        out_specs=pl.BlockSpec((tm, D_MODEL), row),
        out_shape=jax.ShapeDtypeStruct((t, D_MODEL), F32),
        compiler_params=_params("parallel"),
        name="moe_combine",
    )(h2, y8, norm_g)


def _norm_kernel(h_ref, g_ref, o_ref):
    x = h_ref[...]
    o_ref[...] = x * lax.rsqrt(jnp.mean(x * x, axis=-1, keepdims=True) + RMS_EPS) * g_ref[...]


def _final_norm(h2, norm_g):
    t = h2.shape[0]
    tm = ROW_TILE
    return pl.pallas_call(
        _norm_kernel,
        grid=(t // tm,),
        in_specs=[pl.BlockSpec((tm, D_MODEL), lambda i: (i, 0)),
                  pl.BlockSpec((1, D_MODEL), lambda i: (0, 0))],
        out_specs=pl.BlockSpec((tm, D_MODEL), lambda i: (i, 0)),
        out_shape=jax.ShapeDtypeStruct((t, D_MODEL), F32),
        compiler_params=_params("parallel"),
        name="final_norm",
    )(h2, norm_g)


def _dispatch_plan(top_i, gates, t):
    n = t * TOP_K
    tm = MOE_BLOCK
    e_flat = top_i.reshape(n)
    onehot = (e_flat[:, None] == jnp.arange(N_EXPERTS)[None, :]).astype(jnp.int32)
    counts = jnp.sum(onehot, axis=0)
    rank = jnp.sum((jnp.cumsum(onehot, axis=0) - onehot) * onehot, axis=1)
    padded = (counts + tm - 1) // tm * tm
    pends = jnp.cumsum(padded)
    pstarts = pends - padded
    dest = pstarts[e_flat] + rank
    p_rows = -(-n // tm) * tm + N_EXPERTS * tm
    nblk = p_rows // tm
    flat = jnp.arange(n, dtype=jnp.int32)
    src_tok = jnp.zeros((p_rows,), jnp.int32).at[dest].set(flat // TOP_K)
    dst_slot = jnp.zeros((p_rows,), jnp.int32).at[dest].set(flat)
    gates_p = jnp.zeros((p_rows,), F32).at[dest].set(gates.reshape(n))
    blk_start = jnp.arange(nblk, dtype=jnp.int32) * tm
    blk_e = jnp.minimum(jnp.sum(blk_start[:, None] >= pends[None, :], axis=1), N_EXPERTS - 1)
    blk_nv = jnp.clip(pstarts[blk_e] + counts[blk_e] - blk_start, 0, tm)
    blk_nv = jnp.where(blk_start < pends[-1], blk_nv, 0)
    last_used = jnp.maximum(pends[-1] // tm - 1, 0)
    blk_e = jnp.where(blk_start < pends[-1], blk_e, blk_e[last_used])
    return (blk_e.astype(jnp.int32), blk_nv.astype(jnp.int32), src_tok, dst_slot,
            gates_p.reshape(p_rows, 1))


def _rope_tables(seq):
    inv = 1.0 / (ROPE_THETA ** (jnp.arange(0, HEAD_DIM, 2, dtype=F32) / HEAD_DIM))
    ang = jnp.arange(seq, dtype=F32)[:, None] * inv[None, :]
    cos = jnp.concatenate([jnp.cos(ang)] * 4, axis=-1)
    sin = jnp.concatenate([jnp.sin(ang)] * 4, axis=-1)
    upper = (jnp.arange(LANES) % HEAD_DIM) >= HEAD_DIM // 2
    sa = jnp.where(upper[None, :], sin, 0.0)
    sb = jnp.where(upper[None, :], 0.0, -sin)
    return cos, sa, sb


def _in_proj_columns():
    split = np.cumsum([0, SWA_WIDTH, SWA_KV_WIDTH, SWA_KV_WIDTH, FOX_WIDTH, FOX_WIDTH, FOX_WIDTH,
                       FOX_HEADS, MOBA_WIDTH, MOBA_WIDTH, MOBA_WIDTH])
    aq, ak, av, bq, bk, bv, bf, cq, ck, cv = split[:-1]
    half = SWA_Q_HEADS // 2
    cols = []
    for j in range(half):
        cols += list(range(aq + j * HEAD_DIM, aq + (j + 1) * HEAD_DIM))
        cols += list(range(aq + (j + half) * HEAD_DIM, aq + (j + half + 1) * HEAD_DIM))
    for start, width in ((ak, SWA_KV_WIDTH), (av, SWA_KV_WIDTH), (bq, FOX_WIDTH), (bk, FOX_WIDTH),
                         (bv, FOX_WIDTH), (cq, MOBA_WIDTH), (ck, MOBA_WIDTH), (cv, MOBA_WIDTH)):
        cols += list(range(start, start + width))
    forget = list(range(bf, bf + FOX_HEADS))
    return np.array(cols, np.int32), np.array(forget, np.int32)


def _swa_out_perm():
    half = SWA_Q_HEADS // 2
    perm = []
    for j in range(half):
        perm += list(range(j * HEAD_DIM, (j + 1) * HEAD_DIM))
        perm += list(range((j + half) * HEAD_DIM, (j + half + 1) * HEAD_DIM))
    return np.array(perm + list(range(SWA_WIDTH, MIX_WIDTH)), np.int32)


def _mixer(h2, b, s, norm_g, w_in, forget_bias, sinks, mix_gain, w_out, tables):
    cols, fcols = _in_proj_columns()
    w_all = jnp.concatenate(
        [w_in[:, cols], jnp.pad(w_in[:, fcols], ((0, 0), (0, LANES - FOX_HEADS)))], axis=1
    ).astype(BF16)
    fbias = jnp.pad(forget_bias.astype(F32), (0, LANES - FOX_HEADS)).reshape(1, LANES)
    u, lf = _project(h2, norm_g.reshape(1, D_MODEL), w_all, *tables, fbias, s)
    u3 = u.reshape(b, s, U_WIDTH)
    oa = _swa(u3, sinks.astype(F32))
    c3, ct3 = _forget_cumsum(lf.reshape(b, s, LANES))
    ob = _fox(u3, c3, ct3)
    oc = _moba(u3)
    perm = _swa_out_perm()
    t = b * s
    return _outproj(oa.reshape(t, SWA_WIDTH), ob.reshape(t, FOX_WIDTH), oc.reshape(t, MOBA_WIDTH),
                    mix_gain[perm].reshape(1, MIX_WIDTH).astype(F32),
                    w_out[perm].astype(BF16), h2)


def _moe_ffn(h2, norm_g, w_router, w_gate, w_up, w_down, out_g):
    t = h2.shape[0]
    wr = jnp.pad(w_router.astype(F32), ((0, 0), (0, LANES - N_EXPERTS)))
    wr3 = jnp.stack(_split3(wr))
    z8, top, gate = _router(h2, norm_g.reshape(1, D_MODEL), wr3)
    plan = _dispatch_plan(top[:, :TOP_K], gate[:, :TOP_K], t)
    blk_e, blk_nv, src_tok, dst_slot, gates_p = plan
    y8 = _moe_experts(z8, gates_p, blk_e, blk_nv, src_tok, dst_slot,
                      w_gate.astype(BF16), w_up.astype(BF16), w_down.astype(BF16), t)
    if out_g is None:
        return _combine(h2, y8, norm_g.reshape(1, D_MODEL), False)
    return _combine(h2, y8, out_g.reshape(1, D_MODEL), True)


def kernel(x, attn_norm, w_in, fox_forget_bias, swa_sinks, mix_gain, w_out, ffn_norm,
           dense_w_gate, dense_w_up, dense_w_down, router_w, moe_w_gate, moe_w_up,
           moe_w_down, final_norm):
    b, s, d = x.shape
    depth = attn_norm.shape[0]
    assert d == D_MODEL and s % MOBA_BLOCK == 0 and s % ROW_TILE == 0
    tables = _rope_tables(s)
    h = x.reshape(b * s, d)
    normed = False
    for layer in range(depth):
        h = _mixer(h, b, s, attn_norm[layer], w_in[layer], fox_forget_bias[layer],
                   swa_sinks[layer], mix_gain[layer], w_out[layer], tables)
        j = layer // 2
        last = layer == depth - 1
        if layer % 2 == 0:
            h = _dense_ffn(h, ffn_norm[layer].reshape(1, d), dense_w_gate[j].astype(BF16),
                           dense_w_up[j].astype(BF16), dense_w_down[j].astype(BF16))
        else:
            h = _moe_ffn(h, ffn_norm[layer], router_w[j], moe_w_gate[j], moe_w_up[j],
                         moe_w_down[j], final_norm if last else None)
            normed = last
    if not normed:
        h = _final_norm(h, final_norm.reshape(1, d))
    return h.reshape(b, s, d)
```

```python
import functools

import jax
import jax.numpy as jnp
import numpy as np
from jax import lax
from jax.experimental import pallas as pl
from jax.experimental.pallas import tpu as pltpu

F32 = jnp.float32
BF16 = jnp.bfloat16

D_MODEL = 1024
HEAD_DIM = 64
LANES = 128
SWA_Q_HEADS = 8
SWA_KV_HEADS = 2
SWA_WINDOW = 128
FOX_HEADS = 4
MOBA_HEADS = 4
MOBA_BLOCK = 256
MOBA_TOPK = 3
ROPE_THETA = 10000.0
RMS_EPS = 1e-5
D_FF = 3584
N_EXPERTS = 8
TOP_K = 2
MOE_BLOCK = 512
NEG_INF = -1e30
ATTN_SCALE = HEAD_DIM ** -0.5

SWA_WIDTH = SWA_Q_HEADS * HEAD_DIM
SWA_KV_WIDTH = SWA_KV_HEADS * HEAD_DIM
FOX_WIDTH = FOX_HEADS * HEAD_DIM
MOBA_WIDTH = MOBA_HEADS * HEAD_DIM
MIX_WIDTH = SWA_WIDTH + FOX_WIDTH + MOBA_WIDTH

U_BLOCKS = 18
U_WIDTH = U_BLOCKS * LANES
ROPE_BLOCKS = (0, 1, 2, 3, 4, 12, 13, 14, 15)
Q_BLOCKS = (0, 1, 2, 3, 6, 7, 12, 13)
W_ALL_WIDTH = U_WIDTH + LANES

VMEM_LIMIT = 56 * 1024 * 1024

ROW_TILE = 512
ATT_TILE = 256
FF_TILE = 512
CHUNKS = D_MODEL // LANES


def _params(*sem):
    return pltpu.CompilerParams(dimension_semantics=sem, vmem_limit_bytes=VMEM_LIMIT)


def _split3(x):
    hi = x.astype(BF16)
    r1 = x - hi.astype(F32)
    mid = r1.astype(BF16)
    lo = (r1 - mid.astype(F32)).astype(BF16)
    return hi, mid, lo


def _dot_nt(a, b):
    return lax.dot_general(a, b, (((1,), (1,)), ((), ())), preferred_element_type=F32)


def _dot(a, b):
    return jnp.dot(a, b, preferred_element_type=F32)


def _proj_kernel(x_ref, g_ref, w_ref, cos_ref, sa_ref, sb_ref, fb_ref, u_ref, lf_ref):
    x = x_ref[...]
    inv = lax.rsqrt(jnp.mean(x * x, axis=-1, keepdims=True) + RMS_EPS)
    h = (x * inv * g_ref[...]).astype(BF16)
    cos = cos_ref[...]
    sa = sa_ref[...]
    sb = sb_ref[...]
    for c in range(U_BLOCKS // 2):
        acc = _dot(h, w_ref[:, c * 2 * LANES:(c + 1) * 2 * LANES])
        for half in range(2):
            blk = 2 * c + half
            a = acc[:, half * LANES:(half + 1) * LANES]
            if blk in ROPE_BLOCKS:
                a = a * cos + pltpu.roll(a, 32, 1) * sa + pltpu.roll(a, 96, 1) * sb
            if blk in Q_BLOCKS:
                a = a * ATTN_SCALE
            u_ref[:, blk * LANES:(blk + 1) * LANES] = a.astype(BF16)
    f = _dot(h, w_ref[:, U_WIDTH:W_ALL_WIDTH]) + fb_ref[...]
    lf_ref[...] = jnp.minimum(f, 0.0) - jnp.log(1.0 + jnp.exp(-jnp.abs(f)))


def _project(x2, norm_g, w_all, cos, sa, sb, fbias, seq):
    t = x2.shape[0]
    tm = ROW_TILE
    nseq = seq // tm
    row = lambda i: (i, 0)
    pos = lambda i: (i % nseq, 0)
    fixed = lambda i: (0, 0)
    return pl.pallas_call(
        _proj_kernel,
        grid=(t // tm,),
        in_specs=[
            pl.BlockSpec((tm, D_MODEL), row),
            pl.BlockSpec((1, D_MODEL), fixed),
            pl.BlockSpec((D_MODEL, W_ALL_WIDTH), fixed),
            pl.BlockSpec((tm, LANES), pos),
            pl.BlockSpec((tm, LANES), pos),
            pl.BlockSpec((tm, LANES), pos),
            pl.BlockSpec((1, LANES), fixed),
        ],
        out_specs=[pl.BlockSpec((tm, U_WIDTH), row), pl.BlockSpec((tm, LANES), row)],
        out_shape=[jax.ShapeDtypeStruct((t, U_WIDTH), BF16),
                   jax.ShapeDtypeStruct((t, LANES), F32)],
        compiler_params=_params("parallel"),
        name="proj",
    )(x2, norm_g, w_all, cos, sa, sb, fbias)


def _swa_kernel(sink_ref, q_ref, kc_ref, kp_ref, vc_ref, vp_ref, o_ref):
    n = pl.program_id(1)
    w = SWA_WINDOW
    group = SWA_Q_HEADS // SWA_KV_HEADS
    lane = lax.broadcasted_iota(jnp.int32, (1, LANES), 1)
    k2 = jnp.concatenate([kp_ref[0], kc_ref[0]], axis=0)
    v2 = jnp.concatenate([vp_ref[0], vc_ref[0]], axis=0)
    qi = lax.broadcasted_iota(jnp.int32, (group * w, 2 * w), 0) % w
    kj = lax.broadcasted_iota(jnp.int32, (group * w, 2 * w), 1)
    valid = (kj > qi) & (kj <= qi + w) & ((kj >= w) | (n > 0))
    outs = []
    for g in range(SWA_KV_HEADS):
        keep = (lane < HEAD_DIM) if g == 0 else (lane >= HEAD_DIM)
        qs = jnp.concatenate(
            [jnp.where(keep, q_ref[0, :, j * LANES:(j + 1) * LANES], 0) for j in range(group)],
            axis=0)
        s = jnp.where(valid, _dot_nt(qs, k2), NEG_INF)
        ps, inv = [], []
        for j in range(group):
            sj = s[j * w:(j + 1) * w]
            sink = sink_ref[g * group + j]
            m = jnp.maximum(jnp.max(sj, axis=1, keepdims=True), sink)
            p = jnp.exp(sj - m)
            inv.append(1.0 / (jnp.sum(p, axis=1, keepdims=True) + jnp.exp(sink - m)))
            ps.append(p.astype(BF16))
        o = _dot(jnp.concatenate(ps, axis=0), v2)
        outs.append([o[j * w:(j + 1) * w] * inv[j] for j in range(group)])
    for j in range(group):
        oj = jnp.where(lane < HEAD_DIM, outs[0][j], outs[1][j])
        o_ref[0, :, j * LANES:(j + 1) * LANES] = oj.astype(BF16)


def _swa(u3, sinks):
    b, s, _ = u3.shape
    w = SWA_WINDOW
    cur = lambda blk: (lambda bi, n, sk: (bi, n, blk))
    prev = lambda blk: (lambda bi, n, sk: (bi, jnp.maximum(n - 1, 0), blk))
    grid_spec = pltpu.PrefetchScalarGridSpec(
        num_scalar_prefetch=1,
        grid=(b, s // w),
        in_specs=[
            pl.BlockSpec((1, w, SWA_WIDTH), lambda bi, n, sk: (bi, n, 0)),
            pl.BlockSpec((1, w, LANES), cur(4)),
            pl.BlockSpec((1, w, LANES), prev(4)),
            pl.BlockSpec((1, w, LANES), cur(5)),
            pl.BlockSpec((1, w, LANES), prev(5)),
        ],
        out_specs=pl.BlockSpec((1, w, SWA_WIDTH), lambda bi, n, sk: (bi, n, 0)),
    )
    return pl.pallas_call(
        _swa_kernel,
        grid_spec=grid_spec,
        out_shape=jax.ShapeDtypeStruct((b, s, SWA_WIDTH), BF16),
        compiler_params=_params("parallel", "parallel"),
        name="swa",
    )(sinks, u3, u3, u3, u3, u3)


def _head_lanes(h):
    lane = lax.broadcasted_iota(jnp.int32, (1, LANES), 1)
    half = h % 2
    return (lane // HEAD_DIM) == half, lane - (1 - half) * HEAD_DIM


def _causal_attention(h, qp_scr, kp_scr, v_ref, o_ref, t):
    s_len = qp_scr.shape[0]
    lane = lax.broadcasted_iota(jnp.int32, (1, LANES), 1)
    row = lax.broadcasted_iota(jnp.int32, (t, t), 0)
    col = lax.broadcasted_iota(jnp.int32, (t, t), 1)
    for qi in range(s_len // t):
        rows = slice(qi * t, (qi + 1) * t)
        q = qp_scr[rows, :]
        sd = jnp.where(col <= row, _dot_nt(q, kp_scr[rows, :]), NEG_INF)
        m = jnp.max(sd, axis=1, keepdims=True)
        if qi:
            sp = _dot_nt(q, kp_scr[:qi * t, :])
            m = jnp.maximum(m, jnp.max(sp, axis=1, keepdims=True))
        pd = jnp.exp(sd - m)
        l = jnp.sum(pd, axis=1, keepdims=True)
        o = _dot(pd.astype(BF16), v_ref[0, rows, :])
        if qi:
            pp = jnp.exp(sp - m)
            l = l + jnp.sum(pp, axis=1, keepdims=True)
            o = o + _dot(pp.astype(BF16), v_ref[0, :qi * t, :])
        o = (o * (1.0 / l)).astype(BF16)

        @pl.when(h % 2 == 0)
        def _():
            o_ref[0, rows, :] = o

        @pl.when(h % 2 == 1)
        def _():
            o_ref[0, rows, :] = jnp.where(lane < HEAD_DIM, o_ref[0, rows, :], o)


def _head_attention(kernel_fn, u3, extra, q_blk, k_blk, v_blk, n_heads, name):
    b, s, _ = u3.shape
    pair = lambda blk: (lambda bi, h: (bi, 0, blk + h // 2))
    seq_block = pl.BlockSpec((1, s, LANES), lambda bi, h: (bi, 0, 0))
    return pl.pallas_call(
        kernel_fn,
        grid=(b, n_heads),
        in_specs=[pl.BlockSpec((1, s, LANES), pair(q_blk)),
                  pl.BlockSpec((1, s, LANES), pair(k_blk)),
                  pl.BlockSpec((1, s, LANES), pair(v_blk))] + [seq_block] * len(extra),
        out_specs=pl.BlockSpec((1, s, LANES), pair(0)),
        out_shape=jax.ShapeDtypeStruct((b, s, n_heads * HEAD_DIM), BF16),
        scratch_shapes=[pltpu.VMEM((s, LANES), BF16), pltpu.VMEM((s, LANES), BF16)],
        compiler_params=_params("parallel", "arbitrary"),
        name=name,
    )(u3, u3, u3, *extra)


def _fox_kernel(q_ref, k_ref, v_ref, lf_ref, o_ref, qp_scr, kp_scr):
    h = pl.program_id(1)
    s_len = q_ref.shape[1]
    t = ATT_TILE
    lane = lax.broadcasted_iota(jnp.int32, (1, LANES), 1)
    keep, f = _head_lanes(h)
    r = lax.broadcasted_iota(jnp.int32, (t, t), 0)
    cc = lax.broadcasted_iota(jnp.int32, (t, t), 1)
    tri = jnp.where(cc <= r, 1.0, 0.0).astype(BF16)
    ones_q = jnp.where((f >= 3) & (f < 6), 1.0, 0.0).astype(BF16)
    ones_k = jnp.where((f >= 0) & (f < 3), 1.0, 0.0).astype(BF16)
    carry = jnp.zeros((1, LANES), F32)
    for i in range(s_len // t):
        rows = slice(i * t, (i + 1) * t)
        lf = jnp.sum(jnp.where(lane == h, lf_ref[0, rows, :], 0.0), axis=1, keepdims=True)
        hi, mid, lo = _split3(jnp.broadcast_to(lf, (t, LANES)))
        c = (_dot(tri, hi) + _dot(tri, mid)) + _dot(tri, lo) + carry
        carry = c[t - 1:t, :]
        chi, cmid, clo = _split3(c)
        qf = jnp.where(f == 0, chi, jnp.where(f == 1, cmid, jnp.where(f == 2, clo, ones_q)))
        kf = jnp.where(f == 3, -chi, jnp.where(f == 4, -cmid, jnp.where(f == 5, -clo, ones_k)))
        qp_scr[rows, :] = jnp.where(keep, q_ref[0, rows, :], qf)
        kp_scr[rows, :] = jnp.where(keep, k_ref[0, rows, :], kf)
    _causal_attention(h, qp_scr, kp_scr, v_ref, o_ref, t)


def _fox(u3, lf3):
    return _head_attention(_fox_kernel, u3, (lf3,), 6, 8, 10, FOX_HEADS, "fox")


def _moba_kernel(q_ref, k_ref, v_ref, o_ref, qp_scr, kp_scr):
    h = pl.program_id(1)
    s_len = q_ref.shape[1]
    t = MOBA_BLOCK
    nkb = s_len // t
    sub = 8
    assert nkb <= sub
    keep, f = _head_lanes(h)
    means = [jnp.mean(jnp.where(keep, k_ref[0, j * t:(j + 1) * t, :], 0).astype(F32),
                      axis=0, keepdims=True) for j in range(nkb)]
    km = _split3(jnp.concatenate(means + [jnp.zeros((LANES - nkb, LANES), F32)], axis=0))
    blk = lax.broadcasted_iota(jnp.int32, (sub, t), 0)
    for i in range(nkb):
        rows = slice(i * t, (i + 1) * t)
        q = q_ref[0, rows, :]
        g = ((_dot_nt(km[0], q) + _dot_nt(km[1], q)) + _dot_nt(km[2], q))[:sub, :]
        past = blk < i
        g = jnp.where(past, g, NEG_INF)
        rank = jnp.zeros((sub, t), jnp.int32)
        for r in range(i):
            gr = jnp.broadcast_to(g[r:r + 1, :], (sub, t))
            beats = (gr > g) | ((gr == g) & (r < blk))
            rank = rank + beats.astype(jnp.int32)
        drop = jnp.where(past & (rank >= MOBA_TOPK), 1.0, 0.0)
        lo = jnp.concatenate([drop, jnp.zeros((LANES - sub, t), F32)], axis=0)
        hi = jnp.concatenate([jnp.zeros((HEAD_DIM, t), F32), drop,
                              jnp.zeros((HEAD_DIM - sub, t), F32)], axis=0)
        qf = jnp.where(h % 2 == 0, hi, lo).T.astype(BF16)
        kf = jnp.where(f == i, NEG_INF, 0.0).astype(BF16)
        qp_scr[rows, :] = jnp.where(keep, q, qf)
        kp_scr[rows, :] = jnp.where(keep, k_ref[0, rows, :], kf)
    _causal_attention(h, qp_scr, kp_scr, v_ref, o_ref, t)


def _moba(u3):
    return _head_attention(_moba_kernel, u3, (), 12, 14, 16, MOBA_HEADS, "moba")


def _rms_f32(x):
    x = x.astype(F32)
    return x * lax.rsqrt(jnp.mean(x * x, axis=-1, keepdims=True) + RMS_EPS)


def _outproj_kernel(oa_ref, ob_ref, oc_ref, gain_ref, w_ref, x_ref, h_ref):
    a0, a1, a2 = 0, SWA_WIDTH, SWA_WIDTH + FOX_WIDTH
    ya = (_rms_f32(oa_ref[...]) * gain_ref[:, a0:a1]).astype(BF16)
    yb = (_rms_f32(ob_ref[...]) * gain_ref[:, a1:a2]).astype(BF16)
    yc = (_rms_f32(oc_ref[...]) * gain_ref[:, a2:]).astype(BF16)
    y = (_dot(ya, w_ref[a0:a1, :]) + _dot(yb, w_ref[a1:a2, :])) + _dot(yc, w_ref[a2:, :])
    h_ref[...] = x_ref[...] + y


def _outproj(oa, ob, oc, gain, w_out, x2):
    t = x2.shape[0]
    tm = ROW_TILE
    row = lambda i: (i, 0)
    fixed = lambda i: (0, 0)
    return pl.pallas_call(
        _outproj_kernel,
        grid=(t // tm,),
        in_specs=[
            pl.BlockSpec((tm, SWA_WIDTH), row),
            pl.BlockSpec((tm, FOX_WIDTH), row),
            pl.BlockSpec((tm, MOBA_WIDTH), row),
            pl.BlockSpec((1, MIX_WIDTH), fixed),
            pl.BlockSpec((MIX_WIDTH, D_MODEL), fixed),
            pl.BlockSpec((tm, D_MODEL), row),
        ],
        out_specs=pl.BlockSpec((tm, D_MODEL), row),
        out_shape=jax.ShapeDtypeStruct((t, D_MODEL), F32),
        compiler_params=_params("parallel"),
        name="outproj",
    )(oa, ob, oc, gain, w_out, x2)


def _silu(x):
    return x / (1.0 + jnp.exp(-x))


def _ffn_kernel(h_ref, g_ref, wg_ref, wu_ref, wd_ref, o_ref, z_scr, acc_scr):
    j = pl.program_id(1)

    @pl.when(j == 0)
    def _():
        x = h_ref[...]
        inv = lax.rsqrt(jnp.mean(x * x, axis=-1, keepdims=True) + RMS_EPS)
        z_scr[...] = (x * inv * g_ref[...]).astype(BF16)

    z = z_scr[...]
    a = (_silu(_dot(z, wg_ref[...])) * _dot(z, wu_ref[...])).astype(BF16)
    y = _dot(a, wd_ref[...])

    @pl.when(j == 0)
    def _():
        acc_scr[...] = y

    @pl.when(j > 0)
    def _():
        acc_scr[...] += y

    @pl.when(j == pl.num_programs(1) - 1)
    def _():
        o_ref[...] = h_ref[...] + acc_scr[...]


def _dense_ffn(h2, norm_g, w_gate, w_up, w_down):
    t = h2.shape[0]
    tm, tf = 2 * ROW_TILE, FF_TILE
    return pl.pallas_call(
        _ffn_kernel,
        grid=(t // tm, D_FF // tf),
        in_specs=[
            pl.BlockSpec((tm, D_MODEL), lambda i, j: (i, 0)),
            pl.BlockSpec((1, D_MODEL), lambda i, j: (0, 0)),
            pl.BlockSpec((D_MODEL, tf), lambda i, j: (0, j)),
            pl.BlockSpec((D_MODEL, tf), lambda i, j: (0, j)),
            pl.BlockSpec((tf, D_MODEL), lambda i, j: (j, 0)),
        ],
        out_specs=pl.BlockSpec((tm, D_MODEL), lambda i, j: (i, 0)),
        out_shape=jax.ShapeDtypeStruct((t, D_MODEL), F32),
        scratch_shapes=[pltpu.VMEM((tm, D_MODEL), BF16), pltpu.VMEM((tm, D_MODEL), F32)],
        compiler_params=_params("parallel", "arbitrary"),
        name="dense_ffn",
    )(h2, norm_g, w_gate, w_up, w_down)


def _router_kernel(h_ref, g_ref, wr_ref, z_ref, idx_ref, gate_ref):
    tm = h_ref.shape[0]
    x = h_ref[...]
    inv = lax.rsqrt(jnp.mean(x * x, axis=-1, keepdims=True) + RMS_EPS)
    z = x * inv * g_ref[...]
    for c in range(CHUNKS):
        z_ref[pl.ds(c, tm, stride=CHUNKS), :] = z[:, c * LANES:(c + 1) * LANES]
    zh, zm, zl = _split3(z)
    wh, wm, wl = wr_ref[0], wr_ref[1], wr_ref[2]
    logits = ((_dot(zh, wh) + (_dot(zh, wm) + _dot(zm, wh)))
              + ((_dot(zh, wl) + _dot(zl, wh)) + _dot(zm, wm)))
    lane = lax.broadcasted_iota(jnp.int32, (tm, LANES), 1)
    logits = jnp.where(lane < N_EXPERTS, logits, -jnp.inf)
    m1 = jnp.max(logits, axis=1, keepdims=True)
    i1 = jnp.min(jnp.where(logits == m1, lane, LANES), axis=1, keepdims=True)
    rest = jnp.where(lane == i1, -jnp.inf, logits)
    m2 = jnp.max(rest, axis=1, keepdims=True)
    i2 = jnp.min(jnp.where(rest == m2, lane, LANES), axis=1, keepdims=True)
    e2 = jnp.exp(m2 - m1)
    g1 = 1.0 / (1.0 + e2)
    idx_ref[...] = jnp.where(lane == 0, i1, i2)
    gate_ref[...] = jnp.where(lane == 0, g1, e2 * g1)


def _router(h2, norm_g, wr3):
    t = h2.shape[0]
    tm = ROW_TILE
    row = lambda i: (i, 0)
    return pl.pallas_call(
        _router_kernel,
        grid=(t // tm,),
        in_specs=[
            pl.BlockSpec((tm, D_MODEL), row),
            pl.BlockSpec((1, D_MODEL), lambda i: (0, 0)),
            pl.BlockSpec((3, D_MODEL, LANES), lambda i: (0, 0, 0)),
        ],
        out_specs=[pl.BlockSpec((tm * CHUNKS, LANES), row),
                   pl.BlockSpec((tm, LANES), row),
                   pl.BlockSpec((tm, LANES), row)],
        out_shape=[jax.ShapeDtypeStruct((t * CHUNKS, LANES), F32),
                   jax.ShapeDtypeStruct((t, LANES), jnp.int32),
                   jax.ShapeDtypeStruct((t, LANES), F32)],
        compiler_params=_params("parallel"),
        name="router",
    )(h2, norm_g, wr3)


def _row_copy(src, src_row, dst, dst_row, sem):
    return pltpu.make_async_copy(src.at[pl.ds(src_row * CHUNKS, CHUNKS), :],
                                 dst.at[pl.ds(dst_row * CHUNKS, CHUNKS), :], sem)


def _moe_kernel(be_ref, nv_ref, src_ref, dst_ref,
                z_hbm, gate_ref, wg_ref, wu_ref, wd_ref,
                y_hbm,
                xg, xb, acc, yc, gsem, ssem):
    i = pl.program_id(0)
    j = pl.program_id(1)
    tm = xb.shape[0]
    nv = nv_ref[i]
    base = i * tm

    @pl.when((i == 0) & (j == 0))
    def _():
        xg[...] = jnp.zeros_like(xg)

    @pl.when(nv > 0)
    def _():
        @pl.when(j == 0)
        def _():
            def issue(r, carry):
                _row_copy(z_hbm, src_ref[base + r], xg, r, gsem).start()
                return carry
            lax.fori_loop(0, nv, issue, 0)

            def wait(r, carry):
                _row_copy(z_hbm, 0, xg, 0, gsem).wait()
                return carry
            lax.fori_loop(0, nv, wait, 0)
            live = lax.broadcasted_iota(jnp.int32, (tm, LANES), 0) < nv
            for c in range(CHUNKS):
                xc = xg[pl.ds(c, tm, stride=CHUNKS), :]
                xb[:, c * LANES:(c + 1) * LANES] = jnp.where(live, xc, 0.0).astype(BF16)

        x = xb[...]
        a = (_silu(_dot(x, wg_ref[0])) * _dot(x, wu_ref[0])).astype(BF16)
        y = _dot(a, wd_ref[0])

        @pl.when(j == 0)
        def _():
            acc[...] = y

        @pl.when(j > 0)
        def _():
            acc[...] += y

        @pl.when(j == pl.num_programs(1) - 1)
        def _():
            out = acc[...] * gate_ref[...]
            for c in range(CHUNKS):
                yc[pl.ds(c, tm, stride=CHUNKS), :] = out[:, c * LANES:(c + 1) * LANES]

            def issue(r, carry):
                _row_copy(yc, r, y_hbm, dst_ref[base + r], ssem).start()
                return carry
            lax.fori_loop(0, nv, issue, 0)

            def wait(r, carry):
                _row_copy(yc, 0, y_hbm, 0, ssem).wait()
                return carry
            lax.fori_loop(0, nv, wait, 0)


def _moe_experts(z8, gates_p, blk_e, blk_nv, src_tok, dst_slot, w_gate, w_up, w_down, t):
    tm, tf = MOE_BLOCK, FF_TILE
    nblk = src_tok.shape[0] // tm
    nff = D_FF // tf
    ff = lambda i, j, nv: jnp.where(nv[i] > 0, j, nff - 1)
    grid_spec = pltpu.PrefetchScalarGridSpec(
        num_scalar_prefetch=4,
        grid=(nblk, nff),
        in_specs=[
            pl.BlockSpec(memory_space=pl.ANY),
            pl.BlockSpec((tm, 1), lambda i, j, be, nv, s, d: (i, 0)),
            pl.BlockSpec((1, D_MODEL, tf), lambda i, j, be, nv, s, d: (be[i], 0, ff(i, j, nv))),
            pl.BlockSpec((1, D_MODEL, tf), lambda i, j, be, nv, s, d: (be[i], 0, ff(i, j, nv))),
            pl.BlockSpec((1, tf, D_MODEL), lambda i, j, be, nv, s, d: (be[i], ff(i, j, nv), 0)),
        ],
        out_specs=pl.BlockSpec(memory_space=pl.ANY),
        scratch_shapes=[
            pltpu.VMEM((tm * CHUNKS, LANES), F32),
            pltpu.VMEM((tm, D_MODEL), BF16),
            pltpu.VMEM((tm, D_MODEL), F32),
            pltpu.VMEM((tm * CHUNKS, LANES), F32),
            pltpu.SemaphoreType.DMA(()),
            pltpu.SemaphoreType.DMA(()),
        ],
    )
    return pl.pallas_call(
        _moe_kernel,
        grid_spec=grid_spec,
        out_shape=jax.ShapeDtypeStruct((t * TOP_K * CHUNKS, LANES), F32),
        compiler_params=_params("arbitrary", "arbitrary"),
        name="moe_experts",
    )(blk_e, blk_nv, src_tok, dst_slot, z8, gates_p, w_gate, w_up, w_down)


def _combine_kernel(h_ref, y_ref, g_ref, o_ref, *, final_norm):
    tm = h_ref.shape[0]
    stride = TOP_K * CHUNKS
    parts = []
    ss = jnp.zeros((tm, 1), F32)
    for c in range(CHUNKS):
        y = y_ref[pl.ds(c, tm, stride=stride), :]
        for k in range(1, TOP_K):
            y = y + y_ref[pl.ds(k * CHUNKS + c, tm, stride=stride), :]
        hc = h_ref[:, c * LANES:(c + 1) * LANES] + y
        parts.append(hc)
        ss = ss + jnp.sum(hc * hc, axis=1, keepdims=True)
    inv = lax.rsqrt(ss / D_MODEL + RMS_EPS)
    for c in range(CHUNKS):
        cols = slice(c * LANES, (c + 1) * LANES)
        o_ref[:, cols] = parts[c] * inv * g_ref[:, cols] if final_norm else parts[c]


def _combine(h2, y8, norm_g, final_norm):
    t = h2.shape[0]
    tm = ROW_TILE
    row = lambda i: (i, 0)
    return pl.pallas_call(
        functools.partial(_combine_kernel, final_norm=final_norm),
        grid=(t // tm,),
        in_specs=[
            pl.BlockSpec((tm, D_MODEL), row),
            pl.BlockSpec((tm * TOP_K * CHUNKS, LANES), row),
            pl.BlockSpec((1, D_MODEL), lambda i: (0, 0)),
        ],
        out_specs=pl.BlockSpec((tm, D_MODEL), row),
        out_shape=jax.ShapeDtypeStruct((t, D_MODEL), F32),
        compiler_params=_params("parallel"),
        name="moe_combine",
    )(h2, y8, norm_g)


def _norm_kernel(h_ref, g_ref, o_ref):
    x = h_ref[...]
    o_ref[...] = x * lax.rsqrt(jnp.mean(x * x, axis=-1, keepdims=True) + RMS_EPS) * g_ref[...]


def _final_norm(h2, norm_g):
    t = h2.shape[0]
    tm = ROW_TILE
    return pl.pallas_call(
        _norm_kernel,
        grid=(t // tm,),
        in_specs=[pl.BlockSpec((tm, D_MODEL), lambda i: (i, 0)),
                  pl.BlockSpec((1, D_MODEL), lambda i: (0, 0))],
        out_specs=pl.BlockSpec((tm, D_MODEL), lambda i: (i, 0)),
        out_shape=jax.ShapeDtypeStruct((t, D_MODEL), F32),
        compiler_params=_params("parallel"),
        name="final_norm",
    )(h2, norm_g)


def _dispatch_plan(top_i, gates, t):
    n = t * TOP_K
    tm = MOE_BLOCK
    e_flat = top_i.reshape(n)
    onehot = (e_flat[:, None] == jnp.arange(N_EXPERTS)[None, :]).astype(jnp.int32)
    counts = jnp.sum(onehot, axis=0)
    rank = jnp.sum((jnp.cumsum(onehot, axis=0) - onehot) * onehot, axis=1)
    padded = (counts + tm - 1) // tm * tm
    pends = jnp.cumsum(padded)
    pstarts = pends - padded
    dest = pstarts[e_flat] + rank
    p_rows = -(-n // tm) * tm + N_EXPERTS * tm
    nblk = p_rows // tm
    flat = jnp.arange(n, dtype=jnp.int32)
    src_tok = jnp.zeros((p_rows,), jnp.int32).at[dest].set(flat // TOP_K)
    dst_slot = jnp.zeros((p_rows,), jnp.int32).at[dest].set(flat)
    gates_p = jnp.zeros((p_rows,), F32).at[dest].set(gates.reshape(n))
    blk_start = jnp.arange(nblk, dtype=jnp.int32) * tm
    blk_e = jnp.minimum(jnp.sum(blk_start[:, None] >= pends[None, :], axis=1), N_EXPERTS - 1)
    blk_nv = jnp.clip(pstarts[blk_e] + counts[blk_e] - blk_start, 0, tm)
    blk_nv = jnp.where(blk_start < pends[-1], blk_nv, 0)
    last_used = jnp.maximum(pends[-1] // tm - 1, 0)
    blk_e = jnp.where(blk_start < pends[-1], blk_e, blk_e[last_used])
    return (blk_e.astype(jnp.int32), blk_nv.astype(jnp.int32), src_tok, dst_slot,
            gates_p.reshape(p_rows, 1))


def _rope_tables(seq):
    inv = 1.0 / (ROPE_THETA ** (jnp.arange(0, HEAD_DIM, 2, dtype=F32) / HEAD_DIM))
    ang = jnp.arange(seq, dtype=F32)[:, None] * inv[None, :]
    cos = jnp.concatenate([jnp.cos(ang)] * 4, axis=-1)
    sin = jnp.concatenate([jnp.sin(ang)] * 4, axis=-1)
    upper = (jnp.arange(LANES) % HEAD_DIM) >= HEAD_DIM // 2
    sa = jnp.where(upper[None, :], sin, 0.0)
    sb = jnp.where(upper[None, :], 0.0, -sin)
    return cos, sa, sb


def _in_proj_columns():
    split = np.cumsum([0, SWA_WIDTH, SWA_KV_WIDTH, SWA_KV_WIDTH, FOX_WIDTH, FOX_WIDTH, FOX_WIDTH,
                       FOX_HEADS, MOBA_WIDTH, MOBA_WIDTH, MOBA_WIDTH])
    aq, ak, av, bq, bk, bv, bf, cq, ck, cv = split[:-1]
    half = SWA_Q_HEADS // 2
    cols = []
    for j in range(half):
        cols += list(range(aq + j * HEAD_DIM, aq + (j + 1) * HEAD_DIM))
        cols += list(range(aq + (j + half) * HEAD_DIM, aq + (j + half + 1) * HEAD_DIM))
    for start, width in ((ak, SWA_KV_WIDTH), (av, SWA_KV_WIDTH), (bq, FOX_WIDTH), (bk, FOX_WIDTH),
                         (bv, FOX_WIDTH), (cq, MOBA_WIDTH), (ck, MOBA_WIDTH), (cv, MOBA_WIDTH)):
        cols += list(range(start, start + width))
    forget = list(range(bf, bf + FOX_HEADS))
    return np.array(cols, np.int32), np.array(forget, np.int32)


def _swa_out_perm():
    half = SWA_Q_HEADS // 2
    perm = []
    for j in range(half):
        perm += list(range(j * HEAD_DIM, (j + 1) * HEAD_DIM))
        perm += list(range((j + half) * HEAD_DIM, (j + half + 1) * HEAD_DIM))
    return np.array(perm + list(range(SWA_WIDTH, MIX_WIDTH)), np.int32)


def _mixer(h2, b, s, norm_g, w_in, forget_bias, sinks, mix_gain, w_out, tables):
    cols, fcols = _in_proj_columns()
    w_all = jnp.concatenate(
        [w_in[:, cols], jnp.pad(w_in[:, fcols], ((0, 0), (0, LANES - FOX_HEADS)))], axis=1
    ).astype(BF16)
    fbias = jnp.pad(forget_bias.astype(F32), (0, LANES - FOX_HEADS)).reshape(1, LANES)
    u, lf = _project(h2, norm_g.reshape(1, D_MODEL), w_all, *tables, fbias, s)
    u3 = u.reshape(b, s, U_WIDTH)
    oa = _swa(u3, sinks.astype(F32))
    ob = _fox(u3, lf.reshape(b, s, LANES))
    oc = _moba(u3)
    perm = _swa_out_perm()
    t = b * s
    return _outproj(oa.reshape(t, SWA_WIDTH), ob.reshape(t, FOX_WIDTH), oc.reshape(t, MOBA_WIDTH),
                    mix_gain[perm].reshape(1, MIX_WIDTH).astype(F32),
                    w_out[perm].astype(BF16), h2)


def _moe_ffn(h2, norm_g, w_router, w_gate, w_up, w_down, out_g):
    t = h2.shape[0]
    wr = jnp.pad(w_router.astype(F32), ((0, 0), (0, LANES - N_EXPERTS)))
    wr3 = jnp.stack(_split3(wr))
    z8, top, gate = _router(h2, norm_g.reshape(1, D_MODEL), wr3)
    plan = _dispatch_plan(top[:, :TOP_K], gate[:, :TOP_K], t)
    blk_e, blk_nv, src_tok, dst_slot, gates_p = plan
    y8 = _moe_experts(z8, gates_p, blk_e, blk_nv, src_tok, dst_slot,
                      w_gate.astype(BF16), w_up.astype(BF16), w_down.astype(BF16), t)
    if out_g is None:
        return _combine(h2, y8, norm_g.reshape(1, D_MODEL), False)
    return _combine(h2, y8, out_g.reshape(1, D_MODEL), True)


def kernel(x, attn_norm, w_in, fox_forget_bias, swa_sinks, mix_gain, w_out, ffn_norm,
           dense_w_gate, dense_w_up, dense_w_down, router_w, moe_w_gate, moe_w_up,
           moe_w_down, final_norm):
    b, s, d = x.shape
    depth = attn_norm.shape[0]
    assert d == D_MODEL and s % MOBA_BLOCK == 0 and s % ROW_TILE == 0
    tables = _rope_tables(s)
    h = x.reshape(b * s, d)
    normed = False
    for layer in range(depth):
        h = _mixer(h, b, s, attn_norm[layer], w_in[layer], fox_forget_bias[layer],
                   swa_sinks[layer], mix_gain[layer], w_out[layer], tables)
        j = layer // 2
        last = layer == depth - 1
        if layer % 2 == 0:
            h = _dense_ffn(h, ffn_norm[layer].reshape(1, d), dense_w_gate[j].astype(BF16),
                           dense_w_up[j].astype(BF16), dense_w_down[j].astype(BF16))
        else:
            h = _moe_ffn(h, ffn_norm[layer], router_w[j], moe_w_gate[j], moe_w_up[j],
                         moe_w_down[j], final_norm if last else None)
            normed = last
    if not normed:
        h = _final_norm(h, final_norm.reshape(1, d))
    return h.reshape(b, s, d)
```

```python
import functools

import jax
import jax.numpy as jnp
import numpy as np
from jax import lax
from jax.experimental import pallas as pl
from jax.experimental.pallas import tpu as pltpu

F32 = jnp.float32
BF16 = jnp.bfloat16

D_MODEL = 1024
HEAD_DIM = 64
LANES = 128
SWA_Q_HEADS = 8
SWA_KV_HEADS = 2
SWA_WINDOW = 128
FOX_HEADS = 4
MOBA_HEADS = 4
MOBA_BLOCK = 256
MOBA_TOPK = 3
ROPE_THETA = 10000.0
RMS_EPS = 1e-5
D_FF = 3584
N_EXPERTS = 8
TOP_K = 2
MOE_BLOCK = 512
NEG_INF = -1e30
ATTN_SCALE = HEAD_DIM ** -0.5

SWA_WIDTH = SWA_Q_HEADS * HEAD_DIM
SWA_KV_WIDTH = SWA_KV_HEADS * HEAD_DIM
FOX_WIDTH = FOX_HEADS * HEAD_DIM
MOBA_WIDTH = MOBA_HEADS * HEAD_DIM
MIX_WIDTH = SWA_WIDTH + FOX_WIDTH + MOBA_WIDTH

U_BLOCKS = 18
U_WIDTH = U_BLOCKS * LANES
ROPE_BLOCKS = (0, 1, 2, 3, 4, 12, 13, 14, 15)
Q_BLOCKS = (0, 1, 2, 3, 6, 7, 12, 13)
W_ALL_WIDTH = U_WIDTH + LANES

VMEM_LIMIT = 56 * 1024 * 1024

ROW_TILE = 512
ATT_TILE = 256
FF_TILE = 512
MOE_FF_TILE = 1792
CHUNKS = D_MODEL // LANES
MOE_ISSUE_UNROLL = 8
MOE_WAVE = 512


def _params(*sem):
    return pltpu.CompilerParams(dimension_semantics=sem, vmem_limit_bytes=VMEM_LIMIT)


def _split3(x):
    hi = x.astype(BF16)
    r1 = x - hi.astype(F32)
    mid = r1.astype(BF16)
    lo = (r1 - mid.astype(F32)).astype(BF16)
    return hi, mid, lo


def _dot_nt(a, b):
    return lax.dot_general(a, b, (((1,), (1,)), ((), ())), preferred_element_type=F32)


def _dot(a, b):
    return jnp.dot(a, b, preferred_element_type=F32)


def _proj_kernel(x_ref, g_ref, w_ref, cos_ref, sa_ref, sb_ref, fb_ref, u_ref, lf_ref):
    x = x_ref[...]
    inv = lax.rsqrt(jnp.mean(x * x, axis=-1, keepdims=True) + RMS_EPS)
    h = (x * inv * g_ref[...]).astype(BF16)
    cos = cos_ref[...]
    sa = sa_ref[...]
    sb = sb_ref[...]
    for c in range(U_BLOCKS // 2):
        acc = _dot(h, w_ref[:, c * 2 * LANES:(c + 1) * 2 * LANES])
        for half in range(2):
            blk = 2 * c + half
            a = acc[:, half * LANES:(half + 1) * LANES]
            if blk in ROPE_BLOCKS:
                a = a * cos + pltpu.roll(a, 32, 1) * sa + pltpu.roll(a, 96, 1) * sb
            if blk in Q_BLOCKS:
                a = a * ATTN_SCALE
            u_ref[:, blk * LANES:(blk + 1) * LANES] = a.astype(BF16)
    f = _dot(h, w_ref[:, U_WIDTH:W_ALL_WIDTH]) + fb_ref[...]
    lf_ref[...] = jnp.minimum(f, 0.0) - jnp.log(1.0 + jnp.exp(-jnp.abs(f)))


def _project(x2, norm_g, w_all, cos, sa, sb, fbias, seq):
    t = x2.shape[0]
    tm = ROW_TILE
    nseq = seq // tm
    row = lambda i: (i, 0)
    pos = lambda i: (i % nseq, 0)
    fixed = lambda i: (0, 0)
    return pl.pallas_call(
        _proj_kernel,
        grid=(t // tm,),
        in_specs=[
            pl.BlockSpec((tm, D_MODEL), row),
            pl.BlockSpec((1, D_MODEL), fixed),
            pl.BlockSpec((D_MODEL, W_ALL_WIDTH), fixed),
            pl.BlockSpec((tm, LANES), pos),
            pl.BlockSpec((tm, LANES), pos),
            pl.BlockSpec((tm, LANES), pos),
            pl.BlockSpec((1, LANES), fixed),
        ],
        out_specs=[pl.BlockSpec((tm, U_WIDTH), row), pl.BlockSpec((tm, LANES), row)],
        out_shape=[jax.ShapeDtypeStruct((t, U_WIDTH), BF16),
                   jax.ShapeDtypeStruct((t, LANES), F32)],
        compiler_params=_params("parallel"),
        name="proj",
    )(x2, norm_g, w_all, cos, sa, sb, fbias)


def _swa_kernel(sink_ref, q_ref, kc_ref, kp_ref, vc_ref, vp_ref, o_ref):
    n = pl.program_id(1)
    w = SWA_WINDOW
    group = SWA_Q_HEADS // SWA_KV_HEADS
    lane = lax.broadcasted_iota(jnp.int32, (1, LANES), 1)
    k2 = jnp.concatenate([kp_ref[0], kc_ref[0]], axis=0)
    v2 = jnp.concatenate([vp_ref[0], vc_ref[0]], axis=0)
    qi = lax.broadcasted_iota(jnp.int32, (group * w, 2 * w), 0) % w
    kj = lax.broadcasted_iota(jnp.int32, (group * w, 2 * w), 1)
    valid = (kj > qi) & (kj <= qi + w) & ((kj >= w) | (n > 0))
    outs = []
    for g in range(SWA_KV_HEADS):
        keep = (lane < HEAD_DIM) if g == 0 else (lane >= HEAD_DIM)
        qs = jnp.concatenate(
            [jnp.where(keep, q_ref[0, :, j * LANES:(j + 1) * LANES], 0) for j in range(group)],
            axis=0)
        s = jnp.where(valid, _dot_nt(qs, k2), NEG_INF)
        ps, inv = [], []
        for j in range(group):
            sj = s[j * w:(j + 1) * w]
            sink = sink_ref[g * group + j]
            m = jnp.maximum(jnp.max(sj, axis=1, keepdims=True), sink)
            p = jnp.exp(sj - m)
            inv.append(1.0 / (jnp.sum(p, axis=1, keepdims=True) + jnp.exp(sink - m)))
            ps.append(p.astype(BF16))
        o = _dot(jnp.concatenate(ps, axis=0), v2)
        outs.append([o[j * w:(j + 1) * w] * inv[j] for j in range(group)])
    for j in range(group):
        oj = jnp.where(lane < HEAD_DIM, outs[0][j], outs[1][j])
        o_ref[0, :, j * LANES:(j + 1) * LANES] = oj.astype(BF16)


def _swa(u3, sinks):
    b, s, _ = u3.shape
    w = SWA_WINDOW
    cur = lambda blk: (lambda bi, n, sk: (bi, n, blk))
    prev = lambda blk: (lambda bi, n, sk: (bi, jnp.maximum(n - 1, 0), blk))
    grid_spec = pltpu.PrefetchScalarGridSpec(
        num_scalar_prefetch=1,
        grid=(b, s // w),
        in_specs=[
            pl.BlockSpec((1, w, SWA_WIDTH), lambda bi, n, sk: (bi, n, 0)),
            pl.BlockSpec((1, w, LANES), cur(4)),
            pl.BlockSpec((1, w, LANES), prev(4)),
            pl.BlockSpec((1, w, LANES), cur(5)),
            pl.BlockSpec((1, w, LANES), prev(5)),
        ],
        out_specs=pl.BlockSpec((1, w, SWA_WIDTH), lambda bi, n, sk: (bi, n, 0)),
    )
    return pl.pallas_call(
        _swa_kernel,
        grid_spec=grid_spec,
        out_shape=jax.ShapeDtypeStruct((b, s, SWA_WIDTH), BF16),
        compiler_params=_params("parallel", "parallel"),
        name="swa",
    )(sinks, u3, u3, u3, u3, u3)


def _head_lanes(h):
    lane = lax.broadcasted_iota(jnp.int32, (1, LANES), 1)
    half = h % 2
    return (lane // HEAD_DIM) == half, lane - (1 - half) * HEAD_DIM


def _causal_attention(h, qp_scr, kp_scr, v_ref, o_ref, t):
    s_len = qp_scr.shape[0]
    lane = lax.broadcasted_iota(jnp.int32, (1, LANES), 1)
    row = lax.broadcasted_iota(jnp.int32, (t, t), 0)
    col = lax.broadcasted_iota(jnp.int32, (t, t), 1)
    for qi in range(s_len // t):
        rows = slice(qi * t, (qi + 1) * t)
        q = qp_scr[rows, :]
        sd = jnp.where(col <= row, _dot_nt(q, kp_scr[rows, :]), NEG_INF)
        m = jnp.max(sd, axis=1, keepdims=True)
        if qi:
            sp = _dot_nt(q, kp_scr[:qi * t, :])
            m = jnp.maximum(m, jnp.max(sp, axis=1, keepdims=True))
        pd = jnp.exp(sd - m)
        l = jnp.sum(pd, axis=1, keepdims=True)
        o = _dot(pd.astype(BF16), v_ref[0, rows, :])
        if qi:
            pp = jnp.exp(sp - m)
            l = l + jnp.sum(pp, axis=1, keepdims=True)
            o = o + _dot(pp.astype(BF16), v_ref[0, :qi * t, :])
        o = (o * (1.0 / l)).astype(BF16)

        @pl.when(h % 2 == 0)
        def _():
            o_ref[0, rows, :] = o

        @pl.when(h % 2 == 1)
        def _():
            o_ref[0, rows, :] = jnp.where(lane < HEAD_DIM, o_ref[0, rows, :], o)


def _head_attention(kernel_fn, u3, extra, q_blk, k_blk, v_blk, n_heads, name):
    b, s, _ = u3.shape
    pair = lambda blk: (lambda bi, h: (bi, 0, blk + h // 2))
    seq_block = pl.BlockSpec((1, s, LANES), lambda bi, h: (bi, 0, 0))
    return pl.pallas_call(
        kernel_fn,
        grid=(b, n_heads),
        in_specs=[pl.BlockSpec((1, s, LANES), pair(q_blk)),
                  pl.BlockSpec((1, s, LANES), pair(k_blk)),
                  pl.BlockSpec((1, s, LANES), pair(v_blk))] + [seq_block] * len(extra),
        out_specs=pl.BlockSpec((1, s, LANES), pair(0)),
        out_shape=jax.ShapeDtypeStruct((b, s, n_heads * HEAD_DIM), BF16),
        scratch_shapes=[pltpu.VMEM((s, LANES), BF16), pltpu.VMEM((s, LANES), BF16)],
        compiler_params=_params("parallel", "arbitrary"),
        name=name,
    )(u3, u3, u3, *extra)


def _fox_kernel(q_ref, k_ref, v_ref, lf_ref, o_ref, qp_scr, kp_scr):
    h = pl.program_id(1)
    s_len = q_ref.shape[1]
    t = ATT_TILE
    lane = lax.broadcasted_iota(jnp.int32, (1, LANES), 1)
    keep, f = _head_lanes(h)
    r = lax.broadcasted_iota(jnp.int32, (t, t), 0)
    cc = lax.broadcasted_iota(jnp.int32, (t, t), 1)
    tri = jnp.where(cc <= r, 1.0, 0.0).astype(BF16)
    ones_q = jnp.where((f >= 3) & (f < 6), 1.0, 0.0).astype(BF16)
    ones_k = jnp.where((f >= 0) & (f < 3), 1.0, 0.0).astype(BF16)
    carry = jnp.zeros((1, LANES), F32)
    for i in range(s_len // t):
        rows = slice(i * t, (i + 1) * t)
        lf = jnp.sum(jnp.where(lane == h, lf_ref[0, rows, :], 0.0), axis=1, keepdims=True)
        hi, mid, lo = _split3(jnp.broadcast_to(lf, (t, LANES)))
        c = (_dot(tri, hi) + _dot(tri, mid)) + _dot(tri, lo) + carry
        carry = c[t - 1:t, :]
        chi, cmid, clo = _split3(c)
        qf = jnp.where(f == 0, chi, jnp.where(f == 1, cmid, jnp.where(f == 2, clo, ones_q)))
        kf = jnp.where(f == 3, -chi, jnp.where(f == 4, -cmid, jnp.where(f == 5, -clo, ones_k)))
        qp_scr[rows, :] = jnp.where(keep, q_ref[0, rows, :], qf)
        kp_scr[rows, :] = jnp.where(keep, k_ref[0, rows, :], kf)
    _causal_attention(h, qp_scr, kp_scr, v_ref, o_ref, t)


def _fox(u3, lf3):
    return _head_attention(_fox_kernel, u3, (lf3,), 6, 8, 10, FOX_HEADS, "fox")


def _moba_kernel(q_ref, k_ref, v_ref, o_ref, qp_scr, kp_scr):
    h = pl.program_id(1)
    s_len = q_ref.shape[1]
    t = MOBA_BLOCK
    nkb = s_len // t
    sub = 8
    assert nkb <= sub
    keep, f = _head_lanes(h)
    means = [jnp.mean(jnp.where(keep, k_ref[0, j * t:(j + 1) * t, :], 0).astype(F32),
                      axis=0, keepdims=True) for j in range(nkb)]
    km = _split3(jnp.concatenate(means + [jnp.zeros((LANES - nkb, LANES), F32)], axis=0))
    blk = lax.broadcasted_iota(jnp.int32, (sub, t), 0)
    for i in range(nkb):
        rows = slice(i * t, (i + 1) * t)
        q = q_ref[0, rows, :]
        g = ((_dot_nt(km[0], q) + _dot_nt(km[1], q)) + _dot_nt(km[2], q))[:sub, :]
        past = blk < i
        g = jnp.where(past, g, NEG_INF)
        rank = jnp.zeros((sub, t), jnp.int32)
        for r in range(i):
            gr = jnp.broadcast_to(g[r:r + 1, :], (sub, t))
            beats = (gr > g) | ((gr == g) & (r < blk))
            rank = rank + beats.astype(jnp.int32)
        drop = jnp.where(past & (rank >= MOBA_TOPK), 1.0, 0.0)
        lo = jnp.concatenate([drop, jnp.zeros((LANES - sub, t), F32)], axis=0)
        hi = jnp.concatenate([jnp.zeros((HEAD_DIM, t), F32), drop,
                              jnp.zeros((HEAD_DIM - sub, t), F32)], axis=0)
        qf = jnp.where(h % 2 == 0, hi, lo).T.astype(BF16)
        kf = jnp.where(f == i, NEG_INF, 0.0).astype(BF16)
        qp_scr[rows, :] = jnp.where(keep, q, qf)
        kp_scr[rows, :] = jnp.where(keep, k_ref[0, rows, :], kf)
    _causal_attention(h, qp_scr, kp_scr, v_ref, o_ref, t)


def _moba(u3):
    return _head_attention(_moba_kernel, u3, (), 12, 14, 16, MOBA_HEADS, "moba")


def _rms_f32(x):
    x = x.astype(F32)
    return x * lax.rsqrt(jnp.mean(x * x, axis=-1, keepdims=True) + RMS_EPS)


def _outproj_kernel(oa_ref, ob_ref, oc_ref, gain_ref, w_ref, x_ref, h_ref):
    a0, a1, a2 = 0, SWA_WIDTH, SWA_WIDTH + FOX_WIDTH
    ya = (_rms_f32(oa_ref[...]) * gain_ref[:, a0:a1]).astype(BF16)
    yb = (_rms_f32(ob_ref[...]) * gain_ref[:, a1:a2]).astype(BF16)
    yc = (_rms_f32(oc_ref[...]) * gain_ref[:, a2:]).astype(BF16)
    y = (_dot(ya, w_ref[a0:a1, :]) + _dot(yb, w_ref[a1:a2, :])) + _dot(yc, w_ref[a2:, :])
    h_ref[...] = x_ref[...] + y


def _outproj(oa, ob, oc, gain, w_out, x2):
    t = x2.shape[0]
    tm = ROW_TILE
    row = lambda i: (i, 0)
    fixed = lambda i: (0, 0)
    return pl.pallas_call(
        _outproj_kernel,
        grid=(t // tm,),
        in_specs=[
            pl.BlockSpec((tm, SWA_WIDTH), row),
            pl.BlockSpec((tm, FOX_WIDTH), row),
            pl.BlockSpec((tm, MOBA_WIDTH), row),
            pl.BlockSpec((1, MIX_WIDTH), fixed),
            pl.BlockSpec((MIX_WIDTH, D_MODEL), fixed),
            pl.BlockSpec((tm, D_MODEL), row),
        ],
        out_specs=pl.BlockSpec((tm, D_MODEL), row),
        out_shape=jax.ShapeDtypeStruct((t, D_MODEL), F32),
        compiler_params=_params("parallel"),
        name="outproj",
    )(oa, ob, oc, gain, w_out, x2)


def _silu(x):
    return x / (1.0 + jnp.exp(-x))


def _ffn_kernel(h_ref, g_ref, wg_ref, wu_ref, wd_ref, o_ref, z_scr, acc_scr):
    j = pl.program_id(1)

    @pl.when(j == 0)
    def _():
        x = h_ref[...]
        inv = lax.rsqrt(jnp.mean(x * x, axis=-1, keepdims=True) + RMS_EPS)
        z_scr[...] = (x * inv * g_ref[...]).astype(BF16)

    z = z_scr[...]
    a = (_silu(_dot(z, wg_ref[...].astype(BF16))) * _dot(z, wu_ref[...].astype(BF16))).astype(BF16)
    y = _dot(a, wd_ref[...].astype(BF16))

    @pl.when(j == 0)
    def _():
        acc_scr[...] = y

    @pl.when(j > 0)
    def _():
        acc_scr[...] += y

    @pl.when(j == pl.num_programs(1) - 1)
    def _():
        o_ref[...] = h_ref[...] + acc_scr[...]


def _dense_ffn(h2, norm_g, w_gate, w_up, w_down):
    t = h2.shape[0]
    tm, tf = 2 * ROW_TILE, FF_TILE
    return pl.pallas_call(
        _ffn_kernel,
        grid=(t // tm, D_FF // tf),
        in_specs=[
            pl.BlockSpec((tm, D_MODEL), lambda i, j: (i, 0)),
            pl.BlockSpec((1, D_MODEL), lambda i, j: (0, 0)),
            pl.BlockSpec((D_MODEL, tf), lambda i, j: (0, j)),
            pl.BlockSpec((D_MODEL, tf), lambda i, j: (0, j)),
            pl.BlockSpec((tf, D_MODEL), lambda i, j: (j, 0)),
        ],
        out_specs=pl.BlockSpec((tm, D_MODEL), lambda i, j: (i, 0)),
        out_shape=jax.ShapeDtypeStruct((t, D_MODEL), F32),
        scratch_shapes=[pltpu.VMEM((tm, D_MODEL), BF16), pltpu.VMEM((tm, D_MODEL), F32)],
        compiler_params=_params("parallel", "arbitrary"),
        name="dense_ffn",
    )(h2, norm_g, w_gate, w_up, w_down)


def _router_kernel(h_ref, g_ref, wr_ref, z_ref, idx_ref, gate_ref):
    tm = h_ref.shape[0]
    x = h_ref[...]
    inv = lax.rsqrt(jnp.mean(x * x, axis=-1, keepdims=True) + RMS_EPS)
    z = x * inv * g_ref[...]
    for c in range(CHUNKS):
        z_ref[pl.ds(c, tm, stride=CHUNKS), :] = z[:, c * LANES:(c + 1) * LANES]
    zh, zm, zl = _split3(z)
    wh, wm, wl = wr_ref[0], wr_ref[1], wr_ref[2]
    logits = ((_dot(zh, wh) + (_dot(zh, wm) + _dot(zm, wh)))
              + ((_dot(zh, wl) + _dot(zl, wh)) + _dot(zm, wm)))
    lane = lax.broadcasted_iota(jnp.int32, (tm, LANES), 1)
    logits = jnp.where(lane < N_EXPERTS, logits, -jnp.inf)
    m1 = jnp.max(logits, axis=1, keepdims=True)
    i1 = jnp.min(jnp.where(logits == m1, lane, LANES), axis=1, keepdims=True)
    rest = jnp.where(lane == i1, -jnp.inf, logits)
    m2 = jnp.max(rest, axis=1, keepdims=True)
    i2 = jnp.min(jnp.where(rest == m2, lane, LANES), axis=1, keepdims=True)
    e2 = jnp.exp(m2 - m1)
    g1 = 1.0 / (1.0 + e2)
    idx_ref[...] = jnp.where(lane == 0, i1, i2)
    gate_ref[...] = jnp.where(lane == 0, g1, e2 * g1)


def _router(h2, norm_g, wr3):
    t = h2.shape[0]
    tm = ROW_TILE
    row = lambda i: (i, 0)
    return pl.pallas_call(
        _router_kernel,
        grid=(t // tm,),
        in_specs=[
            pl.BlockSpec((tm, D_MODEL), row),
            pl.BlockSpec((1, D_MODEL), lambda i: (0, 0)),
            pl.BlockSpec((3, D_MODEL, LANES), lambda i: (0, 0, 0)),
        ],
        out_specs=[pl.BlockSpec((tm * CHUNKS, LANES), row),
                   pl.BlockSpec((tm, LANES), row),
                   pl.BlockSpec((tm, LANES), row)],
        out_shape=[jax.ShapeDtypeStruct((t * CHUNKS, LANES), F32),
                   jax.ShapeDtypeStruct((t, LANES), jnp.int32),
                   jax.ShapeDtypeStruct((t, LANES), F32)],
        compiler_params=_params("parallel"),
        name="router",
    )(h2, norm_g, wr3)


def _row_copy(src, src_row, dst, dst_row, sem):
    return pltpu.make_async_copy(src.at[pl.ds(src_row * CHUNKS, CHUNKS), :],
                                 dst.at[pl.ds(dst_row * CHUNKS, CHUNKS), :], sem)


def _rows_wait(src, dst, rows, sem):
    pltpu.make_async_copy(src.at[pl.ds(0, rows * CHUNKS), :],
                          dst.at[pl.ds(0, rows * CHUNKS), :], sem).wait()


def _dispatch_kernel(dest_ref, fill_ref, z_hbm, x_hbm, zero_scr, fsem, sem):
    tm = MOE_BLOCK
    zero_scr[...] = jnp.zeros_like(zero_scr)
    for e in range(fill_ref.shape[0]):
        @pl.when(fill_ref[e] >= 0)
        def _():
            start = pl.multiple_of(fill_ref[e] * CHUNKS, tm * CHUNKS)
            pltpu.make_async_copy(zero_scr, x_hbm.at[pl.ds(start, tm * CHUNKS), :], fsem).start()
    for e in range(fill_ref.shape[0]):
        @pl.when(fill_ref[e] >= 0)
        def _():
            pltpu.make_async_copy(zero_scr, x_hbm.at[pl.ds(0, tm * CHUNKS), :], fsem).wait()

    def wave(w, carry):
        def trip(it, c):
            for k in range(MOE_ISSUE_UNROLL):
                i = w * MOE_WAVE + it * MOE_ISSUE_UNROLL + k
                _row_copy(z_hbm, i // TOP_K, x_hbm, dest_ref[i], sem).start()
            return c
        lax.fori_loop(0, MOE_WAVE // MOE_ISSUE_UNROLL, trip, 0)

        @pl.when(w > 0)
        def _():
            _rows_wait(z_hbm, x_hbm, MOE_WAVE, sem)
        return carry

    lax.fori_loop(0, dest_ref.shape[0] // MOE_WAVE, wave, 0)
    _rows_wait(z_hbm, x_hbm, MOE_WAVE, sem)


def _dispatch(z8, dest, fill, p_rows):
    assert dest.shape[0] % MOE_WAVE == 0
    grid_spec = pltpu.PrefetchScalarGridSpec(
        num_scalar_prefetch=2,
        grid=(1,),
        in_specs=[pl.BlockSpec(memory_space=pl.ANY)],
        out_specs=pl.BlockSpec(memory_space=pl.ANY),
        scratch_shapes=[pltpu.VMEM((MOE_BLOCK * CHUNKS, LANES), F32),
                        pltpu.SemaphoreType.DMA(()), pltpu.SemaphoreType.DMA(())],
    )
    return pl.pallas_call(
        _dispatch_kernel,
        grid_spec=grid_spec,
        out_shape=jax.ShapeDtypeStruct((p_rows * CHUNKS, LANES), F32),
        compiler_params=_params("arbitrary"),
        name="moe_dispatch",
    )(dest, fill, z8)


def _moe_kernel(be_ref, live_ref, bx_ref, x_ref, wg_ref, wu_ref, wd_ref, y_ref, xb, acc):
    i = pl.program_id(0)
    j = pl.program_id(1)
    tm = xb.shape[0]

    @pl.when((live_ref[i] == 0) & (j == 0))
    def _():
        y_ref[...] = jnp.zeros_like(y_ref)

    @pl.when(live_ref[i] > 0)
    def _():
        @pl.when(j == 0)
        def _():
            for c in range(CHUNKS):
                xb[:, c * LANES:(c + 1) * LANES] = x_ref[pl.ds(c, tm, stride=CHUNKS), :].astype(BF16)

        x = xb[...]
        a = (_silu(_dot(x, wg_ref[0])) * _dot(x, wu_ref[0])).astype(BF16)
        y = _dot(a, wd_ref[0])

        @pl.when(j == 0)
        def _():
            acc[...] = y

        @pl.when(j > 0)
        def _():
            acc[...] += y

        @pl.when(j == pl.num_programs(1) - 1)
        def _():
            for c in range(CHUNKS):
                y_ref[pl.ds(c, tm, stride=CHUNKS), :] = acc[:, c * LANES:(c + 1) * LANES]


def _moe_experts(x8, blk_e, blk_live, blk_x, w_gate, w_up, w_down):
    tm, tf = MOE_BLOCK, MOE_FF_TILE
    nblk = blk_e.shape[0]
    nff = D_FF // tf
    ff = lambda i, j, live: jnp.where(live[i] > 0, j, nff - 1)
    grid_spec = pltpu.PrefetchScalarGridSpec(
        num_scalar_prefetch=3,
        grid=(nblk, nff),
        in_specs=[
            pl.BlockSpec((tm * CHUNKS, LANES), lambda i, j, be, lv, bx: (bx[i], 0)),
            pl.BlockSpec((1, D_MODEL, tf), lambda i, j, be, lv, bx: (be[i], 0, ff(i, j, lv))),
            pl.BlockSpec((1, D_MODEL, tf), lambda i, j, be, lv, bx: (be[i], 0, ff(i, j, lv))),
            pl.BlockSpec((1, tf, D_MODEL), lambda i, j, be, lv, bx: (be[i], ff(i, j, lv), 0)),
        ],
        out_specs=pl.BlockSpec((tm * CHUNKS, LANES), lambda i, j, be, lv, bx: (i, 0)),
        scratch_shapes=[pltpu.VMEM((tm, D_MODEL), BF16), pltpu.VMEM((tm, D_MODEL), F32)],
    )
    return pl.pallas_call(
        _moe_kernel,
        grid_spec=grid_spec,
        out_shape=jax.ShapeDtypeStruct(x8.shape, F32),
        compiler_params=_params("arbitrary", "arbitrary"),
        name="moe_experts",
    )(blk_e, blk_live, blk_x, x8, w_gate, w_up, w_down)


def _combine_kernel(dest_ref, h_ref, gate_ref, *rest, final_norm):
    if final_norm:
        g_ref, y_hbm, o_ref, ybuf, sem = rest
    else:
        y_hbm, o_ref, ybuf, sem = rest
    i = pl.program_id(0)
    tm = h_ref.shape[0]
    rows = tm * TOP_K
    stride = TOP_K * CHUNKS

    def pull(tile, slot):
        def trip(it, c):
            for k in range(MOE_ISSUE_UNROLL):
                r = it * MOE_ISSUE_UNROLL + k
                _row_copy(y_hbm, dest_ref[tile * rows + r], ybuf.at[slot], r, sem.at[slot]).start()
            return c
        lax.fori_loop(0, rows // MOE_ISSUE_UNROLL, trip, 0)

    @pl.when(i == 0)
    def _():
        pull(0, 0)

    @pl.when(i + 1 < pl.num_programs(0))
    def _():
        pull(i + 1, (i + 1) % 2)

    slot = i % 2
    yb = ybuf.at[slot]
    _rows_wait(y_hbm, yb, rows, sem.at[slot])
    gates = [gate_ref[:, k:k + 1] for k in range(TOP_K)]
    parts = []
    ss = jnp.zeros((tm, 1), F32)
    for c in range(CHUNKS):
        y = gates[0] * yb[pl.ds(c, tm, stride=stride), :]
        for k in range(1, TOP_K):
            y = y + gates[k] * yb[pl.ds(k * CHUNKS + c, tm, stride=stride), :]
        hc = h_ref[:, c * LANES:(c + 1) * LANES] + y
        parts.append(hc)
        ss = ss + jnp.sum(hc * hc, axis=1, keepdims=True)
    if final_norm:
        inv = lax.rsqrt(ss / D_MODEL + RMS_EPS)
    for c in range(CHUNKS):
        cols = slice(c * LANES, (c + 1) * LANES)
        o_ref[:, cols] = parts[c] * inv * g_ref[:, cols] if final_norm else parts[c]


def _combine(h2, y8, dest, gate, out_g):
    t = h2.shape[0]
    tm = ROW_TILE
    final_norm = out_g is not None
    row = lambda i, d: (i, 0)
    in_specs = [pl.BlockSpec((tm, D_MODEL), row), pl.BlockSpec((tm, LANES), row)]
    args = [h2, gate]
    if final_norm:
        in_specs.append(pl.BlockSpec((1, D_MODEL), lambda i, d: (0, 0)))
        args.append(out_g.reshape(1, D_MODEL))
    grid_spec = pltpu.PrefetchScalarGridSpec(
        num_scalar_prefetch=1,
        grid=(t // tm,),
        in_specs=in_specs + [pl.BlockSpec(memory_space=pl.ANY)],
        out_specs=pl.BlockSpec((tm, D_MODEL), row),
        scratch_shapes=[pltpu.VMEM((2, tm * TOP_K * CHUNKS, LANES), F32),
                        pltpu.SemaphoreType.DMA((2,))],
    )
    return pl.pallas_call(
        functools.partial(_combine_kernel, final_norm=final_norm),
        grid_spec=grid_spec,
        out_shape=jax.ShapeDtypeStruct((t, D_MODEL), F32),
        compiler_params=_params("arbitrary"),
        name="moe_combine",
    )(dest, *args, y8)


def _norm_kernel(h_ref, g_ref, o_ref):
    x = h_ref[...]
    o_ref[...] = x * lax.rsqrt(jnp.mean(x * x, axis=-1, keepdims=True) + RMS_EPS) * g_ref[...]


def _final_norm(h2, norm_g):
    t = h2.shape[0]
    tm = ROW_TILE
    return pl.pallas_call(
        _norm_kernel,
        grid=(t // tm,),
        in_specs=[pl.BlockSpec((tm, D_MODEL), lambda i: (i, 0)),
                  pl.BlockSpec((1, D_MODEL), lambda i: (0, 0))],
        out_specs=pl.BlockSpec((tm, D_MODEL), lambda i: (i, 0)),
        out_shape=jax.ShapeDtypeStruct((t, D_MODEL), F32),
        compiler_params=_params("parallel"),
        name="final_norm",
    )(h2, norm_g)


def _dispatch_plan(top_i, t):
    n = t * TOP_K
    tm = MOE_BLOCK
    e_flat = top_i.reshape(n)
    onehot = (e_flat[:, None] == jnp.arange(N_EXPERTS)[None, :]).astype(jnp.int32)
    counts = jnp.sum(onehot, axis=0)
    rank = jnp.sum((jnp.cumsum(onehot, axis=0) - onehot) * onehot, axis=1)
    padded = (counts + tm - 1) // tm * tm
    pends = jnp.cumsum(padded)
    pstarts = pends - padded
    dest = (pstarts[e_flat] + rank).astype(jnp.int32)
    p_rows = -(-n // tm) * tm + N_EXPERTS * tm
    nblk = p_rows // tm
    blk = jnp.arange(nblk, dtype=jnp.int32)
    blk_e = jnp.minimum(jnp.sum((blk * tm)[:, None] >= pends[None, :], axis=1), N_EXPERTS - 1)
    live = blk * tm < pends[-1]
    last_live = jnp.maximum(pends[-1] // tm - 1, 0)
    blk_e = jnp.where(live, blk_e, blk_e[last_live])
    blk_x = jnp.where(live, blk, 0)
    tail = jnp.where(padded > 0, pends - tm, -1)
    spare = pends[-1] + jnp.arange(N_EXPERTS) * tm
    fill = jnp.concatenate([tail, jnp.where(spare < p_rows, spare, -1)])
    i32 = lambda a: a.astype(jnp.int32)
    return dest, i32(fill), i32(blk_e), i32(live), i32(blk_x), p_rows


def _rope_tables(seq):
    inv = 1.0 / (ROPE_THETA ** (jnp.arange(0, HEAD_DIM, 2, dtype=F32) / HEAD_DIM))
    ang = jnp.arange(seq, dtype=F32)[:, None] * inv[None, :]
    cos = jnp.concatenate([jnp.cos(ang)] * 4, axis=-1)
    sin = jnp.concatenate([jnp.sin(ang)] * 4, axis=-1)
    upper = (jnp.arange(LANES) % HEAD_DIM) >= HEAD_DIM // 2
    sa = jnp.where(upper[None, :], sin, 0.0)
    sb = jnp.where(upper[None, :], 0.0, -sin)
    return cos, sa, sb


def _in_proj_columns():
    split = np.cumsum([0, SWA_WIDTH, SWA_KV_WIDTH, SWA_KV_WIDTH, FOX_WIDTH, FOX_WIDTH, FOX_WIDTH,
                       FOX_HEADS, MOBA_WIDTH, MOBA_WIDTH, MOBA_WIDTH])
    aq, ak, av, bq, bk, bv, bf, cq, ck, cv = split[:-1]
    half = SWA_Q_HEADS // 2
    cols = []
    for j in range(half):
        cols += list(range(aq + j * HEAD_DIM, aq + (j + 1) * HEAD_DIM))
        cols += list(range(aq + (j + half) * HEAD_DIM, aq + (j + half + 1) * HEAD_DIM))
    for start, width in ((ak, SWA_KV_WIDTH), (av, SWA_KV_WIDTH), (bq, FOX_WIDTH), (bk, FOX_WIDTH),
                         (bv, FOX_WIDTH), (cq, MOBA_WIDTH), (ck, MOBA_WIDTH), (cv, MOBA_WIDTH)):
        cols += list(range(start, start + width))
    forget = list(range(bf, bf + FOX_HEADS))
    return np.array(cols, np.int32), np.array(forget, np.int32)


def _swa_out_perm():
    half = SWA_Q_HEADS // 2
    perm = []
    for j in range(half):
        perm += list(range(j * HEAD_DIM, (j + 1) * HEAD_DIM))
        perm += list(range((j + half) * HEAD_DIM, (j + half + 1) * HEAD_DIM))
    return np.array(perm + list(range(SWA_WIDTH, MIX_WIDTH)), np.int32)


def _mixer(h2, b, s, norm_g, w_in, forget_bias, sinks, mix_gain, w_out, tables):
    cols, fcols = _in_proj_columns()
    w_all = jnp.concatenate(
        [w_in[:, cols], jnp.pad(w_in[:, fcols], ((0, 0), (0, LANES - FOX_HEADS)))], axis=1
    ).astype(BF16)
    fbias = jnp.pad(forget_bias.astype(F32), (0, LANES - FOX_HEADS)).reshape(1, LANES)
    u, lf = _project(h2, norm_g.reshape(1, D_MODEL), w_all, *tables, fbias, s)
    u3 = u.reshape(b, s, U_WIDTH)
    oa = _swa(u3, sinks.astype(F32))
    ob = _fox(u3, lf.reshape(b, s, LANES))
    oc = _moba(u3)
    perm = _swa_out_perm()
    t = b * s
    return _outproj(oa.reshape(t, SWA_WIDTH), ob.reshape(t, FOX_WIDTH), oc.reshape(t, MOBA_WIDTH),
                    mix_gain[perm].reshape(1, MIX_WIDTH).astype(F32),
                    w_out[perm].astype(BF16), h2)


def _moe_ffn(h2, norm_g, w_router, w_gate, w_up, w_down, out_g):
    t = h2.shape[0]
    wr = jnp.pad(w_router.astype(F32), ((0, 0), (0, LANES - N_EXPERTS)))
    wr3 = jnp.stack(_split3(wr))
    z8, top, gate = _router(h2, norm_g.reshape(1, D_MODEL), wr3)
    dest, fill, blk_e, blk_live, blk_x, p_rows = _dispatch_plan(top[:, :TOP_K], t)
    x8 = _dispatch(z8, dest, fill, p_rows)
    y8 = _moe_experts(x8, blk_e, blk_live, blk_x,
                      w_gate.astype(BF16), w_up.astype(BF16), w_down.astype(BF16))
    return _combine(h2, y8, dest, gate, out_g)


def kernel(x, attn_norm, w_in, fox_forget_bias, swa_sinks, mix_gain, w_out, ffn_norm,
           dense_w_gate, dense_w_up, dense_w_down, router_w, moe_w_gate, moe_w_up,
           moe_w_down, final_norm):
    b, s, d = x.shape
    depth = attn_norm.shape[0]
    assert d == D_MODEL and s % MOBA_BLOCK == 0 and s % ROW_TILE == 0
    tables = _rope_tables(s)
    h = x.reshape(b * s, d)
    normed = False
    for layer in range(depth):
        h = _mixer(h, b, s, attn_norm[layer], w_in[layer], fox_forget_bias[layer],
                   swa_sinks[layer], mix_gain[layer], w_out[layer], tables)
        j = layer // 2
        last = layer == depth - 1
        if layer % 2 == 0:
            h = _dense_ffn(h, ffn_norm[layer].reshape(1, d), dense_w_gate[j], dense_w_up[j],
                           dense_w_down[j])
        else:
            h = _moe_ffn(h, ffn_norm[layer], router_w[j], moe_w_gate[j], moe_w_up[j],
                         moe_w_down[j], final_norm if last else None)
            normed = last
    if not normed:
        h = _final_norm(h, final_norm.reshape(1, d))
    return h.reshape(b, s, d)
```

```python
import functools

import jax
import jax.numpy as jnp
import numpy as np
from jax import lax
from jax.experimental import pallas as pl
from jax.experimental.pallas import tpu as pltpu

F32 = jnp.float32
BF16 = jnp.bfloat16

D_MODEL = 1024
HEAD_DIM = 64
LANES = 128
SWA_Q_HEADS = 8
SWA_KV_HEADS = 2
SWA_WINDOW = 128
FOX_HEADS = 4
MOBA_HEADS = 4
MOBA_BLOCK = 256
MOBA_TOPK = 3
ROPE_THETA = 10000.0
RMS_EPS = 1e-5
D_FF = 3584
N_EXPERTS = 8
TOP_K = 2
MOE_BLOCK = 512
NEG_INF = -1e30
ATTN_SCALE = HEAD_DIM ** -0.5

SWA_WIDTH = SWA_Q_HEADS * HEAD_DIM
SWA_KV_WIDTH = SWA_KV_HEADS * HEAD_DIM
FOX_WIDTH = FOX_HEADS * HEAD_DIM
MOBA_WIDTH = MOBA_HEADS * HEAD_DIM
MIX_WIDTH = SWA_WIDTH + FOX_WIDTH + MOBA_WIDTH

U_BLOCKS = 18
U_WIDTH = U_BLOCKS * LANES
ROPE_BLOCKS = (0, 1, 2, 3, 4, 12, 13, 14, 15)
Q_BLOCKS = (0, 1, 2, 3, 6, 7, 12, 13)
W_ALL_WIDTH = U_WIDTH + LANES

VMEM_LIMIT = 56 * 1024 * 1024

ROW_TILE = 512
ATT_TILE = 256
FF_TILE = 512
MOE_FF_TILE = 1792
CHUNKS = D_MODEL // LANES
MOE_ISSUE_UNROLL = 8


def _params(*sem):
    return pltpu.CompilerParams(dimension_semantics=sem, vmem_limit_bytes=VMEM_LIMIT)


def _split3(x):
    hi = x.astype(BF16)
    r1 = x - hi.astype(F32)
    mid = r1.astype(BF16)
    lo = (r1 - mid.astype(F32)).astype(BF16)
    return hi, mid, lo


def _dot_nt(a, b):
    return lax.dot_general(a, b, (((1,), (1,)), ((), ())), preferred_element_type=F32)


def _dot(a, b):
    return jnp.dot(a, b, preferred_element_type=F32)


def _proj_kernel(x_ref, g_ref, w_ref, cos_ref, sa_ref, sb_ref, fb_ref, u_ref, lf_ref):
    x = x_ref[...]
    inv = lax.rsqrt(jnp.mean(x * x, axis=-1, keepdims=True) + RMS_EPS)
    h = (x * inv * g_ref[...]).astype(BF16)
    cos = cos_ref[...]
    sa = sa_ref[...]
    sb = sb_ref[...]
    for c in range(U_BLOCKS // 2):
        acc = _dot(h, w_ref[:, c * 2 * LANES:(c + 1) * 2 * LANES])
        for half in range(2):
            blk = 2 * c + half
            a = acc[:, half * LANES:(half + 1) * LANES]
            if blk in ROPE_BLOCKS:
                a = a * cos + pltpu.roll(a, 32, 1) * sa + pltpu.roll(a, 96, 1) * sb
            if blk in Q_BLOCKS:
                a = a * ATTN_SCALE
            u_ref[:, blk * LANES:(blk + 1) * LANES] = a.astype(BF16)
    f = _dot(h, w_ref[:, U_WIDTH:W_ALL_WIDTH]) + fb_ref[...]
    lf_ref[...] = jnp.minimum(f, 0.0) - jnp.log(1.0 + jnp.exp(-jnp.abs(f)))


def _project(x2, norm_g, w_all, cos, sa, sb, fbias, seq):
    t = x2.shape[0]
    tm = ROW_TILE
    nseq = seq // tm
    row = lambda i: (i, 0)
    pos = lambda i: (i % nseq, 0)
    fixed = lambda i: (0, 0)
    return pl.pallas_call(
        _proj_kernel,
        grid=(t // tm,),
        in_specs=[
            pl.BlockSpec((tm, D_MODEL), row),
            pl.BlockSpec((1, D_MODEL), fixed),
            pl.BlockSpec((D_MODEL, W_ALL_WIDTH), fixed),
            pl.BlockSpec((tm, LANES), pos),
            pl.BlockSpec((tm, LANES), pos),
            pl.BlockSpec((tm, LANES), pos),
            pl.BlockSpec((1, LANES), fixed),
        ],
        out_specs=[pl.BlockSpec((tm, U_WIDTH), row), pl.BlockSpec((tm, LANES), row)],
        out_shape=[jax.ShapeDtypeStruct((t, U_WIDTH), BF16),
                   jax.ShapeDtypeStruct((t, LANES), F32)],
        compiler_params=_params("parallel"),
        name="proj",
    )(x2, norm_g, w_all, cos, sa, sb, fbias)


def _swa_kernel(sink_ref, q_ref, kc_ref, kp_ref, vc_ref, vp_ref, o_ref):
    n = pl.program_id(1)
    w = SWA_WINDOW
    group = SWA_Q_HEADS // SWA_KV_HEADS
    lane = lax.broadcasted_iota(jnp.int32, (1, LANES), 1)
    k2 = jnp.concatenate([kp_ref[0], kc_ref[0]], axis=0)
    v2 = jnp.concatenate([vp_ref[0], vc_ref[0]], axis=0)
    qi = lax.broadcasted_iota(jnp.int32, (group * w, 2 * w), 0) % w
    kj = lax.broadcasted_iota(jnp.int32, (group * w, 2 * w), 1)
    valid = (kj > qi) & (kj <= qi + w) & ((kj >= w) | (n > 0))
    outs = []
    for g in range(SWA_KV_HEADS):
        keep = (lane < HEAD_DIM) if g == 0 else (lane >= HEAD_DIM)
        qs = jnp.concatenate(
            [jnp.where(keep, q_ref[0, :, j * LANES:(j + 1) * LANES], 0) for j in range(group)],
            axis=0)
        s = jnp.where(valid, _dot_nt(qs, k2), NEG_INF)
        ps, inv = [], []
        for j in range(group):
            sj = s[j * w:(j + 1) * w]
            sink = sink_ref[g * group + j]
            m = jnp.maximum(jnp.max(sj, axis=1, keepdims=True), sink)
            p = jnp.exp(sj - m)
            inv.append(1.0 / (jnp.sum(p, axis=1, keepdims=True) + jnp.exp(sink - m)))
            ps.append(p.astype(BF16))
        o = _dot(jnp.concatenate(ps, axis=0), v2)
        outs.append([o[j * w:(j + 1) * w] * inv[j] for j in range(group)])
    for j in range(group):
        oj = jnp.where(lane < HEAD_DIM, outs[0][j], outs[1][j])
        o_ref[0, :, j * LANES:(j + 1) * LANES] = oj.astype(BF16)


def _swa(u3, sinks):
    b, s, _ = u3.shape
    w = SWA_WINDOW
    cur = lambda blk: (lambda bi, n, sk: (bi, n, blk))
    prev = lambda blk: (lambda bi, n, sk: (bi, jnp.maximum(n - 1, 0), blk))
    grid_spec = pltpu.PrefetchScalarGridSpec(
        num_scalar_prefetch=1,
        grid=(b, s // w),
        in_specs=[
            pl.BlockSpec((1, w, SWA_WIDTH), lambda bi, n, sk: (bi, n, 0)),
            pl.BlockSpec((1, w, LANES), cur(4)),
            pl.BlockSpec((1, w, LANES), prev(4)),
            pl.BlockSpec((1, w, LANES), cur(5)),
            pl.BlockSpec((1, w, LANES), prev(5)),
        ],
        out_specs=pl.BlockSpec((1, w, SWA_WIDTH), lambda bi, n, sk: (bi, n, 0)),
    )
    return pl.pallas_call(
        _swa_kernel,
        grid_spec=grid_spec,
        out_shape=jax.ShapeDtypeStruct((b, s, SWA_WIDTH), BF16),
        compiler_params=_params("parallel", "parallel"),
        name="swa",
    )(sinks, u3, u3, u3, u3, u3)


def _head_lanes(h):
    lane = lax.broadcasted_iota(jnp.int32, (1, LANES), 1)
    half = h % 2
    return (lane // HEAD_DIM) == half, lane - (1 - half) * HEAD_DIM


def _causal_attention(h, qp_scr, kp_scr, v_ref, o_ref, t):
    s_len = qp_scr.shape[0]
    lane = lax.broadcasted_iota(jnp.int32, (1, LANES), 1)
    row = lax.broadcasted_iota(jnp.int32, (t, t), 0)
    col = lax.broadcasted_iota(jnp.int32, (t, t), 1)
    for qi in range(s_len // t):
        rows = slice(qi * t, (qi + 1) * t)
        q = qp_scr[rows, :]
        sd = jnp.where(col <= row, _dot_nt(q, kp_scr[rows, :]), NEG_INF)
        m = jnp.max(sd, axis=1, keepdims=True)
        if qi:
            sp = _dot_nt(q, kp_scr[:qi * t, :])
            m = jnp.maximum(m, jnp.max(sp, axis=1, keepdims=True))
        pd = jnp.exp(sd - m)
        l = jnp.sum(pd, axis=1, keepdims=True)
        o = _dot(pd.astype(BF16), v_ref[0, rows, :])
        if qi:
            pp = jnp.exp(sp - m)
            l = l + jnp.sum(pp, axis=1, keepdims=True)
            o = o + _dot(pp.astype(BF16), v_ref[0, :qi * t, :])
        o = (o * (1.0 / l)).astype(BF16)

        @pl.when(h % 2 == 0)
        def _():
            o_ref[0, rows, :] = o

        @pl.when(h % 2 == 1)
        def _():
            o_ref[0, rows, :] = jnp.where(lane < HEAD_DIM, o_ref[0, rows, :], o)


def _head_attention(kernel_fn, u3, extra, q_blk, k_blk, v_blk, n_heads, name):
    b, s, _ = u3.shape
    pair = lambda blk: (lambda bi, h: (bi, 0, blk + h // 2))
    seq_block = pl.BlockSpec((1, s, LANES), lambda bi, h: (bi, 0, 0))
    return pl.pallas_call(
        kernel_fn,
        grid=(b, n_heads),
        in_specs=[pl.BlockSpec((1, s, LANES), pair(q_blk)),
                  pl.BlockSpec((1, s, LANES), pair(k_blk)),
                  pl.BlockSpec((1, s, LANES), pair(v_blk))] + [seq_block] * len(extra),
        out_specs=pl.BlockSpec((1, s, LANES), pair(0)),
        out_shape=jax.ShapeDtypeStruct((b, s, n_heads * HEAD_DIM), BF16),
        scratch_shapes=[pltpu.VMEM((s, LANES), BF16), pltpu.VMEM((s, LANES), BF16)],
        compiler_params=_params("parallel", "arbitrary"),
        name=name,
    )(u3, u3, u3, *extra)


def _fox_kernel(q_ref, k_ref, v_ref, lf_ref, o_ref, qp_scr, kp_scr):
    h = pl.program_id(1)
    s_len = q_ref.shape[1]
    t = ATT_TILE
    lane = lax.broadcasted_iota(jnp.int32, (1, LANES), 1)
    keep, f = _head_lanes(h)
    r = lax.broadcasted_iota(jnp.int32, (t, t), 0)
    cc = lax.broadcasted_iota(jnp.int32, (t, t), 1)
    tri = jnp.where(cc <= r, 1.0, 0.0).astype(BF16)
    ones_q = jnp.where((f >= 3) & (f < 6), 1.0, 0.0).astype(BF16)
    ones_k = jnp.where((f >= 0) & (f < 3), 1.0, 0.0).astype(BF16)
    carry = jnp.zeros((1, LANES), F32)
    for i in range(s_len // t):
        rows = slice(i * t, (i + 1) * t)
        lf = jnp.sum(jnp.where(lane == h, lf_ref[0, rows, :], 0.0), axis=1, keepdims=True)
        hi, mid, lo = _split3(jnp.broadcast_to(lf, (t, LANES)))
        c = (_dot(tri, hi) + _dot(tri, mid)) + _dot(tri, lo) + carry
        carry = c[t - 1:t, :]
        chi, cmid, clo = _split3(c)
        qf = jnp.where(f == 0, chi, jnp.where(f == 1, cmid, jnp.where(f == 2, clo, ones_q)))
        kf = jnp.where(f == 3, -chi, jnp.where(f == 4, -cmid, jnp.where(f == 5, -clo, ones_k)))
        qp_scr[rows, :] = jnp.where(keep, q_ref[0, rows, :], qf)
        kp_scr[rows, :] = jnp.where(keep, k_ref[0, rows, :], kf)
    _causal_attention(h, qp_scr, kp_scr, v_ref, o_ref, t)


def _fox(u3, lf3):
    return _head_attention(_fox_kernel, u3, (lf3,), 6, 8, 10, FOX_HEADS, "fox")


def _moba_kernel(q_ref, k_ref, v_ref, o_ref, qp_scr, kp_scr):
    h = pl.program_id(1)
    s_len = q_ref.shape[1]
    t = MOBA_BLOCK
    nkb = s_len // t
    sub = 8
    assert nkb <= sub
    keep, f = _head_lanes(h)
    means = [jnp.mean(jnp.where(keep, k_ref[0, j * t:(j + 1) * t, :], 0).astype(F32),
                      axis=0, keepdims=True) for j in range(nkb)]
    km = _split3(jnp.concatenate(means + [jnp.zeros((LANES - nkb, LANES), F32)], axis=0))
    blk = lax.broadcasted_iota(jnp.int32, (sub, t), 0)
    for i in range(nkb):
        rows = slice(i * t, (i + 1) * t)
        q = q_ref[0, rows, :]
        g = ((_dot_nt(km[0], q) + _dot_nt(km[1], q)) + _dot_nt(km[2], q))[:sub, :]
        past = blk < i
        g = jnp.where(past, g, NEG_INF)
        rank = jnp.zeros((sub, t), jnp.int32)
        for r in range(i):
            gr = jnp.broadcast_to(g[r:r + 1, :], (sub, t))
            beats = (gr > g) | ((gr == g) & (r < blk))
            rank = rank + beats.astype(jnp.int32)
        drop = jnp.where(past & (rank >= MOBA_TOPK), 1.0, 0.0)
        lo = jnp.concatenate([drop, jnp.zeros((LANES - sub, t), F32)], axis=0)
        hi = jnp.concatenate([jnp.zeros((HEAD_DIM, t), F32), drop,
                              jnp.zeros((HEAD_DIM - sub, t), F32)], axis=0)
        qf = jnp.where(h % 2 == 0, hi, lo).T.astype(BF16)
        kf = jnp.where(f == i, NEG_INF, 0.0).astype(BF16)
        qp_scr[rows, :] = jnp.where(keep, q, qf)
        kp_scr[rows, :] = jnp.where(keep, k_ref[0, rows, :], kf)
    _causal_attention(h, qp_scr, kp_scr, v_ref, o_ref, t)


def _moba(u3):
    return _head_attention(_moba_kernel, u3, (), 12, 14, 16, MOBA_HEADS, "moba")


def _rms_f32(x):
    x = x.astype(F32)
    return x * lax.rsqrt(jnp.mean(x * x, axis=-1, keepdims=True) + RMS_EPS)


def _outproj_kernel(oa_ref, ob_ref, oc_ref, gain_ref, w_ref, x_ref, h_ref):
    a0, a1, a2 = 0, SWA_WIDTH, SWA_WIDTH + FOX_WIDTH
    ya = (_rms_f32(oa_ref[...]) * gain_ref[:, a0:a1]).astype(BF16)
    yb = (_rms_f32(ob_ref[...]) * gain_ref[:, a1:a2]).astype(BF16)
    yc = (_rms_f32(oc_ref[...]) * gain_ref[:, a2:]).astype(BF16)
    y = (_dot(ya, w_ref[a0:a1, :]) + _dot(yb, w_ref[a1:a2, :])) + _dot(yc, w_ref[a2:, :])
    h_ref[...] = x_ref[...] + y


def _outproj(oa, ob, oc, gain, w_out, x2):
    t = x2.shape[0]
    tm = ROW_TILE
    row = lambda i: (i, 0)
    fixed = lambda i: (0, 0)
    return pl.pallas_call(
        _outproj_kernel,
        grid=(t // tm,),
        in_specs=[
            pl.BlockSpec((tm, SWA_WIDTH), row),
            pl.BlockSpec((tm, FOX_WIDTH), row),
            pl.BlockSpec((tm, MOBA_WIDTH), row),
            pl.BlockSpec((1, MIX_WIDTH), fixed),
            pl.BlockSpec((MIX_WIDTH, D_MODEL), fixed),
            pl.BlockSpec((tm, D_MODEL), row),
        ],
        out_specs=pl.BlockSpec((tm, D_MODEL), row),
        out_shape=jax.ShapeDtypeStruct((t, D_MODEL), F32),
        compiler_params=_params("parallel"),
        name="outproj",
    )(oa, ob, oc, gain, w_out, x2)


def _silu(x):
    return x / (1.0 + jnp.exp(-x))


def _ffn_kernel(h_ref, g_ref, wg_ref, wu_ref, wd_ref, o_ref, z_scr, acc_scr):
    j = pl.program_id(1)

    @pl.when(j == 0)
    def _():
        x = h_ref[...]
        inv = lax.rsqrt(jnp.mean(x * x, axis=-1, keepdims=True) + RMS_EPS)
        z_scr[...] = (x * inv * g_ref[...]).astype(BF16)

    z = z_scr[...]
    a = (_silu(_dot(z, wg_ref[...].astype(BF16))) * _dot(z, wu_ref[...].astype(BF16))).astype(BF16)
    y = _dot(a, wd_ref[...].astype(BF16))

    @pl.when(j == 0)
    def _():
        acc_scr[...] = y

    @pl.when(j > 0)
    def _():
        acc_scr[...] += y

    @pl.when(j == pl.num_programs(1) - 1)
    def _():
        o_ref[...] = h_ref[...] + acc_scr[...]


def _dense_ffn(h2, norm_g, w_gate, w_up, w_down):
    t = h2.shape[0]
    tm, tf = 2 * ROW_TILE, FF_TILE
    return pl.pallas_call(
        _ffn_kernel,
        grid=(t // tm, D_FF // tf),
        in_specs=[
            pl.BlockSpec((tm, D_MODEL), lambda i, j: (i, 0)),
            pl.BlockSpec((1, D_MODEL), lambda i, j: (0, 0)),
            pl.BlockSpec((D_MODEL, tf), lambda i, j: (0, j)),
            pl.BlockSpec((D_MODEL, tf), lambda i, j: (0, j)),
            pl.BlockSpec((tf, D_MODEL), lambda i, j: (j, 0)),
        ],
        out_specs=pl.BlockSpec((tm, D_MODEL), lambda i, j: (i, 0)),
        out_shape=jax.ShapeDtypeStruct((t, D_MODEL), F32),
        scratch_shapes=[pltpu.VMEM((tm, D_MODEL), BF16), pltpu.VMEM((tm, D_MODEL), F32)],
        compiler_params=_params("parallel", "arbitrary"),
        name="dense_ffn",
    )(h2, norm_g, w_gate, w_up, w_down)


def _router_kernel(h_ref, g_ref, wr_ref, z_ref, idx_ref, gate_ref):
    tm = h_ref.shape[0]
    x = h_ref[...]
    inv = lax.rsqrt(jnp.mean(x * x, axis=-1, keepdims=True) + RMS_EPS)
    z = x * inv * g_ref[...]
    for c in range(CHUNKS):
        z_ref[pl.ds(c, tm, stride=CHUNKS), :] = z[:, c * LANES:(c + 1) * LANES]
    zh, zm, zl = _split3(z)
    wh, wm, wl = wr_ref[0], wr_ref[1], wr_ref[2]
    logits = ((_dot(zh, wh) + (_dot(zh, wm) + _dot(zm, wh)))
              + ((_dot(zh, wl) + _dot(zl, wh)) + _dot(zm, wm)))
    lane = lax.broadcasted_iota(jnp.int32, (tm, LANES), 1)
    logits = jnp.where(lane < N_EXPERTS, logits, -jnp.inf)
    m1 = jnp.max(logits, axis=1, keepdims=True)
    i1 = jnp.min(jnp.where(logits == m1, lane, LANES), axis=1, keepdims=True)
    rest = jnp.where(lane == i1, -jnp.inf, logits)
    m2 = jnp.max(rest, axis=1, keepdims=True)
    i2 = jnp.min(jnp.where(rest == m2, lane, LANES), axis=1, keepdims=True)
    e2 = jnp.exp(m2 - m1)
    g1 = 1.0 / (1.0 + e2)
    idx_ref[...] = jnp.where(lane == 0, i1, i2)
    gate_ref[...] = jnp.where(lane == 0, g1, e2 * g1)


def _router(h2, norm_g, wr3):
    t = h2.shape[0]
    tm = ROW_TILE
    row = lambda i: (i, 0)
    return pl.pallas_call(
        _router_kernel,
        grid=(t // tm,),
        in_specs=[
            pl.BlockSpec((tm, D_MODEL), row),
            pl.BlockSpec((1, D_MODEL), lambda i: (0, 0)),
            pl.BlockSpec((3, D_MODEL, LANES), lambda i: (0, 0, 0)),
        ],
        out_specs=[pl.BlockSpec((tm * CHUNKS, LANES), row),
                   pl.BlockSpec((tm, LANES), row),
                   pl.BlockSpec((tm, LANES), row)],
        out_shape=[jax.ShapeDtypeStruct((t * CHUNKS, LANES), F32),
                   jax.ShapeDtypeStruct((t, LANES), jnp.int32),
                   jax.ShapeDtypeStruct((t, LANES), F32)],
        compiler_params=_params("parallel"),
        name="router",
    )(h2, norm_g, wr3)


def _row_copy(src, src_row, dst, dst_row, sem):
    return pltpu.make_async_copy(src.at[pl.ds(src_row * CHUNKS, CHUNKS), :],
                                 dst.at[pl.ds(dst_row * CHUNKS, CHUNKS), :], sem)


def _rows_wait(src, dst, rows, sem):
    pltpu.make_async_copy(src.at[pl.ds(0, rows * CHUNKS), :],
                          dst.at[pl.ds(0, rows * CHUNKS), :], sem).wait()


def _dispatch_kernel(dest_ref, fill_ref, z_ref, x_hbm, zero_scr, fsem, sem):
    i = pl.program_id(0)
    tm = MOE_BLOCK
    assert z_ref.shape[0] == tm * CHUNKS

    @pl.when(i == 0)
    def _():
        zero_scr[...] = jnp.zeros_like(zero_scr)
        for e in range(fill_ref.shape[0]):
            @pl.when(fill_ref[e] >= 0)
            def _():
                start = pl.multiple_of(fill_ref[e] * CHUNKS, tm * CHUNKS)
                pltpu.make_async_copy(zero_scr, x_hbm.at[pl.ds(start, tm * CHUNKS), :], fsem).start()
        for e in range(fill_ref.shape[0]):
            @pl.when(fill_ref[e] >= 0)
            def _():
                _rows_wait(zero_scr, x_hbm, tm, fsem)

    base = i * tm * TOP_K

    def trip(it, c):
        for k in range(MOE_ISSUE_UNROLL):
            r = it * MOE_ISSUE_UNROLL + k
            _row_copy(z_ref, r // TOP_K, x_hbm, dest_ref[base + r], sem).start()
        return c
    lax.fori_loop(0, tm * TOP_K // MOE_ISSUE_UNROLL, trip, 0)
    for _ in range(TOP_K):
        _rows_wait(z_ref, x_hbm, tm, sem)


def _dispatch(z8, dest, fill, p_rows):
    tm = MOE_BLOCK
    grid_spec = pltpu.PrefetchScalarGridSpec(
        num_scalar_prefetch=2,
        grid=(z8.shape[0] // (tm * CHUNKS),),
        in_specs=[pl.BlockSpec((tm * CHUNKS, LANES), lambda i, d, f: (i, 0))],
        out_specs=pl.BlockSpec(memory_space=pl.ANY),
        scratch_shapes=[pltpu.VMEM((tm * CHUNKS, LANES), F32),
                        pltpu.SemaphoreType.DMA(()), pltpu.SemaphoreType.DMA(())],
    )
    return pl.pallas_call(
        _dispatch_kernel,
        grid_spec=grid_spec,
        out_shape=jax.ShapeDtypeStruct((p_rows * CHUNKS, LANES), F32),
        compiler_params=_params("arbitrary"),
        name="moe_dispatch",
    )(dest, fill, z8)


def _moe_kernel(be_ref, live_ref, bx_ref, x_ref, wg_ref, wu_ref, wd_ref, y_ref, xb, acc):
    i = pl.program_id(0)
    j = pl.program_id(1)
    tm = xb.shape[0]

    @pl.when((live_ref[i] == 0) & (j == 0))
    def _():
        y_ref[...] = jnp.zeros_like(y_ref)

    @pl.when(live_ref[i] > 0)
    def _():
        @pl.when(j == 0)
        def _():
            for c in range(CHUNKS):
                xb[:, c * LANES:(c + 1) * LANES] = x_ref[pl.ds(c, tm, stride=CHUNKS), :].astype(BF16)

        x = xb[...]
        a = (_silu(_dot(x, wg_ref[0])) * _dot(x, wu_ref[0])).astype(BF16)
        y = _dot(a, wd_ref[0])

        @pl.when(j == 0)
        def _():
            acc[...] = y

        @pl.when(j > 0)
        def _():
            acc[...] += y

        @pl.when(j == pl.num_programs(1) - 1)
        def _():
            for c in range(CHUNKS):
                y_ref[pl.ds(c, tm, stride=CHUNKS), :] = acc[:, c * LANES:(c + 1) * LANES]


def _moe_experts(x8, blk_e, blk_live, blk_x, w_gate, w_up, w_down):
    tm, tf = MOE_BLOCK, MOE_FF_TILE
    nblk = blk_e.shape[0]
    nff = D_FF // tf
    ff = lambda i, j, live: jnp.where(live[i] > 0, j, nff - 1)
    grid_spec = pltpu.PrefetchScalarGridSpec(
        num_scalar_prefetch=3,
        grid=(nblk, nff),
        in_specs=[
            pl.BlockSpec((tm * CHUNKS, LANES), lambda i, j, be, lv, bx: (bx[i], 0)),
            pl.BlockSpec((1, D_MODEL, tf), lambda i, j, be, lv, bx: (be[i], 0, ff(i, j, lv))),
            pl.BlockSpec((1, D_MODEL, tf), lambda i, j, be, lv, bx: (be[i], 0, ff(i, j, lv))),
            pl.BlockSpec((1, tf, D_MODEL), lambda i, j, be, lv, bx: (be[i], ff(i, j, lv), 0)),
        ],
        out_specs=pl.BlockSpec((tm * CHUNKS, LANES), lambda i, j, be, lv, bx: (i, 0)),
        scratch_shapes=[pltpu.VMEM((tm, D_MODEL), BF16), pltpu.VMEM((tm, D_MODEL), F32)],
    )
    return pl.pallas_call(
        _moe_kernel,
        grid_spec=grid_spec,
        out_shape=jax.ShapeDtypeStruct(x8.shape, F32),
        compiler_params=_params("arbitrary", "arbitrary"),
        name="moe_experts",
    )(blk_e, blk_live, blk_x, x8, w_gate, w_up, w_down)


def _combine_kernel(dest_ref, h_ref, gate_ref, *rest, final_norm):
    if final_norm:
        g_ref, y_hbm, o_ref, ybuf, sem = rest
    else:
        y_hbm, o_ref, ybuf, sem = rest
    i = pl.program_id(0)
    tm = h_ref.shape[0]
    rows = tm * TOP_K
    stride = TOP_K * CHUNKS

    def pull(tile, slot):
        def trip(it, c):
            for k in range(MOE_ISSUE_UNROLL):
                r = it * MOE_ISSUE_UNROLL + k
                _row_copy(y_hbm, dest_ref[tile * rows + r], ybuf.at[slot], r, sem.at[slot]).start()
            return c
        lax.fori_loop(0, rows // MOE_ISSUE_UNROLL, trip, 0)

    @pl.when(i == 0)
    def _():
        pull(0, 0)

    @pl.when(i + 1 < pl.num_programs(0))
    def _():
        pull(i + 1, (i + 1) % 2)

    slot = i % 2
    yb = ybuf.at[slot]
    _rows_wait(y_hbm, yb, rows, sem.at[slot])
    gates = [gate_ref[:, k:k + 1] for k in range(TOP_K)]
    parts = []
    ss = jnp.zeros((tm, 1), F32)
    for c in range(CHUNKS):
        y = gates[0] * yb[pl.ds(c, tm, stride=stride), :]
        for k in range(1, TOP_K):
            y = y + gates[k] * yb[pl.ds(k * CHUNKS + c, tm, stride=stride), :]
        hc = h_ref[:, c * LANES:(c + 1) * LANES] + y
        parts.append(hc)
        ss = ss + jnp.sum(hc * hc, axis=1, keepdims=True)
    if final_norm:
        inv = lax.rsqrt(ss / D_MODEL + RMS_EPS)
    for c in range(CHUNKS):
        cols = slice(c * LANES, (c + 1) * LANES)
        o_ref[:, cols] = parts[c] * inv * g_ref[:, cols] if final_norm else parts[c]


def _combine(h2, y8, dest, gate, out_g):
    t = h2.shape[0]
    tm = ROW_TILE
    final_norm = out_g is not None
    row = lambda i, d: (i, 0)
    in_specs = [pl.BlockSpec((tm, D_MODEL), row), pl.BlockSpec((tm, LANES), row)]
    args = [h2, gate]
    if final_norm:
        in_specs.append(pl.BlockSpec((1, D_MODEL), lambda i, d: (0, 0)))
        args.append(out_g.reshape(1, D_MODEL))
    grid_spec = pltpu.PrefetchScalarGridSpec(
        num_scalar_prefetch=1,
        grid=(t // tm,),
        in_specs=in_specs + [pl.BlockSpec(memory_space=pl.ANY)],
        out_specs=pl.BlockSpec((tm, D_MODEL), row),
        scratch_shapes=[pltpu.VMEM((2, tm * TOP_K * CHUNKS, LANES), F32),
                        pltpu.SemaphoreType.DMA((2,))],
    )
    return pl.pallas_call(
        functools.partial(_combine_kernel, final_norm=final_norm),
        grid_spec=grid_spec,
        out_shape=jax.ShapeDtypeStruct((t, D_MODEL), F32),
        compiler_params=_params("arbitrary"),
        name="moe_combine",
    )(dest, *args, y8)


def _norm_kernel(h_ref, g_ref, o_ref):
    x = h_ref[...]
    o_ref[...] = x * lax.rsqrt(jnp.mean(x * x, axis=-1, keepdims=True) + RMS_EPS) * g_ref[...]


def _final_norm(h2, norm_g):
    t = h2.shape[0]
    tm = ROW_TILE
    return pl.pallas_call(
        _norm_kernel,
        grid=(t // tm,),
        in_specs=[pl.BlockSpec((tm, D_MODEL), lambda i: (i, 0)),
                  pl.BlockSpec((1, D_MODEL), lambda i: (0, 0))],
        out_specs=pl.BlockSpec((tm, D_MODEL), lambda i: (i, 0)),
        out_shape=jax.ShapeDtypeStruct((t, D_MODEL), F32),
        compiler_params=_params("parallel"),
        name="final_norm",
    )(h2, norm_g)


def _dispatch_plan(top_i, t):
    n = t * TOP_K
    tm = MOE_BLOCK
    e_flat = top_i.reshape(n)
    onehot = (e_flat[:, None] == jnp.arange(N_EXPERTS)[None, :]).astype(jnp.int32)
    counts = jnp.sum(onehot, axis=0)
    rank = jnp.sum((jnp.cumsum(onehot, axis=0) - onehot) * onehot, axis=1)
    padded = (counts + tm - 1) // tm * tm
    pends = jnp.cumsum(padded)
    pstarts = pends - padded
    dest = (pstarts[e_flat] + rank).astype(jnp.int32)
    p_rows = -(-n // tm) * tm + N_EXPERTS * tm
    nblk = p_rows // tm
    blk = jnp.arange(nblk, dtype=jnp.int32)
    blk_e = jnp.minimum(jnp.sum((blk * tm)[:, None] >= pends[None, :], axis=1), N_EXPERTS - 1)
    live = blk * tm < pends[-1]
    last_live = jnp.maximum(pends[-1] // tm - 1, 0)
    blk_e = jnp.where(live, blk_e, blk_e[last_live])
    blk_x = jnp.where(live, blk, 0)
    tail = jnp.where(padded > 0, pends - tm, -1)
    spare = pends[-1] + jnp.arange(N_EXPERTS) * tm
    fill = jnp.concatenate([tail, jnp.where(spare < p_rows, spare, -1)])
    i32 = lambda a: a.astype(jnp.int32)
    return dest, i32(fill), i32(blk_e), i32(live), i32(blk_x), p_rows


def _rope_tables(seq):
    inv = 1.0 / (ROPE_THETA ** (jnp.arange(0, HEAD_DIM, 2, dtype=F32) / HEAD_DIM))
    ang = jnp.arange(seq, dtype=F32)[:, None] * inv[None, :]
    cos = jnp.concatenate([jnp.cos(ang)] * 4, axis=-1)
    sin = jnp.concatenate([jnp.sin(ang)] * 4, axis=-1)
    upper = (jnp.arange(LANES) % HEAD_DIM) >= HEAD_DIM // 2
    sa = jnp.where(upper[None, :], sin, 0.0)
    sb = jnp.where(upper[None, :], 0.0, -sin)
    return cos, sa, sb


def _in_proj_columns():
    split = np.cumsum([0, SWA_WIDTH, SWA_KV_WIDTH, SWA_KV_WIDTH, FOX_WIDTH, FOX_WIDTH, FOX_WIDTH,
                       FOX_HEADS, MOBA_WIDTH, MOBA_WIDTH, MOBA_WIDTH])
    aq, ak, av, bq, bk, bv, bf, cq, ck, cv = split[:-1]
    half = SWA_Q_HEADS // 2
    cols = []
    for j in range(half):
        cols += list(range(aq + j * HEAD_DIM, aq + (j + 1) * HEAD_DIM))
        cols += list(range(aq + (j + half) * HEAD_DIM, aq + (j + half + 1) * HEAD_DIM))
    for start, width in ((ak, SWA_KV_WIDTH), (av, SWA_KV_WIDTH), (bq, FOX_WIDTH), (bk, FOX_WIDTH),
                         (bv, FOX_WIDTH), (cq, MOBA_WIDTH), (ck, MOBA_WIDTH), (cv, MOBA_WIDTH)):
        cols += list(range(start, start + width))
    forget = list(range(bf, bf + FOX_HEADS))
    return np.array(cols, np.int32), np.array(forget, np.int32)


def _swa_out_perm():
    half = SWA_Q_HEADS // 2
    perm = []
    for j in range(half):
        perm += list(range(j * HEAD_DIM, (j + 1) * HEAD_DIM))
        perm += list(range((j + half) * HEAD_DIM, (j + half + 1) * HEAD_DIM))
    return np.array(perm + list(range(SWA_WIDTH, MIX_WIDTH)), np.int32)


def _mixer(h2, b, s, norm_g, w_in, forget_bias, sinks, mix_gain, w_out, tables):
    cols, fcols = _in_proj_columns()
    w_all = jnp.concatenate(
        [w_in[:, cols], jnp.pad(w_in[:, fcols], ((0, 0), (0, LANES - FOX_HEADS)))], axis=1
    ).astype(BF16)
    fbias = jnp.pad(forget_bias.astype(F32), (0, LANES - FOX_HEADS)).reshape(1, LANES)
    u, lf = _project(h2, norm_g.reshape(1, D_MODEL), w_all, *tables, fbias, s)
    u3 = u.reshape(b, s, U_WIDTH)
    oa = _swa(u3, sinks.astype(F32))
    ob = _fox(u3, lf.reshape(b, s, LANES))
    oc = _moba(u3)
    perm = _swa_out_perm()
    t = b * s
    return _outproj(oa.reshape(t, SWA_WIDTH), ob.reshape(t, FOX_WIDTH), oc.reshape(t, MOBA_WIDTH),
                    mix_gain[perm].reshape(1, MIX_WIDTH).astype(F32),
                    w_out[perm].astype(BF16), h2)


def _moe_ffn(h2, norm_g, w_router, w_gate, w_up, w_down, out_g):
    t = h2.shape[0]
    wr = jnp.pad(w_router.astype(F32), ((0, 0), (0, LANES - N_EXPERTS)))
    wr3 = jnp.stack(_split3(wr))
    z8, top, gate = _router(h2, norm_g.reshape(1, D_MODEL), wr3)
    dest, fill, blk_e, blk_live, blk_x, p_rows = _dispatch_plan(top[:, :TOP_K], t)
    x8 = _dispatch(z8, dest, fill, p_rows)
    y8 = _moe_experts(x8, blk_e, blk_live, blk_x,
                      w_gate.astype(BF16), w_up.astype(BF16), w_down.astype(BF16))
    return _combine(h2, y8, dest, gate, out_g)


def kernel(x, attn_norm, w_in, fox_forget_bias, swa_sinks, mix_gain, w_out, ffn_norm,
           dense_w_gate, dense_w_up, dense_w_down, router_w, moe_w_gate, moe_w_up,
           moe_w_down, final_norm):
    b, s, d = x.shape
    depth = attn_norm.shape[0]
    assert d == D_MODEL and s % MOBA_BLOCK == 0 and s % ROW_TILE == 0
    tables = _rope_tables(s)
    h = x.reshape(b * s, d)
    normed = False
    for layer in range(depth):
        h = _mixer(h, b, s, attn_norm[layer], w_in[layer], fox_forget_bias[layer],
                   swa_sinks[layer], mix_gain[layer], w_out[layer], tables)
        j = layer // 2
        last = layer == depth - 1
        if layer % 2 == 0:
            h = _dense_ffn(h, ffn_norm[layer].reshape(1, d), dense_w_gate[j], dense_w_up[j],
                           dense_w_down[j])
        else:
            h = _moe_ffn(h, ffn_norm[layer], router_w[j], moe_w_gate[j], moe_w_up[j],
                         moe_w_down[j], final_norm if last else None)
            normed = last
    if not normed:
        h = _final_norm(h, final_norm.reshape(1, d))
    return h.reshape(b, s, d)
```

```python
import functools

import jax
import jax.numpy as jnp
import numpy as np
from jax import lax
from jax.experimental import pallas as pl
from jax.experimental.pallas import tpu as pltpu

F32 = jnp.float32
BF16 = jnp.bfloat16

D_MODEL = 1024
HEAD_DIM = 64
LANES = 128
SWA_Q_HEADS = 8
SWA_KV_HEADS = 2
SWA_WINDOW = 128
FOX_HEADS = 4
MOBA_HEADS = 4
MOBA_BLOCK = 256
MOBA_TOPK = 3
ROPE_THETA = 10000.0
RMS_EPS = 1e-5
D_FF = 3584
N_EXPERTS = 8
TOP_K = 2
MOE_BLOCK = 512
NEG_INF = -1e30
ATTN_SCALE = HEAD_DIM ** -0.5
LOG2E = 1.4426950408889634
Q_SCALE = ATTN_SCALE * LOG2E

SWA_WIDTH = SWA_Q_HEADS * HEAD_DIM
SWA_KV_WIDTH = SWA_KV_HEADS * HEAD_DIM
FOX_WIDTH = FOX_HEADS * HEAD_DIM
MOBA_WIDTH = MOBA_HEADS * HEAD_DIM
MIX_WIDTH = SWA_WIDTH + FOX_WIDTH + MOBA_WIDTH

U_BLOCKS = 18
U_WIDTH = U_BLOCKS * LANES
ROPE_BLOCKS = (0, 1, 2, 3, 4, 12, 13, 14, 15)
Q_BLOCKS = (0, 1, 2, 3, 6, 7, 12, 13)
W_ALL_WIDTH = U_WIDTH + LANES

VMEM_LIMIT = 56 * 1024 * 1024

ROW_TILE = 512
ATT_TILE = 256
SWA_STEP_BLOCKS = 4
FF_TILE = 512
MOE_FF_TILE = 1792
CHUNKS = D_MODEL // LANES
MOE_ISSUE_UNROLL = 8
DISPATCH_TILE = 2048


def _params(*sem):
    return pltpu.CompilerParams(dimension_semantics=sem, vmem_limit_bytes=VMEM_LIMIT)


def _split3(x):
    hi = x.astype(BF16)
    r1 = x - hi.astype(F32)
    mid = r1.astype(BF16)
    lo = (r1 - mid.astype(F32)).astype(BF16)
    return hi, mid, lo


def _dot_nt(a, b):
    return lax.dot_general(a, b, (((1,), (1,)), ((), ())), preferred_element_type=F32)


def _dot(a, b):
    return jnp.dot(a, b, preferred_element_type=F32)


def _proj_kernel(x_ref, g_ref, w_ref, cos_ref, sa_ref, sb_ref, fb_ref, u_ref, lf_ref):
    x = x_ref[...]
    inv = lax.rsqrt(jnp.mean(x * x, axis=-1, keepdims=True) + RMS_EPS)
    h = (x * inv * g_ref[...]).astype(BF16)
    cos = cos_ref[...]
    sa = sa_ref[...]
    sb = sb_ref[...]
    for c in range(U_BLOCKS // 2):
        acc = _dot(h, w_ref[:, c * 2 * LANES:(c + 1) * 2 * LANES])
        for half in range(2):
            blk = 2 * c + half
            a = acc[:, half * LANES:(half + 1) * LANES]
            if blk in ROPE_BLOCKS:
                a = a * cos + pltpu.roll(a, 32, 1) * sa + pltpu.roll(a, 96, 1) * sb
            if blk in Q_BLOCKS:
                a = a * Q_SCALE
            u_ref[:, blk * LANES:(blk + 1) * LANES] = a.astype(BF16)
    f = _dot(h, w_ref[:, U_WIDTH:W_ALL_WIDTH]) + fb_ref[...]
    lf_ref[...] = jnp.minimum(f, 0.0) - jnp.log(1.0 + jnp.exp(-jnp.abs(f)))


def _project(x2, norm_g, w_all, cos, sa, sb, fbias, seq):
    t = x2.shape[0]
    tm = ROW_TILE
    nseq = seq // tm
    row = lambda i: (i, 0)
    pos = lambda i: (i % nseq, 0)
    fixed = lambda i: (0, 0)
    return pl.pallas_call(
        _proj_kernel,
        grid=(t // tm,),
        in_specs=[
            pl.BlockSpec((tm, D_MODEL), row),
            pl.BlockSpec((1, D_MODEL), fixed),
            pl.BlockSpec((D_MODEL, W_ALL_WIDTH), fixed),
            pl.BlockSpec((tm, LANES), pos),
            pl.BlockSpec((tm, LANES), pos),
            pl.BlockSpec((tm, LANES), pos),
            pl.BlockSpec((1, LANES), fixed),
        ],
        out_specs=[pl.BlockSpec((tm, U_WIDTH), row), pl.BlockSpec((tm, LANES), row)],
        out_shape=[jax.ShapeDtypeStruct((t, U_WIDTH), BF16),
                   jax.ShapeDtypeStruct((t, LANES), F32)],
        compiler_params=_params("parallel"),
        name="proj",
    )(x2, norm_g, w_all, cos, sa, sb, fbias)


def _swa_kernel(sink_ref, q_ref, kc_ref, kp_ref, vc_ref, vp_ref, o_ref):
    n = pl.program_id(1)
    w = SWA_WINDOW
    group = SWA_Q_HEADS // SWA_KV_HEADS
    lane = lax.broadcasted_iota(jnp.int32, (1, LANES), 1)
    k_all = jnp.concatenate([kp_ref[0], kc_ref[0]], axis=0)
    v_all = jnp.concatenate([vp_ref[0], vc_ref[0]], axis=0)
    qi = lax.broadcasted_iota(jnp.int32, (group * w, 2 * w), 0) % w
    kj = lax.broadcasted_iota(jnp.int32, (group * w, 2 * w), 1)
    window = (kj > qi) & (kj <= qi + w)
    for a in range(SWA_STEP_BLOCKS):
        rows = slice(a * w, (a + 1) * w)
        k2 = k_all[a * w:(a + 2) * w]
        v2 = v_all[a * w:(a + 2) * w]
        valid = (window & ((kj >= w) | (n > 0))) if a == 0 else window
        outs = []
        for g in range(SWA_KV_HEADS):
            keep = (lane < HEAD_DIM) if g == 0 else (lane >= HEAD_DIM)
            qs = jnp.concatenate(
                [jnp.where(keep, q_ref[0, rows, j * LANES:(j + 1) * LANES], 0)
                 for j in range(group)], axis=0)
            s = jnp.where(valid, _dot_nt(qs, k2), NEG_INF)
            ps, inv = [], []
            for j in range(group):
                sj = s[j * w:(j + 1) * w]
                sink = sink_ref[g * group + j] * LOG2E
                m = jnp.maximum(jnp.max(sj, axis=1, keepdims=True), sink)
                p = jnp.exp2(sj - m)
                inv.append(1.0 / (jnp.sum(p, axis=1, keepdims=True) + jnp.exp2(sink - m)))
                ps.append(p.astype(BF16))
            o = _dot(jnp.concatenate(ps, axis=0), v2)
            outs.append([o[j * w:(j + 1) * w] * inv[j] for j in range(group)])
        for j in range(group):
            oj = jnp.where(lane < HEAD_DIM, outs[0][j], outs[1][j])
            o_ref[0, rows, j * LANES:(j + 1) * LANES] = oj.astype(BF16)


def _swa(u3, sinks):
    b, s, _ = u3.shape
    w = SWA_WINDOW
    nb = SWA_STEP_BLOCKS
    assert s % (nb * w) == 0
    cur = lambda blk: (lambda bi, n, sk: (bi, n, blk))
    prev = lambda blk: (lambda bi, n, sk: (bi, jnp.maximum(n * nb - 1, 0), blk))
    grid_spec = pltpu.PrefetchScalarGridSpec(
        num_scalar_prefetch=1,
        grid=(b, s // (nb * w)),
        in_specs=[
            pl.BlockSpec((1, nb * w, SWA_WIDTH), lambda bi, n, sk: (bi, n, 0)),
            pl.BlockSpec((1, nb * w, LANES), cur(4)),
            pl.BlockSpec((1, w, LANES), prev(4)),
            pl.BlockSpec((1, nb * w, LANES), cur(5)),
            pl.BlockSpec((1, w, LANES), prev(5)),
        ],
        out_specs=pl.BlockSpec((1, nb * w, SWA_WIDTH), lambda bi, n, sk: (bi, n, 0)),
    )
    return pl.pallas_call(
        _swa_kernel,
        grid_spec=grid_spec,
        out_shape=jax.ShapeDtypeStruct((b, s, SWA_WIDTH), BF16),
        compiler_params=_params("parallel", "parallel"),
        name="swa",
    )(sinks, u3, u3, u3, u3, u3)


def _head_lanes(hh):
    lane = lax.broadcasted_iota(jnp.int32, (1, LANES), 1)
    if hh == 0:
        return lane < HEAD_DIM, lane - HEAD_DIM
    return lane >= HEAD_DIM, lane


def _causal_attention(qp_scr, kp_scr, v_ref, o_ref, t):
    s_len = qp_scr.shape[1]
    lane = lax.broadcasted_iota(jnp.int32, (1, LANES), 1)
    row = lax.broadcasted_iota(jnp.int32, (t, t), 0)
    col = lax.broadcasted_iota(jnp.int32, (t, t), 1)
    for qi in range(s_len // t):
        rows = slice(qi * t, (qi + 1) * t)
        outs = []
        for hh in range(2):
            q = qp_scr[hh, rows, :]
            sd = jnp.where(col <= row, _dot_nt(q, kp_scr[hh, rows, :]), NEG_INF)
            m = jnp.max(sd, axis=1, keepdims=True)
            if qi:
                sp = _dot_nt(q, kp_scr[hh, :qi * t, :])
                m = jnp.maximum(m, jnp.max(sp, axis=1, keepdims=True))
            pd = jnp.exp2(sd - m)
            l = jnp.sum(pd, axis=1, keepdims=True)
            o = _dot(pd.astype(BF16), v_ref[0, rows, :])
            if qi:
                pp = jnp.exp2(sp - m)
                l = l + jnp.sum(pp, axis=1, keepdims=True)
                o = o + _dot(pp.astype(BF16), v_ref[0, :qi * t, :])
            outs.append(o * (1.0 / l))
        o_ref[0, rows, :] = jnp.where(lane < HEAD_DIM, outs[0], outs[1]).astype(BF16)


def _pair_attention(kernel_fn, u3, extra, q_blk, k_blk, v_blk, n_heads, name):
    b, s, _ = u3.shape
    pair = lambda blk: (lambda bi, p: (bi, 0, blk + p))
    seq_block = pl.BlockSpec((1, s, LANES), lambda bi, p: (bi, 0, 0))
    return pl.pallas_call(
        kernel_fn,
        grid=(b, n_heads // 2),
        in_specs=[pl.BlockSpec((1, s, LANES), pair(q_blk)),
                  pl.BlockSpec((1, s, LANES), pair(k_blk)),
                  pl.BlockSpec((1, s, LANES), pair(v_blk))] + [seq_block] * len(extra),
        out_specs=pl.BlockSpec((1, s, LANES), pair(0)),
        out_shape=jax.ShapeDtypeStruct((b, s, n_heads * HEAD_DIM), BF16),
        scratch_shapes=[pltpu.VMEM((2, s, LANES), BF16), pltpu.VMEM((2, s, LANES), BF16)],
        compiler_params=_params("parallel", "parallel"),
        name=name,
    )(u3, u3, u3, *extra)


def _fox_kernel(q_ref, k_ref, v_ref, lf_ref, o_ref, qp_scr, kp_scr):
    p = pl.program_id(1)
    s_len = q_ref.shape[1]
    t = ATT_TILE
    lane = lax.broadcasted_iota(jnp.int32, (1, LANES), 1)
    r = lax.broadcasted_iota(jnp.int32, (t, t), 0)
    cc = lax.broadcasted_iota(jnp.int32, (t, t), 1)
    tri = jnp.where(cc <= r, 1.0, 0.0).astype(BF16)
    carry = jnp.zeros((1, LANES), F32)
    for i in range(s_len // t):
        rows = slice(i * t, (i + 1) * t)
        lf = lf_ref[0, rows, :]
        lf0 = jnp.sum(jnp.where(lane == 2 * p, lf, 0.0), axis=1, keepdims=True)
        lf1 = jnp.sum(jnp.where(lane == 2 * p + 1, lf, 0.0), axis=1, keepdims=True)
        hi, mid, lo = _split3(jnp.where(lane < HEAD_DIM, lf1, lf0))
        c = (_dot(tri, hi) + _dot(tri, mid)) + _dot(tri, lo) + carry
        carry = c[t - 1:t, :]
        chi, cmid, clo = _split3(c * LOG2E)
        for hh in range(2):
            keep, f = _head_lanes(hh)
            ones_q = jnp.where((f >= 3) & (f < 6), 1.0, 0.0).astype(BF16)
            ones_k = jnp.where((f >= 0) & (f < 3), 1.0, 0.0).astype(BF16)
            qf = jnp.where(f == 0, chi, jnp.where(f == 1, cmid, jnp.where(f == 2, clo, ones_q)))
            kf = jnp.where(f == 3, -chi, jnp.where(f == 4, -cmid, jnp.where(f == 5, -clo, ones_k)))
            qp_scr[hh, rows, :] = jnp.where(keep, q_ref[0, rows, :], qf)
            kp_scr[hh, rows, :] = jnp.where(keep, k_ref[0, rows, :], kf)
    _causal_attention(qp_scr, kp_scr, v_ref, o_ref, t)


def _fox(u3, lf3):
    return _pair_attention(_fox_kernel, u3, (lf3,), 6, 8, 10, FOX_HEADS, "fox")


def _moba_kernel(q_ref, k_ref, v_ref, o_ref, qp_scr, kp_scr):
    s_len = q_ref.shape[1]
    t = MOBA_BLOCK
    nkb = s_len // t
    sub = 8
    assert nkb <= sub
    means = [jnp.mean(k_ref[0, j * t:(j + 1) * t, :].astype(F32), axis=0, keepdims=True)
             for j in range(nkb)]
    km_pair = jnp.concatenate(means + [jnp.zeros((LANES - nkb, LANES), F32)], axis=0)
    blk = lax.broadcasted_iota(jnp.int32, (sub, t), 0)
    for hh in range(2):
        keep, f = _head_lanes(hh)
        km = _split3(jnp.where(keep, km_pair, 0.0))
        for i in range(nkb):
            rows = slice(i * t, (i + 1) * t)
            q = q_ref[0, rows, :]
            g = ((_dot_nt(km[0], q) + _dot_nt(km[1], q)) + _dot_nt(km[2], q))[:sub, :]
            past = blk < i
            g = jnp.where(past, g, NEG_INF)
            rank = jnp.zeros((sub, t), jnp.int32)
            for r in range(i):
                gr = jnp.broadcast_to(g[r:r + 1, :], (sub, t))
                beats = (gr > g) | ((gr == g) & (r < blk))
                rank = rank + beats.astype(jnp.int32)
            drop = jnp.where(past & (rank >= MOBA_TOPK), 1.0, 0.0)
            below = [jnp.zeros((HEAD_DIM, t), F32)] if hh == 0 else []
            above = jnp.zeros((LANES - sub - (HEAD_DIM if hh == 0 else 0), t), F32)
            padded = jnp.concatenate(below + [drop, above], axis=0)
            kf = jnp.where(f == i, NEG_INF, 0.0).astype(BF16)
            qp_scr[hh, rows, :] = jnp.where(keep, q, padded.T.astype(BF16))
            kp_scr[hh, rows, :] = jnp.where(keep, k_ref[0, rows, :], kf)
    _causal_attention(qp_scr, kp_scr, v_ref, o_ref, t)


def _moba(u3):
    return _pair_attention(_moba_kernel, u3, (), 12, 14, 16, MOBA_HEADS, "moba")


def _rms_f32(x):
    x = x.astype(F32)
    return x * lax.rsqrt(jnp.mean(x * x, axis=-1, keepdims=True) + RMS_EPS)


def _outproj_kernel(oa_ref, ob_ref, oc_ref, gain_ref, w_ref, x_ref, h_ref):
    a0, a1, a2 = 0, SWA_WIDTH, SWA_WIDTH + FOX_WIDTH
    ya = (_rms_f32(oa_ref[...]) * gain_ref[:, a0:a1]).astype(BF16)
    yb = (_rms_f32(ob_ref[...]) * gain_ref[:, a1:a2]).astype(BF16)
    yc = (_rms_f32(oc_ref[...]) * gain_ref[:, a2:]).astype(BF16)
    y = (_dot(ya, w_ref[a0:a1, :]) + _dot(yb, w_ref[a1:a2, :])) + _dot(yc, w_ref[a2:, :])
    h_ref[...] = x_ref[...] + y


def _outproj(oa, ob, oc, gain, w_out, x2):
    t = x2.shape[0]
    tm = ROW_TILE
    row = lambda i: (i, 0)
    fixed = lambda i: (0, 0)
    return pl.pallas_call(
        _outproj_kernel,
        grid=(t // tm,),
        in_specs=[
            pl.BlockSpec((tm, SWA_WIDTH), row),
            pl.BlockSpec((tm, FOX_WIDTH), row),
            pl.BlockSpec((tm, MOBA_WIDTH), row),
            pl.BlockSpec((1, MIX_WIDTH), fixed),
            pl.BlockSpec((MIX_WIDTH, D_MODEL), fixed),
            pl.BlockSpec((tm, D_MODEL), row),
        ],
        out_specs=pl.BlockSpec((tm, D_MODEL), row),
        out_shape=jax.ShapeDtypeStruct((t, D_MODEL), F32),
        compiler_params=_params("parallel"),
        name="outproj",
    )(oa, ob, oc, gain, w_out, x2)


def _silu(x):
    return x / (1.0 + jnp.exp(-x))


def _ffn_kernel(h_ref, g_ref, wg_ref, wu_ref, wd_ref, o_ref, z_scr, acc_scr):
    j = pl.program_id(1)

    @pl.when(j == 0)
    def _():
        x = h_ref[...]
        inv = lax.rsqrt(jnp.mean(x * x, axis=-1, keepdims=True) + RMS_EPS)
        z_scr[...] = (x * inv * g_ref[...]).astype(BF16)

    z = z_scr[...]
    a = (_silu(_dot(z, wg_ref[...].astype(BF16))) * _dot(z, wu_ref[...].astype(BF16))).astype(BF16)
    y = _dot(a, wd_ref[...].astype(BF16))

    @pl.when(j == 0)
    def _():
        acc_scr[...] = y

    @pl.when(j > 0)
    def _():
        acc_scr[...] += y

    @pl.when(j == pl.num_programs(1) - 1)
    def _():
        o_ref[...] = h_ref[...] + acc_scr[...]


def _dense_ffn(h2, norm_g, w_gate, w_up, w_down):
    t = h2.shape[0]
    tm, tf = 2 * ROW_TILE, FF_TILE
    return pl.pallas_call(
        _ffn_kernel,
        grid=(t // tm, D_FF // tf),
        in_specs=[
            pl.BlockSpec((tm, D_MODEL), lambda i, j: (i, 0)),
            pl.BlockSpec((1, D_MODEL), lambda i, j: (0, 0)),
            pl.BlockSpec((D_MODEL, tf), lambda i, j: (0, j)),
            pl.BlockSpec((D_MODEL, tf), lambda i, j: (0, j)),
            pl.BlockSpec((tf, D_MODEL), lambda i, j: (j, 0)),
        ],
        out_specs=pl.BlockSpec((tm, D_MODEL), lambda i, j: (i, 0)),
        out_shape=jax.ShapeDtypeStruct((t, D_MODEL), F32),
        scratch_shapes=[pltpu.VMEM((tm, D_MODEL), BF16), pltpu.VMEM((tm, D_MODEL), F32)],
        compiler_params=_params("parallel", "arbitrary"),
        name="dense_ffn",
    )(h2, norm_g, w_gate, w_up, w_down)


def _router_kernel(h_ref, g_ref, wr_ref, z_ref, idx_ref, gate_ref):
    tm = h_ref.shape[0]
    x = h_ref[...]
    inv = lax.rsqrt(jnp.mean(x * x, axis=-1, keepdims=True) + RMS_EPS)
    z = x * inv * g_ref[...]
    for c in range(CHUNKS):
        z_ref[pl.ds(c, tm, stride=CHUNKS), :] = z[:, c * LANES:(c + 1) * LANES]
    zh, zm, zl = _split3(z)
    wh, wm, wl = wr_ref[0], wr_ref[1], wr_ref[2]
    logits = ((_dot(zh, wh) + (_dot(zh, wm) + _dot(zm, wh)))
              + ((_dot(zh, wl) + _dot(zl, wh)) + _dot(zm, wm)))
    lane = lax.broadcasted_iota(jnp.int32, (tm, LANES), 1)
    logits = jnp.where(lane < N_EXPERTS, logits, -jnp.inf)
    m1 = jnp.max(logits, axis=1, keepdims=True)
    i1 = jnp.min(jnp.where(logits == m1, lane, LANES), axis=1, keepdims=True)
    rest = jnp.where(lane == i1, -jnp.inf, logits)
    m2 = jnp.max(rest, axis=1, keepdims=True)
    i2 = jnp.min(jnp.where(rest == m2, lane, LANES), axis=1, keepdims=True)
    e2 = jnp.exp(m2 - m1)
    g1 = 1.0 / (1.0 + e2)
    idx_ref[...] = jnp.where(lane == 0, i1, i2)
    gate_ref[...] = jnp.where(lane == 0, g1, e2 * g1)


def _router(h2, norm_g, wr3):
    t = h2.shape[0]
    tm = ROW_TILE
    row = lambda i: (i, 0)
    return pl.pallas_call(
        _router_kernel,
        grid=(t // tm,),
        in_specs=[
            pl.BlockSpec((tm, D_MODEL), row),
            pl.BlockSpec((1, D_MODEL), lambda i: (0, 0)),
            pl.BlockSpec((3, D_MODEL, LANES), lambda i: (0, 0, 0)),
        ],
        out_specs=[pl.BlockSpec((tm * CHUNKS, LANES), row),
                   pl.BlockSpec((tm, LANES), row),
                   pl.BlockSpec((tm, LANES), row)],
        out_shape=[jax.ShapeDtypeStruct((t * CHUNKS, LANES), F32),
                   jax.ShapeDtypeStruct((t, LANES), jnp.int32),
                   jax.ShapeDtypeStruct((t, LANES), F32)],
        compiler_params=_params("parallel"),
        name="router",
    )(h2, norm_g, wr3)


def _row_copy(src, src_row, dst, dst_row, sem):
    return pltpu.make_async_copy(src.at[pl.ds(src_row * CHUNKS, CHUNKS), :],
                                 dst.at[pl.ds(dst_row * CHUNKS, CHUNKS), :], sem)


def _rows_wait(src, dst, rows, sem):
    pltpu.make_async_copy(src.at[pl.ds(0, rows * CHUNKS), :],
                          dst.at[pl.ds(0, rows * CHUNKS), :], sem).wait()


def _dispatch_kernel(dest_ref, fill_ref, z_ref, x_hbm, zero_scr, fsem, sem):
    i = pl.program_id(0)
    blk = MOE_BLOCK
    tile = z_ref.shape[0] // CHUNKS

    @pl.when(i == 0)
    def _():
        zero_scr[...] = jnp.zeros_like(zero_scr)
        for e in range(fill_ref.shape[0]):
            @pl.when(fill_ref[e] >= 0)
            def _():
                start = pl.multiple_of(fill_ref[e] * CHUNKS, blk * CHUNKS)
                pltpu.make_async_copy(zero_scr, x_hbm.at[pl.ds(start, blk * CHUNKS), :], fsem).start()
        for e in range(fill_ref.shape[0]):
            @pl.when(fill_ref[e] >= 0)
            def _():
                _rows_wait(zero_scr, x_hbm, blk, fsem)

    base = i * tile * TOP_K

    def trip(it, c):
        for k in range(MOE_ISSUE_UNROLL):
            r = it * MOE_ISSUE_UNROLL + k
            _row_copy(z_ref, r // TOP_K, x_hbm, dest_ref[base + r], sem).start()
        return c
    lax.fori_loop(0, tile * TOP_K // MOE_ISSUE_UNROLL, trip, 0)
    for _ in range(TOP_K):
        _rows_wait(z_ref, x_hbm, tile, sem)


def _dispatch(z8, dest, fill, p_rows):
    tile = DISPATCH_TILE
    assert z8.shape[0] % (tile * CHUNKS) == 0
    grid_spec = pltpu.PrefetchScalarGridSpec(
        num_scalar_prefetch=2,
        grid=(z8.shape[0] // (tile * CHUNKS),),
        in_specs=[pl.BlockSpec((tile * CHUNKS, LANES), lambda i, d, f: (i, 0))],
        out_specs=pl.BlockSpec(memory_space=pl.ANY),
        scratch_shapes=[pltpu.VMEM((MOE_BLOCK * CHUNKS, LANES), F32),
                        pltpu.SemaphoreType.DMA(()), pltpu.SemaphoreType.DMA(())],
    )
    return pl.pallas_call(
        _dispatch_kernel,
        grid_spec=grid_spec,
        out_shape=jax.ShapeDtypeStruct((p_rows * CHUNKS, LANES), F32),
        compiler_params=_params("arbitrary"),
        name="moe_dispatch",
    )(dest, fill, z8)


def _moe_kernel(be_ref, live_ref, bx_ref, x_ref, wg_ref, wu_ref, wd_ref, y_ref, xb, acc):
    i = pl.program_id(0)
    j = pl.program_id(1)
    tm = xb.shape[0]

    @pl.when((live_ref[i] == 0) & (j == 0))
    def _():
        y_ref[...] = jnp.zeros_like(y_ref)

    @pl.when(live_ref[i] > 0)
    def _():
        @pl.when(j == 0)
        def _():
            for c in range(CHUNKS):
                xb[:, c * LANES:(c + 1) * LANES] = x_ref[pl.ds(c, tm, stride=CHUNKS), :].astype(BF16)

        x = xb[...]
        a = (_silu(_dot(x, wg_ref[0])) * _dot(x, wu_ref[0])).astype(BF16)
        y = _dot(a, wd_ref[0])

        @pl.when(j == 0)
        def _():
            acc[...] = y

        @pl.when(j > 0)
        def _():
            acc[...] += y

        @pl.when(j == pl.num_programs(1) - 1)
        def _():
            for c in range(CHUNKS):
                y_ref[pl.ds(c, tm, stride=CHUNKS), :] = acc[:, c * LANES:(c + 1) * LANES]


def _moe_experts(x8, blk_e, blk_live, blk_x, w_gate, w_up, w_down):
    tm, tf = MOE_BLOCK, MOE_FF_TILE
    nblk = blk_e.shape[0]
    nff = D_FF // tf
    ff = lambda i, j, live: jnp.where(live[i] > 0, j, nff - 1)
    grid_spec = pltpu.PrefetchScalarGridSpec(
        num_scalar_prefetch=3,
        grid=(nblk, nff),
        in_specs=[
            pl.BlockSpec((tm * CHUNKS, LANES), lambda i, j, be, lv, bx: (bx[i], 0)),
            pl.BlockSpec((1, D_MODEL, tf), lambda i, j, be, lv, bx: (be[i], 0, ff(i, j, lv))),
            pl.BlockSpec((1, D_MODEL, tf), lambda i, j, be, lv, bx: (be[i], 0, ff(i, j, lv))),
            pl.BlockSpec((1, tf, D_MODEL), lambda i, j, be, lv, bx: (be[i], ff(i, j, lv), 0)),
        ],
        out_specs=pl.BlockSpec((tm * CHUNKS, LANES), lambda i, j, be, lv, bx: (i, 0)),
        scratch_shapes=[pltpu.VMEM((tm, D_MODEL), BF16), pltpu.VMEM((tm, D_MODEL), F32)],
    )
    return pl.pallas_call(
        _moe_kernel,
        grid_spec=grid_spec,
        out_shape=jax.ShapeDtypeStruct(x8.shape, F32),
        compiler_params=_params("arbitrary", "arbitrary"),
        name="moe_experts",
    )(blk_e, blk_live, blk_x, x8, w_gate, w_up, w_down)


def _combine_kernel(dest_ref, h_ref, gate_ref, *rest, final_norm):
    if final_norm:
        g_ref, y_hbm, o_ref, ybuf, sem = rest
    else:
        y_hbm, o_ref, ybuf, sem = rest
    i = pl.program_id(0)
    tm = h_ref.shape[0]
    rows = tm * TOP_K
    stride = TOP_K * CHUNKS

    def pull(tile, slot):
        def trip(it, c):
            for k in range(MOE_ISSUE_UNROLL):
                r = it * MOE_ISSUE_UNROLL + k
                _row_copy(y_hbm, dest_ref[tile * rows + r], ybuf.at[slot], r, sem.at[slot]).start()
            return c
        lax.fori_loop(0, rows // MOE_ISSUE_UNROLL, trip, 0)

    @pl.when(i == 0)
    def _():
        pull(0, 0)

    @pl.when(i + 1 < pl.num_programs(0))
    def _():
        pull(i + 1, (i + 1) % 2)

    slot = i % 2
    yb = ybuf.at[slot]
    _rows_wait(y_hbm, yb, rows, sem.at[slot])
    gates = [gate_ref[:, k:k + 1] for k in range(TOP_K)]
    parts = []
    ss = jnp.zeros((tm, 1), F32)
    for c in range(CHUNKS):
        y = gates[0] * yb[pl.ds(c, tm, stride=stride), :]
        for k in range(1, TOP_K):
            y = y + gates[k] * yb[pl.ds(k * CHUNKS + c, tm, stride=stride), :]
        hc = h_ref[:, c * LANES:(c + 1) * LANES] + y
        parts.append(hc)
        ss = ss + jnp.sum(hc * hc, axis=1, keepdims=True)
    if final_norm:
        inv = lax.rsqrt(ss / D_MODEL + RMS_EPS)
    for c in range(CHUNKS):
        cols = slice(c * LANES, (c + 1) * LANES)
        o_ref[:, cols] = parts[c] * inv * g_ref[:, cols] if final_norm else parts[c]


def _combine(h2, y8, dest, gate, out_g):
    t = h2.shape[0]
    tm = ROW_TILE
    final_norm = out_g is not None
    row = lambda i, d: (i, 0)
    in_specs = [pl.BlockSpec((tm, D_MODEL), row), pl.BlockSpec((tm, LANES), row)]
    args = [h2, gate]
    if final_norm:
        in_specs.append(pl.BlockSpec((1, D_MODEL), lambda i, d: (0, 0)))
        args.append(out_g.reshape(1, D_MODEL))
    grid_spec = pltpu.PrefetchScalarGridSpec(
        num_scalar_prefetch=1,
        grid=(t // tm,),
        in_specs=in_specs + [pl.BlockSpec(memory_space=pl.ANY)],
        out_specs=pl.BlockSpec((tm, D_MODEL), row),
        scratch_shapes=[pltpu.VMEM((2, tm * TOP_K * CHUNKS, LANES), F32),
                        pltpu.SemaphoreType.DMA((2,))],
    )
    return pl.pallas_call(
        functools.partial(_combine_kernel, final_norm=final_norm),
        grid_spec=grid_spec,
        out_shape=jax.ShapeDtypeStruct((t, D_MODEL), F32),
        compiler_params=_params("arbitrary"),
        name="moe_combine",
    )(dest, *args, y8)


def _norm_kernel(h_ref, g_ref, o_ref):
    x = h_ref[...]
    o_ref[...] = x * lax.rsqrt(jnp.mean(x * x, axis=-1, keepdims=True) + RMS_EPS) * g_ref[...]


def _final_norm(h2, norm_g):
    t = h2.shape[0]
    tm = ROW_TILE
    return pl.pallas_call(
        _norm_kernel,
        grid=(t // tm,),
        in_specs=[pl.BlockSpec((tm, D_MODEL), lambda i: (i, 0)),
                  pl.BlockSpec((1, D_MODEL), lambda i: (0, 0))],
        out_specs=pl.BlockSpec((tm, D_MODEL), lambda i: (i, 0)),
        out_shape=jax.ShapeDtypeStruct((t, D_MODEL), F32),
        compiler_params=_params("parallel"),
        name="final_norm",
    )(h2, norm_g)


def _dispatch_plan(top_i, t):
    n = t * TOP_K
    tm = MOE_BLOCK
    e_flat = top_i.reshape(n)
    onehot = (e_flat[:, None] == jnp.arange(N_EXPERTS)[None, :]).astype(jnp.int32)
    counts = jnp.sum(onehot, axis=0)
    rank = jnp.sum((jnp.cumsum(onehot, axis=0) - onehot) * onehot, axis=1)
    padded = (counts + tm - 1) // tm * tm
    pends = jnp.cumsum(padded)
    pstarts = pends - padded
    dest = (pstarts[e_flat] + rank).astype(jnp.int32)
    p_rows = -(-n // tm) * tm + N_EXPERTS * tm
    nblk = p_rows // tm
    blk = jnp.arange(nblk, dtype=jnp.int32)
    blk_e = jnp.minimum(jnp.sum((blk * tm)[:, None] >= pends[None, :], axis=1), N_EXPERTS - 1)
    live = blk * tm < pends[-1]
    last_live = jnp.maximum(pends[-1] // tm - 1, 0)
    blk_e = jnp.where(live, blk_e, blk_e[last_live])
    blk_x = jnp.where(live, blk, 0)
    tail = jnp.where(padded > 0, pends - tm, -1)
    spare = pends[-1] + jnp.arange(N_EXPERTS) * tm
    fill = jnp.concatenate([tail, jnp.where(spare < p_rows, spare, -1)])
    i32 = lambda a: a.astype(jnp.int32)
    return dest, i32(fill), i32(blk_e), i32(live), i32(blk_x), p_rows


def _rope_tables(seq):
    inv = 1.0 / (ROPE_THETA ** (jnp.arange(0, HEAD_DIM, 2, dtype=F32) / HEAD_DIM))
    ang = jnp.arange(seq, dtype=F32)[:, None] * inv[None, :]
    cos = jnp.concatenate([jnp.cos(ang)] * 4, axis=-1)
    sin = jnp.concatenate([jnp.sin(ang)] * 4, axis=-1)
    upper = (jnp.arange(LANES) % HEAD_DIM) >= HEAD_DIM // 2
    sa = jnp.where(upper[None, :], sin, 0.0)
    sb = jnp.where(upper[None, :], 0.0, -sin)
    return cos, sa, sb


def _in_proj_columns():
    split = np.cumsum([0, SWA_WIDTH, SWA_KV_WIDTH, SWA_KV_WIDTH, FOX_WIDTH, FOX_WIDTH, FOX_WIDTH,
                       FOX_HEADS, MOBA_WIDTH, MOBA_WIDTH, MOBA_WIDTH])
    aq, ak, av, bq, bk, bv, bf, cq, ck, cv = split[:-1]
    half = SWA_Q_HEADS // 2
    cols = []
    for j in range(half):
        cols += list(range(aq + j * HEAD_DIM, aq + (j + 1) * HEAD_DIM))
        cols += list(range(aq + (j + half) * HEAD_DIM, aq + (j + half + 1) * HEAD_DIM))
    for start, width in ((ak, SWA_KV_WIDTH), (av, SWA_KV_WIDTH), (bq, FOX_WIDTH), (bk, FOX_WIDTH),
                         (bv, FOX_WIDTH), (cq, MOBA_WIDTH), (ck, MOBA_WIDTH), (cv, MOBA_WIDTH)):
        cols += list(range(start, start + width))
    forget = list(range(bf, bf + FOX_HEADS))
    return np.array(cols, np.int32), np.array(forget, np.int32)


def _swa_out_perm():
    half = SWA_Q_HEADS // 2
    perm = []
    for j in range(half):
        perm += list(range(j * HEAD_DIM, (j + 1) * HEAD_DIM))
        perm += list(range((j + half) * HEAD_DIM, (j + half + 1) * HEAD_DIM))
    return np.array(perm + list(range(SWA_WIDTH, MIX_WIDTH)), np.int32)


def _mixer(h2, b, s, norm_g, w_in, forget_bias, sinks, mix_gain, w_out, tables):
    cols, fcols = _in_proj_columns()
    w_all = jnp.concatenate(
        [w_in[:, cols], jnp.pad(w_in[:, fcols], ((0, 0), (0, LANES - FOX_HEADS)))], axis=1
    ).astype(BF16)
    fbias = jnp.pad(forget_bias.astype(F32), (0, LANES - FOX_HEADS)).reshape(1, LANES)
    u, lf = _project(h2, norm_g.reshape(1, D_MODEL), w_all, *tables, fbias, s)
    u3 = u.reshape(b, s, U_WIDTH)
    oa = _swa(u3, sinks.astype(F32))
    ob = _fox(u3, lf.reshape(b, s, LANES))
    oc = _moba(u3)
    perm = _swa_out_perm()
    t = b * s
    return _outproj(oa.reshape(t, SWA_WIDTH), ob.reshape(t, FOX_WIDTH), oc.reshape(t, MOBA_WIDTH),
                    mix_gain[perm].reshape(1, MIX_WIDTH).astype(F32),
                    w_out[perm].astype(BF16), h2)


def _moe_ffn(h2, norm_g, w_router, w_gate, w_up, w_down, out_g):
    t = h2.shape[0]
    wr = jnp.pad(w_router.astype(F32), ((0, 0), (0, LANES - N_EXPERTS)))
    wr3 = jnp.stack(_split3(wr))
    z8, top, gate = _router(h2, norm_g.reshape(1, D_MODEL), wr3)
    dest, fill, blk_e, blk_live, blk_x, p_rows = _dispatch_plan(top[:, :TOP_K], t)
    x8 = _dispatch(z8, dest, fill, p_rows)
    y8 = _moe_experts(x8, blk_e, blk_live, blk_x,
                      w_gate.astype(BF16), w_up.astype(BF16), w_down.astype(BF16))
    return _combine(h2, y8, dest, gate, out_g)


def kernel(x, attn_norm, w_in, fox_forget_bias, swa_sinks, mix_gain, w_out, ffn_norm,
           dense_w_gate, dense_w_up, dense_w_down, router_w, moe_w_gate, moe_w_up,
           moe_w_down, final_norm):
    b, s, d = x.shape
    depth = attn_norm.shape[0]
    assert d == D_MODEL and s % MOBA_BLOCK == 0 and s % ROW_TILE == 0
    tables = _rope_tables(s)
    h = x.reshape(b * s, d)
    normed = False
    for layer in range(depth):
        h = _mixer(h, b, s, attn_norm[layer], w_in[layer], fox_forget_bias[layer],
                   swa_sinks[layer], mix_gain[layer], w_out[layer], tables)
        j = layer // 2
        last = layer == depth - 1
        if layer % 2 == 0:
            h = _dense_ffn(h, ffn_norm[layer].reshape(1, d), dense_w_gate[j], dense_w_up[j],
                           dense_w_down[j])
        else:
            h = _moe_ffn(h, ffn_norm[layer], router_w[j], moe_w_gate[j], moe_w_up[j],
                         moe_w_down[j], final_norm if last else None)
            normed = last
    if not normed:
        h = _final_norm(h, final_norm.reshape(1, d))
    return h.reshape(b, s, d)
```

```python
import functools

import jax
import jax.numpy as jnp
import numpy as np
from jax import lax
from jax.experimental import pallas as pl
from jax.experimental.pallas import tpu as pltpu

F32 = jnp.float32
BF16 = jnp.bfloat16

D_MODEL = 1024
HEAD_DIM = 64
LANES = 128
SWA_Q_HEADS = 8
SWA_KV_HEADS = 2
SWA_WINDOW = 128
FOX_HEADS = 4
MOBA_HEADS = 4
MOBA_BLOCK = 256
MOBA_TOPK = 3
ROPE_THETA = 10000.0
RMS_EPS = 1e-5
D_FF = 3584
N_EXPERTS = 8
TOP_K = 2
MOE_BLOCK = 512
NEG_INF = -1e30
ATTN_SCALE = HEAD_DIM ** -0.5
LOG2E = 1.4426950408889634
Q_SCALE = ATTN_SCALE * LOG2E

SWA_WIDTH = SWA_Q_HEADS * HEAD_DIM
SWA_KV_WIDTH = SWA_KV_HEADS * HEAD_DIM
FOX_WIDTH = FOX_HEADS * HEAD_DIM
MOBA_WIDTH = MOBA_HEADS * HEAD_DIM
MIX_WIDTH = SWA_WIDTH + FOX_WIDTH + MOBA_WIDTH

U_BLOCKS = 18
U_WIDTH = U_BLOCKS * LANES
ROPE_BLOCKS = (0, 1, 2, 3, 4, 12, 13, 14, 15)
Q_BLOCKS = (0, 1, 2, 3, 6, 7, 12, 13)
W_ALL_WIDTH = U_WIDTH + LANES

VMEM_LIMIT = 56 * 1024 * 1024

ROW_TILE = 512
ATT_TILE = 256
SWA_STEP_BLOCKS = 4
FF_TILE = 512
MOE_FF_TILE = 1792
CHUNKS = D_MODEL // LANES
MOE_ISSUE_UNROLL = 8
DISPATCH_TILE = 512


def _params(*sem):
    return pltpu.CompilerParams(dimension_semantics=sem, vmem_limit_bytes=VMEM_LIMIT)


def _split3(x):
    hi = x.astype(BF16)
    r1 = x - hi.astype(F32)
    mid = r1.astype(BF16)
    lo = (r1 - mid.astype(F32)).astype(BF16)
    return hi, mid, lo


def _dot_nt(a, b):
    return lax.dot_general(a, b, (((1,), (1,)), ((), ())), preferred_element_type=F32)


def _dot(a, b):
    return jnp.dot(a, b, preferred_element_type=F32)


def _proj_kernel(x_ref, g_ref, w_ref, cos_ref, sa_ref, sb_ref, fb_ref, u_ref, lf_ref):
    x = x_ref[...]
    inv = lax.rsqrt(jnp.mean(x * x, axis=-1, keepdims=True) + RMS_EPS)
    h = (x * inv * g_ref[...]).astype(BF16)
    cos = cos_ref[...]
    sa = sa_ref[...]
    sb = sb_ref[...]
    for c in range(U_BLOCKS // 2):
        acc = _dot(h, w_ref[:, c * 2 * LANES:(c + 1) * 2 * LANES])
        for half in range(2):
            blk = 2 * c + half
            a = acc[:, half * LANES:(half + 1) * LANES]
            if blk in ROPE_BLOCKS:
                a = a * cos + pltpu.roll(a, 32, 1) * sa + pltpu.roll(a, 96, 1) * sb
            if blk in Q_BLOCKS:
                a = a * Q_SCALE
            u_ref[:, blk * LANES:(blk + 1) * LANES] = a.astype(BF16)
    f = _dot(h, w_ref[:, U_WIDTH:W_ALL_WIDTH]) + fb_ref[...]
    lf_ref[...] = jnp.minimum(f, 0.0) - jnp.log(1.0 + jnp.exp(-jnp.abs(f)))


def _project(x2, norm_g, w_all, cos, sa, sb, fbias, seq):
    t = x2.shape[0]
    tm = ROW_TILE
    nseq = seq // tm
    row = lambda i: (i, 0)
    pos = lambda i: (i % nseq, 0)
    fixed = lambda i: (0, 0)
    return pl.pallas_call(
        _proj_kernel,
        grid=(t // tm,),
        in_specs=[
            pl.BlockSpec((tm, D_MODEL), row),
            pl.BlockSpec((1, D_MODEL), fixed),
            pl.BlockSpec((D_MODEL, W_ALL_WIDTH), fixed),
            pl.BlockSpec((tm, LANES), pos),
            pl.BlockSpec((tm, LANES), pos),
            pl.BlockSpec((tm, LANES), pos),
            pl.BlockSpec((1, LANES), fixed),
        ],
        out_specs=[pl.BlockSpec((tm, U_WIDTH), row), pl.BlockSpec((tm, LANES), row)],
        out_shape=[jax.ShapeDtypeStruct((t, U_WIDTH), BF16),
                   jax.ShapeDtypeStruct((t, LANES), F32)],
        compiler_params=_params("parallel"),
        name="proj",
    )(x2, norm_g, w_all, cos, sa, sb, fbias)


def _swa_kernel(sink_ref, q_ref, kc_ref, kp_ref, vc_ref, vp_ref, o_ref):
    n = pl.program_id(1)
    w = SWA_WINDOW
    group = SWA_Q_HEADS // SWA_KV_HEADS
    lane = lax.broadcasted_iota(jnp.int32, (1, LANES), 1)
    k_all = jnp.concatenate([kp_ref[0], kc_ref[0]], axis=0)
    v_all = jnp.concatenate([vp_ref[0], vc_ref[0]], axis=0)
    qi = lax.broadcasted_iota(jnp.int32, (group * w, 2 * w), 0) % w
    kj = lax.broadcasted_iota(jnp.int32, (group * w, 2 * w), 1)
    window = (kj > qi) & (kj <= qi + w)
    for a in range(SWA_STEP_BLOCKS):
        rows = slice(a * w, (a + 1) * w)
        k2 = k_all[a * w:(a + 2) * w]
        v2 = v_all[a * w:(a + 2) * w]
        valid = (window & ((kj >= w) | (n > 0))) if a == 0 else window
        outs = []
        for g in range(SWA_KV_HEADS):
            keep = (lane < HEAD_DIM) if g == 0 else (lane >= HEAD_DIM)
            qs = jnp.concatenate(
                [jnp.where(keep, q_ref[0, rows, j * LANES:(j + 1) * LANES], 0)
                 for j in range(group)], axis=0)
            s = jnp.where(valid, _dot_nt(qs, k2), NEG_INF)
            ps, inv = [], []
            for j in range(group):
                sj = s[j * w:(j + 1) * w]
                sink = sink_ref[g * group + j] * LOG2E
                m = jnp.maximum(jnp.max(sj, axis=1, keepdims=True), sink)
                p = jnp.exp2(sj - m)
                inv.append(1.0 / (jnp.sum(p, axis=1, keepdims=True) + jnp.exp2(sink - m)))
                ps.append(p.astype(BF16))
            o = _dot(jnp.concatenate(ps, axis=0), v2)
            outs.append([o[j * w:(j + 1) * w] * inv[j] for j in range(group)])
        for j in range(group):
            oj = jnp.where(lane < HEAD_DIM, outs[0][j], outs[1][j])
            o_ref[0, rows, j * LANES:(j + 1) * LANES] = oj.astype(BF16)


def _swa(u3, sinks):
    b, s, _ = u3.shape
    w = SWA_WINDOW
    nb = SWA_STEP_BLOCKS
    assert s % (nb * w) == 0
    cur = lambda blk: (lambda bi, n, sk: (bi, n, blk))
    prev = lambda blk: (lambda bi, n, sk: (bi, jnp.maximum(n * nb - 1, 0), blk))
    grid_spec = pltpu.PrefetchScalarGridSpec(
        num_scalar_prefetch=1,
        grid=(b, s // (nb * w)),
        in_specs=[
            pl.BlockSpec((1, nb * w, SWA_WIDTH), lambda bi, n, sk: (bi, n, 0)),
            pl.BlockSpec((1, nb * w, LANES), cur(4)),
            pl.BlockSpec((1, w, LANES), prev(4)),
            pl.BlockSpec((1, nb * w, LANES), cur(5)),
            pl.BlockSpec((1, w, LANES), prev(5)),
        ],
        out_specs=pl.BlockSpec((1, nb * w, SWA_WIDTH), lambda bi, n, sk: (bi, n, 0)),
    )
    return pl.pallas_call(
        _swa_kernel,
        grid_spec=grid_spec,
        out_shape=jax.ShapeDtypeStruct((b, s, SWA_WIDTH), BF16),
        compiler_params=_params("parallel", "parallel"),
        name="swa",
    )(sinks, u3, u3, u3, u3, u3)


def _head_lanes(hh):
    lane = lax.broadcasted_iota(jnp.int32, (1, LANES), 1)
    if hh == 0:
        return lane < HEAD_DIM, lane - HEAD_DIM
    return lane >= HEAD_DIM, lane


def _causal_attention(qp_scr, kp_scr, v_ref, o_ref, t):
    s_len = qp_scr.shape[1]
    lane = lax.broadcasted_iota(jnp.int32, (1, LANES), 1)
    row = lax.broadcasted_iota(jnp.int32, (t, t), 0)
    col = lax.broadcasted_iota(jnp.int32, (t, t), 1)
    for qi in range(s_len // t):
        rows = slice(qi * t, (qi + 1) * t)
        outs = []
        for hh in range(2):
            q = qp_scr[hh, rows, :]
            sd = jnp.where(col <= row, _dot_nt(q, kp_scr[hh, rows, :]), NEG_INF)
            m = jnp.max(sd, axis=1, keepdims=True)
            if qi:
                sp = _dot_nt(q, kp_scr[hh, :qi * t, :])
                m = jnp.maximum(m, jnp.max(sp, axis=1, keepdims=True))
            pd = jnp.exp2(sd - m)
            l = jnp.sum(pd, axis=1, keepdims=True)
            o = _dot(pd.astype(BF16), v_ref[0, rows, :])
            if qi:
                pp = jnp.exp2(sp - m)
                l = l + jnp.sum(pp, axis=1, keepdims=True)
                o = o + _dot(pp.astype(BF16), v_ref[0, :qi * t, :])
            outs.append(o * (1.0 / l))
        o_ref[0, rows, :] = jnp.where(lane < HEAD_DIM, outs[0], outs[1]).astype(BF16)


def _pair_attention(kernel_fn, u3, extra, q_blk, k_blk, v_blk, n_heads, name):
    b, s, _ = u3.shape
    pair = lambda blk: (lambda bi, p: (bi, 0, blk + p))
    seq_block = pl.BlockSpec((1, s, LANES), lambda bi, p: (bi, 0, 0))
    return pl.pallas_call(
        kernel_fn,
        grid=(b, n_heads // 2),
        in_specs=[pl.BlockSpec((1, s, LANES), pair(q_blk)),
                  pl.BlockSpec((1, s, LANES), pair(k_blk)),
                  pl.BlockSpec((1, s, LANES), pair(v_blk))] + [seq_block] * len(extra),
        out_specs=pl.BlockSpec((1, s, LANES), pair(0)),
        out_shape=jax.ShapeDtypeStruct((b, s, n_heads * HEAD_DIM), BF16),
        scratch_shapes=[pltpu.VMEM((2, s, LANES), BF16), pltpu.VMEM((2, s, LANES), BF16)],
        compiler_params=_params("parallel", "parallel"),
        name=name,
    )(u3, u3, u3, *extra)


def _fox_kernel(q_ref, k_ref, v_ref, lf_ref, o_ref, qp_scr, kp_scr):
    p = pl.program_id(1)
    s_len = q_ref.shape[1]
    t = ATT_TILE
    lane = lax.broadcasted_iota(jnp.int32, (1, LANES), 1)
    r = lax.broadcasted_iota(jnp.int32, (t, t), 0)
    cc = lax.broadcasted_iota(jnp.int32, (t, t), 1)
    tri = jnp.where(cc <= r, 1.0, 0.0).astype(BF16)
    carry = jnp.zeros((1, LANES), F32)
    for i in range(s_len // t):
        rows = slice(i * t, (i + 1) * t)
        lf = lf_ref[0, rows, :]
        lf0 = jnp.sum(jnp.where(lane == 2 * p, lf, 0.0), axis=1, keepdims=True)
        lf1 = jnp.sum(jnp.where(lane == 2 * p + 1, lf, 0.0), axis=1, keepdims=True)
        hi, mid, lo = _split3(jnp.where(lane < HEAD_DIM, lf1, lf0))
        c = (_dot(tri, hi) + _dot(tri, mid)) + _dot(tri, lo) + carry
        carry = c[t - 1:t, :]
        chi, cmid, clo = _split3(c * LOG2E)
        for hh in range(2):
            keep, f = _head_lanes(hh)
            ones_q = jnp.where((f >= 3) & (f < 6), 1.0, 0.0).astype(BF16)
            ones_k = jnp.where((f >= 0) & (f < 3), 1.0, 0.0).astype(BF16)
            qf = jnp.where(f == 0, chi, jnp.where(f == 1, cmid, jnp.where(f == 2, clo, ones_q)))
            kf = jnp.where(f == 3, -chi, jnp.where(f == 4, -cmid, jnp.where(f == 5, -clo, ones_k)))
            qp_scr[hh, rows, :] = jnp.where(keep, q_ref[0, rows, :], qf)
            kp_scr[hh, rows, :] = jnp.where(keep, k_ref[0, rows, :], kf)
    _causal_attention(qp_scr, kp_scr, v_ref, o_ref, t)


def _fox(u3, lf3):
    return _pair_attention(_fox_kernel, u3, (lf3,), 6, 8, 10, FOX_HEADS, "fox")


def _moba_kernel(q_ref, k_ref, v_ref, o_ref, qp_scr, kp_scr):
    s_len = q_ref.shape[1]
    t = MOBA_BLOCK
    nkb = s_len // t
    sub = 8
    assert nkb <= sub
    means = [jnp.mean(k_ref[0, j * t:(j + 1) * t, :].astype(F32), axis=0, keepdims=True)
             for j in range(nkb)]
    km_pair = jnp.concatenate(means + [jnp.zeros((LANES - nkb, LANES), F32)], axis=0)
    blk = lax.broadcasted_iota(jnp.int32, (sub, t), 0)
    for hh in range(2):
        keep, f = _head_lanes(hh)
        km = _split3(jnp.where(keep, km_pair, 0.0))
        for i in range(nkb):
            rows = slice(i * t, (i + 1) * t)
            q = q_ref[0, rows, :]
            g = ((_dot_nt(km[0], q) + _dot_nt(km[1], q)) + _dot_nt(km[2], q))[:sub, :]
            past = blk < i
            g = jnp.where(past, g, NEG_INF)
            rank = jnp.zeros((sub, t), jnp.int32)
            for r in range(i):
                gr = jnp.broadcast_to(g[r:r + 1, :], (sub, t))
                beats = (gr > g) | ((gr == g) & (r < blk))
                rank = rank + beats.astype(jnp.int32)
            drop = jnp.where(past & (rank >= MOBA_TOPK), 1.0, 0.0)
            below = [jnp.zeros((HEAD_DIM, t), F32)] if hh == 0 else []
            above = jnp.zeros((LANES - sub - (HEAD_DIM if hh == 0 else 0), t), F32)
            padded = jnp.concatenate(below + [drop, above], axis=0)
            kf = jnp.where(f == i, NEG_INF, 0.0).astype(BF16)
            qp_scr[hh, rows, :] = jnp.where(keep, q, padded.T.astype(BF16))
            kp_scr[hh, rows, :] = jnp.where(keep, k_ref[0, rows, :], kf)
    _causal_attention(qp_scr, kp_scr, v_ref, o_ref, t)


def _moba(u3):
    return _pair_attention(_moba_kernel, u3, (), 12, 14, 16, MOBA_HEADS, "moba")


def _rms_f32(x):
    x = x.astype(F32)
    return x * lax.rsqrt(jnp.mean(x * x, axis=-1, keepdims=True) + RMS_EPS)


def _outproj_kernel(oa_ref, ob_ref, oc_ref, gain_ref, w_ref, x_ref, h_ref):
    a0, a1, a2 = 0, SWA_WIDTH, SWA_WIDTH + FOX_WIDTH
    ya = (_rms_f32(oa_ref[...]) * gain_ref[:, a0:a1]).astype(BF16)
    yb = (_rms_f32(ob_ref[...]) * gain_ref[:, a1:a2]).astype(BF16)
    yc = (_rms_f32(oc_ref[...]) * gain_ref[:, a2:]).astype(BF16)
    y = (_dot(ya, w_ref[a0:a1, :]) + _dot(yb, w_ref[a1:a2, :])) + _dot(yc, w_ref[a2:, :])
    h_ref[...] = x_ref[...] + y


def _outproj(oa, ob, oc, gain, w_out, x2):
    t = x2.shape[0]
    tm = ROW_TILE
    row = lambda i: (i, 0)
    fixed = lambda i: (0, 0)
    return pl.pallas_call(
        _outproj_kernel,
        grid=(t // tm,),
        in_specs=[
            pl.BlockSpec((tm, SWA_WIDTH), row),
            pl.BlockSpec((tm, FOX_WIDTH), row),
            pl.BlockSpec((tm, MOBA_WIDTH), row),
            pl.BlockSpec((1, MIX_WIDTH), fixed),
            pl.BlockSpec((MIX_WIDTH, D_MODEL), fixed),
            pl.BlockSpec((tm, D_MODEL), row),
        ],
        out_specs=pl.BlockSpec((tm, D_MODEL), row),
        out_shape=jax.ShapeDtypeStruct((t, D_MODEL), F32),
        compiler_params=_params("parallel"),
        name="outproj",
    )(oa, ob, oc, gain, w_out, x2)


def _silu(x):
    return x / (1.0 + jnp.exp(-x))


def _ffn_kernel(h_ref, g_ref, wg_ref, wu_ref, wd_ref, o_ref, acc_scr):
    j = pl.program_id(1)

    @pl.when((pl.program_id(0) == 0) & (j == 0))
    def _():
        acc_scr[...] = jnp.zeros_like(acc_scr)

    x = h_ref[...]
    inv = lax.rsqrt(jnp.mean(x * x, axis=-1, keepdims=True) + RMS_EPS)
    z = (x * inv * g_ref[...]).astype(BF16)
    a = (_silu(_dot(z, wg_ref[...].astype(BF16))) * _dot(z, wu_ref[...].astype(BF16))).astype(BF16)
    total = acc_scr[...] + _dot(a, wd_ref[...].astype(BF16))
    o_ref[...] = x + total
    acc_scr[...] = jnp.where(j == pl.num_programs(1) - 1, 0.0, total)


def _dense_ffn(h2, norm_g, w_gate, w_up, w_down):
    t = h2.shape[0]
    tm, tf = 2 * ROW_TILE, FF_TILE
    return pl.pallas_call(
        _ffn_kernel,
        grid=(t // tm, D_FF // tf),
        in_specs=[
            pl.BlockSpec((tm, D_MODEL), lambda i, j: (i, 0)),
            pl.BlockSpec((1, D_MODEL), lambda i, j: (0, 0)),
            pl.BlockSpec((D_MODEL, tf), lambda i, j: (0, j)),
            pl.BlockSpec((D_MODEL, tf), lambda i, j: (0, j)),
            pl.BlockSpec((tf, D_MODEL), lambda i, j: (j, 0)),
        ],
        out_specs=pl.BlockSpec((tm, D_MODEL), lambda i, j: (i, 0)),
        out_shape=jax.ShapeDtypeStruct((t, D_MODEL), F32),
        scratch_shapes=[pltpu.VMEM((tm, D_MODEL), F32)],
        compiler_params=_params("arbitrary", "arbitrary"),
        name="dense_ffn",
    )(h2, norm_g, w_gate, w_up, w_down)


def _router_kernel(h_ref, g_ref, wr_ref, z_ref, idx_ref, gate_ref):
    tm = h_ref.shape[0]
    x = h_ref[...]
    inv = lax.rsqrt(jnp.mean(x * x, axis=-1, keepdims=True) + RMS_EPS)
    z = x * inv * g_ref[...]
    for c in range(CHUNKS):
        z_ref[pl.ds(c, tm, stride=CHUNKS), :] = z[:, c * LANES:(c + 1) * LANES]
    zh, zm, _ = _split3(z)
    wh, wm = wr_ref[0], wr_ref[1]
    logits = _dot(zh, wh) + (_dot(zh, wm) + _dot(zm, wh))
    lane = lax.broadcasted_iota(jnp.int32, (tm, LANES), 1)
    logits = jnp.where(lane < N_EXPERTS, logits, -jnp.inf)
    m1 = jnp.max(logits, axis=1, keepdims=True)
    i1 = jnp.min(jnp.where(logits == m1, lane, LANES), axis=1, keepdims=True)
    rest = jnp.where(lane == i1, -jnp.inf, logits)
    m2 = jnp.max(rest, axis=1, keepdims=True)
    i2 = jnp.min(jnp.where(rest == m2, lane, LANES), axis=1, keepdims=True)
    e2 = jnp.exp(m2 - m1)
    g1 = 1.0 / (1.0 + e2)
    idx_ref[...] = jnp.where(lane == 0, i1, i2)
    gate_ref[...] = jnp.where(lane == 0, g1, e2 * g1)


def _router(h2, norm_g, wr3):
    t = h2.shape[0]
    tm = ROW_TILE
    row = lambda i: (i, 0)
    return pl.pallas_call(
        _router_kernel,
        grid=(t // tm,),
        in_specs=[
            pl.BlockSpec((tm, D_MODEL), row),
            pl.BlockSpec((1, D_MODEL), lambda i: (0, 0)),
            pl.BlockSpec((2, D_MODEL, LANES), lambda i: (0, 0, 0)),
        ],
        out_specs=[pl.BlockSpec((tm * CHUNKS, LANES), row),
                   pl.BlockSpec((tm, LANES), row),
                   pl.BlockSpec((tm, LANES), row)],
        out_shape=[jax.ShapeDtypeStruct((t * CHUNKS, LANES), F32),
                   jax.ShapeDtypeStruct((t, LANES), jnp.int32),
                   jax.ShapeDtypeStruct((t, LANES), F32)],
        compiler_params=_params("parallel"),
        name="router",
    )(h2, norm_g, wr3)


def _row_copy(src, src_row, dst, dst_row, sem):
    return pltpu.make_async_copy(src.at[pl.ds(src_row * CHUNKS, CHUNKS), :],
                                 dst.at[pl.ds(dst_row * CHUNKS, CHUNKS), :], sem)


def _rows_wait(src, dst, rows, sem):
    pltpu.make_async_copy(src.at[pl.ds(0, rows * CHUNKS), :],
                          dst.at[pl.ds(0, rows * CHUNKS), :], sem).wait()


def _dispatch_kernel(dest_ref, fill_ref, z_ref, wg_ref, wu_ref, wd_ref,
                     x_hbm, wg_out, wu_out, wd_out, zero_scr, fsem, sem):
    i = pl.program_id(0)
    blk = MOE_BLOCK
    tile = z_ref.shape[0] // CHUNKS
    wg_out[...] = wg_ref[...].astype(BF16)
    wu_out[...] = wu_ref[...].astype(BF16)
    wd_out[...] = wd_ref[...].astype(BF16)

    @pl.when(i == 0)
    def _():
        zero_scr[...] = jnp.zeros_like(zero_scr)
        for e in range(fill_ref.shape[0]):
            @pl.when(fill_ref[e] >= 0)
            def _():
                start = pl.multiple_of(fill_ref[e] * CHUNKS, blk * CHUNKS)
                pltpu.make_async_copy(zero_scr, x_hbm.at[pl.ds(start, blk * CHUNKS), :], fsem).start()
        for e in range(fill_ref.shape[0]):
            @pl.when(fill_ref[e] >= 0)
            def _():
                _rows_wait(zero_scr, x_hbm, blk, fsem)

    base = i * tile * TOP_K

    tokens_per_trip = MOE_ISSUE_UNROLL // TOP_K

    def trip(it, c):
        for k in range(MOE_ISSUE_UNROLL):
            token = it * tokens_per_trip + k // TOP_K
            slot = dest_ref[base + it * MOE_ISSUE_UNROLL + k]
            _row_copy(z_ref, token, x_hbm, slot, sem).start()
        return c
    lax.fori_loop(0, tile // tokens_per_trip, trip, 0)
    for _ in range(TOP_K):
        _rows_wait(z_ref, x_hbm, tile, sem)


def _dispatch(z8, dest, fill, p_rows, w_gate, w_up, w_down):
    tile = DISPATCH_TILE
    steps = z8.shape[0] // (tile * CHUNKS)
    ne, d, ff = w_gate.shape
    assert z8.shape[0] % (tile * CHUNKS) == 0 and (ne * d) % steps == 0 and (ne * ff) % steps == 0
    up_rows, down_rows = ne * d // steps, ne * ff // steps
    wspec = lambda rows, cols: pl.BlockSpec((rows, cols), lambda i, dd, f: (i, 0))
    grid_spec = pltpu.PrefetchScalarGridSpec(
        num_scalar_prefetch=2,
        grid=(steps,),
        in_specs=[pl.BlockSpec((tile * CHUNKS, LANES), lambda i, dd, f: (i, 0)),
                  wspec(up_rows, ff), wspec(up_rows, ff), wspec(down_rows, d)],
        out_specs=[pl.BlockSpec(memory_space=pl.ANY),
                   wspec(up_rows, ff), wspec(up_rows, ff), wspec(down_rows, d)],
        scratch_shapes=[pltpu.VMEM((MOE_BLOCK * CHUNKS, LANES), F32),
                        pltpu.SemaphoreType.DMA(()), pltpu.SemaphoreType.DMA(())],
    )
    x8, wg, wu, wd = pl.pallas_call(
        _dispatch_kernel,
        grid_spec=grid_spec,
        out_shape=[jax.ShapeDtypeStruct((p_rows * CHUNKS, LANES), F32),
                   jax.ShapeDtypeStruct((ne * d, ff), BF16),
                   jax.ShapeDtypeStruct((ne * d, ff), BF16),
                   jax.ShapeDtypeStruct((ne * ff, d), BF16)],
        compiler_params=_params("arbitrary"),
        name="moe_dispatch",
    )(dest, fill, z8, w_gate.reshape(ne * d, ff), w_up.reshape(ne * d, ff),
      w_down.reshape(ne * ff, d))
    return x8, wg.reshape(ne, d, ff), wu.reshape(ne, d, ff), wd.reshape(ne, ff, d)


def _moe_kernel(be_ref, live_ref, bx_ref, x_ref, wg_ref, wu_ref, wd_ref, y_ref, acc):
    i = pl.program_id(0)
    j = pl.program_id(1)
    tm = acc.shape[0]

    @pl.when((i == 0) & (j == 0))
    def _():
        acc[...] = jnp.zeros_like(acc)

    @pl.when((live_ref[i] == 0) & (j == 0))
    def _():
        y_ref[...] = jnp.zeros_like(y_ref)

    @pl.when(live_ref[i] > 0)
    def _():
        x = jnp.concatenate([x_ref[pl.ds(c, tm, stride=CHUNKS), :].astype(BF16)
                             for c in range(CHUNKS)], axis=1)
        a = (_silu(_dot(x, wg_ref[0])) * _dot(x, wu_ref[0])).astype(BF16)
        total = acc[...] + _dot(a, wd_ref[0])
        for c in range(CHUNKS):
            y_ref[pl.ds(c, tm, stride=CHUNKS), :] = total[:, c * LANES:(c + 1) * LANES]
        acc[...] = jnp.where(j == pl.num_programs(1) - 1, 0.0, total)


def _moe_experts(x8, blk_e, blk_live, blk_x, w_gate, w_up, w_down):
    tm, tf = MOE_BLOCK, MOE_FF_TILE
    nblk = blk_e.shape[0]
    nff = D_FF // tf
    ff = lambda i, j, live: jnp.where(live[i] > 0, j, nff - 1)
    grid_spec = pltpu.PrefetchScalarGridSpec(
        num_scalar_prefetch=3,
        grid=(nblk, nff),
        in_specs=[
            pl.BlockSpec((tm * CHUNKS, LANES), lambda i, j, be, lv, bx: (bx[i], 0)),
            pl.BlockSpec((1, D_MODEL, tf), lambda i, j, be, lv, bx: (be[i], 0, ff(i, j, lv))),
            pl.BlockSpec((1, D_MODEL, tf), lambda i, j, be, lv, bx: (be[i], 0, ff(i, j, lv))),
            pl.BlockSpec((1, tf, D_MODEL), lambda i, j, be, lv, bx: (be[i], ff(i, j, lv), 0)),
        ],
        out_specs=pl.BlockSpec((tm * CHUNKS, LANES), lambda i, j, be, lv, bx: (i, 0)),
        scratch_shapes=[pltpu.VMEM((tm, D_MODEL), F32)],
    )
    return pl.pallas_call(
        _moe_kernel,
        grid_spec=grid_spec,
        out_shape=jax.ShapeDtypeStruct(x8.shape, F32),
        compiler_params=_params("arbitrary", "arbitrary"),
        name="moe_experts",
    )(blk_e, blk_live, blk_x, x8, w_gate, w_up, w_down)


def _combine_kernel(dest_ref, h_ref, gate_ref, *rest, final_norm):
    if final_norm:
        g_ref, y_hbm, o_ref, ybuf, sem = rest
    else:
        y_hbm, o_ref, ybuf, sem = rest
    i = pl.program_id(0)
    tm = h_ref.shape[0]
    rows = tm * TOP_K
    stride = TOP_K * CHUNKS

    def pull(tile, slot):
        def trip(it, c):
            for k in range(MOE_ISSUE_UNROLL):
                r = it * MOE_ISSUE_UNROLL + k
                _row_copy(y_hbm, dest_ref[tile * rows + r], ybuf.at[slot], r, sem.at[slot]).start()
            return c
        lax.fori_loop(0, rows // MOE_ISSUE_UNROLL, trip, 0)

    @pl.when(i == 0)
    def _():
        pull(0, 0)

    @pl.when(i + 1 < pl.num_programs(0))
    def _():
        pull(i + 1, (i + 1) % 2)

    slot = i % 2
    yb = ybuf.at[slot]
    _rows_wait(y_hbm, yb, rows, sem.at[slot])
    gates = [gate_ref[:, k:k + 1] for k in range(TOP_K)]
    parts = []
    ss = jnp.zeros((tm, 1), F32)
    for c in range(CHUNKS):
        y = gates[0] * yb[pl.ds(c, tm, stride=stride), :]
        for k in range(1, TOP_K):
            y = y + gates[k] * yb[pl.ds(k * CHUNKS + c, tm, stride=stride), :]
        hc = h_ref[:, c * LANES:(c + 1) * LANES] + y
        parts.append(hc)
        ss = ss + jnp.sum(hc * hc, axis=1, keepdims=True)
    if final_norm:
        inv = lax.rsqrt(ss / D_MODEL + RMS_EPS)
    for c in range(CHUNKS):
        cols = slice(c * LANES, (c + 1) * LANES)
        o_ref[:, cols] = parts[c] * inv * g_ref[:, cols] if final_norm else parts[c]


def _combine(h2, y8, dest, gate, out_g):
    t = h2.shape[0]
    tm = ROW_TILE
    final_norm = out_g is not None
    row = lambda i, d: (i, 0)
    in_specs = [pl.BlockSpec((tm, D_MODEL), row), pl.BlockSpec((tm, LANES), row)]
    args = [h2, gate]
    if final_norm:
        in_specs.append(pl.BlockSpec((1, D_MODEL), lambda i, d: (0, 0)))
        args.append(out_g.reshape(1, D_MODEL))
    grid_spec = pltpu.PrefetchScalarGridSpec(
        num_scalar_prefetch=1,
        grid=(t // tm,),
        in_specs=in_specs + [pl.BlockSpec(memory_space=pl.ANY)],
        out_specs=pl.BlockSpec((tm, D_MODEL), row),
        scratch_shapes=[pltpu.VMEM((2, tm * TOP_K * CHUNKS, LANES), F32),
                        pltpu.SemaphoreType.DMA((2,))],
    )
    return pl.pallas_call(
        functools.partial(_combine_kernel, final_norm=final_norm),
        grid_spec=grid_spec,
        out_shape=jax.ShapeDtypeStruct((t, D_MODEL), F32),
        compiler_params=_params("arbitrary"),
        name="moe_combine",
    )(dest, *args, y8)


def _norm_kernel(h_ref, g_ref, o_ref):
    x = h_ref[...]
    o_ref[...] = x * lax.rsqrt(jnp.mean(x * x, axis=-1, keepdims=True) + RMS_EPS) * g_ref[...]


def _final_norm(h2, norm_g):
    t = h2.shape[0]
    tm = ROW_TILE
    return pl.pallas_call(
        _norm_kernel,
        grid=(t // tm,),
        in_specs=[pl.BlockSpec((tm, D_MODEL), lambda i: (i, 0)),
                  pl.BlockSpec((1, D_MODEL), lambda i: (0, 0))],
        out_specs=pl.BlockSpec((tm, D_MODEL), lambda i: (i, 0)),
        out_shape=jax.ShapeDtypeStruct((t, D_MODEL), F32),
        compiler_params=_params("parallel"),
        name="final_norm",
    )(h2, norm_g)


def _dispatch_plan(top_i, t):
    n = t * TOP_K
    tm = MOE_BLOCK
    e_flat = top_i.reshape(n)
    onehot = (e_flat[:, None] == jnp.arange(N_EXPERTS)[None, :]).astype(jnp.int32)
    counts = jnp.sum(onehot, axis=0)
    rank = jnp.sum((jnp.cumsum(onehot, axis=0) - onehot) * onehot, axis=1)
    padded = (counts + tm - 1) // tm * tm
    pends = jnp.cumsum(padded)
    pstarts = pends - padded
    dest = (pstarts[e_flat] + rank).astype(jnp.int32)
    p_rows = -(-n // tm) * tm + N_EXPERTS * tm
    nblk = p_rows // tm
    blk = jnp.arange(nblk, dtype=jnp.int32)
    blk_e = jnp.minimum(jnp.sum((blk * tm)[:, None] >= pends[None, :], axis=1), N_EXPERTS - 1)
    live = blk * tm < pends[-1]
    last_live = jnp.maximum(pends[-1] // tm - 1, 0)
    blk_e = jnp.where(live, blk_e, blk_e[last_live])
    blk_x = jnp.where(live, blk, 0)
    tail = jnp.where(padded > 0, pends - tm, -1)
    spare = pends[-1] + jnp.arange(N_EXPERTS) * tm
    fill = jnp.concatenate([tail, jnp.where(spare < p_rows, spare, -1)])
    i32 = lambda a: a.astype(jnp.int32)
    return dest, i32(fill), i32(blk_e), i32(live), i32(blk_x), p_rows


def _rope_tables(seq):
    inv = 1.0 / (ROPE_THETA ** (jnp.arange(0, HEAD_DIM, 2, dtype=F32) / HEAD_DIM))
    ang = jnp.arange(seq, dtype=F32)[:, None] * inv[None, :]
    cos = jnp.concatenate([jnp.cos(ang)] * 4, axis=-1)
    sin = jnp.concatenate([jnp.sin(ang)] * 4, axis=-1)
    upper = (jnp.arange(LANES) % HEAD_DIM) >= HEAD_DIM // 2
    sa = jnp.where(upper[None, :], sin, 0.0)
    sb = jnp.where(upper[None, :], 0.0, -sin)
    return cos, sa, sb


def _in_proj_columns():
    split = np.cumsum([0, SWA_WIDTH, SWA_KV_WIDTH, SWA_KV_WIDTH, FOX_WIDTH, FOX_WIDTH, FOX_WIDTH,
                       FOX_HEADS, MOBA_WIDTH, MOBA_WIDTH, MOBA_WIDTH])
    aq, ak, av, bq, bk, bv, bf, cq, ck, cv = split[:-1]
    half = SWA_Q_HEADS // 2
    cols = []
    for j in range(half):
        cols += list(range(aq + j * HEAD_DIM, aq + (j + 1) * HEAD_DIM))
        cols += list(range(aq + (j + half) * HEAD_DIM, aq + (j + half + 1) * HEAD_DIM))
    for start, width in ((ak, SWA_KV_WIDTH), (av, SWA_KV_WIDTH), (bq, FOX_WIDTH), (bk, FOX_WIDTH),
                         (bv, FOX_WIDTH), (cq, MOBA_WIDTH), (ck, MOBA_WIDTH), (cv, MOBA_WIDTH)):
        cols += list(range(start, start + width))
    forget = list(range(bf, bf + FOX_HEADS))
    return np.array(cols, np.int32), np.array(forget, np.int32)


def _swa_out_perm():
    half = SWA_Q_HEADS // 2
    perm = []
    for j in range(half):
        perm += list(range(j * HEAD_DIM, (j + 1) * HEAD_DIM))
        perm += list(range((j + half) * HEAD_DIM, (j + half + 1) * HEAD_DIM))
    return np.array(perm + list(range(SWA_WIDTH, MIX_WIDTH)), np.int32)


def _mixer(h2, b, s, norm_g, w_in, forget_bias, sinks, mix_gain, w_out, tables):
    cols, fcols = _in_proj_columns()
    w_all = jnp.concatenate(
        [w_in[:, cols], jnp.pad(w_in[:, fcols], ((0, 0), (0, LANES - FOX_HEADS)))], axis=1
    ).astype(BF16)
    fbias = jnp.pad(forget_bias.astype(F32), (0, LANES - FOX_HEADS)).reshape(1, LANES)
    u, lf = _project(h2, norm_g.reshape(1, D_MODEL), w_all, *tables, fbias, s)
    u3 = u.reshape(b, s, U_WIDTH)
    oa = _swa(u3, sinks.astype(F32))
    ob = _fox(u3, lf.reshape(b, s, LANES))
    oc = _moba(u3)
    perm = _swa_out_perm()
    t = b * s
    return _outproj(oa.reshape(t, SWA_WIDTH), ob.reshape(t, FOX_WIDTH), oc.reshape(t, MOBA_WIDTH),
                    mix_gain[perm].reshape(1, MIX_WIDTH).astype(F32),
                    w_out[perm].astype(BF16), h2)


def _moe_ffn(h2, norm_g, w_router, w_gate, w_up, w_down, out_g):
    t = h2.shape[0]
    wr = jnp.pad(w_router.astype(F32), ((0, 0), (0, LANES - N_EXPERTS)))
    wr2 = jnp.stack(_split3(wr)[:2])
    z8, top, gate = _router(h2, norm_g.reshape(1, D_MODEL), wr2)
    dest, fill, blk_e, blk_live, blk_x, p_rows = _dispatch_plan(top[:, :TOP_K], t)
    x8, wg, wu, wd = _dispatch(z8, dest, fill, p_rows, w_gate, w_up, w_down)
    y8 = _moe_experts(x8, blk_e, blk_live, blk_x, wg, wu, wd)
    return _combine(h2, y8, dest, gate, out_g)


def kernel(x, attn_norm, w_in, fox_forget_bias, swa_sinks, mix_gain, w_out, ffn_norm,
           dense_w_gate, dense_w_up, dense_w_down, router_w, moe_w_gate, moe_w_up,
           moe_w_down, final_norm):
    b, s, d = x.shape
    depth = attn_norm.shape[0]
    assert d == D_MODEL and s % MOBA_BLOCK == 0 and s % ROW_TILE == 0
    tables = _rope_tables(s)
    h = x.reshape(b * s, d)
    normed = False
    for layer in range(depth):
        h = _mixer(h, b, s, attn_norm[layer], w_in[layer], fox_forget_bias[layer],
                   swa_sinks[layer], mix_gain[layer], w_out[layer], tables)
        j = layer // 2
        last = layer == depth - 1
        if layer % 2 == 0:
            h = _dense_ffn(h, ffn_norm[layer].reshape(1, d), dense_w_gate[j], dense_w_up[j],
                           dense_w_down[j])
        else:
            h = _moe_ffn(h, ffn_norm[layer], router_w[j], moe_w_gate[j], moe_w_up[j],
                         moe_w_down[j], final_norm if last else None)
            normed = last
    if not normed:
        h = _final_norm(h, final_norm.reshape(1, d))
    return h.reshape(b, s, d)
```

```python
import functools

import jax
import jax.numpy as jnp
import numpy as np
from jax import lax
from jax.experimental import pallas as pl
from jax.experimental.pallas import tpu as pltpu

F32 = jnp.float32
BF16 = jnp.bfloat16

D_MODEL = 1024
HEAD_DIM = 64
LANES = 128
SWA_Q_HEADS = 8
SWA_KV_HEADS = 2
SWA_WINDOW = 128
FOX_HEADS = 4
MOBA_HEADS = 4
MOBA_BLOCK = 256
MOBA_TOPK = 3
ROPE_THETA = 10000.0
RMS_EPS = 1e-5
D_FF = 3584
N_EXPERTS = 8
TOP_K = 2
MOE_BLOCK = 512
NEG_INF = -1e30
ATTN_SCALE = HEAD_DIM ** -0.5
LOG2E = 1.4426950408889634
Q_SCALE = ATTN_SCALE * LOG2E

SWA_WIDTH = SWA_Q_HEADS * HEAD_DIM
SWA_KV_WIDTH = SWA_KV_HEADS * HEAD_DIM
FOX_WIDTH = FOX_HEADS * HEAD_DIM
MOBA_WIDTH = MOBA_HEADS * HEAD_DIM
MIX_WIDTH = SWA_WIDTH + FOX_WIDTH + MOBA_WIDTH

U_BLOCKS = 18
U_WIDTH = U_BLOCKS * LANES
ROPE_BLOCKS = (0, 1, 2, 3, 4, 12, 13, 14, 15)
Q_BLOCKS = (0, 1, 2, 3, 6, 7, 12, 13)
W_ALL_WIDTH = U_WIDTH + LANES

VMEM_LIMIT = 56 * 1024 * 1024

ROW_TILE = 512
ATT_TILE = 256
SWA_STEP_BLOCKS = 4
FF_TILE = 512
MOE_FF_TILE = 1792
CHUNKS = D_MODEL // LANES
MOE_ISSUE_UNROLL = 8
DISPATCH_TILE = 512


def _params(*sem):
    return pltpu.CompilerParams(dimension_semantics=sem, vmem_limit_bytes=VMEM_LIMIT)


def _split3(x):
    hi = x.astype(BF16)
    r1 = x - hi.astype(F32)
    mid = r1.astype(BF16)
    lo = (r1 - mid.astype(F32)).astype(BF16)
    return hi, mid, lo


def _dot_nt(a, b):
    return lax.dot_general(a, b, (((1,), (1,)), ((), ())), preferred_element_type=F32)


def _dot(a, b):
    return jnp.dot(a, b, preferred_element_type=F32)


def _proj_kernel(x_ref, g_ref, w_ref, cos_ref, sa_ref, sb_ref, fb_ref, u_ref, lf_ref):
    x = x_ref[...]
    inv = lax.rsqrt(jnp.mean(x * x, axis=-1, keepdims=True) + RMS_EPS)
    h = (x * inv * g_ref[...]).astype(BF16)
    cos = cos_ref[...]
    sa = sa_ref[...]
    sb = sb_ref[...]
    for c in range(U_BLOCKS // 2):
        acc = _dot(h, w_ref[:, c * 2 * LANES:(c + 1) * 2 * LANES])
        for half in range(2):
            blk = 2 * c + half
            a = acc[:, half * LANES:(half + 1) * LANES]
            if blk in ROPE_BLOCKS:
                a = a * cos + pltpu.roll(a, 32, 1) * sa + pltpu.roll(a, 96, 1) * sb
            if blk in Q_BLOCKS:
                a = a * Q_SCALE
            u_ref[:, blk * LANES:(blk + 1) * LANES] = a.astype(BF16)
    f = _dot(h, w_ref[:, U_WIDTH:W_ALL_WIDTH]) + fb_ref[...]
    lf_ref[...] = jnp.minimum(f, 0.0) - jnp.log(1.0 + jnp.exp(-jnp.abs(f)))


def _project(x2, norm_g, w_all, cos, sa, sb, fbias, seq):
    t = x2.shape[0]
    tm = ROW_TILE
    nseq = seq // tm
    row = lambda i: (i, 0)
    pos = lambda i: (i % nseq, 0)
    fixed = lambda i: (0, 0)
    return pl.pallas_call(
        _proj_kernel,
        grid=(t // tm,),
        in_specs=[
            pl.BlockSpec((tm, D_MODEL), row),
            pl.BlockSpec((1, D_MODEL), fixed),
            pl.BlockSpec((D_MODEL, W_ALL_WIDTH), fixed),
            pl.BlockSpec((tm, LANES), pos),
            pl.BlockSpec((tm, LANES), pos),
            pl.BlockSpec((tm, LANES), pos),
            pl.BlockSpec((1, LANES), fixed),
        ],
        out_specs=[pl.BlockSpec((tm, U_WIDTH), row), pl.BlockSpec((tm, LANES), row)],
        out_shape=[jax.ShapeDtypeStruct((t, U_WIDTH), BF16),
                   jax.ShapeDtypeStruct((t, LANES), F32)],
        compiler_params=_params("parallel"),
        name="proj",
    )(x2, norm_g, w_all, cos, sa, sb, fbias)


def _swa_kernel(sink_ref, q_ref, kc_ref, kp_ref, vc_ref, vp_ref, o_ref):
    n = pl.program_id(1)
    w = SWA_WINDOW
    group = SWA_Q_HEADS // SWA_KV_HEADS
    lane = lax.broadcasted_iota(jnp.int32, (1, LANES), 1)
    k_all = jnp.concatenate([kp_ref[0], kc_ref[0]], axis=0)
    v_all = jnp.concatenate([vp_ref[0], vc_ref[0]], axis=0)
    qi = lax.broadcasted_iota(jnp.int32, (group * w, 2 * w), 0) % w
    kj = lax.broadcasted_iota(jnp.int32, (group * w, 2 * w), 1)
    window = (kj > qi) & (kj <= qi + w)
    for a in range(SWA_STEP_BLOCKS):
        rows = slice(a * w, (a + 1) * w)
        k2 = k_all[a * w:(a + 2) * w]
        v2 = v_all[a * w:(a + 2) * w]
        valid = (window & ((kj >= w) | (n > 0))) if a == 0 else window
        outs = []
        for g in range(SWA_KV_HEADS):
            keep = (lane < HEAD_DIM) if g == 0 else (lane >= HEAD_DIM)
            qs = jnp.concatenate(
                [jnp.where(keep, q_ref[0, rows, j * LANES:(j + 1) * LANES], 0)
                 for j in range(group)], axis=0)
            s = jnp.where(valid, _dot_nt(qs, k2), NEG_INF)
            ps, inv = [], []
            for j in range(group):
                sj = s[j * w:(j + 1) * w]
                sink = sink_ref[g * group + j] * LOG2E
                m = jnp.maximum(jnp.max(sj, axis=1, keepdims=True), sink)
                p = jnp.exp2(sj - m)
                inv.append(1.0 / (jnp.sum(p, axis=1, keepdims=True) + jnp.exp2(sink - m)))
                ps.append(p.astype(BF16))
            o = _dot(jnp.concatenate(ps, axis=0), v2)
            outs.append([o[j * w:(j + 1) * w] * inv[j] for j in range(group)])
        for j in range(group):
            oj = jnp.where(lane < HEAD_DIM, outs[0][j], outs[1][j])
            o_ref[0, rows, j * LANES:(j + 1) * LANES] = oj.astype(BF16)


def _swa(u3, sinks):
    b, s, _ = u3.shape
    w = SWA_WINDOW
    nb = SWA_STEP_BLOCKS
    assert s % (nb * w) == 0
    cur = lambda blk: (lambda bi, n, sk: (bi, n, blk))
    prev = lambda blk: (lambda bi, n, sk: (bi, jnp.maximum(n * nb - 1, 0), blk))
    grid_spec = pltpu.PrefetchScalarGridSpec(
        num_scalar_prefetch=1,
        grid=(b, s // (nb * w)),
        in_specs=[
            pl.BlockSpec((1, nb * w, SWA_WIDTH), lambda bi, n, sk: (bi, n, 0)),
            pl.BlockSpec((1, nb * w, LANES), cur(4)),
            pl.BlockSpec((1, w, LANES), prev(4)),
            pl.BlockSpec((1, nb * w, LANES), cur(5)),
            pl.BlockSpec((1, w, LANES), prev(5)),
        ],
        out_specs=pl.BlockSpec((1, nb * w, SWA_WIDTH), lambda bi, n, sk: (bi, n, 0)),
    )
    return pl.pallas_call(
        _swa_kernel,
        grid_spec=grid_spec,
        out_shape=jax.ShapeDtypeStruct((b, s, SWA_WIDTH), BF16),
        compiler_params=_params("parallel", "parallel"),
        name="swa",
    )(sinks, u3, u3, u3, u3, u3)


def _head_lanes(hh):
    lane = lax.broadcasted_iota(jnp.int32, (1, LANES), 1)
    if hh == 0:
        return lane < HEAD_DIM, lane - HEAD_DIM
    return lane >= HEAD_DIM, lane


def _causal_attention(qp_scr, kp_scr, v_ref, o_ref, t):
    s_len = qp_scr.shape[1]
    lane = lax.broadcasted_iota(jnp.int32, (1, LANES), 1)
    row = lax.broadcasted_iota(jnp.int32, (t, t), 0)
    col = lax.broadcasted_iota(jnp.int32, (t, t), 1)
    for qi in range(s_len // t):
        rows = slice(qi * t, (qi + 1) * t)
        outs = []
        for hh in range(2):
            q = qp_scr[hh, rows, :]
            sd = jnp.where(col <= row, _dot_nt(q, kp_scr[hh, rows, :]), NEG_INF)
            m = jnp.max(sd, axis=1, keepdims=True)
            if qi:
                sp = _dot_nt(q, kp_scr[hh, :qi * t, :])
                m = jnp.maximum(m, jnp.max(sp, axis=1, keepdims=True))
            pd = jnp.exp2(sd - m)
            l = jnp.sum(pd, axis=1, keepdims=True)
            o = _dot(pd.astype(BF16), v_ref[0, rows, :])
            if qi:
                pp = jnp.exp2(sp - m)
                l = l + jnp.sum(pp, axis=1, keepdims=True)
                o = o + _dot(pp.astype(BF16), v_ref[0, :qi * t, :])
            outs.append(o * (1.0 / l))
        o_ref[0, rows, :] = jnp.where(lane < HEAD_DIM, outs[0], outs[1]).astype(BF16)


def _pair_attention(kernel_fn, u3, extra, q_blk, k_blk, v_blk, n_heads, name):
    b, s, _ = u3.shape
    pair = lambda blk: (lambda bi, p: (bi, 0, blk + p))
    seq_block = pl.BlockSpec((1, s, LANES), lambda bi, p: (bi, 0, 0))
    return pl.pallas_call(
        kernel_fn,
        grid=(b, n_heads // 2),
        in_specs=[pl.BlockSpec((1, s, LANES), pair(q_blk)),
                  pl.BlockSpec((1, s, LANES), pair(k_blk)),
                  pl.BlockSpec((1, s, LANES), pair(v_blk))] + [seq_block] * len(extra),
        out_specs=pl.BlockSpec((1, s, LANES), pair(0)),
        out_shape=jax.ShapeDtypeStruct((b, s, n_heads * HEAD_DIM), BF16),
        scratch_shapes=[pltpu.VMEM((2, s, LANES), BF16), pltpu.VMEM((2, s, LANES), BF16)],
        compiler_params=_params("parallel", "parallel"),
        name=name,
    )(u3, u3, u3, *extra)


def _fox_kernel(q_ref, k_ref, v_ref, lf_ref, o_ref, qp_scr, kp_scr):
    p = pl.program_id(1)
    s_len = q_ref.shape[1]
    t = ATT_TILE
    lane = lax.broadcasted_iota(jnp.int32, (1, LANES), 1)
    r = lax.broadcasted_iota(jnp.int32, (t, t), 0)
    cc = lax.broadcasted_iota(jnp.int32, (t, t), 1)
    tri = jnp.where(cc <= r, 1.0, 0.0).astype(BF16)
    carry = jnp.zeros((1, LANES), F32)
    for i in range(s_len // t):
        rows = slice(i * t, (i + 1) * t)
        lf = lf_ref[0, rows, :]
        lf0 = jnp.sum(jnp.where(lane == 2 * p, lf, 0.0), axis=1, keepdims=True)
        lf1 = jnp.sum(jnp.where(lane == 2 * p + 1, lf, 0.0), axis=1, keepdims=True)
        hi, mid, lo = _split3(jnp.where(lane < HEAD_DIM, lf1, lf0))
        c = (_dot(tri, hi) + _dot(tri, mid)) + _dot(tri, lo) + carry
        carry = c[t - 1:t, :]
        chi, cmid, clo = _split3(c * LOG2E)
        for hh in range(2):
            keep, f = _head_lanes(hh)
            ones_q = jnp.where((f >= 3) & (f < 6), 1.0, 0.0).astype(BF16)
            ones_k = jnp.where((f >= 0) & (f < 3), 1.0, 0.0).astype(BF16)
            qf = jnp.where(f == 0, chi, jnp.where(f == 1, cmid, jnp.where(f == 2, clo, ones_q)))
            kf = jnp.where(f == 3, -chi, jnp.where(f == 4, -cmid, jnp.where(f == 5, -clo, ones_k)))
            qp_scr[hh, rows, :] = jnp.where(keep, q_ref[0, rows, :], qf)
            kp_scr[hh, rows, :] = jnp.where(keep, k_ref[0, rows, :], kf)
    _causal_attention(qp_scr, kp_scr, v_ref, o_ref, t)


def _fox(u3, lf3):
    return _pair_attention(_fox_kernel, u3, (lf3,), 6, 8, 10, FOX_HEADS, "fox")


def _moba_kernel(q_ref, k_ref, v_ref, o_ref, qp_scr, kp_scr):
    s_len = q_ref.shape[1]
    t = MOBA_BLOCK
    nkb = s_len // t
    sub = 8
    assert nkb <= sub
    means = [jnp.mean(k_ref[0, j * t:(j + 1) * t, :].astype(F32), axis=0, keepdims=True)
             for j in range(nkb)]
    km_pair = jnp.concatenate(means + [jnp.zeros((LANES - nkb, LANES), F32)], axis=0)
    blk = lax.broadcasted_iota(jnp.int32, (sub, t), 0)
    for hh in range(2):
        keep, f = _head_lanes(hh)
        km = _split3(jnp.where(keep, km_pair, 0.0))
        for i in range(nkb):
            rows = slice(i * t, (i + 1) * t)
            q = q_ref[0, rows, :]
            g = ((_dot_nt(km[0], q) + _dot_nt(km[1], q)) + _dot_nt(km[2], q))[:sub, :]
            past = blk < i
            g = jnp.where(past, g, NEG_INF)
            rank = jnp.zeros((sub, t), jnp.int32)
            for r in range(i):
                gr = jnp.broadcast_to(g[r:r + 1, :], (sub, t))
                beats = (gr > g) | ((gr == g) & (r < blk))
                rank = rank + beats.astype(jnp.int32)
            drop = jnp.where(past & (rank >= MOBA_TOPK), 1.0, 0.0)
            below = [jnp.zeros((HEAD_DIM, t), F32)] if hh == 0 else []
            above = jnp.zeros((LANES - sub - (HEAD_DIM if hh == 0 else 0), t), F32)
            padded = jnp.concatenate(below + [drop, above], axis=0)
            kf = jnp.where(f == i, NEG_INF, 0.0).astype(BF16)
            qp_scr[hh, rows, :] = jnp.where(keep, q, padded.T.astype(BF16))
            kp_scr[hh, rows, :] = jnp.where(keep, k_ref[0, rows, :], kf)
    _causal_attention(qp_scr, kp_scr, v_ref, o_ref, t)


def _moba(u3):
    return _pair_attention(_moba_kernel, u3, (), 12, 14, 16, MOBA_HEADS, "moba")


def _rms_f32(x):
    x = x.astype(F32)
    return x * lax.rsqrt(jnp.mean(x * x, axis=-1, keepdims=True) + RMS_EPS)


def _outproj_kernel(oa_ref, ob_ref, oc_ref, gain_ref, w_ref, x_ref, h_ref):
    a0, a1, a2 = 0, SWA_WIDTH, SWA_WIDTH + FOX_WIDTH
    ya = (_rms_f32(oa_ref[...]) * gain_ref[:, a0:a1]).astype(BF16)
    yb = (_rms_f32(ob_ref[...]) * gain_ref[:, a1:a2]).astype(BF16)
    yc = (_rms_f32(oc_ref[...]) * gain_ref[:, a2:]).astype(BF16)
    y = (_dot(ya, w_ref[a0:a1, :]) + _dot(yb, w_ref[a1:a2, :])) + _dot(yc, w_ref[a2:, :])
    h_ref[...] = x_ref[...] + y


def _outproj(oa, ob, oc, gain, w_out, x2):
    t = x2.shape[0]
    tm = ROW_TILE
    row = lambda i: (i, 0)
    fixed = lambda i: (0, 0)
    return pl.pallas_call(
        _outproj_kernel,
        grid=(t // tm,),
        in_specs=[
            pl.BlockSpec((tm, SWA_WIDTH), row),
            pl.BlockSpec((tm, FOX_WIDTH), row),
            pl.BlockSpec((tm, MOBA_WIDTH), row),
            pl.BlockSpec((1, MIX_WIDTH), fixed),
            pl.BlockSpec((MIX_WIDTH, D_MODEL), fixed),
            pl.BlockSpec((tm, D_MODEL), row),
        ],
        out_specs=pl.BlockSpec((tm, D_MODEL), row),
        out_shape=jax.ShapeDtypeStruct((t, D_MODEL), F32),
        compiler_params=_params("parallel"),
        name="outproj",
    )(oa, ob, oc, gain, w_out, x2)


def _silu(x):
    return x / (1.0 + jnp.exp(-x))


def _ffn_kernel(h_ref, g_ref, wg_ref, wu_ref, wd_ref, o_ref, acc_scr):
    j = pl.program_id(1)

    @pl.when((pl.program_id(0) == 0) & (j == 0))
    def _():
        acc_scr[...] = jnp.zeros_like(acc_scr)

    x = h_ref[...]
    inv = lax.rsqrt(jnp.mean(x * x, axis=-1, keepdims=True) + RMS_EPS)
    z = (x * inv * g_ref[...]).astype(BF16)
    a = (_silu(_dot(z, wg_ref[...].astype(BF16))) * _dot(z, wu_ref[...].astype(BF16))).astype(BF16)
    total = acc_scr[...] + _dot(a, wd_ref[...].astype(BF16))
    o_ref[...] = x + total
    acc_scr[...] = jnp.where(j == pl.num_programs(1) - 1, 0.0, total)


def _dense_ffn(h2, norm_g, w_gate, w_up, w_down):
    t = h2.shape[0]
    tm, tf = 2 * ROW_TILE, FF_TILE
    return pl.pallas_call(
        _ffn_kernel,
        grid=(t // tm, D_FF // tf),
        in_specs=[
            pl.BlockSpec((tm, D_MODEL), lambda i, j: (i, 0)),
            pl.BlockSpec((1, D_MODEL), lambda i, j: (0, 0)),
            pl.BlockSpec((D_MODEL, tf), lambda i, j: (0, j)),
            pl.BlockSpec((D_MODEL, tf), lambda i, j: (0, j)),
            pl.BlockSpec((tf, D_MODEL), lambda i, j: (j, 0)),
        ],
        out_specs=pl.BlockSpec((tm, D_MODEL), lambda i, j: (i, 0)),
        out_shape=jax.ShapeDtypeStruct((t, D_MODEL), F32),
        scratch_shapes=[pltpu.VMEM((tm, D_MODEL), F32)],
        compiler_params=_params("arbitrary", "arbitrary"),
        name="dense_ffn",
    )(h2, norm_g, w_gate, w_up, w_down)


def _router_kernel(h_ref, g_ref, wr_ref, z_ref, idx_ref, gate_ref):
    tm = h_ref.shape[0]
    x = h_ref[...]
    inv = lax.rsqrt(jnp.mean(x * x, axis=-1, keepdims=True) + RMS_EPS)
    z = x * inv * g_ref[...]
    for c in range(CHUNKS):
        z_ref[pl.ds(c, tm, stride=CHUNKS), :] = z[:, c * LANES:(c + 1) * LANES]
    zh, zm, _ = _split3(z)
    wh, wm = wr_ref[0], wr_ref[1]
    logits = _dot(zh, wh) + (_dot(zh, wm) + _dot(zm, wh))
    lane = lax.broadcasted_iota(jnp.int32, (tm, LANES), 1)
    logits = jnp.where(lane < N_EXPERTS, logits, -jnp.inf)
    m1 = jnp.max(logits, axis=1, keepdims=True)
    i1 = jnp.min(jnp.where(logits == m1, lane, LANES), axis=1, keepdims=True)
    rest = jnp.where(lane == i1, -jnp.inf, logits)
    m2 = jnp.max(rest, axis=1, keepdims=True)
    i2 = jnp.min(jnp.where(rest == m2, lane, LANES), axis=1, keepdims=True)
    e2 = jnp.exp(m2 - m1)
    g1 = 1.0 / (1.0 + e2)
    idx_ref[...] = jnp.where(lane == 0, i1, i2)
    gate_ref[...] = jnp.where(lane == 0, g1, e2 * g1)


def _router(h2, norm_g, wr3):
    t = h2.shape[0]
    tm = ROW_TILE
    row = lambda i: (i, 0)
    return pl.pallas_call(
        _router_kernel,
        grid=(t // tm,),
        in_specs=[
            pl.BlockSpec((tm, D_MODEL), row),
            pl.BlockSpec((1, D_MODEL), lambda i: (0, 0)),
            pl.BlockSpec((2, D_MODEL, LANES), lambda i: (0, 0, 0)),
        ],
        out_specs=[pl.BlockSpec((tm * CHUNKS, LANES), row),
                   pl.BlockSpec((tm, LANES), row),
                   pl.BlockSpec((tm, LANES), row)],
        out_shape=[jax.ShapeDtypeStruct((t * CHUNKS, LANES), F32),
                   jax.ShapeDtypeStruct((t, LANES), jnp.int32),
                   jax.ShapeDtypeStruct((t, LANES), F32)],
        compiler_params=_params("parallel"),
        name="router",
    )(h2, norm_g, wr3)


def _row_copy(src, src_row, dst, dst_row, sem):
    return pltpu.make_async_copy(src.at[pl.ds(src_row * CHUNKS, CHUNKS), :],
                                 dst.at[pl.ds(dst_row * CHUNKS, CHUNKS), :], sem)


def _rows_wait(src, dst, rows, sem):
    pltpu.make_async_copy(src.at[pl.ds(0, rows * CHUNKS), :],
                          dst.at[pl.ds(0, rows * CHUNKS), :], sem).wait()


def _dispatch_kernel(dest_ref, fill_ref, z_ref, wg_ref, wu_ref, wd_ref,
                     x_hbm, wg_out, wu_out, wd_out, zero_scr, fsem, sem):
    i = pl.program_id(0)
    blk = MOE_BLOCK
    tile = z_ref.shape[0] // CHUNKS
    wg_out[...] = wg_ref[...].astype(BF16)
    wu_out[...] = wu_ref[...].astype(BF16)
    wd_out[...] = wd_ref[...].astype(BF16)

    @pl.when(i == 0)
    def _():
        zero_scr[...] = jnp.zeros_like(zero_scr)
        for e in range(fill_ref.shape[0]):
            @pl.when(fill_ref[e] >= 0)
            def _():
                start = pl.multiple_of(fill_ref[e] * CHUNKS, blk * CHUNKS)
                pltpu.make_async_copy(zero_scr, x_hbm.at[pl.ds(start, blk * CHUNKS), :], fsem).start()
        for e in range(fill_ref.shape[0]):
            @pl.when(fill_ref[e] >= 0)
            def _():
                _rows_wait(zero_scr, x_hbm, blk, fsem)

    base = i * tile * TOP_K

    tokens_per_trip = MOE_ISSUE_UNROLL // TOP_K

    def trip(it, c):
        for k in range(MOE_ISSUE_UNROLL):
            token = it * tokens_per_trip + k // TOP_K
            slot = dest_ref[base + it * MOE_ISSUE_UNROLL + k]
            _row_copy(z_ref, token, x_hbm, slot, sem).start(priority=k % 2)
        return c
    lax.fori_loop(0, tile // tokens_per_trip, trip, 0)
    for _ in range(TOP_K):
        _rows_wait(z_ref, x_hbm, tile, sem)


def _dispatch(z8, dest, fill, p_rows, w_gate, w_up, w_down):
    tile = DISPATCH_TILE
    steps = z8.shape[0] // (tile * CHUNKS)
    ne, d, ff = w_gate.shape
    assert z8.shape[0] % (tile * CHUNKS) == 0 and (ne * d) % steps == 0 and (ne * ff) % steps == 0
    up_rows, down_rows = ne * d // steps, ne * ff // steps
    wspec = lambda rows, cols: pl.BlockSpec((rows, cols), lambda i, dd, f: (i, 0))
    grid_spec = pltpu.PrefetchScalarGridSpec(
        num_scalar_prefetch=2,
        grid=(steps,),
        in_specs=[pl.BlockSpec((tile * CHUNKS, LANES), lambda i, dd, f: (i, 0)),
                  wspec(up_rows, ff), wspec(up_rows, ff), wspec(down_rows, d)],
        out_specs=[pl.BlockSpec(memory_space=pl.ANY),
                   wspec(up_rows, ff), wspec(up_rows, ff), wspec(down_rows, d)],
        scratch_shapes=[pltpu.VMEM((MOE_BLOCK * CHUNKS, LANES), F32),
                        pltpu.SemaphoreType.DMA(()), pltpu.SemaphoreType.DMA(())],
    )
    x8, wg, wu, wd = pl.pallas_call(
        _dispatch_kernel,
        grid_spec=grid_spec,
        out_shape=[jax.ShapeDtypeStruct((p_rows * CHUNKS, LANES), F32),
                   jax.ShapeDtypeStruct((ne * d, ff), BF16),
                   jax.ShapeDtypeStruct((ne * d, ff), BF16),
                   jax.ShapeDtypeStruct((ne * ff, d), BF16)],
        compiler_params=_params("arbitrary"),
        name="moe_dispatch",
    )(dest, fill, z8, w_gate.reshape(ne * d, ff), w_up.reshape(ne * d, ff),
      w_down.reshape(ne * ff, d))
    return x8, wg.reshape(ne, d, ff), wu.reshape(ne, d, ff), wd.reshape(ne, ff, d)


def _moe_kernel(be_ref, live_ref, bx_ref, x_ref, wg_ref, wu_ref, wd_ref, y_ref, acc):
    i = pl.program_id(0)
    j = pl.program_id(1)
    tm = acc.shape[0]

    @pl.when((i == 0) & (j == 0))
    def _():
        acc[...] = jnp.zeros_like(acc)

    @pl.when((live_ref[i] == 0) & (j == 0))
    def _():
        y_ref[...] = jnp.zeros_like(y_ref)

    @pl.when(live_ref[i] > 0)
    def _():
        x = jnp.concatenate([x_ref[pl.ds(c, tm, stride=CHUNKS), :].astype(BF16)
                             for c in range(CHUNKS)], axis=1)
        a = (_silu(_dot(x, wg_ref[0])) * _dot(x, wu_ref[0])).astype(BF16)
        total = acc[...] + _dot(a, wd_ref[0])
        for c in range(CHUNKS):
            y_ref[pl.ds(c, tm, stride=CHUNKS), :] = total[:, c * LANES:(c + 1) * LANES]
        acc[...] = jnp.where(j == pl.num_programs(1) - 1, 0.0, total)


def _moe_experts(x8, blk_e, blk_live, blk_x, w_gate, w_up, w_down):
    tm, tf = MOE_BLOCK, MOE_FF_TILE
    nblk = blk_e.shape[0]
    nff = D_FF // tf
    ff = lambda i, j, live: jnp.where(live[i] > 0, j, nff - 1)
    grid_spec = pltpu.PrefetchScalarGridSpec(
        num_scalar_prefetch=3,
        grid=(nblk, nff),
        in_specs=[
            pl.BlockSpec((tm * CHUNKS, LANES), lambda i, j, be, lv, bx: (bx[i], 0)),
            pl.BlockSpec((1, D_MODEL, tf), lambda i, j, be, lv, bx: (be[i], 0, ff(i, j, lv))),
            pl.BlockSpec((1, D_MODEL, tf), lambda i, j, be, lv, bx: (be[i], 0, ff(i, j, lv))),
            pl.BlockSpec((1, tf, D_MODEL), lambda i, j, be, lv, bx: (be[i], ff(i, j, lv), 0)),
        ],
        out_specs=pl.BlockSpec((tm * CHUNKS, LANES), lambda i, j, be, lv, bx: (i, 0)),
        scratch_shapes=[pltpu.VMEM((tm, D_MODEL), F32)],
    )
    return pl.pallas_call(
        _moe_kernel,
        grid_spec=grid_spec,
        out_shape=jax.ShapeDtypeStruct(x8.shape, F32),
        compiler_params=_params("arbitrary", "arbitrary"),
        name="moe_experts",
    )(blk_e, blk_live, blk_x, x8, w_gate, w_up, w_down)


def _combine_kernel(dest_ref, h_ref, gate_ref, *rest, final_norm):
    if final_norm:
        g_ref, y_hbm, o_ref, ybuf, sem = rest
    else:
        y_hbm, o_ref, ybuf, sem = rest
    i = pl.program_id(0)
    tm = h_ref.shape[0]
    rows = tm * TOP_K
    stride = TOP_K * CHUNKS

    def pull(tile, slot):
        def trip(it, c):
            for k in range(MOE_ISSUE_UNROLL):
                r = it * MOE_ISSUE_UNROLL + k
                copy = _row_copy(y_hbm, dest_ref[tile * rows + r], ybuf.at[slot], r, sem.at[slot])
                copy.start(priority=k % 2)
            return c
        lax.fori_loop(0, rows // MOE_ISSUE_UNROLL, trip, 0)

    @pl.when(i == 0)
    def _():
        pull(0, 0)

    @pl.when(i + 1 < pl.num_programs(0))
    def _():
        pull(i + 1, (i + 1) % 2)

    slot = i % 2
    yb = ybuf.at[slot]
    _rows_wait(y_hbm, yb, rows, sem.at[slot])
    gates = [gate_ref[:, k:k + 1] for k in range(TOP_K)]
    parts = []
    ss = jnp.zeros((tm, 1), F32)
    for c in range(CHUNKS):
        y = gates[0] * yb[pl.ds(c, tm, stride=stride), :]
        for k in range(1, TOP_K):
            y = y + gates[k] * yb[pl.ds(k * CHUNKS + c, tm, stride=stride), :]
        hc = h_ref[:, c * LANES:(c + 1) * LANES] + y
        parts.append(hc)
        ss = ss + jnp.sum(hc * hc, axis=1, keepdims=True)
    if final_norm:
        inv = lax.rsqrt(ss / D_MODEL + RMS_EPS)
    for c in range(CHUNKS):
        cols = slice(c * LANES, (c + 1) * LANES)
        o_ref[:, cols] = parts[c] * inv * g_ref[:, cols] if final_norm else parts[c]


def _combine(h2, y8, dest, gate, out_g):
    t = h2.shape[0]
    tm = ROW_TILE
    final_norm = out_g is not None
    row = lambda i, d: (i, 0)
    in_specs = [pl.BlockSpec((tm, D_MODEL), row), pl.BlockSpec((tm, LANES), row)]
    args = [h2, gate]
    if final_norm:
        in_specs.append(pl.BlockSpec((1, D_MODEL), lambda i, d: (0, 0)))
        args.append(out_g.reshape(1, D_MODEL))
    grid_spec = pltpu.PrefetchScalarGridSpec(
        num_scalar_prefetch=1,
        grid=(t // tm,),
        in_specs=in_specs + [pl.BlockSpec(memory_space=pl.ANY)],
        out_specs=pl.BlockSpec((tm, D_MODEL), row),
        scratch_shapes=[pltpu.VMEM((2, tm * TOP_K * CHUNKS, LANES), F32),
                        pltpu.SemaphoreType.DMA((2,))],
    )
    return pl.pallas_call(
        functools.partial(_combine_kernel, final_norm=final_norm),
        grid_spec=grid_spec,
        out_shape=jax.ShapeDtypeStruct((t, D_MODEL), F32),
        compiler_params=_params("arbitrary"),
        name="moe_combine",
    )(dest, *args, y8)


def _norm_kernel(h_ref, g_ref, o_ref):
    x = h_ref[...]
    o_ref[...] = x * lax.rsqrt(jnp.mean(x * x, axis=-1, keepdims=True) + RMS_EPS) * g_ref[...]


def _final_norm(h2, norm_g):
    t = h2.shape[0]
    tm = ROW_TILE
    return pl.pallas_call(
        _norm_kernel,
        grid=(t // tm,),
        in_specs=[pl.BlockSpec((tm, D_MODEL), lambda i: (i, 0)),
                  pl.BlockSpec((1, D_MODEL), lambda i: (0, 0))],
        out_specs=pl.BlockSpec((tm, D_MODEL), lambda i: (i, 0)),
        out_shape=jax.ShapeDtypeStruct((t, D_MODEL), F32),
        compiler_params=_params("parallel"),
        name="final_norm",
    )(h2, norm_g)


def _dispatch_plan(top_i, t):
    n = t * TOP_K
    tm = MOE_BLOCK
    e_flat = top_i.reshape(n)
    onehot = (e_flat[:, None] == jnp.arange(N_EXPERTS)[None, :]).astype(jnp.int32)
    counts = jnp.sum(onehot, axis=0)
    rank = jnp.sum((jnp.cumsum(onehot, axis=0) - onehot) * onehot, axis=1)
    padded = (counts + tm - 1) // tm * tm
    pends = jnp.cumsum(padded)
    pstarts = pends - padded
    dest = (jnp.sum(onehot * pstarts[None, :], axis=1) + rank).astype(jnp.int32)
    p_rows = -(-n // tm) * tm + N_EXPERTS * tm
    nblk = p_rows // tm
    blk = jnp.arange(nblk, dtype=jnp.int32)
    blk_e = jnp.minimum(jnp.sum((blk * tm)[:, None] >= pends[None, :], axis=1), N_EXPERTS - 1)
    live = blk * tm < pends[-1]
    last_live = jnp.maximum(pends[-1] // tm - 1, 0)
    blk_e = jnp.where(live, blk_e, blk_e[last_live])
    blk_x = jnp.where(live, blk, 0)
    tail = jnp.where(padded > 0, pends - tm, -1)
    spare = pends[-1] + jnp.arange(N_EXPERTS) * tm
    fill = jnp.concatenate([tail, jnp.where(spare < p_rows, spare, -1)])
    i32 = lambda a: a.astype(jnp.int32)
    return dest, i32(fill), i32(blk_e), i32(live), i32(blk_x), p_rows


def _rope_tables(seq):
    inv = 1.0 / (ROPE_THETA ** (jnp.arange(0, HEAD_DIM, 2, dtype=F32) / HEAD_DIM))
    ang = jnp.arange(seq, dtype=F32)[:, None] * inv[None, :]
    cos = jnp.concatenate([jnp.cos(ang)] * 4, axis=-1)
    sin = jnp.concatenate([jnp.sin(ang)] * 4, axis=-1)
    upper = (jnp.arange(LANES) % HEAD_DIM) >= HEAD_DIM // 2
    sa = jnp.where(upper[None, :], sin, 0.0)
    sb = jnp.where(upper[None, :], 0.0, -sin)
    return cos, sa, sb


def _take(x, segments, axis):
    return jnp.concatenate([lax.slice_in_dim(x, a, a + n, axis=axis) for a, n in segments], axis=axis)


def _swa_pair_segments(base):
    half = SWA_Q_HEADS // 2
    segs = []
    for j in range(half):
        segs += [(base + j * HEAD_DIM, HEAD_DIM), (base + (j + half) * HEAD_DIM, HEAD_DIM)]
    return segs


def _in_proj_segments():
    split = np.cumsum([0, SWA_WIDTH, SWA_KV_WIDTH, SWA_KV_WIDTH, FOX_WIDTH, FOX_WIDTH, FOX_WIDTH,
                       FOX_HEADS, MOBA_WIDTH, MOBA_WIDTH, MOBA_WIDTH])
    aq, ak, bf, cq, end = (int(split[i]) for i in (0, 1, 6, 7, 10))
    return _swa_pair_segments(aq) + [(ak, bf - ak), (cq, end - cq)], (bf, FOX_HEADS)


def _swa_out_segments():
    return _swa_pair_segments(0) + [(SWA_WIDTH, MIX_WIDTH - SWA_WIDTH)]


def _mixer(h2, b, s, norm_g, w_in, forget_bias, sinks, mix_gain, w_out, tables):
    segs, fseg = _in_proj_segments()
    w_all = jnp.concatenate(
        [_take(w_in, segs, 1), jnp.pad(_take(w_in, [fseg], 1), ((0, 0), (0, LANES - FOX_HEADS)))],
        axis=1).astype(BF16)
    fbias = jnp.pad(forget_bias.astype(F32), (0, LANES - FOX_HEADS)).reshape(1, LANES)
    u, lf = _project(h2, norm_g.reshape(1, D_MODEL), w_all, *tables, fbias, s)
    u3 = u.reshape(b, s, U_WIDTH)
    oa = _swa(u3, sinks.astype(F32))
    ob = _fox(u3, lf.reshape(b, s, LANES))
    oc = _moba(u3)
    segs = _swa_out_segments()
    t = b * s
    return _outproj(oa.reshape(t, SWA_WIDTH), ob.reshape(t, FOX_WIDTH), oc.reshape(t, MOBA_WIDTH),
                    _take(mix_gain.astype(F32), segs, 0).reshape(1, MIX_WIDTH),
                    _take(w_out, segs, 0).astype(BF16), h2)


def _moe_ffn(h2, norm_g, w_router, w_gate, w_up, w_down, out_g):
    t = h2.shape[0]
    wr = jnp.pad(w_router.astype(F32), ((0, 0), (0, LANES - N_EXPERTS)))
    wr2 = jnp.stack(_split3(wr)[:2])
    z8, top, gate = _router(h2, norm_g.reshape(1, D_MODEL), wr2)
    dest, fill, blk_e, blk_live, blk_x, p_rows = _dispatch_plan(top[:, :TOP_K], t)
    x8, wg, wu, wd = _dispatch(z8, dest, fill, p_rows, w_gate, w_up, w_down)
    y8 = _moe_experts(x8, blk_e, blk_live, blk_x, wg, wu, wd)
    return _combine(h2, y8, dest, gate, out_g)


def kernel(x, attn_norm, w_in, fox_forget_bias, swa_sinks, mix_gain, w_out, ffn_norm,
           dense_w_gate, dense_w_up, dense_w_down, router_w, moe_w_gate, moe_w_up,
           moe_w_down, final_norm):
    b, s, d = x.shape
    depth = attn_norm.shape[0]
    assert d == D_MODEL and s % MOBA_BLOCK == 0 and s % ROW_TILE == 0
    tables = _rope_tables(s)
    h = x.reshape(b * s, d)
    normed = False
    for layer in range(depth):
        h = _mixer(h, b, s, attn_norm[layer], w_in[layer], fox_forget_bias[layer],
                   swa_sinks[layer], mix_gain[layer], w_out[layer], tables)
        j = layer // 2
        last = layer == depth - 1
        if layer % 2 == 0:
            h = _dense_ffn(h, ffn_norm[layer].reshape(1, d), dense_w_gate[j], dense_w_up[j],
                           dense_w_down[j])
        else:
            h = _moe_ffn(h, ffn_norm[layer], router_w[j], moe_w_gate[j], moe_w_up[j],
                         moe_w_down[j], final_norm if last else None)
            normed = last
    if not normed:
        h = _final_norm(h, final_norm.reshape(1, d))
    return h.reshape(b, s, d)
```

```python
import functools

import jax
import jax.numpy as jnp
import numpy as np
from jax import lax
from jax.experimental import pallas as pl
from jax.experimental.pallas import tpu as pltpu

F32 = jnp.float32
BF16 = jnp.bfloat16

D_MODEL = 1024
HEAD_DIM = 64
LANES = 128
SWA_Q_HEADS = 8
SWA_KV_HEADS = 2
SWA_WINDOW = 128
FOX_HEADS = 4
MOBA_HEADS = 4
MOBA_BLOCK = 256
MOBA_TOPK = 3
ROPE_THETA = 10000.0
RMS_EPS = 1e-5
D_FF = 3584
N_EXPERTS = 8
TOP_K = 2
MOE_BLOCK = 512
NEG_INF = -1e30
ATTN_SCALE = HEAD_DIM ** -0.5
LOG2E = 1.4426950408889634
Q_SCALE = ATTN_SCALE * LOG2E

SWA_WIDTH = SWA_Q_HEADS * HEAD_DIM
SWA_KV_WIDTH = SWA_KV_HEADS * HEAD_DIM
FOX_WIDTH = FOX_HEADS * HEAD_DIM
MOBA_WIDTH = MOBA_HEADS * HEAD_DIM
MIX_WIDTH = SWA_WIDTH + FOX_WIDTH + MOBA_WIDTH

U_BLOCKS = 18
U_WIDTH = U_BLOCKS * LANES
ROPE_BLOCKS = (0, 1, 2, 3, 4, 12, 13, 14, 15)
Q_BLOCKS = (0, 1, 2, 3, 6, 7, 12, 13)
W_ALL_WIDTH = U_WIDTH + LANES

VMEM_LIMIT = 56 * 1024 * 1024

ROW_TILE = 512
ATT_TILE = 256
SWA_STEP_BLOCKS = 4
FF_TILE = 512
MOE_FF_TILE = 1792
CHUNKS = D_MODEL // LANES
MOE_ISSUE_UNROLL = 8
DISPATCH_TILE = 512


def _params(*sem):
    return pltpu.CompilerParams(dimension_semantics=sem, vmem_limit_bytes=VMEM_LIMIT)


def _split3(x):
    hi = x.astype(BF16)
    r1 = x - hi.astype(F32)
    mid = r1.astype(BF16)
    lo = (r1 - mid.astype(F32)).astype(BF16)
    return hi, mid, lo


def _dot_nt(a, b):
    return lax.dot_general(a, b, (((1,), (1,)), ((), ())), preferred_element_type=F32)


def _dot(a, b):
    return jnp.dot(a, b, preferred_element_type=F32)


def _proj_kernel(x_ref, g_ref, w_ref, cos_ref, sa_ref, sb_ref, fb_ref, u_ref, lf_ref):
    x = x_ref[...]
    inv = lax.rsqrt(jnp.mean(x * x, axis=-1, keepdims=True) + RMS_EPS)
    h = (x * inv * g_ref[...]).astype(BF16)
    cos = cos_ref[...]
    sa = sa_ref[...]
    sb = sb_ref[...]
    for c in range(U_BLOCKS // 2):
        acc = _dot(h, w_ref[:, c * 2 * LANES:(c + 1) * 2 * LANES])
        for half in range(2):
            blk = 2 * c + half
            a = acc[:, half * LANES:(half + 1) * LANES]
            if blk in ROPE_BLOCKS:
                a = a * cos + pltpu.roll(a, 32, 1) * sa + pltpu.roll(a, 96, 1) * sb
            if blk in Q_BLOCKS:
                a = a * Q_SCALE
            u_ref[:, blk * LANES:(blk + 1) * LANES] = a.astype(BF16)
    f = _dot(h, w_ref[:, U_WIDTH:W_ALL_WIDTH]) + fb_ref[...]
    lf_ref[...] = jnp.minimum(f, 0.0) - jnp.log(1.0 + jnp.exp(-jnp.abs(f)))


def _project(x2, norm_g, w_all, cos, sa, sb, fbias, seq):
    t = x2.shape[0]
    tm = ROW_TILE
    nseq = seq // tm
    row = lambda i: (i, 0)
    pos = lambda i: (i % nseq, 0)
    fixed = lambda i: (0, 0)
    return pl.pallas_call(
        _proj_kernel,
        grid=(t // tm,),
        in_specs=[
            pl.BlockSpec((tm, D_MODEL), row),
            pl.BlockSpec((1, D_MODEL), fixed),
            pl.BlockSpec((D_MODEL, W_ALL_WIDTH), fixed),
            pl.BlockSpec((tm, LANES), pos),
            pl.BlockSpec((tm, LANES), pos),
            pl.BlockSpec((tm, LANES), pos),
            pl.BlockSpec((1, LANES), fixed),
        ],
        out_specs=[pl.BlockSpec((tm, U_WIDTH), row), pl.BlockSpec((tm, LANES), row)],
        out_shape=[jax.ShapeDtypeStruct((t, U_WIDTH), BF16),
                   jax.ShapeDtypeStruct((t, LANES), F32)],
        compiler_params=_params("parallel"),
        name="proj",
    )(x2, norm_g, w_all, cos, sa, sb, fbias)


def _swa_kernel(sink_ref, q_ref, kc_ref, kp_ref, vc_ref, vp_ref, o_ref):
    n = pl.program_id(1)
    w = SWA_WINDOW
    group = SWA_Q_HEADS // SWA_KV_HEADS
    lane = lax.broadcasted_iota(jnp.int32, (1, LANES), 1)
    k_all = jnp.concatenate([kp_ref[0], kc_ref[0]], axis=0)
    v_all = jnp.concatenate([vp_ref[0], vc_ref[0]], axis=0)
    qi = lax.broadcasted_iota(jnp.int32, (group * w, 2 * w), 0) % w
    kj = lax.broadcasted_iota(jnp.int32, (group * w, 2 * w), 1)
    window = (kj > qi) & (kj <= qi + w)
    for a in range(SWA_STEP_BLOCKS):
        rows = slice(a * w, (a + 1) * w)
        k2 = k_all[a * w:(a + 2) * w]
        v2 = v_all[a * w:(a + 2) * w]
        valid = (window & ((kj >= w) | (n > 0))) if a == 0 else window
        outs = []
        for g in range(SWA_KV_HEADS):
            keep = (lane < HEAD_DIM) if g == 0 else (lane >= HEAD_DIM)
            qs = jnp.concatenate(
                [jnp.where(keep, q_ref[0, rows, j * LANES:(j + 1) * LANES], 0)
                 for j in range(group)], axis=0)
            s = jnp.where(valid, _dot_nt(qs, k2), NEG_INF)
            ps, inv = [], []
            for j in range(group):
                sj = s[j * w:(j + 1) * w]
                sink = sink_ref[g * group + j] * LOG2E
                m = jnp.maximum(jnp.max(sj, axis=1, keepdims=True), sink)
                p = jnp.exp2(sj - m)
                inv.append(1.0 / (jnp.sum(p, axis=1, keepdims=True) + jnp.exp2(sink - m)))
                ps.append(p.astype(BF16))
            o = _dot(jnp.concatenate(ps, axis=0), v2)
            outs.append([o[j * w:(j + 1) * w] * inv[j] for j in range(group)])
        for j in range(group):
            oj = jnp.where(lane < HEAD_DIM, outs[0][j], outs[1][j])
            o_ref[0, rows, j * LANES:(j + 1) * LANES] = oj.astype(BF16)


def _swa(u3, sinks):
    b, s, _ = u3.shape
    w = SWA_WINDOW
    nb = SWA_STEP_BLOCKS
    assert s % (nb * w) == 0
    cur = lambda blk: (lambda bi, n, sk: (bi, n, blk))
    prev = lambda blk: (lambda bi, n, sk: (bi, jnp.maximum(n * nb - 1, 0), blk))
    grid_spec = pltpu.PrefetchScalarGridSpec(
        num_scalar_prefetch=1,
        grid=(b, s // (nb * w)),
        in_specs=[
            pl.BlockSpec((1, nb * w, SWA_WIDTH), lambda bi, n, sk: (bi, n, 0)),
            pl.BlockSpec((1, nb * w, LANES), cur(4)),
            pl.BlockSpec((1, w, LANES), prev(4)),
            pl.BlockSpec((1, nb * w, LANES), cur(5)),
            pl.BlockSpec((1, w, LANES), prev(5)),
        ],
        out_specs=pl.BlockSpec((1, nb * w, SWA_WIDTH), lambda bi, n, sk: (bi, n, 0)),
    )
    return pl.pallas_call(
        _swa_kernel,
        grid_spec=grid_spec,
        out_shape=jax.ShapeDtypeStruct((b, s, SWA_WIDTH), BF16),
        compiler_params=_params("parallel", "parallel"),
        name="swa",
    )(sinks, u3, u3, u3, u3, u3)


def _head_lanes(hh):
    lane = lax.broadcasted_iota(jnp.int32, (1, LANES), 1)
    if hh == 0:
        return lane < HEAD_DIM, lane - HEAD_DIM
    return lane >= HEAD_DIM, lane


def _causal_attention(qp_scr, kp_scr, v_ref, o_ref, t):
    s_len = qp_scr.shape[1]
    lane = lax.broadcasted_iota(jnp.int32, (1, LANES), 1)
    row = lax.broadcasted_iota(jnp.int32, (t, t), 0)
    col = lax.broadcasted_iota(jnp.int32, (t, t), 1)
    for qi in range(s_len // t):
        rows = slice(qi * t, (qi + 1) * t)
        outs = []
        for hh in range(2):
            q = qp_scr[hh, rows, :]
            sd = jnp.where(col <= row, _dot_nt(q, kp_scr[hh, rows, :]), NEG_INF)
            m = jnp.max(sd, axis=1, keepdims=True)
            if qi:
                sp = _dot_nt(q, kp_scr[hh, :qi * t, :])
                m = jnp.maximum(m, jnp.max(sp, axis=1, keepdims=True))
            pd = jnp.exp2(sd - m)
            l = jnp.sum(pd, axis=1, keepdims=True)
            o = _dot(pd.astype(BF16), v_ref[0, rows, :])
            if qi:
                pp = jnp.exp2(sp - m)
                l = l + jnp.sum(pp, axis=1, keepdims=True)
                o = o + _dot(pp.astype(BF16), v_ref[0, :qi * t, :])
            outs.append(o * (1.0 / l))
        o_ref[0, rows, :] = jnp.where(lane < HEAD_DIM, outs[0], outs[1]).astype(BF16)


def _pair_attention(kernel_fn, u3, extra, q_blk, k_blk, v_blk, n_heads, name):
    b, s, _ = u3.shape
    pair = lambda blk: (lambda bi, p: (bi, 0, blk + p))
    seq_block = pl.BlockSpec((1, s, LANES), lambda bi, p: (bi, 0, 0))
    return pl.pallas_call(
        kernel_fn,
        grid=(b, n_heads // 2),
        in_specs=[pl.BlockSpec((1, s, LANES), pair(q_blk)),
                  pl.BlockSpec((1, s, LANES), pair(k_blk)),
                  pl.BlockSpec((1, s, LANES), pair(v_blk))] + [seq_block] * len(extra),
        out_specs=pl.BlockSpec((1, s, LANES), pair(0)),
        out_shape=jax.ShapeDtypeStruct((b, s, n_heads * HEAD_DIM), BF16),
        scratch_shapes=[pltpu.VMEM((2, s, LANES), BF16), pltpu.VMEM((2, s, LANES), BF16)],
        compiler_params=_params("parallel", "parallel"),
        name=name,
    )(u3, u3, u3, *extra)


def _fox_kernel(q_ref, k_ref, v_ref, lf_ref, o_ref, qp_scr, kp_scr):
    p = pl.program_id(1)
    s_len = q_ref.shape[1]
    t = ATT_TILE
    lane = lax.broadcasted_iota(jnp.int32, (1, LANES), 1)
    r = lax.broadcasted_iota(jnp.int32, (t, t), 0)
    cc = lax.broadcasted_iota(jnp.int32, (t, t), 1)
    tri = jnp.where(cc <= r, 1.0, 0.0).astype(BF16)
    carry = jnp.zeros((1, LANES), F32)
    for i in range(s_len // t):
        rows = slice(i * t, (i + 1) * t)
        lf = lf_ref[0, rows, :]
        lf0 = jnp.sum(jnp.where(lane == 2 * p, lf, 0.0), axis=1, keepdims=True)
        lf1 = jnp.sum(jnp.where(lane == 2 * p + 1, lf, 0.0), axis=1, keepdims=True)
        hi, mid, lo = _split3(jnp.where(lane < 3, lf0, jnp.where(lane < 6, lf1, 0.0)))
        terms = jnp.where((lane == 0) | (lane == 3), hi,
                          jnp.where((lane == 1) | (lane == 4), mid, lo))
        sums = _dot(tri, terms) + carry
        carry = sums[t - 1:t, :]
        c0 = jnp.sum(jnp.where(lane < 3, sums, 0.0), axis=1, keepdims=True)
        c1 = jnp.sum(jnp.where((lane >= 3) & (lane < 6), sums, 0.0), axis=1, keepdims=True)
        chi, cmid, clo = _split3(jnp.where(lane < HEAD_DIM, c1, c0) * LOG2E)
        for hh in range(2):
            keep, f = _head_lanes(hh)
            ones_q = jnp.where((f >= 3) & (f < 6), 1.0, 0.0).astype(BF16)
            ones_k = jnp.where((f >= 0) & (f < 3), 1.0, 0.0).astype(BF16)
            qf = jnp.where(f == 0, chi, jnp.where(f == 1, cmid, jnp.where(f == 2, clo, ones_q)))
            kf = jnp.where(f == 3, -chi, jnp.where(f == 4, -cmid, jnp.where(f == 5, -clo, ones_k)))
            qp_scr[hh, rows, :] = jnp.where(keep, q_ref[0, rows, :], qf)
            kp_scr[hh, rows, :] = jnp.where(keep, k_ref[0, rows, :], kf)
    _causal_attention(qp_scr, kp_scr, v_ref, o_ref, t)


def _fox(u3, lf3):
    return _pair_attention(_fox_kernel, u3, (lf3,), 6, 8, 10, FOX_HEADS, "fox")


def _moba_kernel(q_ref, k_ref, v_ref, o_ref, qp_scr, kp_scr):
    s_len = q_ref.shape[1]
    t = MOBA_BLOCK
    nkb = s_len // t
    sub = 8
    assert nkb <= sub
    means = [jnp.mean(k_ref[0, j * t:(j + 1) * t, :].astype(F32), axis=0, keepdims=True)
             for j in range(nkb)]
    km_pair = jnp.concatenate(means + [jnp.zeros((sub - nkb, LANES), F32)], axis=0) \
        if nkb < sub else jnp.concatenate(means, axis=0)
    stack = []
    for hh in range(2):
        keep, _ = _head_lanes(hh)
        stack += [part.astype(F32) for part in _split3(jnp.where(keep, km_pair, 0.0))]
    stack.append(jnp.zeros((LANES - 6 * sub, LANES), F32))
    km_all = jnp.concatenate(stack, axis=0).astype(BF16)
    blk = lax.broadcasted_iota(jnp.int32, (sub, t), 0)
    gates = [_dot_nt(km_all, q_ref[0, i * t:(i + 1) * t, :]) for i in range(nkb)]
    for hh in range(2):
        keep, f = _head_lanes(hh)
        for i in range(nkb):
            rows = slice(i * t, (i + 1) * t)
            q = q_ref[0, rows, :]
            base = 3 * sub * hh
            g = (gates[i][base:base + sub] + gates[i][base + sub:base + 2 * sub]) \
                + gates[i][base + 2 * sub:base + 3 * sub]
            past = blk < i
            g = jnp.where(past, g, NEG_INF)
            rank = jnp.zeros((sub, t), jnp.int32)
            for r in range(i):
                gr = jnp.broadcast_to(g[r:r + 1, :], (sub, t))
                beats = (gr > g) | ((gr == g) & (r < blk))
                rank = rank + beats.astype(jnp.int32)
            drop = jnp.where(past & (rank >= MOBA_TOPK), 1.0, 0.0)
            below = [jnp.zeros((HEAD_DIM, t), F32)] if hh == 0 else []
            above = jnp.zeros((LANES - sub - (HEAD_DIM if hh == 0 else 0), t), F32)
            padded = jnp.concatenate(below + [drop, above], axis=0)
            kf = jnp.where(f == i, NEG_INF, 0.0).astype(BF16)
            qp_scr[hh, rows, :] = jnp.where(keep, q, padded.T.astype(BF16))
            kp_scr[hh, rows, :] = jnp.where(keep, k_ref[0, rows, :], kf)
    _causal_attention(qp_scr, kp_scr, v_ref, o_ref, t)


def _moba(u3):
    return _pair_attention(_moba_kernel, u3, (), 12, 14, 16, MOBA_HEADS, "moba")


def _rms_f32(x):
    x = x.astype(F32)
    return x * lax.rsqrt(jnp.mean(x * x, axis=-1, keepdims=True) + RMS_EPS)


def _outproj_kernel(oa_ref, ob_ref, oc_ref, gain_ref, w_ref, x_ref, h_ref):
    a0, a1, a2 = 0, SWA_WIDTH, SWA_WIDTH + FOX_WIDTH
    ya = (_rms_f32(oa_ref[...]) * gain_ref[:, a0:a1]).astype(BF16)
    yb = (_rms_f32(ob_ref[...]) * gain_ref[:, a1:a2]).astype(BF16)
    yc = (_rms_f32(oc_ref[...]) * gain_ref[:, a2:]).astype(BF16)
    y = (_dot(ya, w_ref[a0:a1, :]) + _dot(yb, w_ref[a1:a2, :])) + _dot(yc, w_ref[a2:, :])
    h_ref[...] = x_ref[...] + y


def _outproj(oa, ob, oc, gain, w_out, x2):
    t = x2.shape[0]
    tm = ROW_TILE
    row = lambda i: (i, 0)
    fixed = lambda i: (0, 0)
    return pl.pallas_call(
        _outproj_kernel,
        grid=(t // tm,),
        in_specs=[
            pl.BlockSpec((tm, SWA_WIDTH), row),
            pl.BlockSpec((tm, FOX_WIDTH), row),
            pl.BlockSpec((tm, MOBA_WIDTH), row),
            pl.BlockSpec((1, MIX_WIDTH), fixed),
            pl.BlockSpec((MIX_WIDTH, D_MODEL), fixed),
            pl.BlockSpec((tm, D_MODEL), row),
        ],
        out_specs=pl.BlockSpec((tm, D_MODEL), row),
        out_shape=jax.ShapeDtypeStruct((t, D_MODEL), F32),
        compiler_params=_params("parallel"),
        name="outproj",
    )(oa, ob, oc, gain, w_out, x2)


def _silu(x):
    return x / (1.0 + jnp.exp(-x))


def _ffn_kernel(h_ref, g_ref, wg_ref, wu_ref, wd_ref, o_ref, acc_scr):
    j = pl.program_id(1)

    @pl.when((pl.program_id(0) == 0) & (j == 0))
    def _():
        acc_scr[...] = jnp.zeros_like(acc_scr)

    x = h_ref[...]
    inv = lax.rsqrt(jnp.mean(x * x, axis=-1, keepdims=True) + RMS_EPS)
    z = (x * inv * g_ref[...]).astype(BF16)
    a = (_silu(_dot(z, wg_ref[...].astype(BF16))) * _dot(z, wu_ref[...].astype(BF16))).astype(BF16)
    total = acc_scr[...] + _dot(a, wd_ref[...].astype(BF16))
    o_ref[...] = x + total
    acc_scr[...] = jnp.where(j == pl.num_programs(1) - 1, 0.0, total)


def _dense_ffn(h2, norm_g, w_gate, w_up, w_down):
    t = h2.shape[0]
    tm, tf = 2 * ROW_TILE, FF_TILE
    return pl.pallas_call(
        _ffn_kernel,
        grid=(t // tm, D_FF // tf),
        in_specs=[
            pl.BlockSpec((tm, D_MODEL), lambda i, j: (i, 0)),
            pl.BlockSpec((1, D_MODEL), lambda i, j: (0, 0)),
            pl.BlockSpec((D_MODEL, tf), lambda i, j: (0, j)),
            pl.BlockSpec((D_MODEL, tf), lambda i, j: (0, j)),
            pl.BlockSpec((tf, D_MODEL), lambda i, j: (j, 0)),
        ],
        out_specs=pl.BlockSpec((tm, D_MODEL), lambda i, j: (i, 0)),
        out_shape=jax.ShapeDtypeStruct((t, D_MODEL), F32),
        scratch_shapes=[pltpu.VMEM((tm, D_MODEL), F32)],
        compiler_params=_params("arbitrary", "arbitrary"),
        name="dense_ffn",
    )(h2, norm_g, w_gate, w_up, w_down)


def _router_kernel(h_ref, g_ref, wr_ref, z_ref, idx_ref, gate_ref):
    tm = h_ref.shape[0]
    x = h_ref[...]
    inv = lax.rsqrt(jnp.mean(x * x, axis=-1, keepdims=True) + RMS_EPS)
    z = x * inv * g_ref[...]
    for c in range(CHUNKS):
        z_ref[pl.ds(c, tm, stride=CHUNKS), :] = z[:, c * LANES:(c + 1) * LANES]
    zh, zm, _ = _split3(z)
    wh, wm = wr_ref[0], wr_ref[1]
    logits = _dot(zh, wh) + (_dot(zh, wm) + _dot(zm, wh))
    lane = lax.broadcasted_iota(jnp.int32, (tm, LANES), 1)
    logits = jnp.where(lane < N_EXPERTS, logits, -jnp.inf)
    m1 = jnp.max(logits, axis=1, keepdims=True)
    i1 = jnp.min(jnp.where(logits == m1, lane, LANES), axis=1, keepdims=True)
    rest = jnp.where(lane == i1, -jnp.inf, logits)
    m2 = jnp.max(rest, axis=1, keepdims=True)
    i2 = jnp.min(jnp.where(rest == m2, lane, LANES), axis=1, keepdims=True)
    e2 = jnp.exp(m2 - m1)
    g1 = 1.0 / (1.0 + e2)
    idx_ref[...] = jnp.where(lane == 0, i1, i2)
    gate_ref[...] = jnp.where(lane == 0, g1, e2 * g1)


def _router(h2, norm_g, wr3):
    t = h2.shape[0]
    tm = ROW_TILE
    row = lambda i: (i, 0)
    return pl.pallas_call(
        _router_kernel,
        grid=(t // tm,),
        in_specs=[
            pl.BlockSpec((tm, D_MODEL), row),
            pl.BlockSpec((1, D_MODEL), lambda i: (0, 0)),
            pl.BlockSpec((2, D_MODEL, LANES), lambda i: (0, 0, 0)),
        ],
        out_specs=[pl.BlockSpec((tm * CHUNKS, LANES), row),
                   pl.BlockSpec((tm, LANES), row),
                   pl.BlockSpec((tm, LANES), row)],
        out_shape=[jax.ShapeDtypeStruct((t * CHUNKS, LANES), F32),
                   jax.ShapeDtypeStruct((t, LANES), jnp.int32),
                   jax.ShapeDtypeStruct((t, LANES), F32)],
        compiler_params=_params("parallel"),
        name="router",
    )(h2, norm_g, wr3)


def _row_copy(src, src_row, dst, dst_row, sem):
    return pltpu.make_async_copy(src.at[pl.ds(src_row * CHUNKS, CHUNKS), :],
                                 dst.at[pl.ds(dst_row * CHUNKS, CHUNKS), :], sem)


def _rows_wait(src, dst, rows, sem):
    pltpu.make_async_copy(src.at[pl.ds(0, rows * CHUNKS), :],
                          dst.at[pl.ds(0, rows * CHUNKS), :], sem).wait()


def _dispatch_kernel(dest_ref, fill_ref, z_ref, wg_ref, wu_ref, wd_ref,
                     x_hbm, wg_out, wu_out, wd_out, zero_scr, fsem, sem):
    i = pl.program_id(0)
    blk = MOE_BLOCK
    tile = z_ref.shape[0] // CHUNKS
    wg_out[...] = wg_ref[...].astype(BF16)
    wu_out[...] = wu_ref[...].astype(BF16)
    wd_out[...] = wd_ref[...].astype(BF16)

    @pl.when(i == 0)
    def _():
        zero_scr[...] = jnp.zeros_like(zero_scr)
        for e in range(fill_ref.shape[0]):
            @pl.when(fill_ref[e] >= 0)
            def _():
                start = pl.multiple_of(fill_ref[e] * CHUNKS, blk * CHUNKS)
                pltpu.make_async_copy(zero_scr, x_hbm.at[pl.ds(start, blk * CHUNKS), :], fsem).start()
        for e in range(fill_ref.shape[0]):
            @pl.when(fill_ref[e] >= 0)
            def _():
                _rows_wait(zero_scr, x_hbm, blk, fsem)

    base = i * tile * TOP_K

    tokens_per_trip = MOE_ISSUE_UNROLL // TOP_K

    def trip(it, c):
        for k in range(MOE_ISSUE_UNROLL):
            token = it * tokens_per_trip + k // TOP_K
            slot = dest_ref[base + it * MOE_ISSUE_UNROLL + k]
            _row_copy(z_ref, token, x_hbm, slot, sem).start(priority=k % 2)
        return c
    lax.fori_loop(0, tile // tokens_per_trip, trip, 0)
    for _ in range(TOP_K):
        _rows_wait(z_ref, x_hbm, tile, sem)


def _dispatch(z8, dest, fill, p_rows, w_gate, w_up, w_down):
    tile = DISPATCH_TILE
    steps = z8.shape[0] // (tile * CHUNKS)
    ne, d, ff = w_gate.shape
    assert z8.shape[0] % (tile * CHUNKS) == 0 and (ne * d) % steps == 0 and (ne * ff) % steps == 0
    up_rows, down_rows = ne * d // steps, ne * ff // steps
    wspec = lambda rows, cols: pl.BlockSpec((rows, cols), lambda i, dd, f: (i, 0))
    grid_spec = pltpu.PrefetchScalarGridSpec(
        num_scalar_prefetch=2,
        grid=(steps,),
        in_specs=[pl.BlockSpec((tile * CHUNKS, LANES), lambda i, dd, f: (i, 0)),
                  wspec(up_rows, ff), wspec(up_rows, ff), wspec(down_rows, d)],
        out_specs=[pl.BlockSpec(memory_space=pl.ANY),
                   wspec(up_rows, ff), wspec(up_rows, ff), wspec(down_rows, d)],
        scratch_shapes=[pltpu.VMEM((MOE_BLOCK * CHUNKS, LANES), F32),
                        pltpu.SemaphoreType.DMA(()), pltpu.SemaphoreType.DMA(())],
    )
    x8, wg, wu, wd = pl.pallas_call(
        _dispatch_kernel,
        grid_spec=grid_spec,
        out_shape=[jax.ShapeDtypeStruct((p_rows * CHUNKS, LANES), F32),
                   jax.ShapeDtypeStruct((ne * d, ff), BF16),
                   jax.ShapeDtypeStruct((ne * d, ff), BF16),
                   jax.ShapeDtypeStruct((ne * ff, d), BF16)],
        compiler_params=_params("arbitrary"),
        name="moe_dispatch",
    )(dest, fill, z8, w_gate.reshape(ne * d, ff), w_up.reshape(ne * d, ff),
      w_down.reshape(ne * ff, d))
    return x8, wg.reshape(ne, d, ff), wu.reshape(ne, d, ff), wd.reshape(ne, ff, d)


def _moe_kernel(be_ref, live_ref, bx_ref, x_ref, wg_ref, wu_ref, wd_ref, y_ref, acc):
    i = pl.program_id(0)
    j = pl.program_id(1)
    tm = acc.shape[0]

    @pl.when((i == 0) & (j == 0))
    def _():
        acc[...] = jnp.zeros_like(acc)

    @pl.when((live_ref[i] == 0) & (j == 0))
    def _():
        y_ref[...] = jnp.zeros_like(y_ref)

    @pl.when(live_ref[i] > 0)
    def _():
        x = jnp.concatenate([x_ref[pl.ds(c, tm, stride=CHUNKS), :].astype(BF16)
                             for c in range(CHUNKS)], axis=1)
        a = (_silu(_dot(x, wg_ref[0])) * _dot(x, wu_ref[0])).astype(BF16)
        total = acc[...] + _dot(a, wd_ref[0])
        for c in range(CHUNKS):
            y_ref[pl.ds(c, tm, stride=CHUNKS), :] = total[:, c * LANES:(c + 1) * LANES]
        acc[...] = jnp.where(j == pl.num_programs(1) - 1, 0.0, total)


def _moe_experts(x8, blk_e, blk_live, blk_x, w_gate, w_up, w_down):
    tm, tf = MOE_BLOCK, MOE_FF_TILE
    nblk = blk_e.shape[0]
    nff = D_FF // tf
    ff = lambda i, j, live: jnp.where(live[i] > 0, j, nff - 1)
    grid_spec = pltpu.PrefetchScalarGridSpec(
        num_scalar_prefetch=3,
        grid=(nblk, nff),
        in_specs=[
            pl.BlockSpec((tm * CHUNKS, LANES), lambda i, j, be, lv, bx: (bx[i], 0)),
            pl.BlockSpec((1, D_MODEL, tf), lambda i, j, be, lv, bx: (be[i], 0, ff(i, j, lv))),
            pl.BlockSpec((1, D_MODEL, tf), lambda i, j, be, lv, bx: (be[i], 0, ff(i, j, lv))),
            pl.BlockSpec((1, tf, D_MODEL), lambda i, j, be, lv, bx: (be[i], ff(i, j, lv), 0)),
        ],
        out_specs=pl.BlockSpec((tm * CHUNKS, LANES), lambda i, j, be, lv, bx: (i, 0)),
        scratch_shapes=[pltpu.VMEM((tm, D_MODEL), F32)],
    )
    return pl.pallas_call(
        _moe_kernel,
        grid_spec=grid_spec,
        out_shape=jax.ShapeDtypeStruct(x8.shape, F32),
        compiler_params=_params("arbitrary", "arbitrary"),
        name="moe_experts",
    )(blk_e, blk_live, blk_x, x8, w_gate, w_up, w_down)


def _combine_kernel(dest_ref, h_ref, gate_ref, *rest, final_norm):
    if final_norm:
        g_ref, y_hbm, o_ref, ybuf, sem = rest
    else:
        y_hbm, o_ref, ybuf, sem = rest
    i = pl.program_id(0)
    tm = h_ref.shape[0]
    rows = tm * TOP_K
    stride = TOP_K * CHUNKS

    def pull(tile, slot):
        def trip(it, c):
            for k in range(MOE_ISSUE_UNROLL):
                r = it * MOE_ISSUE_UNROLL + k
                copy = _row_copy(y_hbm, dest_ref[tile * rows + r], ybuf.at[slot], r, sem.at[slot])
                copy.start(priority=k % 2)
            return c
        lax.fori_loop(0, rows // MOE_ISSUE_UNROLL, trip, 0)

    @pl.when(i == 0)
    def _():
        pull(0, 0)

    @pl.when(i + 1 < pl.num_programs(0))
    def _():
        pull(i + 1, (i + 1) % 2)

    slot = i % 2
    yb = ybuf.at[slot]
    _rows_wait(y_hbm, yb, rows, sem.at[slot])
    gates = [gate_ref[:, k:k + 1] for k in range(TOP_K)]
    parts = []
    ss = jnp.zeros((tm, 1), F32)
    for c in range(CHUNKS):
        y = gates[0] * yb[pl.ds(c, tm, stride=stride), :]
        for k in range(1, TOP_K):
            y = y + gates[k] * yb[pl.ds(k * CHUNKS + c, tm, stride=stride), :]
        hc = h_ref[:, c * LANES:(c + 1) * LANES] + y
        parts.append(hc)
        ss = ss + jnp.sum(hc * hc, axis=1, keepdims=True)
    if final_norm:
        inv = lax.rsqrt(ss / D_MODEL + RMS_EPS)
    for c in range(CHUNKS):
        cols = slice(c * LANES, (c + 1) * LANES)
        o_ref[:, cols] = parts[c] * inv * g_ref[:, cols] if final_norm else parts[c]


def _combine(h2, y8, dest, gate, out_g):
    t = h2.shape[0]
    tm = ROW_TILE
    final_norm = out_g is not None
    row = lambda i, d: (i, 0)
    in_specs = [pl.BlockSpec((tm, D_MODEL), row), pl.BlockSpec((tm, LANES), row)]
    args = [h2, gate]
    if final_norm:
        in_specs.append(pl.BlockSpec((1, D_MODEL), lambda i, d: (0, 0)))
        args.append(out_g.reshape(1, D_MODEL))
    grid_spec = pltpu.PrefetchScalarGridSpec(
        num_scalar_prefetch=1,
        grid=(t // tm,),
        in_specs=in_specs + [pl.BlockSpec(memory_space=pl.ANY)],
        out_specs=pl.BlockSpec((tm, D_MODEL), row),
        scratch_shapes=[pltpu.VMEM((2, tm * TOP_K * CHUNKS, LANES), F32),
                        pltpu.SemaphoreType.DMA((2,))],
    )
    return pl.pallas_call(
        functools.partial(_combine_kernel, final_norm=final_norm),
        grid_spec=grid_spec,
        out_shape=jax.ShapeDtypeStruct((t, D_MODEL), F32),
        compiler_params=_params("arbitrary"),
        name="moe_combine",
    )(dest, *args, y8)


def _norm_kernel(h_ref, g_ref, o_ref):
    x = h_ref[...]
    o_ref[...] = x * lax.rsqrt(jnp.mean(x * x, axis=-1, keepdims=True) + RMS_EPS) * g_ref[...]


def _final_norm(h2, norm_g):
    t = h2.shape[0]
    tm = ROW_TILE
    return pl.pallas_call(
        _norm_kernel,
        grid=(t // tm,),
        in_specs=[pl.BlockSpec((tm, D_MODEL), lambda i: (i, 0)),
                  pl.BlockSpec((1, D_MODEL), lambda i: (0, 0))],
        out_specs=pl.BlockSpec((tm, D_MODEL), lambda i: (i, 0)),
        out_shape=jax.ShapeDtypeStruct((t, D_MODEL), F32),
        compiler_params=_params("parallel"),
        name="final_norm",
    )(h2, norm_g)


def _dispatch_plan(top_i, t):
    n = t * TOP_K
    tm = MOE_BLOCK
    e_flat = top_i.reshape(n)
    onehot = (e_flat[:, None] == jnp.arange(N_EXPERTS)[None, :]).astype(jnp.int32)
    counts = jnp.sum(onehot, axis=0)
    rank = jnp.sum((jnp.cumsum(onehot, axis=0) - onehot) * onehot, axis=1)
    padded = (counts + tm - 1) // tm * tm
    pends = jnp.cumsum(padded)
    pstarts = pends - padded
    dest = (jnp.sum(onehot * pstarts[None, :], axis=1) + rank).astype(jnp.int32)
    p_rows = -(-n // tm) * tm + N_EXPERTS * tm
    nblk = p_rows // tm
    blk = jnp.arange(nblk, dtype=jnp.int32)
    blk_e = jnp.minimum(jnp.sum((blk * tm)[:, None] >= pends[None, :], axis=1), N_EXPERTS - 1)
    live = blk * tm < pends[-1]
    last_live = jnp.maximum(pends[-1] // tm - 1, 0)
    blk_e = jnp.where(live, blk_e, blk_e[last_live])
    blk_x = jnp.where(live, blk, 0)
    tail = jnp.where(padded > 0, pends - tm, -1)
    spare = pends[-1] + jnp.arange(N_EXPERTS) * tm
    fill = jnp.concatenate([tail, jnp.where(spare < p_rows, spare, -1)])
    i32 = lambda a: a.astype(jnp.int32)
    return dest, i32(fill), i32(blk_e), i32(live), i32(blk_x), p_rows


def _rope_tables(seq):
    inv = 1.0 / (ROPE_THETA ** (jnp.arange(0, HEAD_DIM, 2, dtype=F32) / HEAD_DIM))
    ang = jnp.arange(seq, dtype=F32)[:, None] * inv[None, :]
    cos = jnp.concatenate([jnp.cos(ang)] * 4, axis=-1)
    sin = jnp.concatenate([jnp.sin(ang)] * 4, axis=-1)
    upper = (jnp.arange(LANES) % HEAD_DIM) >= HEAD_DIM // 2
    sa = jnp.where(upper[None, :], sin, 0.0)
    sb = jnp.where(upper[None, :], 0.0, -sin)
    return cos, sa, sb


def _take(x, segments, axis):
    return jnp.concatenate([lax.slice_in_dim(x, a, a + n, axis=axis) for a, n in segments], axis=axis)


def _swa_pair_segments(base):
    half = SWA_Q_HEADS // 2
    segs = []
    for j in range(half):
        segs += [(base + j * HEAD_DIM, HEAD_DIM), (base + (j + half) * HEAD_DIM, HEAD_DIM)]
    return segs


def _in_proj_segments():
    split = np.cumsum([0, SWA_WIDTH, SWA_KV_WIDTH, SWA_KV_WIDTH, FOX_WIDTH, FOX_WIDTH, FOX_WIDTH,
                       FOX_HEADS, MOBA_WIDTH, MOBA_WIDTH, MOBA_WIDTH])
    aq, ak, bf, cq, end = (int(split[i]) for i in (0, 1, 6, 7, 10))
    return _swa_pair_segments(aq) + [(ak, bf - ak), (cq, end - cq)], (bf, FOX_HEADS)


def _swa_out_segments():
    return _swa_pair_segments(0) + [(SWA_WIDTH, MIX_WIDTH - SWA_WIDTH)]


def _mixer(h2, b, s, norm_g, w_in, forget_bias, sinks, mix_gain, w_out, tables):
    segs, fseg = _in_proj_segments()
    w_all = jnp.concatenate(
        [_take(w_in, segs, 1), jnp.pad(_take(w_in, [fseg], 1), ((0, 0), (0, LANES - FOX_HEADS)))],
        axis=1).astype(BF16)
    fbias = jnp.pad(forget_bias.astype(F32), (0, LANES - FOX_HEADS)).reshape(1, LANES)
    u, lf = _project(h2, norm_g.reshape(1, D_MODEL), w_all, *tables, fbias, s)
    u3 = u.reshape(b, s, U_WIDTH)
    oa = _swa(u3, sinks.astype(F32))
    ob = _fox(u3, lf.reshape(b, s, LANES))
    oc = _moba(u3)
    segs = _swa_out_segments()
    t = b * s
    return _outproj(oa.reshape(t, SWA_WIDTH), ob.reshape(t, FOX_WIDTH), oc.reshape(t, MOBA_WIDTH),
                    _take(mix_gain.astype(F32), segs, 0).reshape(1, MIX_WIDTH),
                    _take(w_out, segs, 0).astype(BF16), h2)


def _moe_ffn(h2, norm_g, w_router, w_gate, w_up, w_down, out_g):
    t = h2.shape[0]
    wr = jnp.pad(w_router.astype(F32), ((0, 0), (0, LANES - N_EXPERTS)))
    wr2 = jnp.stack(_split3(wr)[:2])
    z8, top, gate = _router(h2, norm_g.reshape(1, D_MODEL), wr2)
    dest, fill, blk_e, blk_live, blk_x, p_rows = _dispatch_plan(top[:, :TOP_K], t)
    x8, wg, wu, wd = _dispatch(z8, dest, fill, p_rows, w_gate, w_up, w_down)
    y8 = _moe_experts(x8, blk_e, blk_live, blk_x, wg, wu, wd)
    return _combine(h2, y8, dest, gate, out_g)


def kernel(x, attn_norm, w_in, fox_forget_bias, swa_sinks, mix_gain, w_out, ffn_norm,
           dense_w_gate, dense_w_up, dense_w_down, router_w, moe_w_gate, moe_w_up,
           moe_w_down, final_norm):
    b, s, d = x.shape
    depth = attn_norm.shape[0]
    assert d == D_MODEL and s % MOBA_BLOCK == 0 and s % ROW_TILE == 0
    tables = _rope_tables(s)
    h = x.reshape(b * s, d)
    normed = False
    for layer in range(depth):
        h = _mixer(h, b, s, attn_norm[layer], w_in[layer], fox_forget_bias[layer],
                   swa_sinks[layer], mix_gain[layer], w_out[layer], tables)
        j = layer // 2
        last = layer == depth - 1
        if layer % 2 == 0:
            h = _dense_ffn(h, ffn_norm[layer].reshape(1, d), dense_w_gate[j], dense_w_up[j],
                           dense_w_down[j])
        else:
            h = _moe_ffn(h, ffn_norm[layer], router_w[j], moe_w_gate[j], moe_w_up[j],
                         moe_w_down[j], final_norm if last else None)
            normed = last
    if not normed:
        h = _final_norm(h, final_norm.reshape(1, d))
    return h.reshape(b, s, d)
```

```python
import functools

import jax
import jax.numpy as jnp
from jax import lax
from jax.experimental import pallas as pl
from jax.experimental.pallas import tpu as pltpu

F32 = jnp.float32
BF16 = jnp.bfloat16

D_MODEL = 1024
HEAD_DIM = 64
LANES = 128
SWA_Q_HEADS = 8
SWA_KV_HEADS = 2
SWA_WINDOW = 128
FOX_HEADS = 4
MOBA_HEADS = 4
MOBA_BLOCK = 256
MOBA_TOPK = 3
ROPE_THETA = 10000.0
RMS_EPS = 1e-5
D_FF = 3584
N_EXPERTS = 8
TOP_K = 2
MOE_BLOCK = 512
NEG_INF = -1e30
ATTN_SCALE = HEAD_DIM ** -0.5
LOG2E = 1.4426950408889634
Q_SCALE = ATTN_SCALE * LOG2E

SWA_WIDTH = SWA_Q_HEADS * HEAD_DIM
SWA_KV_WIDTH = SWA_KV_HEADS * HEAD_DIM
FOX_WIDTH = FOX_HEADS * HEAD_DIM
MOBA_WIDTH = MOBA_HEADS * HEAD_DIM
MIX_WIDTH = SWA_WIDTH + FOX_WIDTH + MOBA_WIDTH

U_BLOCKS = 18
U_WIDTH = U_BLOCKS * LANES
ROPE_BLOCKS = (0, 1, 2, 3, 4, 12, 13, 14, 15)
Q_BLOCKS = (0, 1, 2, 3, 6, 7, 12, 13)
W_ALL_WIDTH = U_WIDTH + LANES

VMEM_LIMIT = 56 * 1024 * 1024

ROW_TILE = 512
ATT_TILE = 256
SWA_STEP_BLOCKS = 4
FF_TILE = 512
MOE_FF_TILE = 1792
CHUNKS = D_MODEL // LANES
MOE_ISSUE_UNROLL = 8
DISPATCH_TILE = 512


def _params(*sem):
    return pltpu.CompilerParams(dimension_semantics=sem, vmem_limit_bytes=VMEM_LIMIT)


def _split3(x):
    hi = x.astype(BF16)
    r1 = x - hi.astype(F32)
    mid = r1.astype(BF16)
    lo = (r1 - mid.astype(F32)).astype(BF16)
    return hi, mid, lo


def _dot_nt(a, b):
    return lax.dot_general(a, b, (((1,), (1,)), ((), ())), preferred_element_type=F32)


def _dot(a, b):
    return jnp.dot(a, b, preferred_element_type=F32)


def _proj_kernel(x_ref, g_ref, w_ref, cos_ref, sa_ref, sb_ref, fb_ref, u_ref, lf_ref):
    x = x_ref[...]
    inv = lax.rsqrt(jnp.mean(x * x, axis=-1, keepdims=True) + RMS_EPS)
    h = (x * inv * g_ref[...]).astype(BF16)
    cos = cos_ref[...]
    sa = sa_ref[...]
    sb = sb_ref[...]
    for c in range(U_BLOCKS // 2):
        acc = _dot(h, w_ref[:, c * 2 * LANES:(c + 1) * 2 * LANES])
        for half in range(2):
            blk = 2 * c + half
            a = acc[:, half * LANES:(half + 1) * LANES]
            if blk in ROPE_BLOCKS:
                a = a * cos + pltpu.roll(a, 32, 1) * sa + pltpu.roll(a, 96, 1) * sb
            if blk in Q_BLOCKS:
                a = a * Q_SCALE
            u_ref[:, blk * LANES:(blk + 1) * LANES] = a.astype(BF16)
    f = _dot(h, w_ref[:, U_WIDTH:W_ALL_WIDTH]) + fb_ref[...]
    lf_ref[...] = jnp.minimum(f, 0.0) - jnp.log(1.0 + jnp.exp(-jnp.abs(f)))


def _project(x2, norm_g, w_all, cos, sa, sb, fbias, seq):
    t = x2.shape[0]
    tm = ROW_TILE
    nseq = seq // tm
    row = lambda i: (i, 0)
    pos = lambda i: (i % nseq, 0)
    fixed = lambda i: (0, 0)
    return pl.pallas_call(
        _proj_kernel,
        grid=(t // tm,),
        in_specs=[
            pl.BlockSpec((tm, D_MODEL), row),
            pl.BlockSpec((1, D_MODEL), fixed),
            pl.BlockSpec((D_MODEL, W_ALL_WIDTH), fixed),
            pl.BlockSpec((tm, LANES), pos),
            pl.BlockSpec((tm, LANES), pos),
            pl.BlockSpec((tm, LANES), pos),
            pl.BlockSpec((1, LANES), fixed),
        ],
        out_specs=[pl.BlockSpec((tm, U_WIDTH), row), pl.BlockSpec((tm, LANES), row)],
        out_shape=[jax.ShapeDtypeStruct((t, U_WIDTH), BF16),
                   jax.ShapeDtypeStruct((t, LANES), F32)],
        compiler_params=_params("parallel"),
        name="proj",
    )(x2, norm_g, w_all, cos, sa, sb, fbias)


def _swa_kernel(sink_ref, q_ref, kc_ref, kp_ref, vc_ref, vp_ref, o_ref):
    n = pl.program_id(1)
    w = SWA_WINDOW
    group = SWA_Q_HEADS // SWA_KV_HEADS
    lane = lax.broadcasted_iota(jnp.int32, (1, LANES), 1)
    low = lane < HEAD_DIM

    def both_halves(kv_ref_prev, kv_ref_cur):
        x = jnp.concatenate([kv_ref_prev[0], kv_ref_cur[0]], axis=0).astype(F32)
        swapped = pltpu.roll(x, HEAD_DIM, 1)
        return (jnp.where(low, x, swapped).astype(BF16), jnp.where(low, swapped, x).astype(BF16))

    k_all = both_halves(kp_ref, kc_ref)
    v_all = both_halves(vp_ref, vc_ref)
    qi = lax.broadcasted_iota(jnp.int32, (group * w, 2 * w), 0) % w
    kj = lax.broadcasted_iota(jnp.int32, (group * w, 2 * w), 1)
    window = (kj > qi) & (kj <= qi + w)
    pairs = group // 2
    for a in range(SWA_STEP_BLOCKS):
        rows = slice(a * w, (a + 1) * w)
        valid = (window & ((kj >= w) | (n > 0))) if a == 0 else window
        for g in range(SWA_KV_HEADS):
            k2 = k_all[g][a * w:(a + 2) * w]
            v2 = v_all[g][a * w:(a + 2) * w]
            qs = jnp.concatenate(
                [jnp.where(low if half == 0 else ~low,
                           q_ref[0, rows, (pairs * g + jj) * LANES:(pairs * g + jj + 1) * LANES], 0)
                 for jj in range(pairs) for half in range(2)], axis=0)
            s = jnp.where(valid, _dot_nt(qs, k2), NEG_INF)
            ps, inv = [], []
            for j in range(group):
                sj = s[j * w:(j + 1) * w]
                sink = sink_ref[g * group + j] * LOG2E
                m = jnp.maximum(jnp.max(sj, axis=1, keepdims=True), sink)
                p = jnp.exp2(sj - m)
                inv.append(1.0 / (jnp.sum(p, axis=1, keepdims=True) + jnp.exp2(sink - m)))
                ps.append(p.astype(BF16))
            o = _dot(jnp.concatenate(ps, axis=0), v2)
            outs = [o[j * w:(j + 1) * w] * inv[j] for j in range(group)]
            for jj in range(pairs):
                oj = jnp.where(low, outs[2 * jj], outs[2 * jj + 1])
                cols = slice((pairs * g + jj) * LANES, (pairs * g + jj + 1) * LANES)
                o_ref[0, rows, cols] = oj.astype(BF16)


def _swa(u3, sinks):
    b, s, _ = u3.shape
    w = SWA_WINDOW
    nb = SWA_STEP_BLOCKS
    assert s % (nb * w) == 0
    cur = lambda blk: (lambda bi, n, sk: (bi, n, blk))
    prev = lambda blk: (lambda bi, n, sk: (bi, jnp.maximum(n * nb - 1, 0), blk))
    grid_spec = pltpu.PrefetchScalarGridSpec(
        num_scalar_prefetch=1,
        grid=(b, s // (nb * w)),
        in_specs=[
            pl.BlockSpec((1, nb * w, SWA_WIDTH), lambda bi, n, sk: (bi, n, 0)),
            pl.BlockSpec((1, nb * w, LANES), cur(4)),
            pl.BlockSpec((1, w, LANES), prev(4)),
            pl.BlockSpec((1, nb * w, LANES), cur(5)),
            pl.BlockSpec((1, w, LANES), prev(5)),
        ],
        out_specs=pl.BlockSpec((1, nb * w, SWA_WIDTH), lambda bi, n, sk: (bi, n, 0)),
    )
    return pl.pallas_call(
        _swa_kernel,
        grid_spec=grid_spec,
        out_shape=jax.ShapeDtypeStruct((b, s, SWA_WIDTH), BF16),
        compiler_params=_params("parallel", "parallel"),
        name="swa",
    )(sinks, u3, u3, u3, u3, u3)


def _head_lanes(hh):
    lane = lax.broadcasted_iota(jnp.int32, (1, LANES), 1)
    if hh == 0:
        return lane < HEAD_DIM, lane - HEAD_DIM
    return lane >= HEAD_DIM, lane


def _causal_attention(qp_scr, kp_scr, v_ref, o_ref, t):
    s_len = qp_scr.shape[1]
    lane = lax.broadcasted_iota(jnp.int32, (1, LANES), 1)
    row = lax.broadcasted_iota(jnp.int32, (t, t), 0)
    col = lax.broadcasted_iota(jnp.int32, (t, t), 1)
    for qi in range(s_len // t):
        rows = slice(qi * t, (qi + 1) * t)
        outs = []
        for hh in range(2):
            q = qp_scr[hh, rows, :]
            sd = jnp.where(col <= row, _dot_nt(q, kp_scr[hh, rows, :]), NEG_INF)
            m = jnp.max(sd, axis=1, keepdims=True)
            if qi:
                sp = _dot_nt(q, kp_scr[hh, :qi * t, :])
                m = jnp.maximum(m, jnp.max(sp, axis=1, keepdims=True))
            pd = jnp.exp2(sd - m)
            l = jnp.sum(pd, axis=1, keepdims=True)
            o = _dot(pd.astype(BF16), v_ref[0, rows, :])
            if qi:
                pp = jnp.exp2(sp - m)
                l = l + jnp.sum(pp, axis=1, keepdims=True)
                o = o + _dot(pp.astype(BF16), v_ref[0, :qi * t, :])
            outs.append(o * (1.0 / l))
        o_ref[0, rows, :] = jnp.where(lane < HEAD_DIM, outs[0], outs[1]).astype(BF16)


def _pair_attention(kernel_fn, u3, extra, q_blk, k_blk, v_blk, n_heads, name):
    b, s, _ = u3.shape
    pair = lambda blk: (lambda bi, p: (bi, 0, blk + p))
    seq_block = pl.BlockSpec((1, s, LANES), lambda bi, p: (bi, 0, 0))
    return pl.pallas_call(
        kernel_fn,
        grid=(b, n_heads // 2),
        in_specs=[pl.BlockSpec((1, s, LANES), pair(q_blk)),
                  pl.BlockSpec((1, s, LANES), pair(k_blk)),
                  pl.BlockSpec((1, s, LANES), pair(v_blk))] + [seq_block] * len(extra),
        out_specs=pl.BlockSpec((1, s, LANES), pair(0)),
        out_shape=jax.ShapeDtypeStruct((b, s, n_heads * HEAD_DIM), BF16),
        scratch_shapes=[pltpu.VMEM((2, s, LANES), BF16), pltpu.VMEM((2, s, LANES), BF16)],
        compiler_params=_params("parallel", "parallel"),
        name=name,
    )(u3, u3, u3, *extra)


def _fox_kernel(q_ref, k_ref, v_ref, lf_ref, o_ref, qp_scr, kp_scr):
    p = pl.program_id(1)
    s_len = q_ref.shape[1]
    t = ATT_TILE
    lane = lax.broadcasted_iota(jnp.int32, (1, LANES), 1)
    r = lax.broadcasted_iota(jnp.int32, (t, t), 0)
    cc = lax.broadcasted_iota(jnp.int32, (t, t), 1)
    tri = jnp.where(cc <= r, 1.0, 0.0).astype(BF16)
    carry = jnp.zeros((1, LANES), F32)
    for i in range(s_len // t):
        rows = slice(i * t, (i + 1) * t)
        lf = lf_ref[0, rows, :]
        lf0 = jnp.sum(jnp.where(lane == 2 * p, lf, 0.0), axis=1, keepdims=True)
        lf1 = jnp.sum(jnp.where(lane == 2 * p + 1, lf, 0.0), axis=1, keepdims=True)
        hi, mid, lo = _split3(jnp.where(lane < 3, lf0, jnp.where(lane < 6, lf1, 0.0)))
        terms = jnp.where((lane == 0) | (lane == 3), hi,
                          jnp.where((lane == 1) | (lane == 4), mid, lo))
        sums = _dot(tri, terms) + carry
        carry = sums[t - 1:t, :]
        c0 = jnp.sum(jnp.where(lane < 3, sums, 0.0), axis=1, keepdims=True)
        c1 = jnp.sum(jnp.where((lane >= 3) & (lane < 6), sums, 0.0), axis=1, keepdims=True)
        chi, cmid, clo = _split3(jnp.where(lane < HEAD_DIM, c1, c0) * LOG2E)
        for hh in range(2):
            keep, f = _head_lanes(hh)
            ones_q = jnp.where((f >= 3) & (f < 6), 1.0, 0.0).astype(BF16)
            ones_k = jnp.where((f >= 0) & (f < 3), 1.0, 0.0).astype(BF16)
            qf = jnp.where(f == 0, chi, jnp.where(f == 1, cmid, jnp.where(f == 2, clo, ones_q)))
            kf = jnp.where(f == 3, -chi, jnp.where(f == 4, -cmid, jnp.where(f == 5, -clo, ones_k)))
            qp_scr[hh, rows, :] = jnp.where(keep, q_ref[0, rows, :], qf)
            kp_scr[hh, rows, :] = jnp.where(keep, k_ref[0, rows, :], kf)
    _causal_attention(qp_scr, kp_scr, v_ref, o_ref, t)


def _fox(u3, lf3):
    return _pair_attention(_fox_kernel, u3, (lf3,), 6, 8, 10, FOX_HEADS, "fox")


def _moba_kernel(q_ref, k_ref, v_ref, o_ref, qp_scr, kp_scr):
    s_len = q_ref.shape[1]
    t = MOBA_BLOCK
    nkb = s_len // t
    sub = 8
    assert nkb <= sub
    means = [jnp.mean(k_ref[0, j * t:(j + 1) * t, :].astype(F32), axis=0, keepdims=True)
             for j in range(nkb)]
    km_pair = jnp.concatenate(means + [jnp.zeros((sub - nkb, LANES), F32)], axis=0) \
        if nkb < sub else jnp.concatenate(means, axis=0)
    stack = []
    for hh in range(2):
        keep, _ = _head_lanes(hh)
        stack += [part.astype(F32) for part in _split3(jnp.where(keep, km_pair, 0.0))]
    stack.append(jnp.zeros((LANES - 6 * sub, LANES), F32))
    km_all = jnp.concatenate(stack, axis=0).astype(BF16)
    blk = lax.broadcasted_iota(jnp.int32, (sub, t), 0)
    gates = [_dot_nt(km_all, q_ref[0, i * t:(i + 1) * t, :]) for i in range(nkb)]
    for hh in range(2):
        keep, f = _head_lanes(hh)
        for i in range(nkb):
            rows = slice(i * t, (i + 1) * t)
            q = q_ref[0, rows, :]
            base = 3 * sub * hh
            g = (gates[i][base:base + sub] + gates[i][base + sub:base + 2 * sub]) \
                + gates[i][base + 2 * sub:base + 3 * sub]
            past = blk < i
            g = jnp.where(past, g, NEG_INF)
            rank = jnp.zeros((sub, t), jnp.int32)
            for r in range(i):
                gr = jnp.broadcast_to(g[r:r + 1, :], (sub, t))
                beats = (gr > g) | ((gr == g) & (r < blk))
                rank = rank + beats.astype(jnp.int32)
            drop = jnp.where(past & (rank >= MOBA_TOPK), 1.0, 0.0)
            below = [jnp.zeros((HEAD_DIM, t), F32)] if hh == 0 else []
            above = jnp.zeros((LANES - sub - (HEAD_DIM if hh == 0 else 0), t), F32)
            padded = jnp.concatenate(below + [drop, above], axis=0)
            kf = jnp.where(f == i, NEG_INF, 0.0).astype(BF16)
            qp_scr[hh, rows, :] = jnp.where(keep, q, padded.T.astype(BF16))
            kp_scr[hh, rows, :] = jnp.where(keep, k_ref[0, rows, :], kf)
    _causal_attention(qp_scr, kp_scr, v_ref, o_ref, t)


def _moba(u3):
    return _pair_attention(_moba_kernel, u3, (), 12, 14, 16, MOBA_HEADS, "moba")


def _rms_f32(x):
    x = x.astype(F32)
    return x * lax.rsqrt(jnp.mean(x * x, axis=-1, keepdims=True) + RMS_EPS)


def _outproj_kernel(oa_ref, ob_ref, oc_ref, gain_ref, w_ref, x_ref, h_ref):
    a0, a1, a2 = 0, SWA_WIDTH, SWA_WIDTH + FOX_WIDTH
    ya = (_rms_f32(oa_ref[...]) * gain_ref[:, a0:a1]).astype(BF16)
    yb = (_rms_f32(ob_ref[...]) * gain_ref[:, a1:a2]).astype(BF16)
    yc = (_rms_f32(oc_ref[...]) * gain_ref[:, a2:]).astype(BF16)
    y = (_dot(ya, w_ref[a0:a1, :]) + _dot(yb, w_ref[a1:a2, :])) + _dot(yc, w_ref[a2:, :])
    h_ref[...] = x_ref[...] + y


def _outproj(oa, ob, oc, gain, w_out, x2):
    t = x2.shape[0]
    tm = ROW_TILE
    row = lambda i: (i, 0)
    fixed = lambda i: (0, 0)
    return pl.pallas_call(
        _outproj_kernel,
        grid=(t // tm,),
        in_specs=[
            pl.BlockSpec((tm, SWA_WIDTH), row),
            pl.BlockSpec((tm, FOX_WIDTH), row),
            pl.BlockSpec((tm, MOBA_WIDTH), row),
            pl.BlockSpec((1, MIX_WIDTH), fixed),
            pl.BlockSpec((MIX_WIDTH, D_MODEL), fixed),
            pl.BlockSpec((tm, D_MODEL), row),
        ],
        out_specs=pl.BlockSpec((tm, D_MODEL), row),
        out_shape=jax.ShapeDtypeStruct((t, D_MODEL), F32),
        compiler_params=_params("parallel"),
        name="outproj",
    )(oa, ob, oc, gain, w_out, x2)


def _silu(x):
    return x / (1.0 + jnp.exp(-x))


def _ffn_kernel(h_ref, g_ref, wg_ref, wu_ref, wd_ref, o_ref, acc_scr):
    j = pl.program_id(1)

    @pl.when((pl.program_id(0) == 0) & (j == 0))
    def _():
        acc_scr[...] = jnp.zeros_like(acc_scr)

    x = h_ref[...]
    inv = lax.rsqrt(jnp.mean(x * x, axis=-1, keepdims=True) + RMS_EPS)
    z = (x * inv * g_ref[...]).astype(BF16)
    a = (_silu(_dot(z, wg_ref[...].astype(BF16))) * _dot(z, wu_ref[...].astype(BF16))).astype(BF16)
    total = acc_scr[...] + _dot(a, wd_ref[...].astype(BF16))
    o_ref[...] = x + total
    acc_scr[...] = jnp.where(j == pl.num_programs(1) - 1, 0.0, total)


def _dense_ffn(h2, norm_g, w_gate, w_up, w_down):
    t = h2.shape[0]
    tm, tf = 2 * ROW_TILE, FF_TILE
    return pl.pallas_call(
        _ffn_kernel,
        grid=(t // tm, D_FF // tf),
        in_specs=[
            pl.BlockSpec((tm, D_MODEL), lambda i, j: (i, 0)),
            pl.BlockSpec((1, D_MODEL), lambda i, j: (0, 0)),
            pl.BlockSpec((D_MODEL, tf), lambda i, j: (0, j)),
            pl.BlockSpec((D_MODEL, tf), lambda i, j: (0, j)),
            pl.BlockSpec((tf, D_MODEL), lambda i, j: (j, 0)),
        ],
        out_specs=pl.BlockSpec((tm, D_MODEL), lambda i, j: (i, 0)),
        out_shape=jax.ShapeDtypeStruct((t, D_MODEL), F32),
        scratch_shapes=[pltpu.VMEM((tm, D_MODEL), F32)],
        compiler_params=_params("arbitrary", "arbitrary"),
        name="dense_ffn",
    )(h2, norm_g, w_gate, w_up, w_down)


def _router_kernel(h_ref, g_ref, wr_ref, z_ref, idx_ref, gate_ref, rank_ref, count_ref, count_scr):
    tm = h_ref.shape[0]
    x = h_ref[...]
    inv = lax.rsqrt(jnp.mean(x * x, axis=-1, keepdims=True) + RMS_EPS)
    z = x * inv * g_ref[...]
    for c in range(CHUNKS):
        z_ref[pl.ds(c, tm, stride=CHUNKS), :] = z[:, c * LANES:(c + 1) * LANES]
    zh, zm, _ = _split3(z)
    wh, wm = wr_ref[0], wr_ref[1]
    logits = _dot(zh, wh) + (_dot(zh, wm) + _dot(zm, wh))
    lane = lax.broadcasted_iota(jnp.int32, (tm, LANES), 1)
    logits = jnp.where(lane < N_EXPERTS, logits, -jnp.inf)
    m1 = jnp.max(logits, axis=1, keepdims=True)
    i1 = jnp.min(jnp.where(logits == m1, lane, LANES), axis=1, keepdims=True)
    rest = jnp.where(lane == i1, -jnp.inf, logits)
    m2 = jnp.max(rest, axis=1, keepdims=True)
    i2 = jnp.min(jnp.where(rest == m2, lane, LANES), axis=1, keepdims=True)
    e2 = jnp.exp(m2 - m1)
    g1 = 1.0 / (1.0 + e2)
    idx_ref[...] = jnp.where(lane == 0, i1, i2)
    gate_ref[...] = jnp.where(lane == 0, g1, e2 * g1)

    @pl.when(pl.program_id(0) == 0)
    def _():
        count_scr[...] = jnp.zeros_like(count_scr)

    chosen = ((lane == i1) | (lane == i2)).astype(F32)
    r = lax.broadcasted_iota(jnp.int32, (tm, tm), 0)
    c = lax.broadcasted_iota(jnp.int32, (tm, tm), 1)
    before = jnp.where(c < r, 1.0, 0.0).astype(BF16)
    earlier = _dot(before, chosen.astype(BF16)) + count_scr[...]
    r1 = jnp.sum(jnp.where(lane == i1, earlier, 0.0), axis=1, keepdims=True)
    r2 = jnp.sum(jnp.where(lane == i2, earlier, 0.0), axis=1, keepdims=True)
    rank_ref[...] = jnp.where(lane == 0, r1, r2).astype(jnp.int32)
    count_scr[...] += jnp.sum(chosen, axis=0, keepdims=True)
    count_ref[...] = count_scr[...].astype(jnp.int32)


def _router(h2, norm_g, wr3):
    t = h2.shape[0]
    tm = ROW_TILE
    row = lambda i: (i, 0)
    return pl.pallas_call(
        _router_kernel,
        grid=(t // tm,),
        in_specs=[
            pl.BlockSpec((tm, D_MODEL), row),
            pl.BlockSpec((1, D_MODEL), lambda i: (0, 0)),
            pl.BlockSpec((2, D_MODEL, LANES), lambda i: (0, 0, 0)),
        ],
        out_specs=[pl.BlockSpec((tm * CHUNKS, LANES), row),
                   pl.BlockSpec((tm, LANES), row),
                   pl.BlockSpec((tm, LANES), row),
                   pl.BlockSpec((tm, LANES), row),
                   pl.BlockSpec((1, LANES), lambda i: (0, 0))],
        out_shape=[jax.ShapeDtypeStruct((t * CHUNKS, LANES), F32),
                   jax.ShapeDtypeStruct((t, LANES), jnp.int32),
                   jax.ShapeDtypeStruct((t, LANES), F32),
                   jax.ShapeDtypeStruct((t, LANES), jnp.int32),
                   jax.ShapeDtypeStruct((1, LANES), jnp.int32)],
        scratch_shapes=[pltpu.VMEM((1, LANES), F32)],
        compiler_params=_params("arbitrary"),
        name="router",
    )(h2, norm_g, wr3)


def _row_copy(src, src_row, dst, dst_row, sem):
    return pltpu.make_async_copy(src.at[pl.ds(src_row * CHUNKS, CHUNKS), :],
                                 dst.at[pl.ds(dst_row * CHUNKS, CHUNKS), :], sem)


def _rows_wait(src, dst, rows, sem):
    pltpu.make_async_copy(src.at[pl.ds(0, rows * CHUNKS), :],
                          dst.at[pl.ds(0, rows * CHUNKS), :], sem).wait()


def _dispatch_kernel(dest_ref, fill_ref, z_ref, wg_ref, wu_ref, wd_ref,
                     x_hbm, wg_out, wu_out, wd_out, zero_scr, fsem, sem):
    i = pl.program_id(0)
    blk = MOE_BLOCK
    tile = z_ref.shape[0] // CHUNKS
    wg_out[...] = wg_ref[...].astype(BF16)
    wu_out[...] = wu_ref[...].astype(BF16)
    wd_out[...] = wd_ref[...].astype(BF16)

    @pl.when(i == 0)
    def _():
        zero_scr[...] = jnp.zeros_like(zero_scr)
        for e in range(fill_ref.shape[0]):
            @pl.when(fill_ref[e] >= 0)
            def _():
                start = pl.multiple_of(fill_ref[e] * CHUNKS, blk * CHUNKS)
                pltpu.make_async_copy(zero_scr, x_hbm.at[pl.ds(start, blk * CHUNKS), :], fsem).start()
        for e in range(fill_ref.shape[0]):
            @pl.when(fill_ref[e] >= 0)
            def _():
                _rows_wait(zero_scr, x_hbm, blk, fsem)

    base = i * tile * TOP_K

    tokens_per_trip = MOE_ISSUE_UNROLL // TOP_K

    def trip(it, c):
        for k in range(MOE_ISSUE_UNROLL):
            token = it * tokens_per_trip + k // TOP_K
            slot = dest_ref[base + it * MOE_ISSUE_UNROLL + k]
            _row_copy(z_ref, token, x_hbm, slot, sem).start(priority=k % 2)
        return c
    lax.fori_loop(0, tile // tokens_per_trip, trip, 0)
    for _ in range(TOP_K):
        _rows_wait(z_ref, x_hbm, tile, sem)


def _dispatch(z8, dest, fill, p_rows, w_gate, w_up, w_down):
    tile = DISPATCH_TILE
    steps = z8.shape[0] // (tile * CHUNKS)
    ne, d, ff = w_gate.shape
    assert z8.shape[0] % (tile * CHUNKS) == 0 and (ne * d) % steps == 0 and (ne * ff) % steps == 0
    up_rows, down_rows = ne * d // steps, ne * ff // steps
    wspec = lambda rows, cols: pl.BlockSpec((rows, cols), lambda i, dd, f: (i, 0))
    grid_spec = pltpu.PrefetchScalarGridSpec(
        num_scalar_prefetch=2,
        grid=(steps,),
        in_specs=[pl.BlockSpec((tile * CHUNKS, LANES), lambda i, dd, f: (i, 0)),
                  wspec(up_rows, ff), wspec(up_rows, ff), wspec(down_rows, d)],
        out_specs=[pl.BlockSpec(memory_space=pl.ANY),
                   wspec(up_rows, ff), wspec(up_rows, ff), wspec(down_rows, d)],
        scratch_shapes=[pltpu.VMEM((MOE_BLOCK * CHUNKS, LANES), F32),
                        pltpu.SemaphoreType.DMA(()), pltpu.SemaphoreType.DMA(())],
    )
    x8, wg, wu, wd = pl.pallas_call(
        _dispatch_kernel,
        grid_spec=grid_spec,
        out_shape=[jax.ShapeDtypeStruct((p_rows * CHUNKS, LANES), F32),
                   jax.ShapeDtypeStruct((ne * d, ff), BF16),
                   jax.ShapeDtypeStruct((ne * d, ff), BF16),
                   jax.ShapeDtypeStruct((ne * ff, d), BF16)],
        compiler_params=_params("arbitrary"),
        name="moe_dispatch",
    )(dest, fill, z8, w_gate.reshape(ne * d, ff), w_up.reshape(ne * d, ff),
      w_down.reshape(ne * ff, d))
    return x8, wg.reshape(ne, d, ff), wu.reshape(ne, d, ff), wd.reshape(ne, ff, d)


def _moe_kernel(be_ref, live_ref, bx_ref, x_ref, wg_ref, wu_ref, wd_ref, y_ref, acc):
    i = pl.program_id(0)
    j = pl.program_id(1)
    tm = acc.shape[0]

    @pl.when((i == 0) & (j == 0))
    def _():
        acc[...] = jnp.zeros_like(acc)

    @pl.when((live_ref[i] == 0) & (j == 0))
    def _():
        y_ref[...] = jnp.zeros_like(y_ref)

    @pl.when(live_ref[i] > 0)
    def _():
        x = jnp.concatenate([x_ref[pl.ds(c, tm, stride=CHUNKS), :].astype(BF16)
                             for c in range(CHUNKS)], axis=1)
        a = (_silu(_dot(x, wg_ref[0])) * _dot(x, wu_ref[0])).astype(BF16)
        total = acc[...] + _dot(a, wd_ref[0])
        for c in range(CHUNKS):
            y_ref[pl.ds(c, tm, stride=CHUNKS), :] = total[:, c * LANES:(c + 1) * LANES]
        acc[...] = jnp.where(j == pl.num_programs(1) - 1, 0.0, total)


def _moe_experts(x8, blk_e, blk_live, blk_x, w_gate, w_up, w_down):
    tm, tf = MOE_BLOCK, MOE_FF_TILE
    nblk = blk_e.shape[0]
    nff = D_FF // tf
    ff = lambda i, j, live: jnp.where(live[i] > 0, j, nff - 1)
    grid_spec = pltpu.PrefetchScalarGridSpec(
        num_scalar_prefetch=3,
        grid=(nblk, nff),
        in_specs=[
            pl.BlockSpec((tm * CHUNKS, LANES), lambda i, j, be, lv, bx: (bx[i], 0)),
            pl.BlockSpec((1, D_MODEL, tf), lambda i, j, be, lv, bx: (be[i], 0, ff(i, j, lv))),
            pl.BlockSpec((1, D_MODEL, tf), lambda i, j, be, lv, bx: (be[i], 0, ff(i, j, lv))),
            pl.BlockSpec((1, tf, D_MODEL), lambda i, j, be, lv, bx: (be[i], ff(i, j, lv), 0)),
        ],
        out_specs=pl.BlockSpec((tm * CHUNKS, LANES), lambda i, j, be, lv, bx: (i, 0)),
        scratch_shapes=[pltpu.VMEM((tm, D_MODEL), F32)],
    )
    return pl.pallas_call(
        _moe_kernel,
        grid_spec=grid_spec,
        out_shape=jax.ShapeDtypeStruct(x8.shape, F32),
        compiler_params=_params("arbitrary", "arbitrary"),
        name="moe_experts",
    )(blk_e, blk_live, blk_x, x8, w_gate, w_up, w_down)


def _combine_kernel(dest_ref, h_ref, gate_ref, *rest, final_norm):
    if final_norm:
        g_ref, y_hbm, o_ref, ybuf, sem = rest
    else:
        y_hbm, o_ref, ybuf, sem = rest
    i = pl.program_id(0)
    tm = h_ref.shape[0]
    rows = tm * TOP_K
    stride = TOP_K * CHUNKS

    def pull(tile, slot):
        def trip(it, c):
            for k in range(MOE_ISSUE_UNROLL):
                r = it * MOE_ISSUE_UNROLL + k
                copy = _row_copy(y_hbm, dest_ref[tile * rows + r], ybuf.at[slot], r, sem.at[slot])
                copy.start(priority=k % 2)
            return c
        lax.fori_loop(0, rows // MOE_ISSUE_UNROLL, trip, 0)

    @pl.when(i == 0)
    def _():
        pull(0, 0)

    @pl.when(i + 1 < pl.num_programs(0))
    def _():
        pull(i + 1, (i + 1) % 2)

    slot = i % 2
    yb = ybuf.at[slot]
    _rows_wait(y_hbm, yb, rows, sem.at[slot])
    gates = [gate_ref[:, k:k + 1] for k in range(TOP_K)]
    parts = []
    ss = jnp.zeros((tm, 1), F32)
    for c in range(CHUNKS):
        y = gates[0] * yb[pl.ds(c, tm, stride=stride), :]
        for k in range(1, TOP_K):
            y = y + gates[k] * yb[pl.ds(k * CHUNKS + c, tm, stride=stride), :]
        hc = h_ref[:, c * LANES:(c + 1) * LANES] + y
        parts.append(hc)
        ss = ss + jnp.sum(hc * hc, axis=1, keepdims=True)
    if final_norm:
        inv = lax.rsqrt(ss / D_MODEL + RMS_EPS)
    for c in range(CHUNKS):
        cols = slice(c * LANES, (c + 1) * LANES)
        o_ref[:, cols] = parts[c] * inv * g_ref[:, cols] if final_norm else parts[c]


def _combine(h2, y8, dest, gate, out_g):
    t = h2.shape[0]
    tm = ROW_TILE
    final_norm = out_g is not None
    row = lambda i, d: (i, 0)
    in_specs = [pl.BlockSpec((tm, D_MODEL), row), pl.BlockSpec((tm, LANES), row)]
    args = [h2, gate]
    if final_norm:
        in_specs.append(pl.BlockSpec((1, D_MODEL), lambda i, d: (0, 0)))
        args.append(out_g.reshape(1, D_MODEL))
    grid_spec = pltpu.PrefetchScalarGridSpec(
        num_scalar_prefetch=1,
        grid=(t // tm,),
        in_specs=in_specs + [pl.BlockSpec(memory_space=pl.ANY)],
        out_specs=pl.BlockSpec((tm, D_MODEL), row),
        scratch_shapes=[pltpu.VMEM((2, tm * TOP_K * CHUNKS, LANES), F32),
                        pltpu.SemaphoreType.DMA((2,))],
    )
    return pl.pallas_call(
        functools.partial(_combine_kernel, final_norm=final_norm),
        grid_spec=grid_spec,
        out_shape=jax.ShapeDtypeStruct((t, D_MODEL), F32),
        compiler_params=_params("arbitrary"),
        name="moe_combine",
    )(dest, *args, y8)


def _norm_kernel(h_ref, g_ref, o_ref):
    x = h_ref[...]
    o_ref[...] = x * lax.rsqrt(jnp.mean(x * x, axis=-1, keepdims=True) + RMS_EPS) * g_ref[...]


def _final_norm(h2, norm_g):
    t = h2.shape[0]
    tm = ROW_TILE
    return pl.pallas_call(
        _norm_kernel,
        grid=(t // tm,),
        in_specs=[pl.BlockSpec((tm, D_MODEL), lambda i: (i, 0)),
                  pl.BlockSpec((1, D_MODEL), lambda i: (0, 0))],
        out_specs=pl.BlockSpec((tm, D_MODEL), lambda i: (i, 0)),
        out_shape=jax.ShapeDtypeStruct((t, D_MODEL), F32),
        compiler_params=_params("parallel"),
        name="final_norm",
    )(h2, norm_g)


def _dispatch_plan(top_i, rank, counts, t):
    n = t * TOP_K
    tm = MOE_BLOCK
    e_flat = top_i.reshape(n)
    rank = rank.reshape(n)
    onehot = (e_flat[:, None] == jnp.arange(N_EXPERTS)[None, :]).astype(jnp.int32)
    padded = (counts + tm - 1) // tm * tm
    pends = jnp.cumsum(padded)
    pstarts = pends - padded
    dest = (jnp.sum(onehot * pstarts[None, :], axis=1) + rank).astype(jnp.int32)
    p_rows = -(-n // tm) * tm + N_EXPERTS * tm
    nblk = p_rows // tm
    blk = jnp.arange(nblk, dtype=jnp.int32)
    blk_e = jnp.minimum(jnp.sum((blk * tm)[:, None] >= pends[None, :], axis=1), N_EXPERTS - 1)
    live = blk * tm < pends[-1]
    last_live = jnp.maximum(pends[-1] // tm - 1, 0)
    blk_e = jnp.where(live, blk_e, blk_e[last_live])
    blk_x = jnp.where(live, blk, 0)
    tail = jnp.where(padded > 0, pends - tm, -1)
    spare = pends[-1] + jnp.arange(N_EXPERTS) * tm
    fill = jnp.concatenate([tail, jnp.where(spare < p_rows, spare, -1)])
    i32 = lambda a: a.astype(jnp.int32)
    return dest, i32(fill), i32(blk_e), i32(live), i32(blk_x), p_rows


def _rope_tables(seq):
    inv = 1.0 / (ROPE_THETA ** (jnp.arange(0, HEAD_DIM, 2, dtype=F32) / HEAD_DIM))
    ang = jnp.arange(seq, dtype=F32)[:, None] * inv[None, :]
    cos = jnp.concatenate([jnp.cos(ang)] * 4, axis=-1)
    sin = jnp.concatenate([jnp.sin(ang)] * 4, axis=-1)
    upper = (jnp.arange(LANES) % HEAD_DIM) >= HEAD_DIM // 2
    sa = jnp.where(upper[None, :], sin, 0.0)
    sb = jnp.where(upper[None, :], 0.0, -sin)
    return cos, sa, sb


def _mixer(h2, b, s, norm_g, w_in, forget_bias, sinks, mix_gain, w_out, tables):
    f0 = SWA_WIDTH + 2 * SWA_KV_WIDTH + 3 * FOX_WIDTH
    f1 = f0 + FOX_HEADS
    assert f0 + 3 * MOBA_WIDTH == U_WIDTH
    w_all = jnp.concatenate(
        [w_in[:, :f0], w_in[:, f1:], jnp.pad(w_in[:, f0:f1], ((0, 0), (0, LANES - FOX_HEADS)))],
        axis=1).astype(BF16)
    fbias = jnp.pad(forget_bias.astype(F32), (0, LANES - FOX_HEADS)).reshape(1, LANES)
    u, lf = _project(h2, norm_g.reshape(1, D_MODEL), w_all, *tables, fbias, s)
    u3 = u.reshape(b, s, U_WIDTH)
    oa = _swa(u3, sinks.astype(F32))
    ob = _fox(u3, lf.reshape(b, s, LANES))
    oc = _moba(u3)
    t = b * s
    return _outproj(oa.reshape(t, SWA_WIDTH), ob.reshape(t, FOX_WIDTH), oc.reshape(t, MOBA_WIDTH),
                    mix_gain.astype(F32).reshape(1, MIX_WIDTH), w_out.astype(BF16), h2)


def _moe_ffn(h2, norm_g, w_router, w_gate, w_up, w_down, out_g):
    t = h2.shape[0]
    wr = jnp.pad(w_router.astype(F32), ((0, 0), (0, LANES - N_EXPERTS)))
    wr2 = jnp.stack(_split3(wr)[:2])
    z8, top, gate, rank, counts = _router(h2, norm_g.reshape(1, D_MODEL), wr2)
    dest, fill, blk_e, blk_live, blk_x, p_rows = _dispatch_plan(
        top[:, :TOP_K], rank[:, :TOP_K], counts[0, :N_EXPERTS], t)
    x8, wg, wu, wd = _dispatch(z8, dest, fill, p_rows, w_gate, w_up, w_down)
    y8 = _moe_experts(x8, blk_e, blk_live, blk_x, wg, wu, wd)
    return _combine(h2, y8, dest, gate, out_g)


def kernel(x, attn_norm, w_in, fox_forget_bias, swa_sinks, mix_gain, w_out, ffn_norm,
           dense_w_gate, dense_w_up, dense_w_down, router_w, moe_w_gate, moe_w_up,
           moe_w_down, final_norm):
    b, s, d = x.shape
    depth = attn_norm.shape[0]
    assert d == D_MODEL and s % MOBA_BLOCK == 0 and s % ROW_TILE == 0
    tables = _rope_tables(s)
    h = x.reshape(b * s, d)
    normed = False
    for layer in range(depth):
        h = _mixer(h, b, s, attn_norm[layer], w_in[layer], fox_forget_bias[layer],
                   swa_sinks[layer], mix_gain[layer], w_out[layer], tables)
        j = layer // 2
        last = layer == depth - 1
        if layer % 2 == 0:
            h = _dense_ffn(h, ffn_norm[layer].reshape(1, d), dense_w_gate[j], dense_w_up[j],
                           dense_w_down[j])
        else:
            h = _moe_ffn(h, ffn_norm[layer], router_w[j], moe_w_gate[j], moe_w_up[j],
                         moe_w_down[j], final_norm if last else None)
            normed = last
    if not normed:
        h = _final_norm(h, final_norm.reshape(1, d))
    return h.reshape(b, s, d)
```

```python
import functools

import jax
import jax.numpy as jnp
from jax import lax
from jax.experimental import pallas as pl
from jax.experimental.pallas import tpu as pltpu

F32 = jnp.float32
BF16 = jnp.bfloat16

D_MODEL = 1024
HEAD_DIM = 64
LANES = 128
SWA_Q_HEADS = 8
SWA_KV_HEADS = 2
SWA_WINDOW = 128
FOX_HEADS = 4
MOBA_HEADS = 4
MOBA_BLOCK = 256
MOBA_TOPK = 3
ROPE_THETA = 10000.0
RMS_EPS = 1e-5
D_FF = 3584
N_EXPERTS = 8
TOP_K = 2
MOE_BLOCK = 512
NEG_INF = -1e30
ATTN_SCALE = HEAD_DIM ** -0.5
LOG2E = 1.4426950408889634
Q_SCALE = ATTN_SCALE * LOG2E

SWA_WIDTH = SWA_Q_HEADS * HEAD_DIM
SWA_KV_WIDTH = SWA_KV_HEADS * HEAD_DIM
FOX_WIDTH = FOX_HEADS * HEAD_DIM
MOBA_WIDTH = MOBA_HEADS * HEAD_DIM
MIX_WIDTH = SWA_WIDTH + FOX_WIDTH + MOBA_WIDTH

U_BLOCKS = 18
U_WIDTH = U_BLOCKS * LANES
ROPE_BLOCKS = (0, 1, 2, 3, 4, 12, 13, 14, 15)
Q_BLOCKS = (0, 1, 2, 3, 6, 7, 12, 13)
W_ALL_WIDTH = U_WIDTH + LANES

VMEM_LIMIT = 56 * 1024 * 1024

ROW_TILE = 512
ATT_TILE = 256
SWA_STEP_BLOCKS = 4
FF_TILE = 512
MOE_FF_TILE = 1792
CHUNKS = D_MODEL // LANES
MOE_ISSUE_UNROLL = 8
DISPATCH_TILE = 512


def _params(*sem):
    return pltpu.CompilerParams(dimension_semantics=sem, vmem_limit_bytes=VMEM_LIMIT)


def _split3(x):
    hi = x.astype(BF16)
    r1 = x - hi.astype(F32)
    mid = r1.astype(BF16)
    lo = (r1 - mid.astype(F32)).astype(BF16)
    return hi, mid, lo


def _dot_nt(a, b):
    return lax.dot_general(a, b, (((1,), (1,)), ((), ())), preferred_element_type=F32)


def _dot(a, b):
    return jnp.dot(a, b, preferred_element_type=F32)


def _proj_kernel(x_ref, g_ref, w_ref, cos_ref, sa_ref, sb_ref, fb_ref, u_ref, lf_ref):
    x = x_ref[...]
    inv = lax.rsqrt(jnp.mean(x * x, axis=-1, keepdims=True) + RMS_EPS)
    h = (x * inv * g_ref[...]).astype(BF16)
    cos = cos_ref[...]
    sa = sa_ref[...]
    sb = sb_ref[...]
    for c in range(U_BLOCKS // 2):
        acc = _dot_nt(h, w_ref[c * 2 * LANES:(c + 1) * 2 * LANES, :])
        for half in range(2):
            blk = 2 * c + half
            a = acc[:, half * LANES:(half + 1) * LANES]
            if blk in ROPE_BLOCKS:
                a = a * cos + pltpu.roll(a, 32, 1) * sa + pltpu.roll(a, 96, 1) * sb
            if blk in Q_BLOCKS:
                a = a * Q_SCALE
            u_ref[:, blk * LANES:(blk + 1) * LANES] = a.astype(BF16)
    f = _dot_nt(h, w_ref[U_WIDTH:W_ALL_WIDTH, :]) + fb_ref[...]
    lf_ref[...] = jnp.minimum(f, 0.0) - jnp.log(1.0 + jnp.exp(-jnp.abs(f)))


def _project(x2, norm_g, w_all, cos, sa, sb, fbias, seq):
    t = x2.shape[0]
    tm = ROW_TILE
    nseq = seq // tm
    row = lambda i: (i, 0)
    pos = lambda i: (i % nseq, 0)
    fixed = lambda i: (0, 0)
    return pl.pallas_call(
        _proj_kernel,
        grid=(t // tm,),
        in_specs=[
            pl.BlockSpec((tm, D_MODEL), row),
            pl.BlockSpec((1, D_MODEL), fixed),
            pl.BlockSpec((W_ALL_WIDTH, D_MODEL), fixed),
            pl.BlockSpec((tm, LANES), pos),
            pl.BlockSpec((tm, LANES), pos),
            pl.BlockSpec((tm, LANES), pos),
            pl.BlockSpec((1, LANES), fixed),
        ],
        out_specs=[pl.BlockSpec((tm, U_WIDTH), row), pl.BlockSpec((tm, LANES), row)],
        out_shape=[jax.ShapeDtypeStruct((t, U_WIDTH), BF16),
                   jax.ShapeDtypeStruct((t, LANES), F32)],
        compiler_params=_params("parallel"),
        name="proj",
    )(x2, norm_g, w_all, cos, sa, sb, fbias)


def _swa_kernel(sink_ref, q_ref, kc_ref, kp_ref, vc_ref, vp_ref, o_ref):
    n = pl.program_id(1)
    w = SWA_WINDOW
    group = SWA_Q_HEADS // SWA_KV_HEADS
    lane = lax.broadcasted_iota(jnp.int32, (1, LANES), 1)
    low = lane < HEAD_DIM

    def both_halves(kv_ref_prev, kv_ref_cur):
        x = jnp.concatenate([kv_ref_prev[0], kv_ref_cur[0]], axis=0)
        swapped = jnp.concatenate([x[:, HEAD_DIM:], x[:, :HEAD_DIM]], axis=1)
        return jnp.where(low, x, swapped), jnp.where(low, swapped, x)

    k_all = both_halves(kp_ref, kc_ref)
    v_all = both_halves(vp_ref, vc_ref)
    qi = lax.broadcasted_iota(jnp.int32, (group * w, 2 * w), 0) % w
    kj = lax.broadcasted_iota(jnp.int32, (group * w, 2 * w), 1)
    window = (kj > qi) & (kj <= qi + w)
    pairs = group // 2
    for a in range(SWA_STEP_BLOCKS):
        rows = slice(a * w, (a + 1) * w)
        valid = (window & ((kj >= w) | (n > 0))) if a == 0 else window
        for g in range(SWA_KV_HEADS):
            k2 = k_all[g][a * w:(a + 2) * w]
            v2 = v_all[g][a * w:(a + 2) * w]
            qs = jnp.concatenate(
                [jnp.where(low if half == 0 else ~low,
                           q_ref[0, rows, (pairs * g + jj) * LANES:(pairs * g + jj + 1) * LANES], 0)
                 for jj in range(pairs) for half in range(2)], axis=0)
            s = jnp.where(valid, _dot_nt(qs, k2), NEG_INF)
            ps, inv = [], []
            for j in range(group):
                sj = s[j * w:(j + 1) * w]
                sink = sink_ref[g * group + j] * LOG2E
                m = jnp.maximum(jnp.max(sj, axis=1, keepdims=True), sink)
                p = jnp.exp2(sj - m)
                inv.append(1.0 / (jnp.sum(p, axis=1, keepdims=True) + jnp.exp2(sink - m)))
                ps.append(p.astype(BF16))
            o = _dot(jnp.concatenate(ps, axis=0), v2)
            outs = [o[j * w:(j + 1) * w] * inv[j] for j in range(group)]
            for jj in range(pairs):
                oj = jnp.where(low, outs[2 * jj], outs[2 * jj + 1])
                cols = slice((pairs * g + jj) * LANES, (pairs * g + jj + 1) * LANES)
                o_ref[0, rows, cols] = oj.astype(BF16)


def _swa(u3, sinks):
    b, s, _ = u3.shape
    w = SWA_WINDOW
    nb = SWA_STEP_BLOCKS
    assert s % (nb * w) == 0
    cur = lambda blk: (lambda bi, n, sk: (bi, n, blk))
    prev = lambda blk: (lambda bi, n, sk: (bi, jnp.maximum(n * nb - 1, 0), blk))
    grid_spec = pltpu.PrefetchScalarGridSpec(
        num_scalar_prefetch=1,
        grid=(b, s // (nb * w)),
        in_specs=[
            pl.BlockSpec((1, nb * w, SWA_WIDTH), lambda bi, n, sk: (bi, n, 0)),
            pl.BlockSpec((1, nb * w, LANES), cur(4)),
            pl.BlockSpec((1, w, LANES), prev(4)),
            pl.BlockSpec((1, nb * w, LANES), cur(5)),
            pl.BlockSpec((1, w, LANES), prev(5)),
        ],
        out_specs=pl.BlockSpec((1, nb * w, SWA_WIDTH), lambda bi, n, sk: (bi, n, 0)),
    )
    return pl.pallas_call(
        _swa_kernel,
        grid_spec=grid_spec,
        out_shape=jax.ShapeDtypeStruct((b, s, SWA_WIDTH), BF16),
        compiler_params=_params("parallel", "parallel"),
        name="swa",
    )(sinks, u3, u3, u3, u3, u3)


def _head_lanes(hh):
    lane = lax.broadcasted_iota(jnp.int32, (1, LANES), 1)
    if hh == 0:
        return lane < HEAD_DIM, lane - HEAD_DIM
    return lane >= HEAD_DIM, lane


def _causal_attention(qp_scr, kp_scr, v_ref, o_ref, t):
    s_len = qp_scr.shape[1]
    lane = lax.broadcasted_iota(jnp.int32, (1, LANES), 1)
    row = lax.broadcasted_iota(jnp.int32, (t, t), 0)
    col = lax.broadcasted_iota(jnp.int32, (t, t), 1)
    for qi in range(s_len // t):
        rows = slice(qi * t, (qi + 1) * t)
        outs = []
        for hh in range(2):
            q = qp_scr[hh, rows, :]
            sd = jnp.where(col <= row, _dot_nt(q, kp_scr[hh, rows, :]), NEG_INF)
            m = jnp.max(sd, axis=1, keepdims=True)
            if qi:
                sp = _dot_nt(q, kp_scr[hh, :qi * t, :])
                m = jnp.maximum(m, jnp.max(sp, axis=1, keepdims=True))
            pd = jnp.exp2(sd - m)
            l = jnp.sum(pd, axis=1, keepdims=True)
            o = _dot(pd.astype(BF16), v_ref[0, rows, :])
            if qi:
                pp = jnp.exp2(sp - m)
                l = l + jnp.sum(pp, axis=1, keepdims=True)
                o = o + _dot(pp.astype(BF16), v_ref[0, :qi * t, :])
            outs.append(o * (1.0 / l))
        o_ref[0, rows, :] = jnp.where(lane < HEAD_DIM, outs[0], outs[1]).astype(BF16)


def _pair_attention(kernel_fn, u3, extra, q_blk, k_blk, v_blk, n_heads, name):
    b, s, _ = u3.shape
    pair = lambda blk: (lambda bi, p: (bi, 0, blk + p))
    seq_block = pl.BlockSpec((1, s, LANES), lambda bi, p: (bi, 0, 0))
    return pl.pallas_call(
        kernel_fn,
        grid=(b, n_heads // 2),
        in_specs=[pl.BlockSpec((1, s, LANES), pair(q_blk)),
                  pl.BlockSpec((1, s, LANES), pair(k_blk)),
                  pl.BlockSpec((1, s, LANES), pair(v_blk))] + [seq_block] * len(extra),
        out_specs=pl.BlockSpec((1, s, LANES), pair(0)),
        out_shape=jax.ShapeDtypeStruct((b, s, n_heads * HEAD_DIM), BF16),
        scratch_shapes=[pltpu.VMEM((2, s, LANES), BF16), pltpu.VMEM((2, s, LANES), BF16)],
        compiler_params=_params("parallel", "parallel"),
        name=name,
    )(u3, u3, u3, *extra)


def _fox_kernel(q_ref, k_ref, v_ref, lf_ref, o_ref, qp_scr, kp_scr):
    p = pl.program_id(1)
    s_len = q_ref.shape[1]
    t = ATT_TILE
    lane = lax.broadcasted_iota(jnp.int32, (1, LANES), 1)
    r = lax.broadcasted_iota(jnp.int32, (t, t), 0)
    cc = lax.broadcasted_iota(jnp.int32, (t, t), 1)
    tri = jnp.where(cc <= r, 1.0, 0.0).astype(BF16)
    carry = jnp.zeros((1, LANES), F32)
    for i in range(s_len // t):
        rows = slice(i * t, (i + 1) * t)
        lf = lf_ref[0, rows, :]
        lf0 = jnp.sum(jnp.where(lane == 2 * p, lf, 0.0), axis=1, keepdims=True)
        lf1 = jnp.sum(jnp.where(lane == 2 * p + 1, lf, 0.0), axis=1, keepdims=True)
        hi, mid, lo = _split3(jnp.where(lane < 3, lf0, jnp.where(lane < 6, lf1, 0.0)))
        terms = jnp.where((lane == 0) | (lane == 3), hi,
                          jnp.where((lane == 1) | (lane == 4), mid, lo))
        sums = _dot(tri, terms) + carry
        carry = sums[t - 1:t, :]
        c0 = jnp.sum(jnp.where(lane < 3, sums, 0.0), axis=1, keepdims=True)
        c1 = jnp.sum(jnp.where((lane >= 3) & (lane < 6), sums, 0.0), axis=1, keepdims=True)
        chi, cmid, clo = _split3(jnp.where(lane < HEAD_DIM, c1, c0) * LOG2E)
        for hh in range(2):
            keep, f = _head_lanes(hh)
            ones_q = jnp.where((f >= 3) & (f < 6), 1.0, 0.0).astype(BF16)
            ones_k = jnp.where((f >= 0) & (f < 3), 1.0, 0.0).astype(BF16)
            qf = jnp.where(f == 0, chi, jnp.where(f == 1, cmid, jnp.where(f == 2, clo, ones_q)))
            kf = jnp.where(f == 3, -chi, jnp.where(f == 4, -cmid, jnp.where(f == 5, -clo, ones_k)))
            qp_scr[hh, rows, :] = jnp.where(keep, q_ref[0, rows, :], qf)
            kp_scr[hh, rows, :] = jnp.where(keep, k_ref[0, rows, :], kf)
    _causal_attention(qp_scr, kp_scr, v_ref, o_ref, t)


def _fox(u3, lf3):
    return _pair_attention(_fox_kernel, u3, (lf3,), 6, 8, 10, FOX_HEADS, "fox")


def _moba_kernel(q_ref, k_ref, v_ref, o_ref, qp_scr, kp_scr):
    s_len = q_ref.shape[1]
    t = MOBA_BLOCK
    nkb = s_len // t
    sub = 8
    assert nkb <= sub
    means = [jnp.mean(k_ref[0, j * t:(j + 1) * t, :].astype(F32), axis=0, keepdims=True)
             for j in range(nkb)]
    km_pair = jnp.concatenate(means + [jnp.zeros((sub - nkb, LANES), F32)], axis=0) \
        if nkb < sub else jnp.concatenate(means, axis=0)
    stack = []
    for hh in range(2):
        keep, _ = _head_lanes(hh)
        stack += [part.astype(F32) for part in _split3(jnp.where(keep, km_pair, 0.0))]
    stack.append(jnp.zeros((LANES - 6 * sub, LANES), F32))
    km_all = jnp.concatenate(stack, axis=0).astype(BF16)
    blk = lax.broadcasted_iota(jnp.int32, (sub, t), 0)
    gates = [_dot_nt(km_all, q_ref[0, i * t:(i + 1) * t, :]) for i in range(nkb)]
    for hh in range(2):
        keep, f = _head_lanes(hh)
        for i in range(nkb):
            rows = slice(i * t, (i + 1) * t)
            q = q_ref[0, rows, :]
            base = 3 * sub * hh
            g = (gates[i][base:base + sub] + gates[i][base + sub:base + 2 * sub]) \
                + gates[i][base + 2 * sub:base + 3 * sub]
            past = blk < i
            g = jnp.where(past, g, NEG_INF)
            rank = jnp.zeros((sub, t), jnp.int32)
            for r in range(i):
                gr = jnp.broadcast_to(g[r:r + 1, :], (sub, t))
                beats = (gr > g) | ((gr == g) & (r < blk))
                rank = rank + beats.astype(jnp.int32)
            drop = jnp.where(past & (rank >= MOBA_TOPK), 1.0, 0.0)
            below = [jnp.zeros((HEAD_DIM, t), F32)] if hh == 0 else []
            above = jnp.zeros((LANES - sub - (HEAD_DIM if hh == 0 else 0), t), F32)
            padded = jnp.concatenate(below + [drop, above], axis=0)
            kf = jnp.where(f == i, NEG_INF, 0.0).astype(BF16)
            qp_scr[hh, rows, :] = jnp.where(keep, q, padded.T.astype(BF16))
            kp_scr[hh, rows, :] = jnp.where(keep, k_ref[0, rows, :], kf)
    _causal_attention(qp_scr, kp_scr, v_ref, o_ref, t)


def _moba(u3):
    return _pair_attention(_moba_kernel, u3, (), 12, 14, 16, MOBA_HEADS, "moba")


def _rms_f32(x):
    x = x.astype(F32)
    return x * lax.rsqrt(jnp.mean(x * x, axis=-1, keepdims=True) + RMS_EPS)


def _outproj_kernel(oa_ref, ob_ref, oc_ref, gain_ref, w_ref, x_ref, h_ref):
    a0, a1, a2 = 0, SWA_WIDTH, SWA_WIDTH + FOX_WIDTH
    ya = (_rms_f32(oa_ref[...]) * gain_ref[:, a0:a1]).astype(BF16)
    yb = (_rms_f32(ob_ref[...]) * gain_ref[:, a1:a2]).astype(BF16)
    yc = (_rms_f32(oc_ref[...]) * gain_ref[:, a2:]).astype(BF16)
    y = (_dot(ya, w_ref[a0:a1, :]) + _dot(yb, w_ref[a1:a2, :])) + _dot(yc, w_ref[a2:, :])
    h_ref[...] = x_ref[...] + y


def _outproj(oa, ob, oc, gain, w_out, x2):
    t = x2.shape[0]
    tm = ROW_TILE
    row = lambda i: (i, 0)
    fixed = lambda i: (0, 0)
    return pl.pallas_call(
        _outproj_kernel,
        grid=(t // tm,),
        in_specs=[
            pl.BlockSpec((tm, SWA_WIDTH), row),
            pl.BlockSpec((tm, FOX_WIDTH), row),
            pl.BlockSpec((tm, MOBA_WIDTH), row),
            pl.BlockSpec((1, MIX_WIDTH), fixed),
            pl.BlockSpec((MIX_WIDTH, D_MODEL), fixed),
            pl.BlockSpec((tm, D_MODEL), row),
        ],
        out_specs=pl.BlockSpec((tm, D_MODEL), row),
        out_shape=jax.ShapeDtypeStruct((t, D_MODEL), F32),
        compiler_params=_params("parallel"),
        name="outproj",
    )(oa, ob, oc, gain, w_out, x2)


def _silu(x):
    return x / (1.0 + jnp.exp(-x))


def _ffn_kernel(h_ref, g_ref, wg_ref, wu_ref, wd_ref, o_ref, acc_scr):
    j = pl.program_id(1)

    @pl.when((pl.program_id(0) == 0) & (j == 0))
    def _():
        acc_scr[...] = jnp.zeros_like(acc_scr)

    x = h_ref[...]
    inv = lax.rsqrt(jnp.mean(x * x, axis=-1, keepdims=True) + RMS_EPS)
    z = (x * inv * g_ref[...]).astype(BF16)
    a = (_silu(_dot(z, wg_ref[...].astype(BF16))) * _dot(z, wu_ref[...].astype(BF16))).astype(BF16)
    total = acc_scr[...] + _dot(a, wd_ref[...].astype(BF16))
    o_ref[...] = x + total
    acc_scr[...] = jnp.where(j == pl.num_programs(1) - 1, 0.0, total)


def _dense_ffn(h2, norm_g, w_gate, w_up, w_down):
    t = h2.shape[0]
    tm, tf = 2 * ROW_TILE, FF_TILE
    return pl.pallas_call(
        _ffn_kernel,
        grid=(t // tm, D_FF // tf),
        in_specs=[
            pl.BlockSpec((tm, D_MODEL), lambda i, j: (i, 0)),
            pl.BlockSpec((1, D_MODEL), lambda i, j: (0, 0)),
            pl.BlockSpec((D_MODEL, tf), lambda i, j: (0, j)),
            pl.BlockSpec((D_MODEL, tf), lambda i, j: (0, j)),
            pl.BlockSpec((tf, D_MODEL), lambda i, j: (j, 0)),
        ],
        out_specs=pl.BlockSpec((tm, D_MODEL), lambda i, j: (i, 0)),
        out_shape=jax.ShapeDtypeStruct((t, D_MODEL), F32),
        scratch_shapes=[pltpu.VMEM((tm, D_MODEL), F32)],
        compiler_params=_params("arbitrary", "arbitrary"),
        name="dense_ffn",
    )(h2, norm_g, w_gate, w_up, w_down)


def _router_kernel(h_ref, g_ref, wr_ref, z_ref, idx_ref, gate_ref, rank_ref, count_ref, count_scr):
    tm = h_ref.shape[0]
    x = h_ref[...]
    inv = lax.rsqrt(jnp.mean(x * x, axis=-1, keepdims=True) + RMS_EPS)
    z = x * inv * g_ref[...]
    for c in range(CHUNKS):
        z_ref[pl.ds(c, tm, stride=CHUNKS), :] = z[:, c * LANES:(c + 1) * LANES]
    zh, zm, _ = _split3(z)
    wh, wm = wr_ref[0], wr_ref[1]
    logits = _dot(zh, wh) + (_dot(zh, wm) + _dot(zm, wh))
    lane = lax.broadcasted_iota(jnp.int32, (tm, LANES), 1)
    logits = jnp.where(lane < N_EXPERTS, logits, -jnp.inf)
    m1 = jnp.max(logits, axis=1, keepdims=True)
    i1 = jnp.min(jnp.where(logits == m1, lane, LANES), axis=1, keepdims=True)
    rest = jnp.where(lane == i1, -jnp.inf, logits)
    m2 = jnp.max(rest, axis=1, keepdims=True)
    i2 = jnp.min(jnp.where(rest == m2, lane, LANES), axis=1, keepdims=True)
    e2 = jnp.exp(m2 - m1)
    g1 = 1.0 / (1.0 + e2)
    idx_ref[...] = jnp.where(lane == 0, i1, i2)
    gate_ref[...] = jnp.where(lane == 0, g1, e2 * g1)

    @pl.when(pl.program_id(0) == 0)
    def _():
        count_scr[...] = jnp.zeros_like(count_scr)

    chosen = ((lane == i1) | (lane == i2)).astype(F32)
    r = lax.broadcasted_iota(jnp.int32, (tm, tm), 0)
    c = lax.broadcasted_iota(jnp.int32, (tm, tm), 1)
    before = jnp.where(c < r, 1.0, 0.0).astype(BF16)
    earlier = _dot(before, chosen.astype(BF16)) + count_scr[...]
    r1 = jnp.sum(jnp.where(lane == i1, earlier, 0.0), axis=1, keepdims=True)
    r2 = jnp.sum(jnp.where(lane == i2, earlier, 0.0), axis=1, keepdims=True)
    rank_ref[...] = jnp.where(lane == 0, r1, r2).astype(jnp.int32)
    count_scr[...] += jnp.sum(chosen, axis=0, keepdims=True)
    count_ref[...] = count_scr[...].astype(jnp.int32)


def _router(h2, norm_g, wr3):
    t = h2.shape[0]
    tm = ROW_TILE
    row = lambda i: (i, 0)
    return pl.pallas_call(
        _router_kernel,
        grid=(t // tm,),
        in_specs=[
            pl.BlockSpec((tm, D_MODEL), row),
            pl.BlockSpec((1, D_MODEL), lambda i: (0, 0)),
            pl.BlockSpec((2, D_MODEL, LANES), lambda i: (0, 0, 0)),
        ],
        out_specs=[pl.BlockSpec((tm * CHUNKS, LANES), row),
                   pl.BlockSpec((tm, LANES), row),
                   pl.BlockSpec((tm, LANES), row),
                   pl.BlockSpec((tm, LANES), row),
                   pl.BlockSpec((1, LANES), lambda i: (0, 0))],
        out_shape=[jax.ShapeDtypeStruct((t * CHUNKS, LANES), F32),
                   jax.ShapeDtypeStruct((t, LANES), jnp.int32),
                   jax.ShapeDtypeStruct((t, LANES), F32),
                   jax.ShapeDtypeStruct((t, LANES), jnp.int32),
                   jax.ShapeDtypeStruct((1, LANES), jnp.int32)],
        scratch_shapes=[pltpu.VMEM((1, LANES), F32)],
        compiler_params=_params("arbitrary"),
        name="router",
    )(h2, norm_g, wr3)


def _row_copy(src, src_row, dst, dst_row, sem):
    return pltpu.make_async_copy(src.at[pl.ds(src_row * CHUNKS, CHUNKS), :],
                                 dst.at[pl.ds(dst_row * CHUNKS, CHUNKS), :], sem)


def _rows_wait(src, dst, rows, sem):
    pltpu.make_async_copy(src.at[pl.ds(0, rows * CHUNKS), :],
                          dst.at[pl.ds(0, rows * CHUNKS), :], sem).wait()


def _dispatch_kernel(dest_ref, fill_ref, z_ref, wg_ref, wu_ref, wd_ref,
                     x_hbm, wg_out, wu_out, wd_out, zero_scr, fsem, sem):
    i = pl.program_id(0)
    blk = MOE_BLOCK
    tile = z_ref.shape[0] // CHUNKS
    wg_out[...] = wg_ref[...].astype(BF16)
    wu_out[...] = wu_ref[...].astype(BF16)
    wd_out[...] = wd_ref[...].astype(BF16)

    @pl.when(i == 0)
    def _():
        zero_scr[...] = jnp.zeros_like(zero_scr)
        for e in range(fill_ref.shape[0]):
            @pl.when(fill_ref[e] >= 0)
            def _():
                start = pl.multiple_of(fill_ref[e] * CHUNKS, blk * CHUNKS)
                pltpu.make_async_copy(zero_scr, x_hbm.at[pl.ds(start, blk * CHUNKS), :], fsem).start()
        for e in range(fill_ref.shape[0]):
            @pl.when(fill_ref[e] >= 0)
            def _():
                _rows_wait(zero_scr, x_hbm, blk, fsem)

    base = i * tile * TOP_K

    tokens_per_trip = MOE_ISSUE_UNROLL // TOP_K

    def trip(it, c):
        for k in range(MOE_ISSUE_UNROLL):
            token = it * tokens_per_trip + k // TOP_K
            slot = dest_ref[base + it * MOE_ISSUE_UNROLL + k]
            _row_copy(z_ref, token, x_hbm, slot, sem).start(priority=k % 2)
        return c
    lax.fori_loop(0, tile // tokens_per_trip, trip, 0)
    for _ in range(TOP_K):
        _rows_wait(z_ref, x_hbm, tile, sem)


def _dispatch(z8, dest, fill, p_rows, w_gate, w_up, w_down):
    tile = DISPATCH_TILE
    steps = z8.shape[0] // (tile * CHUNKS)
    ne, d, ff = w_gate.shape
    assert z8.shape[0] % (tile * CHUNKS) == 0 and (ne * d) % steps == 0 and (ne * ff) % steps == 0
    up_rows, down_rows = ne * d // steps, ne * ff // steps
    wspec = lambda rows, cols: pl.BlockSpec((rows, cols), lambda i, dd, f: (i, 0))
    grid_spec = pltpu.PrefetchScalarGridSpec(
        num_scalar_prefetch=2,
        grid=(steps,),
        in_specs=[pl.BlockSpec((tile * CHUNKS, LANES), lambda i, dd, f: (i, 0)),
                  wspec(up_rows, ff), wspec(up_rows, ff), wspec(down_rows, d)],
        out_specs=[pl.BlockSpec(memory_space=pl.ANY),
                   wspec(up_rows, ff), wspec(up_rows, ff), wspec(down_rows, d)],
        scratch_shapes=[pltpu.VMEM((MOE_BLOCK * CHUNKS, LANES), F32),
                        pltpu.SemaphoreType.DMA(()), pltpu.SemaphoreType.DMA(())],
    )
    x8, wg, wu, wd = pl.pallas_call(
        _dispatch_kernel,
        grid_spec=grid_spec,
        out_shape=[jax.ShapeDtypeStruct((p_rows * CHUNKS, LANES), F32),
                   jax.ShapeDtypeStruct((ne * d, ff), BF16),
                   jax.ShapeDtypeStruct((ne * d, ff), BF16),
                   jax.ShapeDtypeStruct((ne * ff, d), BF16)],
        compiler_params=_params("arbitrary"),
        name="moe_dispatch",
    )(dest, fill, z8, w_gate.reshape(ne * d, ff), w_up.reshape(ne * d, ff),
      w_down.reshape(ne * ff, d))
    return x8, wg.reshape(ne, d, ff), wu.reshape(ne, d, ff), wd.reshape(ne, ff, d)


def _moe_kernel(be_ref, live_ref, bx_ref, x_ref, wg_ref, wu_ref, wd_ref, y_ref, acc):
    i = pl.program_id(0)
    j = pl.program_id(1)
    tm = acc.shape[0]

    @pl.when((i == 0) & (j == 0))
    def _():
        acc[...] = jnp.zeros_like(acc)

    @pl.when((live_ref[i] == 0) & (j == 0))
    def _():
        y_ref[...] = jnp.zeros_like(y_ref)

    @pl.when(live_ref[i] > 0)
    def _():
        x = jnp.concatenate([x_ref[pl.ds(c, tm, stride=CHUNKS), :].astype(BF16)
                             for c in range(CHUNKS)], axis=1)
        a = (_silu(_dot(x, wg_ref[0])) * _dot(x, wu_ref[0])).astype(BF16)
        total = acc[...] + _dot(a, wd_ref[0])
        for c in range(CHUNKS):
            y_ref[pl.ds(c, tm, stride=CHUNKS), :] = total[:, c * LANES:(c + 1) * LANES]
        acc[...] = jnp.where(j == pl.num_programs(1) - 1, 0.0, total)


def _moe_experts(x8, blk_e, blk_live, blk_x, w_gate, w_up, w_down):
    tm, tf = MOE_BLOCK, MOE_FF_TILE
    nblk = blk_e.shape[0]
    nff = D_FF // tf
    ff = lambda i, j, live: jnp.where(live[i] > 0, j, nff - 1)
    grid_spec = pltpu.PrefetchScalarGridSpec(
        num_scalar_prefetch=3,
        grid=(nblk, nff),
        in_specs=[
            pl.BlockSpec((tm * CHUNKS, LANES), lambda i, j, be, lv, bx: (bx[i], 0)),
            pl.BlockSpec((1, D_MODEL, tf), lambda i, j, be, lv, bx: (be[i], 0, ff(i, j, lv))),
            pl.BlockSpec((1, D_MODEL, tf), lambda i, j, be, lv, bx: (be[i], 0, ff(i, j, lv))),
            pl.BlockSpec((1, tf, D_MODEL), lambda i, j, be, lv, bx: (be[i], ff(i, j, lv), 0)),
        ],
        out_specs=pl.BlockSpec((tm * CHUNKS, LANES), lambda i, j, be, lv, bx: (i, 0)),
        scratch_shapes=[pltpu.VMEM((tm, D_MODEL), F32)],
    )
    return pl.pallas_call(
        _moe_kernel,
        grid_spec=grid_spec,
        out_shape=jax.ShapeDtypeStruct(x8.shape, F32),
        compiler_params=_params("arbitrary", "arbitrary"),
        name="moe_experts",
    )(blk_e, blk_live, blk_x, x8, w_gate, w_up, w_down)


def _combine_kernel(dest_ref, h_ref, gate_ref, *rest, final_norm):
    if final_norm:
        g_ref, y_hbm, o_ref, ybuf, sem = rest
    else:
        y_hbm, o_ref, ybuf, sem = rest
    i = pl.program_id(0)
    tm = h_ref.shape[0]
    rows = tm * TOP_K
    stride = TOP_K * CHUNKS

    def pull(tile, slot):
        def trip(it, c):
            for k in range(MOE_ISSUE_UNROLL):
                r = it * MOE_ISSUE_UNROLL + k
                copy = _row_copy(y_hbm, dest_ref[tile * rows + r], ybuf.at[slot], r, sem.at[slot])
                copy.start(priority=k % 2)
            return c
        lax.fori_loop(0, rows // MOE_ISSUE_UNROLL, trip, 0)

    @pl.when(i == 0)
    def _():
        pull(0, 0)

    @pl.when(i + 1 < pl.num_programs(0))
    def _():
        pull(i + 1, (i + 1) % 2)

    slot = i % 2
    yb = ybuf.at[slot]
    _rows_wait(y_hbm, yb, rows, sem.at[slot])
    gates = [gate_ref[:, k:k + 1] for k in range(TOP_K)]
    parts = []
    ss = jnp.zeros((tm, 1), F32)
    for c in range(CHUNKS):
        y = gates[0] * yb[pl.ds(c, tm, stride=stride), :]
        for k in range(1, TOP_K):
            y = y + gates[k] * yb[pl.ds(k * CHUNKS + c, tm, stride=stride), :]
        hc = h_ref[:, c * LANES:(c + 1) * LANES] + y
        parts.append(hc)
        ss = ss + jnp.sum(hc * hc, axis=1, keepdims=True)
    if final_norm:
        inv = lax.rsqrt(ss / D_MODEL + RMS_EPS)
    for c in range(CHUNKS):
        cols = slice(c * LANES, (c + 1) * LANES)
        o_ref[:, cols] = parts[c] * inv * g_ref[:, cols] if final_norm else parts[c]


def _combine(h2, y8, dest, gate, out_g):
    t = h2.shape[0]
    tm = ROW_TILE
    final_norm = out_g is not None
    row = lambda i, d: (i, 0)
    in_specs = [pl.BlockSpec((tm, D_MODEL), row), pl.BlockSpec((tm, LANES), row)]
    args = [h2, gate]
    if final_norm:
        in_specs.append(pl.BlockSpec((1, D_MODEL), lambda i, d: (0, 0)))
        args.append(out_g.reshape(1, D_MODEL))
    grid_spec = pltpu.PrefetchScalarGridSpec(
        num_scalar_prefetch=1,
        grid=(t // tm,),
        in_specs=in_specs + [pl.BlockSpec(memory_space=pl.ANY)],
        out_specs=pl.BlockSpec((tm, D_MODEL), row),
        scratch_shapes=[pltpu.VMEM((2, tm * TOP_K * CHUNKS, LANES), F32),
                        pltpu.SemaphoreType.DMA((2,))],
    )
    return pl.pallas_call(
        functools.partial(_combine_kernel, final_norm=final_norm),
        grid_spec=grid_spec,
        out_shape=jax.ShapeDtypeStruct((t, D_MODEL), F32),
        compiler_params=_params("arbitrary"),
        name="moe_combine",
    )(dest, *args, y8)


def _norm_kernel(h_ref, g_ref, o_ref):
    x = h_ref[...]
    o_ref[...] = x * lax.rsqrt(jnp.mean(x * x, axis=-1, keepdims=True) + RMS_EPS) * g_ref[...]


def _final_norm(h2, norm_g):
    t = h2.shape[0]
    tm = ROW_TILE
    return pl.pallas_call(
        _norm_kernel,
        grid=(t // tm,),
        in_specs=[pl.BlockSpec((tm, D_MODEL), lambda i: (i, 0)),
                  pl.BlockSpec((1, D_MODEL), lambda i: (0, 0))],
        out_specs=pl.BlockSpec((tm, D_MODEL), lambda i: (i, 0)),
        out_shape=jax.ShapeDtypeStruct((t, D_MODEL), F32),
        compiler_params=_params("parallel"),
        name="final_norm",
    )(h2, norm_g)


def _dispatch_plan(top_i, rank, counts, t):
    n = t * TOP_K
    tm = MOE_BLOCK
    padded = (counts + tm - 1) // tm * tm
    pends = jnp.cumsum(padded)
    pstarts = pends - padded
    onehot = (top_i[:, :, None] == jnp.arange(N_EXPERTS)[None, None, :]).astype(jnp.int32)
    dest = (jnp.sum(onehot * pstarts[None, None, :], axis=2) + rank).astype(jnp.int32).reshape(n)
    p_rows = -(-n // tm) * tm + N_EXPERTS * tm
    nblk = p_rows // tm
    blk = jnp.arange(nblk, dtype=jnp.int32)
    blk_e = jnp.minimum(jnp.sum((blk * tm)[:, None] >= pends[None, :], axis=1), N_EXPERTS - 1)
    live = blk * tm < pends[-1]
    last_live = jnp.maximum(pends[-1] // tm - 1, 0)
    blk_e = jnp.where(live, blk_e, blk_e[last_live])
    blk_x = jnp.where(live, blk, 0)
    tail = jnp.where(padded > 0, pends - tm, -1)
    spare = pends[-1] + jnp.arange(N_EXPERTS) * tm
    fill = jnp.concatenate([tail, jnp.where(spare < p_rows, spare, -1)])
    i32 = lambda a: a.astype(jnp.int32)
    return dest, i32(fill), i32(blk_e), i32(live), i32(blk_x), p_rows


def _rope_tables(seq):
    inv = 1.0 / (ROPE_THETA ** (jnp.arange(0, HEAD_DIM, 2, dtype=F32) / HEAD_DIM))
    ang = jnp.arange(seq, dtype=F32)[:, None] * inv[None, :]
    cos = jnp.concatenate([jnp.cos(ang)] * 4, axis=-1)
    sin = jnp.concatenate([jnp.sin(ang)] * 4, axis=-1)
    upper = (jnp.arange(LANES) % HEAD_DIM) >= HEAD_DIM // 2
    sa = jnp.where(upper[None, :], sin, 0.0)
    sb = jnp.where(upper[None, :], 0.0, -sin)
    return cos, sa, sb


def _mixer(h2, b, s, norm_g, w_in, forget_bias, sinks, mix_gain, w_out, tables):
    f0 = SWA_WIDTH + 2 * SWA_KV_WIDTH + 3 * FOX_WIDTH
    f1 = f0 + FOX_HEADS
    assert f0 + 3 * MOBA_WIDTH == U_WIDTH
    wt = jnp.swapaxes(w_in, 0, 1)
    w_all = jnp.concatenate(
        [wt[:f0], wt[f1:], jnp.pad(wt[f0:f1], ((0, LANES - FOX_HEADS), (0, 0)))],
        axis=0).astype(BF16)
    fbias = jnp.pad(forget_bias.astype(F32), (0, LANES - FOX_HEADS)).reshape(1, LANES)
    u, lf = _project(h2, norm_g.reshape(1, D_MODEL), w_all, *tables, fbias, s)
    u3 = u.reshape(b, s, U_WIDTH)
    oa = _swa(u3, sinks.astype(F32))
    ob = _fox(u3, lf.reshape(b, s, LANES))
    oc = _moba(u3)
    t = b * s
    return _outproj(oa.reshape(t, SWA_WIDTH), ob.reshape(t, FOX_WIDTH), oc.reshape(t, MOBA_WIDTH),
                    mix_gain.astype(F32).reshape(1, MIX_WIDTH), w_out.astype(BF16), h2)


def _moe_ffn(h2, norm_g, w_router, w_gate, w_up, w_down, out_g):
    t = h2.shape[0]
    wr = jnp.pad(w_router.astype(F32), ((0, 0), (0, LANES - N_EXPERTS)))
    wr2 = jnp.stack(_split3(wr)[:2])
    z8, top, gate, rank, counts = _router(h2, norm_g.reshape(1, D_MODEL), wr2)
    dest, fill, blk_e, blk_live, blk_x, p_rows = _dispatch_plan(
        top[:, :TOP_K], rank[:, :TOP_K], counts[0, :N_EXPERTS], t)
    x8, wg, wu, wd = _dispatch(z8, dest, fill, p_rows, w_gate, w_up, w_down)
    y8 = _moe_experts(x8, blk_e, blk_live, blk_x, wg, wu, wd)
    return _combine(h2, y8, dest, gate, out_g)


def kernel(x, attn_norm, w_in, fox_forget_bias, swa_sinks, mix_gain, w_out, ffn_norm,
           dense_w_gate, dense_w_up, dense_w_down, router_w, moe_w_gate, moe_w_up,
           moe_w_down, final_norm):
    b, s, d = x.shape
    depth = attn_norm.shape[0]
    assert d == D_MODEL and s % MOBA_BLOCK == 0 and s % ROW_TILE == 0
    tables = _rope_tables(s)
    h = x.reshape(b * s, d)
    normed = False
    for layer in range(depth):
        h = _mixer(h, b, s, attn_norm[layer], w_in[layer], fox_forget_bias[layer],
                   swa_sinks[layer], mix_gain[layer], w_out[layer], tables)
        j = layer // 2
        last = layer == depth - 1
        if layer % 2 == 0:
            h = _dense_ffn(h, ffn_norm[layer].reshape(1, d), dense_w_gate[j], dense_w_up[j],
                           dense_w_down[j])
        else:
            h = _moe_ffn(h, ffn_norm[layer], router_w[j], moe_w_gate[j], moe_w_up[j],
                         moe_w_down[j], final_norm if last else None)
            normed = last
    if not normed:
        h = _final_norm(h, final_norm.reshape(1, d))
    return h.reshape(b, s, d)
```

```python
import functools

import jax
import jax.numpy as jnp
from jax import lax
from jax.experimental import pallas as pl
from jax.experimental.pallas import tpu as pltpu

F32 = jnp.float32
BF16 = jnp.bfloat16

D_MODEL = 1024
HEAD_DIM = 64
LANES = 128
SWA_Q_HEADS = 8
SWA_KV_HEADS = 2
SWA_WINDOW = 128
FOX_HEADS = 4
MOBA_HEADS = 4
MOBA_BLOCK = 256
MOBA_TOPK = 3
ROPE_THETA = 10000.0
RMS_EPS = 1e-5
D_FF = 3584
N_EXPERTS = 8
TOP_K = 2
MOE_BLOCK = 512
NEG_INF = -1e30
ATTN_SCALE = HEAD_DIM ** -0.5
LOG2E = 1.4426950408889634
Q_SCALE = ATTN_SCALE * LOG2E

SWA_WIDTH = SWA_Q_HEADS * HEAD_DIM
SWA_KV_WIDTH = SWA_KV_HEADS * HEAD_DIM
FOX_WIDTH = FOX_HEADS * HEAD_DIM
MOBA_WIDTH = MOBA_HEADS * HEAD_DIM
MIX_WIDTH = SWA_WIDTH + FOX_WIDTH + MOBA_WIDTH

U_BLOCKS = 18
U_WIDTH = U_BLOCKS * LANES
ROPE_BLOCKS = (0, 1, 2, 3, 4, 12, 13, 14, 15)
Q_BLOCKS = (0, 1, 2, 3, 6, 7, 12, 13)
W_ALL_WIDTH = U_WIDTH + LANES

VMEM_LIMIT = 56 * 1024 * 1024

ROW_TILE = 512
ATT_TILE = 256
SWA_STEP_BLOCKS = 4
FF_TILE = 512
MOE_FF_TILE = 1792
CHUNKS = D_MODEL // LANES
MOE_ISSUE_UNROLL = 8
DISPATCH_TILE = 512


def _params(*sem):
    return pltpu.CompilerParams(dimension_semantics=sem, vmem_limit_bytes=VMEM_LIMIT)


def _split3(x):
    hi = x.astype(BF16)
    r1 = x - hi.astype(F32)
    mid = r1.astype(BF16)
    lo = (r1 - mid.astype(F32)).astype(BF16)
    return hi, mid, lo


def _dot_nt(a, b):
    return lax.dot_general(a, b, (((1,), (1,)), ((), ())), preferred_element_type=F32)


def _dot(a, b):
    return jnp.dot(a, b, preferred_element_type=F32)


def _side_cast(w2d, steps, index_map):
    rows, cols = w2d.shape
    assert rows % steps == 0
    spec = pl.BlockSpec((rows // steps, cols), index_map)
    return spec, spec, jax.ShapeDtypeStruct((rows, cols), BF16)


def _proj_kernel(x_ref, g_ref, w_ref, cos_ref, sa_ref, sb_ref, fb_ref, *rest):
    if len(rest) == 4:
        side_in, u_ref, lf_ref, side_out = rest
        side_out[...] = side_in[...].astype(BF16)
    else:
        u_ref, lf_ref = rest
    x = x_ref[...]
    inv = lax.rsqrt(jnp.mean(x * x, axis=-1, keepdims=True) + RMS_EPS)
    h = (x * inv * g_ref[...]).astype(BF16)
    cos = cos_ref[...]
    sa = sa_ref[...]
    sb = sb_ref[...]
    for c in range(U_BLOCKS // 2):
        acc = _dot_nt(h, w_ref[c * 2 * LANES:(c + 1) * 2 * LANES, :])
        for half in range(2):
            blk = 2 * c + half
            a = acc[:, half * LANES:(half + 1) * LANES]
            if blk in ROPE_BLOCKS:
                a = a * cos + pltpu.roll(a, 32, 1) * sa + pltpu.roll(a, 96, 1) * sb
            if blk in Q_BLOCKS:
                a = a * Q_SCALE
            u_ref[:, blk * LANES:(blk + 1) * LANES] = a.astype(BF16)
    f = _dot_nt(h, w_ref[U_WIDTH:W_ALL_WIDTH, :]) + fb_ref[...]
    lf_ref[...] = jnp.minimum(f, 0.0) - jnp.log(1.0 + jnp.exp(-jnp.abs(f)))


def _project(x2, norm_g, w_all, cos, sa, sb, fbias, seq, side=None):
    t = x2.shape[0]
    tm = ROW_TILE
    nseq = seq // tm
    row = lambda i: (i, 0)
    pos = lambda i: (i % nseq, 0)
    fixed = lambda i: (0, 0)
    in_specs = [
        pl.BlockSpec((tm, D_MODEL), row),
        pl.BlockSpec((1, D_MODEL), fixed),
        pl.BlockSpec((W_ALL_WIDTH, D_MODEL), fixed),
        pl.BlockSpec((tm, LANES), pos),
        pl.BlockSpec((tm, LANES), pos),
        pl.BlockSpec((tm, LANES), pos),
        pl.BlockSpec((1, LANES), fixed),
    ]
    out_specs = [pl.BlockSpec((tm, U_WIDTH), row), pl.BlockSpec((tm, LANES), row)]
    out_shape = [jax.ShapeDtypeStruct((t, U_WIDTH), BF16), jax.ShapeDtypeStruct((t, LANES), F32)]
    args = [x2, norm_g, w_all, cos, sa, sb, fbias]
    if side is not None:
        spec_in, spec_out, shape = _side_cast(side, t // tm, row)
        in_specs.append(spec_in)
        out_specs.append(spec_out)
        out_shape.append(shape)
        args.append(side)
    outs = pl.pallas_call(
        _proj_kernel,
        grid=(t // tm,),
        in_specs=in_specs,
        out_specs=out_specs,
        out_shape=out_shape,
        compiler_params=_params("parallel"),
        name="proj",
    )(*args)
    return tuple(outs) if side is not None else (*outs, None)


def _swa_kernel(sink_ref, q_ref, kc_ref, kp_ref, vc_ref, vp_ref, *rest):
    if len(rest) == 3:
        side_in, o_ref, side_out = rest
        side_out[...] = side_in[...].astype(BF16)
    else:
        (o_ref,) = rest
    n = pl.program_id(1)
    w = SWA_WINDOW
    group = SWA_Q_HEADS // SWA_KV_HEADS
    lane = lax.broadcasted_iota(jnp.int32, (1, LANES), 1)
    low = lane < HEAD_DIM

    def both_halves(kv_ref_prev, kv_ref_cur):
        x = jnp.concatenate([kv_ref_prev[0], kv_ref_cur[0]], axis=0)
        swapped = jnp.concatenate([x[:, HEAD_DIM:], x[:, :HEAD_DIM]], axis=1)
        return jnp.where(low, x, swapped), jnp.where(low, swapped, x)

    k_all = both_halves(kp_ref, kc_ref)
    v_all = both_halves(vp_ref, vc_ref)
    qi = lax.broadcasted_iota(jnp.int32, (group * w, 2 * w), 0) % w
    kj = lax.broadcasted_iota(jnp.int32, (group * w, 2 * w), 1)
    window = (kj > qi) & (kj <= qi + w)
    pairs = group // 2
    for a in range(SWA_STEP_BLOCKS):
        rows = slice(a * w, (a + 1) * w)
        valid = (window & ((kj >= w) | (n > 0))) if a == 0 else window
        for g in range(SWA_KV_HEADS):
            k2 = k_all[g][a * w:(a + 2) * w]
            v2 = v_all[g][a * w:(a + 2) * w]
            qs = jnp.concatenate(
                [jnp.where(low if half == 0 else ~low,
                           q_ref[0, rows, (pairs * g + jj) * LANES:(pairs * g + jj + 1) * LANES], 0)
                 for jj in range(pairs) for half in range(2)], axis=0)
            s = jnp.where(valid, _dot_nt(qs, k2), NEG_INF)
            ps, inv = [], []
            for j in range(group):
                sj = s[j * w:(j + 1) * w]
                sink = sink_ref[g * group + j] * LOG2E
                m = jnp.maximum(jnp.max(sj, axis=1, keepdims=True), sink)
                p = jnp.exp2(sj - m)
                inv.append(1.0 / (jnp.sum(p, axis=1, keepdims=True) + jnp.exp2(sink - m)))
                ps.append(p.astype(BF16))
            o = _dot(jnp.concatenate(ps, axis=0), v2)
            outs = [o[j * w:(j + 1) * w] * inv[j] for j in range(group)]
            for jj in range(pairs):
                oj = jnp.where(low, outs[2 * jj], outs[2 * jj + 1])
                cols = slice((pairs * g + jj) * LANES, (pairs * g + jj + 1) * LANES)
                o_ref[0, rows, cols] = oj.astype(BF16)


def _swa(u3, sinks, side=None):
    b, s, _ = u3.shape
    w = SWA_WINDOW
    nb = SWA_STEP_BLOCKS
    assert s % (nb * w) == 0
    steps = s // (nb * w)
    cur = lambda blk: (lambda bi, n, sk: (bi, n, blk))
    prev = lambda blk: (lambda bi, n, sk: (bi, jnp.maximum(n * nb - 1, 0), blk))
    in_specs = [
        pl.BlockSpec((1, nb * w, SWA_WIDTH), lambda bi, n, sk: (bi, n, 0)),
        pl.BlockSpec((1, nb * w, LANES), cur(4)),
        pl.BlockSpec((1, w, LANES), prev(4)),
        pl.BlockSpec((1, nb * w, LANES), cur(5)),
        pl.BlockSpec((1, w, LANES), prev(5)),
    ]
    out_specs = [pl.BlockSpec((1, nb * w, SWA_WIDTH), lambda bi, n, sk: (bi, n, 0))]
    out_shape = [jax.ShapeDtypeStruct((b, s, SWA_WIDTH), BF16)]
    args = [sinks, u3, u3, u3, u3, u3]
    if side is not None:
        spec_in, spec_out, shape = _side_cast(side, b * steps, lambda bi, n, sk: (bi * steps + n, 0))
        in_specs.append(spec_in)
        out_specs.append(spec_out)
        out_shape.append(shape)
        args.append(side)
    grid_spec = pltpu.PrefetchScalarGridSpec(
        num_scalar_prefetch=1, grid=(b, steps), in_specs=in_specs, out_specs=out_specs)
    outs = pl.pallas_call(
        _swa_kernel,
        grid_spec=grid_spec,
        out_shape=out_shape,
        compiler_params=_params("parallel", "parallel"),
        name="swa",
    )(*args)
    return tuple(outs) if side is not None else (outs[0], None)


def _head_lanes(hh):
    lane = lax.broadcasted_iota(jnp.int32, (1, LANES), 1)
    if hh == 0:
        return lane < HEAD_DIM, lane - HEAD_DIM
    return lane >= HEAD_DIM, lane


def _causal_attention(qp_scr, kp_scr, v_ref, o_ref, t):
    s_len = qp_scr.shape[1]
    lane = lax.broadcasted_iota(jnp.int32, (1, LANES), 1)
    row = lax.broadcasted_iota(jnp.int32, (t, t), 0)
    col = lax.broadcasted_iota(jnp.int32, (t, t), 1)
    for qi in range(s_len // t):
        rows = slice(qi * t, (qi + 1) * t)
        outs = []
        for hh in range(2):
            q = qp_scr[hh, rows, :]
            sd = jnp.where(col <= row, _dot_nt(q, kp_scr[hh, rows, :]), NEG_INF)
            m = jnp.max(sd, axis=1, keepdims=True)
            if qi:
                sp = _dot_nt(q, kp_scr[hh, :qi * t, :])
                m = jnp.maximum(m, jnp.max(sp, axis=1, keepdims=True))
            pd = jnp.exp2(sd - m)
            l = jnp.sum(pd, axis=1, keepdims=True)
            o = _dot(pd.astype(BF16), v_ref[0, rows, :])
            if qi:
                pp = jnp.exp2(sp - m)
                l = l + jnp.sum(pp, axis=1, keepdims=True)
                o = o + _dot(pp.astype(BF16), v_ref[0, :qi * t, :])
            outs.append(o * (1.0 / l))
        o_ref[0, rows, :] = jnp.where(lane < HEAD_DIM, outs[0], outs[1]).astype(BF16)


def _pair_attention(kernel_fn, u3, extra, q_blk, k_blk, v_blk, n_heads, name):
    b, s, _ = u3.shape
    pair = lambda blk: (lambda bi, p: (bi, 0, blk + p))
    seq_block = pl.BlockSpec((1, s, LANES), lambda bi, p: (bi, 0, 0))
    return pl.pallas_call(
        kernel_fn,
        grid=(b, n_heads // 2),
        in_specs=[pl.BlockSpec((1, s, LANES), pair(q_blk)),
                  pl.BlockSpec((1, s, LANES), pair(k_blk)),
                  pl.BlockSpec((1, s, LANES), pair(v_blk))] + [seq_block] * len(extra),
        out_specs=pl.BlockSpec((1, s, LANES), pair(0)),
        out_shape=jax.ShapeDtypeStruct((b, s, n_heads * HEAD_DIM), BF16),
        scratch_shapes=[pltpu.VMEM((2, s, LANES), BF16), pltpu.VMEM((2, s, LANES), BF16)],
        compiler_params=_params("parallel", "parallel"),
        name=name,
    )(u3, u3, u3, *extra)


def _fox_kernel(q_ref, k_ref, v_ref, lf_ref, o_ref, qp_scr, kp_scr):
    p = pl.program_id(1)
    s_len = q_ref.shape[1]
    t = ATT_TILE
    lane = lax.broadcasted_iota(jnp.int32, (1, LANES), 1)
    r = lax.broadcasted_iota(jnp.int32, (t, t), 0)
    cc = lax.broadcasted_iota(jnp.int32, (t, t), 1)
    tri = jnp.where(cc <= r, 1.0, 0.0).astype(BF16)
    carry = jnp.zeros((1, LANES), F32)
    for i in range(s_len // t):
        rows = slice(i * t, (i + 1) * t)
        lf = lf_ref[0, rows, :]
        lf0 = jnp.sum(jnp.where(lane == 2 * p, lf, 0.0), axis=1, keepdims=True)
        lf1 = jnp.sum(jnp.where(lane == 2 * p + 1, lf, 0.0), axis=1, keepdims=True)
        hi, mid, lo = _split3(jnp.where(lane < 3, lf0, jnp.where(lane < 6, lf1, 0.0)))
        terms = jnp.where((lane == 0) | (lane == 3), hi,
                          jnp.where((lane == 1) | (lane == 4), mid, lo))
        sums = _dot(tri, terms) + carry
        carry = sums[t - 1:t, :]
        c0 = jnp.sum(jnp.where(lane < 3, sums, 0.0), axis=1, keepdims=True)
        c1 = jnp.sum(jnp.where((lane >= 3) & (lane < 6), sums, 0.0), axis=1, keepdims=True)
        chi, cmid, clo = _split3(jnp.where(lane < HEAD_DIM, c1, c0) * LOG2E)
        for hh in range(2):
            keep, f = _head_lanes(hh)
            ones_q = jnp.where((f >= 3) & (f < 6), 1.0, 0.0).astype(BF16)
            ones_k = jnp.where((f >= 0) & (f < 3), 1.0, 0.0).astype(BF16)
            qf = jnp.where(f == 0, chi, jnp.where(f == 1, cmid, jnp.where(f == 2, clo, ones_q)))
            kf = jnp.where(f == 3, -chi, jnp.where(f == 4, -cmid, jnp.where(f == 5, -clo, ones_k)))
            qp_scr[hh, rows, :] = jnp.where(keep, q_ref[0, rows, :], qf)
            kp_scr[hh, rows, :] = jnp.where(keep, k_ref[0, rows, :], kf)
    _causal_attention(qp_scr, kp_scr, v_ref, o_ref, t)


def _fox(u3, lf3):
    return _pair_attention(_fox_kernel, u3, (lf3,), 6, 8, 10, FOX_HEADS, "fox")


def _moba_kernel(q_ref, k_ref, v_ref, o_ref, qp_scr, kp_scr):
    s_len = q_ref.shape[1]
    t = MOBA_BLOCK
    nkb = s_len // t
    sub = 8
    assert nkb <= sub
    means = [jnp.mean(k_ref[0, j * t:(j + 1) * t, :].astype(F32), axis=0, keepdims=True)
             for j in range(nkb)]
    km_pair = jnp.concatenate(means + [jnp.zeros((sub - nkb, LANES), F32)], axis=0) \
        if nkb < sub else jnp.concatenate(means, axis=0)
    stack = []
    for hh in range(2):
        keep, _ = _head_lanes(hh)
        stack += [part.astype(F32) for part in _split3(jnp.where(keep, km_pair, 0.0))]
    stack.append(jnp.zeros((LANES - 6 * sub, LANES), F32))
    km_all = jnp.concatenate(stack, axis=0).astype(BF16)
    blk = lax.broadcasted_iota(jnp.int32, (sub, t), 0)
    gates = [_dot_nt(km_all, q_ref[0, i * t:(i + 1) * t, :]) for i in range(nkb)]
    for hh in range(2):
        keep, f = _head_lanes(hh)
        for i in range(nkb):
            rows = slice(i * t, (i + 1) * t)
            q = q_ref[0, rows, :]
            base = 3 * sub * hh
            g = (gates[i][base:base + sub] + gates[i][base + sub:base + 2 * sub]) \
                + gates[i][base + 2 * sub:base + 3 * sub]
            past = blk < i
            g = jnp.where(past, g, NEG_INF)
            rank = jnp.zeros((sub, t), jnp.int32)
            for r in range(i):
                gr = jnp.broadcast_to(g[r:r + 1, :], (sub, t))
                beats = (gr > g) | ((gr == g) & (r < blk))
                rank = rank + beats.astype(jnp.int32)
            drop = jnp.where(past & (rank >= MOBA_TOPK), 1.0, 0.0)
            below = [jnp.zeros((HEAD_DIM, t), F32)] if hh == 0 else []
            above = jnp.zeros((LANES - sub - (HEAD_DIM if hh == 0 else 0), t), F32)
            padded = jnp.concatenate(below + [drop, above], axis=0)
            kf = jnp.where(f == i, NEG_INF, 0.0).astype(BF16)
            qp_scr[hh, rows, :] = jnp.where(keep, q, padded.T.astype(BF16))
            kp_scr[hh, rows, :] = jnp.where(keep, k_ref[0, rows, :], kf)
    _causal_attention(qp_scr, kp_scr, v_ref, o_ref, t)


def _moba(u3):
    return _pair_attention(_moba_kernel, u3, (), 12, 14, 16, MOBA_HEADS, "moba")


def _rms_f32(x):
    x = x.astype(F32)
    return x * lax.rsqrt(jnp.mean(x * x, axis=-1, keepdims=True) + RMS_EPS)


def _outproj_kernel(oa_ref, ob_ref, oc_ref, gain_ref, w_ref, x_ref, h_ref):
    a0, a1, a2 = 0, SWA_WIDTH, SWA_WIDTH + FOX_WIDTH
    ya = (_rms_f32(oa_ref[...]) * gain_ref[:, a0:a1]).astype(BF16)
    yb = (_rms_f32(ob_ref[...]) * gain_ref[:, a1:a2]).astype(BF16)
    yc = (_rms_f32(oc_ref[...]) * gain_ref[:, a2:]).astype(BF16)
    y = (_dot(ya, w_ref[a0:a1, :]) + _dot(yb, w_ref[a1:a2, :])) + _dot(yc, w_ref[a2:, :])
    h_ref[...] = x_ref[...] + y


def _outproj(oa, ob, oc, gain, w_out, x2):
    t = x2.shape[0]
    tm = ROW_TILE
    row = lambda i: (i, 0)
    fixed = lambda i: (0, 0)
    return pl.pallas_call(
        _outproj_kernel,
        grid=(t // tm,),
        in_specs=[
            pl.BlockSpec((tm, SWA_WIDTH), row),
            pl.BlockSpec((tm, FOX_WIDTH), row),
            pl.BlockSpec((tm, MOBA_WIDTH), row),
            pl.BlockSpec((1, MIX_WIDTH), fixed),
            pl.BlockSpec((MIX_WIDTH, D_MODEL), fixed),
            pl.BlockSpec((tm, D_MODEL), row),
        ],
        out_specs=pl.BlockSpec((tm, D_MODEL), row),
        out_shape=jax.ShapeDtypeStruct((t, D_MODEL), F32),
        compiler_params=_params("parallel"),
        name="outproj",
    )(oa, ob, oc, gain, w_out, x2)


def _silu(x):
    return x / (1.0 + jnp.exp(-x))


def _ffn_kernel(h_ref, g_ref, wg_ref, wu_ref, wd_ref, o_ref, acc_scr):
    j = pl.program_id(1)

    @pl.when((pl.program_id(0) == 0) & (j == 0))
    def _():
        acc_scr[...] = jnp.zeros_like(acc_scr)

    x = h_ref[...]
    inv = lax.rsqrt(jnp.mean(x * x, axis=-1, keepdims=True) + RMS_EPS)
    z = (x * inv * g_ref[...]).astype(BF16)
    a = (_silu(_dot(z, wg_ref[...].astype(BF16))) * _dot(z, wu_ref[...].astype(BF16))).astype(BF16)
    total = acc_scr[...] + _dot(a, wd_ref[...].astype(BF16))
    o_ref[...] = x + total
    acc_scr[...] = jnp.where(j == pl.num_programs(1) - 1, 0.0, total)


def _dense_ffn(h2, norm_g, w_gate, w_up, w_down):
    t = h2.shape[0]
    tm, tf = 2 * ROW_TILE, FF_TILE
    return pl.pallas_call(
        _ffn_kernel,
        grid=(t // tm, D_FF // tf),
        in_specs=[
            pl.BlockSpec((tm, D_MODEL), lambda i, j: (i, 0)),
            pl.BlockSpec((1, D_MODEL), lambda i, j: (0, 0)),
            pl.BlockSpec((D_MODEL, tf), lambda i, j: (0, j)),
            pl.BlockSpec((D_MODEL, tf), lambda i, j: (0, j)),
            pl.BlockSpec((tf, D_MODEL), lambda i, j: (j, 0)),
        ],
        out_specs=pl.BlockSpec((tm, D_MODEL), lambda i, j: (i, 0)),
        out_shape=jax.ShapeDtypeStruct((t, D_MODEL), F32),
        scratch_shapes=[pltpu.VMEM((tm, D_MODEL), F32)],
        compiler_params=_params("arbitrary", "arbitrary"),
        name="dense_ffn",
    )(h2, norm_g, w_gate, w_up, w_down)


def _router_kernel(h_ref, g_ref, wr_ref, z_ref, idx_ref, gate_ref, rank_ref, count_ref, count_scr):
    tm = h_ref.shape[0]
    x = h_ref[...]
    inv = lax.rsqrt(jnp.mean(x * x, axis=-1, keepdims=True) + RMS_EPS)
    z = x * inv * g_ref[...]
    for c in range(CHUNKS):
        z_ref[pl.ds(c, tm, stride=CHUNKS), :] = z[:, c * LANES:(c + 1) * LANES]
    zh, zm, _ = _split3(z)
    wh, wm = wr_ref[0], wr_ref[1]
    logits = _dot(zh, wh) + (_dot(zh, wm) + _dot(zm, wh))
    lane = lax.broadcasted_iota(jnp.int32, (tm, LANES), 1)
    logits = jnp.where(lane < N_EXPERTS, logits, -jnp.inf)
    m1 = jnp.max(logits, axis=1, keepdims=True)
    i1 = jnp.min(jnp.where(logits == m1, lane, LANES), axis=1, keepdims=True)
    rest = jnp.where(lane == i1, -jnp.inf, logits)
    m2 = jnp.max(rest, axis=1, keepdims=True)
    i2 = jnp.min(jnp.where(rest == m2, lane, LANES), axis=1, keepdims=True)
    e2 = jnp.exp(m2 - m1)
    g1 = 1.0 / (1.0 + e2)
    idx_ref[...] = jnp.where(lane == 0, i1, i2)
    gate_ref[...] = jnp.where(lane == 0, g1, e2 * g1)

    @pl.when(pl.program_id(0) == 0)
    def _():
        count_scr[...] = jnp.zeros_like(count_scr)

    chosen = ((lane == i1) | (lane == i2)).astype(F32)
    r = lax.broadcasted_iota(jnp.int32, (tm, tm), 0)
    c = lax.broadcasted_iota(jnp.int32, (tm, tm), 1)
    before = jnp.where(c < r, 1.0, 0.0).astype(BF16)
    earlier = _dot(before, chosen.astype(BF16)) + count_scr[...]
    r1 = jnp.sum(jnp.where(lane == i1, earlier, 0.0), axis=1, keepdims=True)
    r2 = jnp.sum(jnp.where(lane == i2, earlier, 0.0), axis=1, keepdims=True)
    rank_ref[...] = jnp.where(lane == 0, r1, r2).astype(jnp.int32)
    count_scr[...] += jnp.sum(chosen, axis=0, keepdims=True)
    count_ref[...] = count_scr[...].astype(jnp.int32)


def _router(h2, norm_g, wr3):
    t = h2.shape[0]
    tm = ROW_TILE
    row = lambda i: (i, 0)
    return pl.pallas_call(
        _router_kernel,
        grid=(t // tm,),
        in_specs=[
            pl.BlockSpec((tm, D_MODEL), row),
            pl.BlockSpec((1, D_MODEL), lambda i: (0, 0)),
            pl.BlockSpec((2, D_MODEL, LANES), lambda i: (0, 0, 0)),
        ],
        out_specs=[pl.BlockSpec((tm * CHUNKS, LANES), row),
                   pl.BlockSpec((tm, LANES), row),
                   pl.BlockSpec((tm, LANES), row),
                   pl.BlockSpec((tm, LANES), row),
                   pl.BlockSpec((1, LANES), lambda i: (0, 0))],
        out_shape=[jax.ShapeDtypeStruct((t * CHUNKS, LANES), F32),
                   jax.ShapeDtypeStruct((t, LANES), jnp.int32),
                   jax.ShapeDtypeStruct((t, LANES), F32),
                   jax.ShapeDtypeStruct((t, LANES), jnp.int32),
                   jax.ShapeDtypeStruct((1, LANES), jnp.int32)],
        scratch_shapes=[pltpu.VMEM((1, LANES), F32)],
        compiler_params=_params("arbitrary"),
        name="router",
    )(h2, norm_g, wr3)


def _row_copy(src, src_row, dst, dst_row, sem):
    return pltpu.make_async_copy(src.at[pl.ds(src_row * CHUNKS, CHUNKS), :],
                                 dst.at[pl.ds(dst_row * CHUNKS, CHUNKS), :], sem)


def _rows_wait(src, dst, rows, sem):
    pltpu.make_async_copy(src.at[pl.ds(0, rows * CHUNKS), :],
                          dst.at[pl.ds(0, rows * CHUNKS), :], sem).wait()


def _dispatch_kernel(dest_ref, fill_ref, z_ref, x_hbm, zero_scr, fsem, sem):
    i = pl.program_id(0)
    blk = MOE_BLOCK
    tile = z_ref.shape[0] // CHUNKS

    @pl.when(i == 0)
    def _():
        zero_scr[...] = jnp.zeros_like(zero_scr)
        for e in range(fill_ref.shape[0]):
            @pl.when(fill_ref[e] >= 0)
            def _():
                start = pl.multiple_of(fill_ref[e] * CHUNKS, blk * CHUNKS)
                pltpu.make_async_copy(zero_scr, x_hbm.at[pl.ds(start, blk * CHUNKS), :], fsem).start()
        for e in range(fill_ref.shape[0]):
            @pl.when(fill_ref[e] >= 0)
            def _():
                _rows_wait(zero_scr, x_hbm, blk, fsem)

    base = i * tile * TOP_K

    tokens_per_trip = MOE_ISSUE_UNROLL // TOP_K

    def trip(it, c):
        for k in range(MOE_ISSUE_UNROLL):
            token = it * tokens_per_trip + k // TOP_K
            slot = dest_ref[base + it * MOE_ISSUE_UNROLL + k]
            _row_copy(z_ref, token, x_hbm, slot, sem).start(priority=k % 2)
        return c
    lax.fori_loop(0, tile // tokens_per_trip, trip, 0)
    for _ in range(TOP_K):
        _rows_wait(z_ref, x_hbm, tile, sem)


def _dispatch(z8, dest, fill, p_rows):
    tile = DISPATCH_TILE
    assert z8.shape[0] % (tile * CHUNKS) == 0
    grid_spec = pltpu.PrefetchScalarGridSpec(
        num_scalar_prefetch=2,
        grid=(z8.shape[0] // (tile * CHUNKS),),
        in_specs=[pl.BlockSpec((tile * CHUNKS, LANES), lambda i, dd, f: (i, 0))],
        out_specs=pl.BlockSpec(memory_space=pl.ANY),
        scratch_shapes=[pltpu.VMEM((MOE_BLOCK * CHUNKS, LANES), F32),
                        pltpu.SemaphoreType.DMA(()), pltpu.SemaphoreType.DMA(())],
    )
    return pl.pallas_call(
        _dispatch_kernel,
        grid_spec=grid_spec,
        out_shape=jax.ShapeDtypeStruct((p_rows * CHUNKS, LANES), F32),
        compiler_params=_params("arbitrary"),
        name="moe_dispatch",
    )(dest, fill, z8)


def _moe_kernel(be_ref, live_ref, bx_ref, x_ref, wg_ref, wu_ref, wd_ref, y_ref, acc):
    i = pl.program_id(0)
    j = pl.program_id(1)
    tm = acc.shape[0]

    @pl.when((i == 0) & (j == 0))
    def _():
        acc[...] = jnp.zeros_like(acc)

    @pl.when((live_ref[i] == 0) & (j == 0))
    def _():
        y_ref[...] = jnp.zeros_like(y_ref)

    @pl.when(live_ref[i] > 0)
    def _():
        x = jnp.concatenate([x_ref[pl.ds(c, tm, stride=CHUNKS), :].astype(BF16)
                             for c in range(CHUNKS)], axis=1)
        a = (_silu(_dot(x, wg_ref[0])) * _dot(x, wu_ref[0])).astype(BF16)
        total = acc[...] + _dot(a, wd_ref[0])
        for c in range(CHUNKS):
            y_ref[pl.ds(c, tm, stride=CHUNKS), :] = total[:, c * LANES:(c + 1) * LANES]
        acc[...] = jnp.where(j == pl.num_programs(1) - 1, 0.0, total)


def _moe_experts(x8, blk_e, blk_live, blk_x, w_gate, w_up, w_down):
    tm, tf = MOE_BLOCK, MOE_FF_TILE
    nblk = blk_e.shape[0]
    nff = D_FF // tf
    ff = lambda i, j, live: jnp.where(live[i] > 0, j, nff - 1)
    grid_spec = pltpu.PrefetchScalarGridSpec(
        num_scalar_prefetch=3,
        grid=(nblk, nff),
        in_specs=[
            pl.BlockSpec((tm * CHUNKS, LANES), lambda i, j, be, lv, bx: (bx[i], 0)),
            pl.BlockSpec((1, D_MODEL, tf), lambda i, j, be, lv, bx: (be[i], 0, ff(i, j, lv))),
            pl.BlockSpec((1, D_MODEL, tf), lambda i, j, be, lv, bx: (be[i], 0, ff(i, j, lv))),
            pl.BlockSpec((1, tf, D_MODEL), lambda i, j, be, lv, bx: (be[i], ff(i, j, lv), 0)),
        ],
        out_specs=pl.BlockSpec((tm * CHUNKS, LANES), lambda i, j, be, lv, bx: (i, 0)),
        scratch_shapes=[pltpu.VMEM((tm, D_MODEL), F32)],
    )
    return pl.pallas_call(
        _moe_kernel,
        grid_spec=grid_spec,
        out_shape=jax.ShapeDtypeStruct(x8.shape, F32),
        compiler_params=_params("arbitrary", "arbitrary"),
        name="moe_experts",
    )(blk_e, blk_live, blk_x, x8, w_gate, w_up, w_down)


def _combine_kernel(dest_ref, h_ref, gate_ref, *rest, final_norm):
    if final_norm:
        g_ref, y_hbm, o_ref, ybuf, sem = rest
    else:
        y_hbm, o_ref, ybuf, sem = rest
    i = pl.program_id(0)
    tm = h_ref.shape[0]
    rows = tm * TOP_K
    stride = TOP_K * CHUNKS

    def pull(tile, slot):
        def trip(it, c):
            for k in range(MOE_ISSUE_UNROLL):
                r = it * MOE_ISSUE_UNROLL + k
                copy = _row_copy(y_hbm, dest_ref[tile * rows + r], ybuf.at[slot], r, sem.at[slot])
                copy.start(priority=k % 2)
            return c
        lax.fori_loop(0, rows // MOE_ISSUE_UNROLL, trip, 0)

    @pl.when(i == 0)
    def _():
        pull(0, 0)

    @pl.when(i + 1 < pl.num_programs(0))
    def _():
        pull(i + 1, (i + 1) % 2)

    slot = i % 2
    yb = ybuf.at[slot]
    _rows_wait(y_hbm, yb, rows, sem.at[slot])
    gates = [gate_ref[:, k:k + 1] for k in range(TOP_K)]
    parts = []
    ss = jnp.zeros((tm, 1), F32)
    for c in range(CHUNKS):
        y = gates[0] * yb[pl.ds(c, tm, stride=stride), :]
        for k in range(1, TOP_K):
            y = y + gates[k] * yb[pl.ds(k * CHUNKS + c, tm, stride=stride), :]
        hc = h_ref[:, c * LANES:(c + 1) * LANES] + y
        parts.append(hc)
        ss = ss + jnp.sum(hc * hc, axis=1, keepdims=True)
    if final_norm:
        inv = lax.rsqrt(ss / D_MODEL + RMS_EPS)
    for c in range(CHUNKS):
        cols = slice(c * LANES, (c + 1) * LANES)
        o_ref[:, cols] = parts[c] * inv * g_ref[:, cols] if final_norm else parts[c]


def _combine(h2, y8, dest, gate, out_g):
    t = h2.shape[0]
    tm = ROW_TILE
    final_norm = out_g is not None
    row = lambda i, d: (i, 0)
    in_specs = [pl.BlockSpec((tm, D_MODEL), row), pl.BlockSpec((tm, LANES), row)]
    args = [h2, gate]
    if final_norm:
        in_specs.append(pl.BlockSpec((1, D_MODEL), lambda i, d: (0, 0)))
        args.append(out_g.reshape(1, D_MODEL))
    grid_spec = pltpu.PrefetchScalarGridSpec(
        num_scalar_prefetch=1,
        grid=(t // tm,),
        in_specs=in_specs + [pl.BlockSpec(memory_space=pl.ANY)],
        out_specs=pl.BlockSpec((tm, D_MODEL), row),
        scratch_shapes=[pltpu.VMEM((2, tm * TOP_K * CHUNKS, LANES), F32),
                        pltpu.SemaphoreType.DMA((2,))],
    )
    return pl.pallas_call(
        functools.partial(_combine_kernel, final_norm=final_norm),
        grid_spec=grid_spec,
        out_shape=jax.ShapeDtypeStruct((t, D_MODEL), F32),
        compiler_params=_params("arbitrary"),
        name="moe_combine",
    )(dest, *args, y8)


def _norm_kernel(h_ref, g_ref, o_ref):
    x = h_ref[...]
    o_ref[...] = x * lax.rsqrt(jnp.mean(x * x, axis=-1, keepdims=True) + RMS_EPS) * g_ref[...]


def _final_norm(h2, norm_g):
    t = h2.shape[0]
    tm = ROW_TILE
    return pl.pallas_call(
        _norm_kernel,
        grid=(t // tm,),
        in_specs=[pl.BlockSpec((tm, D_MODEL), lambda i: (i, 0)),
                  pl.BlockSpec((1, D_MODEL), lambda i: (0, 0))],
        out_specs=pl.BlockSpec((tm, D_MODEL), lambda i: (i, 0)),
        out_shape=jax.ShapeDtypeStruct((t, D_MODEL), F32),
        compiler_params=_params("parallel"),
        name="final_norm",
    )(h2, norm_g)


def _dispatch_plan(top_i, rank, counts, t):
    n = t * TOP_K
    tm = MOE_BLOCK
    padded = (counts + tm - 1) // tm * tm
    pends = jnp.cumsum(padded)
    pstarts = pends - padded
    onehot = (top_i[:, :, None] == jnp.arange(N_EXPERTS)[None, None, :]).astype(jnp.int32)
    dest = (jnp.sum(onehot * pstarts[None, None, :], axis=2) + rank).astype(jnp.int32).reshape(n)
    p_rows = -(-n // tm) * tm + N_EXPERTS * tm
    nblk = p_rows // tm
    blk = jnp.arange(nblk, dtype=jnp.int32)
    blk_e = jnp.minimum(jnp.sum((blk * tm)[:, None] >= pends[None, :], axis=1), N_EXPERTS - 1)
    live = blk * tm < pends[-1]
    last_live = jnp.maximum(pends[-1] // tm - 1, 0)
    blk_e = jnp.where(live, blk_e, blk_e[last_live])
    blk_x = jnp.where(live, blk, 0)
    tail = jnp.where(padded > 0, pends - tm, -1)
    spare = pends[-1] + jnp.arange(N_EXPERTS) * tm
    fill = jnp.concatenate([tail, jnp.where(spare < p_rows, spare, -1)])
    i32 = lambda a: a.astype(jnp.int32)
    return dest, i32(fill), i32(blk_e), i32(live), i32(blk_x), p_rows


def _rope_tables(seq):
    inv = 1.0 / (ROPE_THETA ** (jnp.arange(0, HEAD_DIM, 2, dtype=F32) / HEAD_DIM))
    ang = jnp.arange(seq, dtype=F32)[:, None] * inv[None, :]
    cos = jnp.concatenate([jnp.cos(ang)] * 4, axis=-1)
    sin = jnp.concatenate([jnp.sin(ang)] * 4, axis=-1)
    upper = (jnp.arange(LANES) % HEAD_DIM) >= HEAD_DIM // 2
    sa = jnp.where(upper[None, :], sin, 0.0)
    sb = jnp.where(upper[None, :], 0.0, -sin)
    return cos, sa, sb


def _mixer(h2, b, s, norm_g, w_in, forget_bias, sinks, mix_gain, w_out, tables,
           proj_side=None, swa_side=None):
    f0 = SWA_WIDTH + 2 * SWA_KV_WIDTH + 3 * FOX_WIDTH
    f1 = f0 + FOX_HEADS
    assert f0 + 3 * MOBA_WIDTH == U_WIDTH
    wt = jnp.swapaxes(w_in, 0, 1)
    w_all = jnp.concatenate(
        [wt[:f0], wt[f1:], jnp.pad(wt[f0:f1], ((0, LANES - FOX_HEADS), (0, 0)))],
        axis=0).astype(BF16)
    fbias = jnp.pad(forget_bias.astype(F32), (0, LANES - FOX_HEADS)).reshape(1, LANES)
    u, lf, proj_cast = _project(h2, norm_g.reshape(1, D_MODEL), w_all, *tables, fbias, s, proj_side)
    u3 = u.reshape(b, s, U_WIDTH)
    oa, swa_cast = _swa(u3, sinks.astype(F32), swa_side)
    ob = _fox(u3, lf.reshape(b, s, LANES))
    oc = _moba(u3)
    t = b * s
    h = _outproj(oa.reshape(t, SWA_WIDTH), ob.reshape(t, FOX_WIDTH), oc.reshape(t, MOBA_WIDTH),
                 mix_gain.astype(F32).reshape(1, MIX_WIDTH), w_out.astype(BF16), h2)
    return h, proj_cast, swa_cast


def _moe_ffn(h2, norm_g, w_router, wg, wu, wd, out_g):
    t = h2.shape[0]
    wr = jnp.pad(w_router.astype(F32), ((0, 0), (0, LANES - N_EXPERTS)))
    wr2 = jnp.stack(_split3(wr)[:2])
    z8, top, gate, rank, counts = _router(h2, norm_g.reshape(1, D_MODEL), wr2)
    dest, fill, blk_e, blk_live, blk_x, p_rows = _dispatch_plan(
        top[:, :TOP_K], rank[:, :TOP_K], counts[0, :N_EXPERTS], t)
    x8 = _dispatch(z8, dest, fill, p_rows)
    y8 = _moe_experts(x8, blk_e, blk_live, blk_x, wg, wu, wd)
    return _combine(h2, y8, dest, gate, out_g)


def kernel(x, attn_norm, w_in, fox_forget_bias, swa_sinks, mix_gain, w_out, ffn_norm,
           dense_w_gate, dense_w_up, dense_w_down, router_w, moe_w_gate, moe_w_up,
           moe_w_down, final_norm):
    b, s, d = x.shape
    depth = attn_norm.shape[0]
    assert d == D_MODEL and s % MOBA_BLOCK == 0 and s % ROW_TILE == 0
    tables = _rope_tables(s)
    h = x.reshape(b * s, d)
    normed = False
    ne, _, ff = moe_w_gate.shape[1:]
    flat = lambda w: w.reshape(w.shape[0] * w.shape[1], w.shape[2])
    wg = None
    for layer in range(depth):
        j = layer // 2
        routed = layer % 2 == 1
        routed_next = layer + 1 < depth and (layer + 1) % 2 == 1
        proj_side = flat(moe_w_up[j]) if routed else (flat(moe_w_gate[(layer + 1) // 2])
                                                     if routed_next else None)
        swa_side = flat(moe_w_down[j]) if routed else None
        h, proj_cast, swa_cast = _mixer(h, b, s, attn_norm[layer], w_in[layer],
                                        fox_forget_bias[layer], swa_sinks[layer], mix_gain[layer],
                                        w_out[layer], tables, proj_side, swa_side)
        last = layer == depth - 1
        if not routed:
            wg = proj_cast
            h = _dense_ffn(h, ffn_norm[layer].reshape(1, d), dense_w_gate[j], dense_w_up[j],
                           dense_w_down[j])
        else:
            h = _moe_ffn(h, ffn_norm[layer], router_w[j], wg.reshape(ne, d, ff),
                         proj_cast.reshape(ne, d, ff), swa_cast.reshape(ne, ff, d),
                         final_norm if last else None)
            normed = last
    if not normed:
        h = _final_norm(h, final_norm.reshape(1, d))
    return h.reshape(b, s, d)
```

```python
import functools

import jax
import jax.numpy as jnp
from jax import lax
from jax.experimental import pallas as pl
from jax.experimental.pallas import tpu as pltpu

F32 = jnp.float32
BF16 = jnp.bfloat16

D_MODEL = 1024
HEAD_DIM = 64
LANES = 128
SWA_Q_HEADS = 8
SWA_KV_HEADS = 2
SWA_WINDOW = 128
FOX_HEADS = 4
MOBA_HEADS = 4
MOBA_BLOCK = 256
MOBA_TOPK = 3
ROPE_THETA = 10000.0
RMS_EPS = 1e-5
D_FF = 3584
N_EXPERTS = 8
TOP_K = 2
MOE_BLOCK = 512
NEG_INF = -1e30
ATTN_SCALE = HEAD_DIM ** -0.5
LOG2E = 1.4426950408889634
Q_SCALE = ATTN_SCALE * LOG2E

SWA_WIDTH = SWA_Q_HEADS * HEAD_DIM
SWA_KV_WIDTH = SWA_KV_HEADS * HEAD_DIM
FOX_WIDTH = FOX_HEADS * HEAD_DIM
MOBA_WIDTH = MOBA_HEADS * HEAD_DIM
MIX_WIDTH = SWA_WIDTH + FOX_WIDTH + MOBA_WIDTH

U_BLOCKS = 18
U_WIDTH = U_BLOCKS * LANES
ROPE_BLOCKS = (0, 1, 2, 3, 4, 12, 13, 14, 15)
Q_BLOCKS = (0, 1, 2, 3, 6, 7, 12, 13)
W_ALL_WIDTH = U_WIDTH + LANES

VMEM_LIMIT = 56 * 1024 * 1024

ROW_TILE = 512
ATT_TILE = 256
SWA_STEP_BLOCKS = 4
FF_TILE = 512
MOE_FF_TILE = 1792
CHUNKS = D_MODEL // LANES
MOE_ISSUE_UNROLL = 8
DISPATCH_TILE = 512


def _params(*sem):
    return pltpu.CompilerParams(dimension_semantics=sem, vmem_limit_bytes=VMEM_LIMIT)


def _split3(x):
    hi = x.astype(BF16)
    r1 = x - hi.astype(F32)
    mid = r1.astype(BF16)
    lo = (r1 - mid.astype(F32)).astype(BF16)
    return hi, mid, lo


def _dot_nt(a, b):
    return lax.dot_general(a, b, (((1,), (1,)), ((), ())), preferred_element_type=F32)


def _dot(a, b):
    return jnp.dot(a, b, preferred_element_type=F32)


def _side_cast(w2d, steps, index_map):
    rows, cols = w2d.shape
    assert rows % steps == 0
    spec = pl.BlockSpec((rows // steps, cols), index_map)
    return spec, spec, jax.ShapeDtypeStruct((rows, cols), BF16)


def _with_side_cast(kernel_fn, n_in):
    def body(*refs):
        side_in, out_ref, side_out = refs[n_in:n_in + 3]
        side_out[...] = side_in[...].astype(BF16)
        kernel_fn(*refs[:n_in], out_ref, *refs[n_in + 3:])
    return body


def _proj_kernel(x_ref, g_ref, w_ref, cos_ref, sa_ref, sb_ref, fb_ref, u_ref, lf_ref):
    x = x_ref[...]
    inv = lax.rsqrt(jnp.mean(x * x, axis=-1, keepdims=True) + RMS_EPS)
    h = (x * inv * g_ref[...]).astype(BF16)
    cos = cos_ref[...]
    sa = sa_ref[...]
    sb = sb_ref[...]
    for c in range(U_BLOCKS // 2):
        acc = _dot_nt(h, w_ref[c * 2 * LANES:(c + 1) * 2 * LANES, :])
        for half in range(2):
            blk = 2 * c + half
            a = acc[:, half * LANES:(half + 1) * LANES]
            if blk in ROPE_BLOCKS:
                a = a * cos + pltpu.roll(a, 32, 1) * sa + pltpu.roll(a, 96, 1) * sb
            if blk in Q_BLOCKS:
                a = a * Q_SCALE
            u_ref[:, blk * LANES:(blk + 1) * LANES] = a.astype(BF16)
    f = _dot_nt(h, w_ref[U_WIDTH:W_ALL_WIDTH, :]) + fb_ref[...]
    lf_ref[...] = jnp.minimum(f, 0.0) - jnp.log(1.0 + jnp.exp(-jnp.abs(f)))


def _project(x2, norm_g, w_all, cos, sa, sb, fbias, seq):
    t = x2.shape[0]
    tm = ROW_TILE
    nseq = seq // tm
    row = lambda i: (i, 0)
    pos = lambda i: (i % nseq, 0)
    fixed = lambda i: (0, 0)
    return pl.pallas_call(
        _proj_kernel,
        grid=(t // tm,),
        in_specs=[
            pl.BlockSpec((tm, D_MODEL), row),
            pl.BlockSpec((1, D_MODEL), fixed),
            pl.BlockSpec((W_ALL_WIDTH, D_MODEL), fixed),
            pl.BlockSpec((tm, LANES), pos),
            pl.BlockSpec((tm, LANES), pos),
            pl.BlockSpec((tm, LANES), pos),
            pl.BlockSpec((1, LANES), fixed),
        ],
        out_specs=[pl.BlockSpec((tm, U_WIDTH), row), pl.BlockSpec((tm, LANES), row)],
        out_shape=[jax.ShapeDtypeStruct((t, U_WIDTH), BF16),
                   jax.ShapeDtypeStruct((t, LANES), F32)],
        compiler_params=_params("parallel"),
        name="proj",
    )(x2, norm_g, w_all, cos, sa, sb, fbias)


def _swa_kernel(sink_ref, q_ref, kc_ref, kp_ref, vc_ref, vp_ref, o_ref):
    n = pl.program_id(1)
    w = SWA_WINDOW
    group = SWA_Q_HEADS // SWA_KV_HEADS
    lane = lax.broadcasted_iota(jnp.int32, (1, LANES), 1)
    low = lane < HEAD_DIM

    def both_halves(kv_ref_prev, kv_ref_cur):
        x = jnp.concatenate([kv_ref_prev[0], kv_ref_cur[0]], axis=0)
        swapped = jnp.concatenate([x[:, HEAD_DIM:], x[:, :HEAD_DIM]], axis=1)
        return jnp.where(low, x, swapped), jnp.where(low, swapped, x)

    k_all = both_halves(kp_ref, kc_ref)
    v_all = both_halves(vp_ref, vc_ref)
    qi = lax.broadcasted_iota(jnp.int32, (group * w, 2 * w), 0) % w
    kj = lax.broadcasted_iota(jnp.int32, (group * w, 2 * w), 1)
    window = (kj > qi) & (kj <= qi + w)
    pairs = group // 2
    for a in range(SWA_STEP_BLOCKS):
        rows = slice(a * w, (a + 1) * w)
        valid = (window & ((kj >= w) | (n > 0))) if a == 0 else window
        for g in range(SWA_KV_HEADS):
            k2 = k_all[g][a * w:(a + 2) * w]
            v2 = v_all[g][a * w:(a + 2) * w]
            qs = jnp.concatenate(
                [jnp.where(low if half == 0 else ~low,
                           q_ref[0, rows, (pairs * g + jj) * LANES:(pairs * g + jj + 1) * LANES], 0)
                 for jj in range(pairs) for half in range(2)], axis=0)
            s = jnp.where(valid, _dot_nt(qs, k2), NEG_INF)
            ps, inv = [], []
            for j in range(group):
                sj = s[j * w:(j + 1) * w]
                sink = sink_ref[g * group + j] * LOG2E
                m = jnp.maximum(jnp.max(sj, axis=1, keepdims=True), sink)
                p = jnp.exp2(sj - m)
                inv.append(1.0 / (jnp.sum(p, axis=1, keepdims=True) + jnp.exp2(sink - m)))
                ps.append(p.astype(BF16))
            o = _dot(jnp.concatenate(ps, axis=0), v2)
            outs = [o[j * w:(j + 1) * w] * inv[j] for j in range(group)]
            for jj in range(pairs):
                oj = jnp.where(low, outs[2 * jj], outs[2 * jj + 1])
                cols = slice((pairs * g + jj) * LANES, (pairs * g + jj + 1) * LANES)
                o_ref[0, rows, cols] = oj.astype(BF16)


def _swa(u3, sinks):
    b, s, _ = u3.shape
    w = SWA_WINDOW
    nb = SWA_STEP_BLOCKS
    assert s % (nb * w) == 0
    cur = lambda blk: (lambda bi, n, sk: (bi, n, blk))
    prev = lambda blk: (lambda bi, n, sk: (bi, jnp.maximum(n * nb - 1, 0), blk))
    grid_spec = pltpu.PrefetchScalarGridSpec(
        num_scalar_prefetch=1,
        grid=(b, s // (nb * w)),
        in_specs=[
            pl.BlockSpec((1, nb * w, SWA_WIDTH), lambda bi, n, sk: (bi, n, 0)),
            pl.BlockSpec((1, nb * w, LANES), cur(4)),
            pl.BlockSpec((1, w, LANES), prev(4)),
            pl.BlockSpec((1, nb * w, LANES), cur(5)),
            pl.BlockSpec((1, w, LANES), prev(5)),
        ],
        out_specs=pl.BlockSpec((1, nb * w, SWA_WIDTH), lambda bi, n, sk: (bi, n, 0)),
    )
    return pl.pallas_call(
        _swa_kernel,
        grid_spec=grid_spec,
        out_shape=jax.ShapeDtypeStruct((b, s, SWA_WIDTH), BF16),
        compiler_params=_params("parallel", "parallel"),
        name="swa",
    )(sinks, u3, u3, u3, u3, u3)


def _head_lanes(hh):
    lane = lax.broadcasted_iota(jnp.int32, (1, LANES), 1)
    if hh == 0:
        return lane < HEAD_DIM, lane - HEAD_DIM
    return lane >= HEAD_DIM, lane


def _causal_attention(qp_scr, kp_scr, v_ref, o_ref, t):
    s_len = qp_scr.shape[1]
    lane = lax.broadcasted_iota(jnp.int32, (1, LANES), 1)
    row = lax.broadcasted_iota(jnp.int32, (t, t), 0)
    col = lax.broadcasted_iota(jnp.int32, (t, t), 1)
    for qi in range(s_len // t):
        rows = slice(qi * t, (qi + 1) * t)
        outs = []
        for hh in range(2):
            q = qp_scr[hh, rows, :]
            sd = jnp.where(col <= row, _dot_nt(q, kp_scr[hh, rows, :]), NEG_INF)
            m = jnp.max(sd, axis=1, keepdims=True)
            if qi:
                sp = _dot_nt(q, kp_scr[hh, :qi * t, :])
                m = jnp.maximum(m, jnp.max(sp, axis=1, keepdims=True))
            pd = jnp.exp2(sd - m)
            l = jnp.sum(pd, axis=1, keepdims=True)
            o = _dot(pd.astype(BF16), v_ref[0, rows, :])
            if qi:
                pp = jnp.exp2(sp - m)
                l = l + jnp.sum(pp, axis=1, keepdims=True)
                o = o + _dot(pp.astype(BF16), v_ref[0, :qi * t, :])
            outs.append(o * (1.0 / l))
        o_ref[0, rows, :] = jnp.where(lane < HEAD_DIM, outs[0], outs[1]).astype(BF16)


def _pair_attention(kernel_fn, u3, extra, q_blk, k_blk, v_blk, n_heads, name, side=None):
    b, s, _ = u3.shape
    pairs = n_heads // 2
    pair = lambda blk: (lambda bi, p: (bi, 0, blk + p))
    seq_block = pl.BlockSpec((1, s, LANES), lambda bi, p: (bi, 0, 0))
    in_specs = [pl.BlockSpec((1, s, LANES), pair(q_blk)),
                pl.BlockSpec((1, s, LANES), pair(k_blk)),
                pl.BlockSpec((1, s, LANES), pair(v_blk))] + [seq_block] * len(extra)
    out_specs = [pl.BlockSpec((1, s, LANES), pair(0))]
    out_shape = [jax.ShapeDtypeStruct((b, s, n_heads * HEAD_DIM), BF16)]
    args = [u3, u3, u3, *extra]
    if side is not None:
        kernel_fn = _with_side_cast(kernel_fn, len(args))
        spec_in, spec_out, shape = _side_cast(side, b * pairs, lambda bi, p: (bi * pairs + p, 0))
        in_specs.append(spec_in)
        out_specs.append(spec_out)
        out_shape.append(shape)
        args.append(side)
    outs = pl.pallas_call(
        kernel_fn,
        grid=(b, pairs),
        in_specs=in_specs,
        out_specs=out_specs,
        out_shape=out_shape,
        scratch_shapes=[pltpu.VMEM((2, s, LANES), BF16), pltpu.VMEM((2, s, LANES), BF16)],
        compiler_params=_params("parallel", "parallel"),
        name=name,
    )(*args)
    return tuple(outs) if side is not None else (outs[0], None)


def _fox_kernel(q_ref, k_ref, v_ref, lf_ref, o_ref, qp_scr, kp_scr):
    p = pl.program_id(1)
    s_len = q_ref.shape[1]
    t = ATT_TILE
    lane = lax.broadcasted_iota(jnp.int32, (1, LANES), 1)
    r = lax.broadcasted_iota(jnp.int32, (t, t), 0)
    cc = lax.broadcasted_iota(jnp.int32, (t, t), 1)
    tri = jnp.where(cc <= r, 1.0, 0.0).astype(BF16)
    carry = jnp.zeros((1, LANES), F32)
    for i in range(s_len // t):
        rows = slice(i * t, (i + 1) * t)
        lf = lf_ref[0, rows, :]
        lf0 = jnp.sum(jnp.where(lane == 2 * p, lf, 0.0), axis=1, keepdims=True)
        lf1 = jnp.sum(jnp.where(lane == 2 * p + 1, lf, 0.0), axis=1, keepdims=True)
        hi, mid, lo = _split3(jnp.where(lane < 3, lf0, jnp.where(lane < 6, lf1, 0.0)))
        terms = jnp.where((lane == 0) | (lane == 3), hi,
                          jnp.where((lane == 1) | (lane == 4), mid, lo))
        sums = _dot(tri, terms) + carry
        carry = sums[t - 1:t, :]
        c0 = jnp.sum(jnp.where(lane < 3, sums, 0.0), axis=1, keepdims=True)
        c1 = jnp.sum(jnp.where((lane >= 3) & (lane < 6), sums, 0.0), axis=1, keepdims=True)
        chi, cmid, clo = _split3(jnp.where(lane < HEAD_DIM, c1, c0) * LOG2E)
        for hh in range(2):
            keep, f = _head_lanes(hh)
            ones_q = jnp.where((f >= 3) & (f < 6), 1.0, 0.0).astype(BF16)
            ones_k = jnp.where((f >= 0) & (f < 3), 1.0, 0.0).astype(BF16)
            qf = jnp.where(f == 0, chi, jnp.where(f == 1, cmid, jnp.where(f == 2, clo, ones_q)))
            kf = jnp.where(f == 3, -chi, jnp.where(f == 4, -cmid, jnp.where(f == 5, -clo, ones_k)))
            qp_scr[hh, rows, :] = jnp.where(keep, q_ref[0, rows, :], qf)
            kp_scr[hh, rows, :] = jnp.where(keep, k_ref[0, rows, :], kf)
    _causal_attention(qp_scr, kp_scr, v_ref, o_ref, t)


def _fox(u3, lf3, side=None):
    return _pair_attention(_fox_kernel, u3, (lf3,), 6, 8, 10, FOX_HEADS, "fox", side)


def _moba_kernel(q_ref, k_ref, v_ref, o_ref, qp_scr, kp_scr):
    s_len = q_ref.shape[1]
    t = MOBA_BLOCK
    nkb = s_len // t
    sub = 8
    assert nkb <= sub
    means = [jnp.mean(k_ref[0, j * t:(j + 1) * t, :].astype(F32), axis=0, keepdims=True)
             for j in range(nkb)]
    km_pair = jnp.concatenate(means + [jnp.zeros((sub - nkb, LANES), F32)], axis=0) \
        if nkb < sub else jnp.concatenate(means, axis=0)
    stack = []
    for hh in range(2):
        keep, _ = _head_lanes(hh)
        stack += [part.astype(F32) for part in _split3(jnp.where(keep, km_pair, 0.0))]
    stack.append(jnp.zeros((LANES - 6 * sub, LANES), F32))
    km_all = jnp.concatenate(stack, axis=0).astype(BF16)
    blk = lax.broadcasted_iota(jnp.int32, (sub, t), 0)
    gates = [_dot_nt(km_all, q_ref[0, i * t:(i + 1) * t, :]) for i in range(nkb)]
    for hh in range(2):
        keep, f = _head_lanes(hh)
        for i in range(nkb):
            rows = slice(i * t, (i + 1) * t)
            q = q_ref[0, rows, :]
            base = 3 * sub * hh
            g = (gates[i][base:base + sub] + gates[i][base + sub:base + 2 * sub]) \
                + gates[i][base + 2 * sub:base + 3 * sub]
            past = blk < i
            g = jnp.where(past, g, NEG_INF)
            rank = jnp.zeros((sub, t), jnp.int32)
            for r in range(i):
                gr = jnp.broadcast_to(g[r:r + 1, :], (sub, t))
                beats = (gr > g) | ((gr == g) & (r < blk))
                rank = rank + beats.astype(jnp.int32)
            drop = jnp.where(past & (rank >= MOBA_TOPK), 1.0, 0.0)
            below = [jnp.zeros((HEAD_DIM, t), F32)] if hh == 0 else []
            above = jnp.zeros((LANES - sub - (HEAD_DIM if hh == 0 else 0), t), F32)
            padded = jnp.concatenate(below + [drop, above], axis=0)
            kf = jnp.where(f == i, NEG_INF, 0.0).astype(BF16)
            qp_scr[hh, rows, :] = jnp.where(keep, q, padded.T.astype(BF16))
            kp_scr[hh, rows, :] = jnp.where(keep, k_ref[0, rows, :], kf)
    _causal_attention(qp_scr, kp_scr, v_ref, o_ref, t)


def _moba(u3, side=None):
    return _pair_attention(_moba_kernel, u3, (), 12, 14, 16, MOBA_HEADS, "moba", side)


def _rms_f32(x):
    x = x.astype(F32)
    return x * lax.rsqrt(jnp.mean(x * x, axis=-1, keepdims=True) + RMS_EPS)


def _outproj_kernel(oa_ref, ob_ref, oc_ref, gain_ref, w_ref, x_ref, h_ref):
    a0, a1, a2 = 0, SWA_WIDTH, SWA_WIDTH + FOX_WIDTH
    ya = (_rms_f32(oa_ref[...]) * gain_ref[:, a0:a1]).astype(BF16)
    yb = (_rms_f32(ob_ref[...]) * gain_ref[:, a1:a2]).astype(BF16)
    yc = (_rms_f32(oc_ref[...]) * gain_ref[:, a2:]).astype(BF16)
    y = (_dot(ya, w_ref[a0:a1, :]) + _dot(yb, w_ref[a1:a2, :])) + _dot(yc, w_ref[a2:, :])
    h_ref[...] = x_ref[...] + y


def _outproj(oa, ob, oc, gain, w_out, x2):
    t = x2.shape[0]
    tm = ROW_TILE
    row = lambda i: (i, 0)
    fixed = lambda i: (0, 0)
    return pl.pallas_call(
        _outproj_kernel,
        grid=(t // tm,),
        in_specs=[
            pl.BlockSpec((tm, SWA_WIDTH), row),
            pl.BlockSpec((tm, FOX_WIDTH), row),
            pl.BlockSpec((tm, MOBA_WIDTH), row),
            pl.BlockSpec((1, MIX_WIDTH), fixed),
            pl.BlockSpec((MIX_WIDTH, D_MODEL), fixed),
            pl.BlockSpec((tm, D_MODEL), row),
        ],
        out_specs=pl.BlockSpec((tm, D_MODEL), row),
        out_shape=jax.ShapeDtypeStruct((t, D_MODEL), F32),
        compiler_params=_params("parallel"),
        name="outproj",
    )(oa, ob, oc, gain, w_out, x2)


def _silu(x):
    return x / (1.0 + jnp.exp(-x))


def _ffn_kernel(h_ref, g_ref, wg_ref, wu_ref, wd_ref, o_ref, acc_scr):
    j = pl.program_id(1)

    @pl.when((pl.program_id(0) == 0) & (j == 0))
    def _():
        acc_scr[...] = jnp.zeros_like(acc_scr)

    x = h_ref[...]
    inv = lax.rsqrt(jnp.mean(x * x, axis=-1, keepdims=True) + RMS_EPS)
    z = (x * inv * g_ref[...]).astype(BF16)
    a = (_silu(_dot(z, wg_ref[...].astype(BF16))) * _dot(z, wu_ref[...].astype(BF16))).astype(BF16)
    total = acc_scr[...] + _dot(a, wd_ref[...].astype(BF16))
    o_ref[...] = x + total
    acc_scr[...] = jnp.where(j == pl.num_programs(1) - 1, 0.0, total)


def _dense_ffn(h2, norm_g, w_gate, w_up, w_down):
    t = h2.shape[0]
    tm, tf = 2 * ROW_TILE, FF_TILE
    return pl.pallas_call(
        _ffn_kernel,
        grid=(t // tm, D_FF // tf),
        in_specs=[
            pl.BlockSpec((tm, D_MODEL), lambda i, j: (i, 0)),
            pl.BlockSpec((1, D_MODEL), lambda i, j: (0, 0)),
            pl.BlockSpec((D_MODEL, tf), lambda i, j: (0, j)),
            pl.BlockSpec((D_MODEL, tf), lambda i, j: (0, j)),
            pl.BlockSpec((tf, D_MODEL), lambda i, j: (j, 0)),
        ],
        out_specs=pl.BlockSpec((tm, D_MODEL), lambda i, j: (i, 0)),
        out_shape=jax.ShapeDtypeStruct((t, D_MODEL), F32),
        scratch_shapes=[pltpu.VMEM((tm, D_MODEL), F32)],
        compiler_params=_params("arbitrary", "arbitrary"),
        name="dense_ffn",
    )(h2, norm_g, w_gate, w_up, w_down)


def _router_kernel(h_ref, g_ref, wr_ref, z_ref, idx_ref, gate_ref, rank_ref, count_ref, count_scr):
    tm = h_ref.shape[0]
    x = h_ref[...]
    inv = lax.rsqrt(jnp.mean(x * x, axis=-1, keepdims=True) + RMS_EPS)
    z = x * inv * g_ref[...]
    for c in range(CHUNKS):
        z_ref[pl.ds(c, tm, stride=CHUNKS), :] = z[:, c * LANES:(c + 1) * LANES]
    zh, zm, _ = _split3(z)
    wh, wm = wr_ref[0], wr_ref[1]
    logits = _dot(zh, wh) + (_dot(zh, wm) + _dot(zm, wh))
    lane = lax.broadcasted_iota(jnp.int32, (tm, LANES), 1)
    logits = jnp.where(lane < N_EXPERTS, logits, -jnp.inf)
    m1 = jnp.max(logits, axis=1, keepdims=True)
    i1 = jnp.min(jnp.where(logits == m1, lane, LANES), axis=1, keepdims=True)
    rest = jnp.where(lane == i1, -jnp.inf, logits)
    m2 = jnp.max(rest, axis=1, keepdims=True)
    i2 = jnp.min(jnp.where(rest == m2, lane, LANES), axis=1, keepdims=True)
    e2 = jnp.exp(m2 - m1)
    g1 = 1.0 / (1.0 + e2)
    idx_ref[...] = jnp.where(lane == 0, i1, i2)
    gate_ref[...] = jnp.where(lane == 0, g1, e2 * g1)

    @pl.when(pl.program_id(0) == 0)
    def _():
        count_scr[...] = jnp.zeros_like(count_scr)

    chosen = ((lane == i1) | (lane == i2)).astype(F32)
    r = lax.broadcasted_iota(jnp.int32, (tm, tm), 0)
    c = lax.broadcasted_iota(jnp.int32, (tm, tm), 1)
    before = jnp.where(c < r, 1.0, 0.0).astype(BF16)
    earlier = _dot(before, chosen.astype(BF16)) + count_scr[...]
    r1 = jnp.sum(jnp.where(lane == i1, earlier, 0.0), axis=1, keepdims=True)
    r2 = jnp.sum(jnp.where(lane == i2, earlier, 0.0), axis=1, keepdims=True)
    rank_ref[...] = jnp.where(lane == 0, r1, r2).astype(jnp.int32)
    count_scr[...] += jnp.sum(chosen, axis=0, keepdims=True)
    count_ref[...] = count_scr[...].astype(jnp.int32)


def _router(h2, norm_g, wr3):
    t = h2.shape[0]
    tm = ROW_TILE
    row = lambda i: (i, 0)
    return pl.pallas_call(
        _router_kernel,
        grid=(t // tm,),
        in_specs=[
            pl.BlockSpec((tm, D_MODEL), row),
            pl.BlockSpec((1, D_MODEL), lambda i: (0, 0)),
            pl.BlockSpec((2, D_MODEL, LANES), lambda i: (0, 0, 0)),
        ],
        out_specs=[pl.BlockSpec((tm * CHUNKS, LANES), row),
                   pl.BlockSpec((tm, LANES), row),
                   pl.BlockSpec((tm, LANES), row),
                   pl.BlockSpec((tm, LANES), row),
                   pl.BlockSpec((1, LANES), lambda i: (0, 0))],
        out_shape=[jax.ShapeDtypeStruct((t * CHUNKS, LANES), F32),
                   jax.ShapeDtypeStruct((t, LANES), jnp.int32),
                   jax.ShapeDtypeStruct((t, LANES), F32),
                   jax.ShapeDtypeStruct((t, LANES), jnp.int32),
                   jax.ShapeDtypeStruct((1, LANES), jnp.int32)],
        scratch_shapes=[pltpu.VMEM((1, LANES), F32)],
        compiler_params=_params("arbitrary"),
        name="router",
    )(h2, norm_g, wr3)


def _row_copy(src, src_row, dst, dst_row, sem):
    return pltpu.make_async_copy(src.at[pl.ds(src_row * CHUNKS, CHUNKS), :],
                                 dst.at[pl.ds(dst_row * CHUNKS, CHUNKS), :], sem)


def _rows_wait(src, dst, rows, sem):
    pltpu.make_async_copy(src.at[pl.ds(0, rows * CHUNKS), :],
                          dst.at[pl.ds(0, rows * CHUNKS), :], sem).wait()


def _dispatch_kernel(dest_ref, fill_ref, z_ref, x_hbm, zero_scr, fsem, sem):
    i = pl.program_id(0)
    blk = MOE_BLOCK
    tile = z_ref.shape[0] // CHUNKS

    @pl.when(i == 0)
    def _():
        zero_scr[...] = jnp.zeros_like(zero_scr)
        for e in range(fill_ref.shape[0]):
            @pl.when(fill_ref[e] >= 0)
            def _():
                start = pl.multiple_of(fill_ref[e] * CHUNKS, blk * CHUNKS)
                pltpu.make_async_copy(zero_scr, x_hbm.at[pl.ds(start, blk * CHUNKS), :], fsem).start()
        for e in range(fill_ref.shape[0]):
            @pl.when(fill_ref[e] >= 0)
            def _():
                _rows_wait(zero_scr, x_hbm, blk, fsem)

    base = i * tile * TOP_K

    tokens_per_trip = MOE_ISSUE_UNROLL // TOP_K

    def trip(it, c):
        for k in range(MOE_ISSUE_UNROLL):
            token = it * tokens_per_trip + k // TOP_K
            slot = dest_ref[base + it * MOE_ISSUE_UNROLL + k]
            _row_copy(z_ref, token, x_hbm, slot, sem).start(priority=k % 2)
        return c
    lax.fori_loop(0, tile // tokens_per_trip, trip, 0)
    for _ in range(TOP_K):
        _rows_wait(z_ref, x_hbm, tile, sem)


def _dispatch(z8, dest, fill, p_rows):
    tile = DISPATCH_TILE
    assert z8.shape[0] % (tile * CHUNKS) == 0
    grid_spec = pltpu.PrefetchScalarGridSpec(
        num_scalar_prefetch=2,
        grid=(z8.shape[0] // (tile * CHUNKS),),
        in_specs=[pl.BlockSpec((tile * CHUNKS, LANES), lambda i, dd, f: (i, 0))],
        out_specs=pl.BlockSpec(memory_space=pl.ANY),
        scratch_shapes=[pltpu.VMEM((MOE_BLOCK * CHUNKS, LANES), F32),
                        pltpu.SemaphoreType.DMA(()), pltpu.SemaphoreType.DMA(())],
    )
    return pl.pallas_call(
        _dispatch_kernel,
        grid_spec=grid_spec,
        out_shape=jax.ShapeDtypeStruct((p_rows * CHUNKS, LANES), F32),
        compiler_params=_params("arbitrary"),
        name="moe_dispatch",
    )(dest, fill, z8)


def _moe_kernel(be_ref, live_ref, bx_ref, x_ref, wg_ref, wu_ref, wd_ref, y_ref, acc):
    i = pl.program_id(0)
    j = pl.program_id(1)
    tm = acc.shape[0]

    @pl.when((i == 0) & (j == 0))
    def _():
        acc[...] = jnp.zeros_like(acc)

    @pl.when((live_ref[i] == 0) & (j == 0))
    def _():
        y_ref[...] = jnp.zeros_like(y_ref)

    @pl.when(live_ref[i] > 0)
    def _():
        x = jnp.concatenate([x_ref[pl.ds(c, tm, stride=CHUNKS), :].astype(BF16)
                             for c in range(CHUNKS)], axis=1)
        a = (_silu(_dot(x, wg_ref[0])) * _dot(x, wu_ref[0])).astype(BF16)
        total = acc[...] + _dot(a, wd_ref[0])
        for c in range(CHUNKS):
            y_ref[pl.ds(c, tm, stride=CHUNKS), :] = total[:, c * LANES:(c + 1) * LANES]
        acc[...] = jnp.where(j == pl.num_programs(1) - 1, 0.0, total)


def _moe_experts(x8, blk_e, blk_live, blk_x, w_gate, w_up, w_down):
    tm, tf = MOE_BLOCK, MOE_FF_TILE
    nblk = blk_e.shape[0]
    nff = D_FF // tf
    ff = lambda i, j, live: jnp.where(live[i] > 0, j, nff - 1)
    grid_spec = pltpu.PrefetchScalarGridSpec(
        num_scalar_prefetch=3,
        grid=(nblk, nff),
        in_specs=[
            pl.BlockSpec((tm * CHUNKS, LANES), lambda i, j, be, lv, bx: (bx[i], 0)),
            pl.BlockSpec((1, D_MODEL, tf), lambda i, j, be, lv, bx: (be[i], 0, ff(i, j, lv))),
            pl.BlockSpec((1, D_MODEL, tf), lambda i, j, be, lv, bx: (be[i], 0, ff(i, j, lv))),
            pl.BlockSpec((1, tf, D_MODEL), lambda i, j, be, lv, bx: (be[i], ff(i, j, lv), 0)),
        ],
        out_specs=pl.BlockSpec((tm * CHUNKS, LANES), lambda i, j, be, lv, bx: (i, 0)),
        scratch_shapes=[pltpu.VMEM((tm, D_MODEL), F32)],
    )
    return pl.pallas_call(
        _moe_kernel,
        grid_spec=grid_spec,
        out_shape=jax.ShapeDtypeStruct(x8.shape, F32),
        compiler_params=_params("arbitrary", "arbitrary"),
        name="moe_experts",
    )(blk_e, blk_live, blk_x, x8, w_gate, w_up, w_down)


def _combine_kernel(dest_ref, h_ref, gate_ref, *rest, final_norm):
    if final_norm:
        g_ref, y_hbm, o_ref, ybuf, sem = rest
    else:
        y_hbm, o_ref, ybuf, sem = rest
    i = pl.program_id(0)
    tm = h_ref.shape[0]
    rows = tm * TOP_K
    stride = TOP_K * CHUNKS

    def pull(tile, slot):
        def trip(it, c):
            for k in range(MOE_ISSUE_UNROLL):
                r = it * MOE_ISSUE_UNROLL + k
                copy = _row_copy(y_hbm, dest_ref[tile * rows + r], ybuf.at[slot], r, sem.at[slot])
                copy.start(priority=k % 2)
            return c
        lax.fori_loop(0, rows // MOE_ISSUE_UNROLL, trip, 0)

    @pl.when(i == 0)
    def _():
        pull(0, 0)

    @pl.when(i + 1 < pl.num_programs(0))
    def _():
        pull(i + 1, (i + 1) % 2)

    slot = i % 2
    yb = ybuf.at[slot]
    _rows_wait(y_hbm, yb, rows, sem.at[slot])
    gates = [gate_ref[:, k:k + 1] for k in range(TOP_K)]
    parts = []
    ss = jnp.zeros((tm, 1), F32)
    for c in range(CHUNKS):
        y = gates[0] * yb[pl.ds(c, tm, stride=stride), :]
        for k in range(1, TOP_K):
            y = y + gates[k] * yb[pl.ds(k * CHUNKS + c, tm, stride=stride), :]
        hc = h_ref[:, c * LANES:(c + 1) * LANES] + y
        parts.append(hc)
        ss = ss + jnp.sum(hc * hc, axis=1, keepdims=True)
    if final_norm:
        inv = lax.rsqrt(ss / D_MODEL + RMS_EPS)
    for c in range(CHUNKS):
        cols = slice(c * LANES, (c + 1) * LANES)
        o_ref[:, cols] = parts[c] * inv * g_ref[:, cols] if final_norm else parts[c]


def _combine(h2, y8, dest, gate, out_g):
    t = h2.shape[0]
    tm = ROW_TILE
    final_norm = out_g is not None
    row = lambda i, d: (i, 0)
    in_specs = [pl.BlockSpec((tm, D_MODEL), row), pl.BlockSpec((tm, LANES), row)]
    args = [h2, gate]
    if final_norm:
        in_specs.append(pl.BlockSpec((1, D_MODEL), lambda i, d: (0, 0)))
        args.append(out_g.reshape(1, D_MODEL))
    grid_spec = pltpu.PrefetchScalarGridSpec(
        num_scalar_prefetch=1,
        grid=(t // tm,),
        in_specs=in_specs + [pl.BlockSpec(memory_space=pl.ANY)],
        out_specs=pl.BlockSpec((tm, D_MODEL), row),
        scratch_shapes=[pltpu.VMEM((2, tm * TOP_K * CHUNKS, LANES), F32),
                        pltpu.SemaphoreType.DMA((2,))],
    )
    return pl.pallas_call(
        functools.partial(_combine_kernel, final_norm=final_norm),
        grid_spec=grid_spec,
        out_shape=jax.ShapeDtypeStruct((t, D_MODEL), F32),
        compiler_params=_params("arbitrary"),
        name="moe_combine",
    )(dest, *args, y8)


def _norm_kernel(h_ref, g_ref, o_ref):
    x = h_ref[...]
    o_ref[...] = x * lax.rsqrt(jnp.mean(x * x, axis=-1, keepdims=True) + RMS_EPS) * g_ref[...]


def _final_norm(h2, norm_g):
    t = h2.shape[0]
    tm = ROW_TILE
    return pl.pallas_call(
        _norm_kernel,
        grid=(t // tm,),
        in_specs=[pl.BlockSpec((tm, D_MODEL), lambda i: (i, 0)),
                  pl.BlockSpec((1, D_MODEL), lambda i: (0, 0))],
        out_specs=pl.BlockSpec((tm, D_MODEL), lambda i: (i, 0)),
        out_shape=jax.ShapeDtypeStruct((t, D_MODEL), F32),
        compiler_params=_params("parallel"),
        name="final_norm",
    )(h2, norm_g)


def _dispatch_plan(top_i, rank, counts, t):
    n = t * TOP_K
    tm = MOE_BLOCK
    padded = (counts + tm - 1) // tm * tm
    pends = jnp.cumsum(padded)
    pstarts = pends - padded
    onehot = (top_i[:, :, None] == jnp.arange(N_EXPERTS)[None, None, :]).astype(jnp.int32)
    dest = (jnp.sum(onehot * pstarts[None, None, :], axis=2) + rank).astype(jnp.int32).reshape(n)
    p_rows = -(-n // tm) * tm + N_EXPERTS * tm
    nblk = p_rows // tm
    blk = jnp.arange(nblk, dtype=jnp.int32)
    blk_e = jnp.minimum(jnp.sum((blk * tm)[:, None] >= pends[None, :], axis=1), N_EXPERTS - 1)
    live = blk * tm < pends[-1]
    last_live = jnp.maximum(pends[-1] // tm - 1, 0)
    blk_e = jnp.where(live, blk_e, blk_e[last_live])
    blk_x = jnp.where(live, blk, 0)
    tail = jnp.where(padded > 0, pends - tm, -1)
    spare = pends[-1] + jnp.arange(N_EXPERTS) * tm
    fill = jnp.concatenate([tail, jnp.where(spare < p_rows, spare, -1)])
    i32 = lambda a: a.astype(jnp.int32)
    return dest, i32(fill), i32(blk_e), i32(live), i32(blk_x), p_rows


def _rope_tables(seq):
    inv = 1.0 / (ROPE_THETA ** (jnp.arange(0, HEAD_DIM, 2, dtype=F32) / HEAD_DIM))
    ang = jnp.arange(seq, dtype=F32)[:, None] * inv[None, :]
    cos = jnp.concatenate([jnp.cos(ang)] * 4, axis=-1)
    sin = jnp.concatenate([jnp.sin(ang)] * 4, axis=-1)
    upper = (jnp.arange(LANES) % HEAD_DIM) >= HEAD_DIM // 2
    sa = jnp.where(upper[None, :], sin, 0.0)
    sb = jnp.where(upper[None, :], 0.0, -sin)
    return cos, sa, sb


def _mixer(h2, b, s, norm_g, w_in, forget_bias, sinks, mix_gain, w_out, tables,
           fox_side=None, moba_side=None):
    f0 = SWA_WIDTH + 2 * SWA_KV_WIDTH + 3 * FOX_WIDTH
    f1 = f0 + FOX_HEADS
    assert f0 + 3 * MOBA_WIDTH == U_WIDTH
    wt = jnp.swapaxes(w_in, 0, 1)
    w_all = jnp.concatenate(
        [wt[:f0], wt[f1:], jnp.pad(wt[f0:f1], ((0, LANES - FOX_HEADS), (0, 0)))],
        axis=0).astype(BF16)
    fbias = jnp.pad(forget_bias.astype(F32), (0, LANES - FOX_HEADS)).reshape(1, LANES)
    u, lf = _project(h2, norm_g.reshape(1, D_MODEL), w_all, *tables, fbias, s)
    u3 = u.reshape(b, s, U_WIDTH)
    oa = _swa(u3, sinks.astype(F32))
    ob, fox_cast = _fox(u3, lf.reshape(b, s, LANES), fox_side)
    oc, moba_cast = _moba(u3, moba_side)
    t = b * s
    h = _outproj(oa.reshape(t, SWA_WIDTH), ob.reshape(t, FOX_WIDTH), oc.reshape(t, MOBA_WIDTH),
                 mix_gain.astype(F32).reshape(1, MIX_WIDTH), w_out.astype(BF16), h2)
    return h, fox_cast, moba_cast


def _moe_ffn(h2, norm_g, w_router, wg, wu, wd, out_g):
    t = h2.shape[0]
    wr = jnp.pad(w_router.astype(F32), ((0, 0), (0, LANES - N_EXPERTS)))
    wr2 = jnp.stack(_split3(wr)[:2])
    z8, top, gate, rank, counts = _router(h2, norm_g.reshape(1, D_MODEL), wr2)
    dest, fill, blk_e, blk_live, blk_x, p_rows = _dispatch_plan(
        top[:, :TOP_K], rank[:, :TOP_K], counts[0, :N_EXPERTS], t)
    x8 = _dispatch(z8, dest, fill, p_rows)
    y8 = _moe_experts(x8, blk_e, blk_live, blk_x, wg, wu, wd)
    return _combine(h2, y8, dest, gate, out_g)


def kernel(x, attn_norm, w_in, fox_forget_bias, swa_sinks, mix_gain, w_out, ffn_norm,
           dense_w_gate, dense_w_up, dense_w_down, router_w, moe_w_gate, moe_w_up,
           moe_w_down, final_norm):
    b, s, d = x.shape
    depth = attn_norm.shape[0]
    assert d == D_MODEL and s % MOBA_BLOCK == 0 and s % ROW_TILE == 0
    tables = _rope_tables(s)
    h = x.reshape(b * s, d)
    normed = False
    ne, _, ff = moe_w_gate.shape[1:]
    flat = lambda w: w.reshape(w.shape[0] * w.shape[1], w.shape[2])
    wg = wu = None
    for layer in range(depth):
        j = layer // 2
        routed = layer % 2 == 1
        routed_next = layer + 1 < depth and (layer + 1) % 2 == 1
        fox_side = flat(moe_w_down[j]) if routed else (flat(moe_w_gate[(layer + 1) // 2])
                                                      if routed_next else None)
        moba_side = flat(moe_w_up[(layer + 1) // 2]) if routed_next and not routed else None
        h, fox_cast, moba_cast = _mixer(h, b, s, attn_norm[layer], w_in[layer],
                                        fox_forget_bias[layer], swa_sinks[layer], mix_gain[layer],
                                        w_out[layer], tables, fox_side, moba_side)
        last = layer == depth - 1
        if not routed:
            wg, wu = fox_cast, moba_cast
            h = _dense_ffn(h, ffn_norm[layer].reshape(1, d), dense_w_gate[j], dense_w_up[j],
                           dense_w_down[j])
        else:
            h = _moe_ffn(h, ffn_norm[layer], router_w[j], wg.reshape(ne, d, ff),
                         wu.reshape(ne, d, ff), fox_cast.reshape(ne, ff, d),
                         final_norm if last else None)
            normed = last
    if not normed:
        h = _final_norm(h, final_norm.reshape(1, d))
    return h.reshape(b, s, d)
```

```python
import functools

import jax
import jax.numpy as jnp
from jax import lax
from jax.experimental import pallas as pl
from jax.experimental.pallas import tpu as pltpu

F32 = jnp.float32
BF16 = jnp.bfloat16

D_MODEL = 1024
HEAD_DIM = 64
LANES = 128
SWA_Q_HEADS = 8
SWA_KV_HEADS = 2
SWA_WINDOW = 128
FOX_HEADS = 4
MOBA_HEADS = 4
MOBA_BLOCK = 256
MOBA_TOPK = 3
ROPE_THETA = 10000.0
RMS_EPS = 1e-5
D_FF = 3584
N_EXPERTS = 8
TOP_K = 2
MOE_BLOCK = 512
NEG_INF = -1e30
ATTN_SCALE = HEAD_DIM ** -0.5
LOG2E = 1.4426950408889634
Q_SCALE = ATTN_SCALE * LOG2E

SWA_WIDTH = SWA_Q_HEADS * HEAD_DIM
SWA_KV_WIDTH = SWA_KV_HEADS * HEAD_DIM
FOX_WIDTH = FOX_HEADS * HEAD_DIM
MOBA_WIDTH = MOBA_HEADS * HEAD_DIM
MIX_WIDTH = SWA_WIDTH + FOX_WIDTH + MOBA_WIDTH

W_BLOCKS = 18
W_ALL_WIDTH = (W_BLOCKS + 1) * LANES
ROPE_BLOCKS = (0, 1, 2, 3, 4, 12, 13, 14, 15)
Q_BLOCKS = (0, 1, 2, 3, 6, 7, 12, 13)
SWA_KV_BLOCKS = (4, 5)
U_BLOCKS = W_BLOCKS + len(SWA_KV_BLOCKS)
U_WIDTH = U_BLOCKS * LANES

VMEM_LIMIT = 56 * 1024 * 1024

ROW_TILE = 512
ATT_TILE = 256
SWA_STEP_BLOCKS = 4
FF_TILE = 512
MOE_FF_TILE = 1792
CHUNKS = D_MODEL // LANES
MOE_ISSUE_UNROLL = 8
DISPATCH_TILE = 512


def _params(*sem):
    return pltpu.CompilerParams(dimension_semantics=sem, vmem_limit_bytes=VMEM_LIMIT)


def _split3(x):
    hi = x.astype(BF16)
    r1 = x - hi.astype(F32)
    mid = r1.astype(BF16)
    lo = (r1 - mid.astype(F32)).astype(BF16)
    return hi, mid, lo


def _dot_nt(a, b):
    return lax.dot_general(a, b, (((1,), (1,)), ((), ())), preferred_element_type=F32)


def _dot(a, b):
    return jnp.dot(a, b, preferred_element_type=F32)


def _side_cast(w2d, steps, index_map):
    rows, cols = w2d.shape
    assert rows % steps == 0
    spec = pl.BlockSpec((rows // steps, cols), index_map)
    return spec, spec, jax.ShapeDtypeStruct((rows, cols), BF16)


def _with_side_cast(kernel_fn, n_in):
    def body(*refs):
        side_in, out_ref, side_out = refs[n_in:n_in + 3]
        side_out[...] = side_in[...].astype(BF16)
        kernel_fn(*refs[:n_in], out_ref, *refs[n_in + 3:])
    return body


def _proj_kernel(x_ref, g_ref, w_ref, cos_ref, sa_ref, sb_ref, fb_ref, u_ref, lf_ref):
    x = x_ref[...]
    inv = lax.rsqrt(jnp.mean(x * x, axis=-1, keepdims=True) + RMS_EPS)
    h = (x * inv * g_ref[...]).astype(BF16)
    cos = cos_ref[...]
    sa = sa_ref[...]
    sb = sb_ref[...]
    low = lax.broadcasted_iota(jnp.int32, (1, LANES), 1) < HEAD_DIM
    dst = 0
    for c in range(W_BLOCKS // 2):
        acc = _dot_nt(h, w_ref[c * 2 * LANES:(c + 1) * 2 * LANES, :])
        for half in range(2):
            blk = 2 * c + half
            a = acc[:, half * LANES:(half + 1) * LANES]
            if blk in ROPE_BLOCKS:
                a = a * cos + pltpu.roll(a, 32, 1) * sa + pltpu.roll(a, 96, 1) * sb
            if blk in Q_BLOCKS:
                a = a * Q_SCALE
            if blk in SWA_KV_BLOCKS:
                swapped = pltpu.roll(a, HEAD_DIM, 1)
                outs = [jnp.where(low, a, swapped), jnp.where(low, swapped, a)]
            else:
                outs = [a]
            for o in outs:
                u_ref[:, dst * LANES:(dst + 1) * LANES] = o.astype(BF16)
                dst += 1
    assert dst == U_BLOCKS
    f = _dot_nt(h, w_ref[W_BLOCKS * LANES:W_ALL_WIDTH, :]) + fb_ref[...]
    lf_ref[...] = jnp.minimum(f, 0.0) - jnp.log(1.0 + jnp.exp(-jnp.abs(f)))


def _project(x2, norm_g, w_all, cos, sa, sb, fbias, seq):
    t = x2.shape[0]
    tm = ROW_TILE
    nseq = seq // tm
    row = lambda i: (i, 0)
    pos = lambda i: (i % nseq, 0)
    fixed = lambda i: (0, 0)
    return pl.pallas_call(
        _proj_kernel,
        grid=(t // tm,),
        in_specs=[
            pl.BlockSpec((tm, D_MODEL), row),
            pl.BlockSpec((1, D_MODEL), fixed),
            pl.BlockSpec((W_ALL_WIDTH, D_MODEL), fixed),
            pl.BlockSpec((tm, LANES), pos),
            pl.BlockSpec((tm, LANES), pos),
            pl.BlockSpec((tm, LANES), pos),
            pl.BlockSpec((1, LANES), fixed),
        ],
        out_specs=[pl.BlockSpec((tm, U_WIDTH), row), pl.BlockSpec((tm, LANES), row)],
        out_shape=[jax.ShapeDtypeStruct((t, U_WIDTH), BF16),
                   jax.ShapeDtypeStruct((t, LANES), F32)],
        compiler_params=_params("parallel"),
        name="proj",
    )(x2, norm_g, w_all, cos, sa, sb, fbias)


def _swa_kernel(sink_ref, q_ref, kc_ref, kp_ref, vc_ref, vp_ref, o_ref):
    n = pl.program_id(1)
    w = SWA_WINDOW
    group = SWA_Q_HEADS // SWA_KV_HEADS
    lane = lax.broadcasted_iota(jnp.int32, (1, LANES), 1)
    low = lane < HEAD_DIM

    k_all = jnp.concatenate([kp_ref[0], kc_ref[0]], axis=0)
    v_all = jnp.concatenate([vp_ref[0], vc_ref[0]], axis=0)
    qi = lax.broadcasted_iota(jnp.int32, (group * w, 2 * w), 0) % w
    kj = lax.broadcasted_iota(jnp.int32, (group * w, 2 * w), 1)
    window = (kj > qi) & (kj <= qi + w)
    pairs = group // 2
    for a in range(SWA_STEP_BLOCKS):
        rows = slice(a * w, (a + 1) * w)
        valid = (window & ((kj >= w) | (n > 0))) if a == 0 else window
        for g in range(SWA_KV_HEADS):
            k2 = k_all[a * w:(a + 2) * w, g * LANES:(g + 1) * LANES]
            v2 = v_all[a * w:(a + 2) * w, g * LANES:(g + 1) * LANES]
            qs = jnp.concatenate(
                [jnp.where(low if half == 0 else ~low,
                           q_ref[0, rows, (pairs * g + jj) * LANES:(pairs * g + jj + 1) * LANES], 0)
                 for jj in range(pairs) for half in range(2)], axis=0)
            s = jnp.where(valid, _dot_nt(qs, k2), NEG_INF)
            ps, inv = [], []
            for j in range(group):
                sj = s[j * w:(j + 1) * w]
                sink = sink_ref[g * group + j] * LOG2E
                m = jnp.maximum(jnp.max(sj, axis=1, keepdims=True), sink)
                p = jnp.exp2(sj - m)
                inv.append(1.0 / (jnp.sum(p, axis=1, keepdims=True) + jnp.exp2(sink - m)))
                ps.append(p.astype(BF16))
            o = _dot(jnp.concatenate(ps, axis=0), v2)
            outs = [o[j * w:(j + 1) * w] * inv[j] for j in range(group)]
            for jj in range(pairs):
                oj = jnp.where(low, outs[2 * jj], outs[2 * jj + 1])
                cols = slice((pairs * g + jj) * LANES, (pairs * g + jj + 1) * LANES)
                o_ref[0, rows, cols] = oj.astype(BF16)


def _swa(u3, sinks):
    b, s, _ = u3.shape
    w = SWA_WINDOW
    nb = SWA_STEP_BLOCKS
    assert s % (nb * w) == 0
    kv = SWA_KV_HEADS * LANES
    cur = lambda unit: (lambda bi, n, sk: (bi, n, unit))
    prev = lambda unit: (lambda bi, n, sk: (bi, jnp.maximum(n * nb - 1, 0), unit))
    grid_spec = pltpu.PrefetchScalarGridSpec(
        num_scalar_prefetch=1,
        grid=(b, s // (nb * w)),
        in_specs=[
            pl.BlockSpec((1, nb * w, SWA_WIDTH), lambda bi, n, sk: (bi, n, 0)),
            pl.BlockSpec((1, nb * w, kv), cur(2)),
            pl.BlockSpec((1, w, kv), prev(2)),
            pl.BlockSpec((1, nb * w, kv), cur(3)),
            pl.BlockSpec((1, w, kv), prev(3)),
        ],
        out_specs=pl.BlockSpec((1, nb * w, SWA_WIDTH), lambda bi, n, sk: (bi, n, 0)),
    )
    return pl.pallas_call(
        _swa_kernel,
        grid_spec=grid_spec,
        out_shape=jax.ShapeDtypeStruct((b, s, SWA_WIDTH), BF16),
        compiler_params=_params("parallel", "parallel"),
        name="swa",
    )(sinks, u3, u3, u3, u3, u3)


def _head_lanes(hh):
    lane = lax.broadcasted_iota(jnp.int32, (1, LANES), 1)
    if hh == 0:
        return lane < HEAD_DIM, lane - HEAD_DIM
    return lane >= HEAD_DIM, lane


def _causal_attention(qp_scr, kp_scr, v_ref, o_ref, t):
    s_len = qp_scr.shape[1]
    lane = lax.broadcasted_iota(jnp.int32, (1, LANES), 1)
    row = lax.broadcasted_iota(jnp.int32, (t, t), 0)
    col = lax.broadcasted_iota(jnp.int32, (t, t), 1)
    for qi in range(s_len // t):
        rows = slice(qi * t, (qi + 1) * t)
        outs = []
        for hh in range(2):
            q = qp_scr[hh, rows, :]
            sd = jnp.where(col <= row, _dot_nt(q, kp_scr[hh, rows, :]), NEG_INF)
            m = jnp.max(sd, axis=1, keepdims=True)
            if qi:
                sp = _dot_nt(q, kp_scr[hh, :qi * t, :])
                m = jnp.maximum(m, jnp.max(sp, axis=1, keepdims=True))
            pd = jnp.exp2(sd - m)
            l = jnp.sum(pd, axis=1, keepdims=True)
            o = _dot(pd.astype(BF16), v_ref[0, rows, :])
            if qi:
                pp = jnp.exp2(sp - m)
                l = l + jnp.sum(pp, axis=1, keepdims=True)
                o = o + _dot(pp.astype(BF16), v_ref[0, :qi * t, :])
            outs.append(o * (1.0 / l))
        o_ref[0, rows, :] = jnp.where(lane < HEAD_DIM, outs[0], outs[1]).astype(BF16)


def _pair_attention(kernel_fn, u3, extra, q_blk, k_blk, v_blk, n_heads, name, side=None):
    b, s, _ = u3.shape
    pairs = n_heads // 2
    pair = lambda blk: (lambda bi, p: (bi, 0, blk + p))
    seq_block = pl.BlockSpec((1, s, LANES), lambda bi, p: (bi, 0, 0))
    in_specs = [pl.BlockSpec((1, s, LANES), pair(q_blk)),
                pl.BlockSpec((1, s, LANES), pair(k_blk)),
                pl.BlockSpec((1, s, LANES), pair(v_blk))] + [seq_block] * len(extra)
    out_specs = [pl.BlockSpec((1, s, LANES), pair(0))]
    out_shape = [jax.ShapeDtypeStruct((b, s, n_heads * HEAD_DIM), BF16)]
    args = [u3, u3, u3, *extra]
    if side is not None:
        kernel_fn = _with_side_cast(kernel_fn, len(args))
        spec_in, spec_out, shape = _side_cast(side, b * pairs, lambda bi, p: (bi * pairs + p, 0))
        in_specs.append(spec_in)
        out_specs.append(spec_out)
        out_shape.append(shape)
        args.append(side)
    outs = pl.pallas_call(
        kernel_fn,
        grid=(b, pairs),
        in_specs=in_specs,
        out_specs=out_specs,
        out_shape=out_shape,
        scratch_shapes=[pltpu.VMEM((2, s, LANES), BF16), pltpu.VMEM((2, s, LANES), BF16)],
        compiler_params=_params("parallel", "parallel"),
        name=name,
    )(*args)
    return tuple(outs) if side is not None else (outs[0], None)


def _fox_kernel(q_ref, k_ref, v_ref, lf_ref, o_ref, qp_scr, kp_scr):
    p = pl.program_id(1)
    s_len = q_ref.shape[1]
    t = ATT_TILE
    lane = lax.broadcasted_iota(jnp.int32, (1, LANES), 1)
    r = lax.broadcasted_iota(jnp.int32, (t, t), 0)
    cc = lax.broadcasted_iota(jnp.int32, (t, t), 1)
    tri = jnp.where(cc <= r, 1.0, 0.0).astype(BF16)
    carry = jnp.zeros((1, LANES), F32)
    for i in range(s_len // t):
        rows = slice(i * t, (i + 1) * t)
        lf = lf_ref[0, rows, :]
        lf0 = jnp.sum(jnp.where(lane == 2 * p, lf, 0.0), axis=1, keepdims=True)
        lf1 = jnp.sum(jnp.where(lane == 2 * p + 1, lf, 0.0), axis=1, keepdims=True)
        hi, mid, lo = _split3(jnp.where(lane < 3, lf0, jnp.where(lane < 6, lf1, 0.0)))
        terms = jnp.where((lane == 0) | (lane == 3), hi,
                          jnp.where((lane == 1) | (lane == 4), mid, lo))
        sums = _dot(tri, terms) + carry
        carry = sums[t - 1:t, :]
        c0 = jnp.sum(jnp.where(lane < 3, sums, 0.0), axis=1, keepdims=True)
        c1 = jnp.sum(jnp.where((lane >= 3) & (lane < 6), sums, 0.0), axis=1, keepdims=True)
        chi, cmid, clo = _split3(jnp.where(lane < HEAD_DIM, c1, c0) * LOG2E)
        for hh in range(2):
            keep, f = _head_lanes(hh)
            ones_q = jnp.where((f >= 3) & (f < 6), 1.0, 0.0).astype(BF16)
            ones_k = jnp.where((f >= 0) & (f < 3), 1.0, 0.0).astype(BF16)
            qf = jnp.where(f == 0, chi, jnp.where(f == 1, cmid, jnp.where(f == 2, clo, ones_q)))
            kf = jnp.where(f == 3, -chi, jnp.where(f == 4, -cmid, jnp.where(f == 5, -clo, ones_k)))
            qp_scr[hh, rows, :] = jnp.where(keep, q_ref[0, rows, :], qf)
            kp_scr[hh, rows, :] = jnp.where(keep, k_ref[0, rows, :], kf)
    _causal_attention(qp_scr, kp_scr, v_ref, o_ref, t)


def _fox(u3, lf3, side=None):
    return _pair_attention(_fox_kernel, u3, (lf3,), 8, 10, 12, FOX_HEADS, "fox", side)


def _moba_kernel(q_ref, k_ref, v_ref, o_ref, qp_scr, kp_scr):
    s_len = q_ref.shape[1]
    t = MOBA_BLOCK
    nkb = s_len // t
    sub = 8
    assert nkb <= sub
    means = [jnp.mean(k_ref[0, j * t:(j + 1) * t, :].astype(F32), axis=0, keepdims=True)
             for j in range(nkb)]
    km_pair = jnp.concatenate(means + [jnp.zeros((sub - nkb, LANES), F32)], axis=0) \
        if nkb < sub else jnp.concatenate(means, axis=0)
    stack = []
    for hh in range(2):
        keep, _ = _head_lanes(hh)
        stack += [part.astype(F32) for part in _split3(jnp.where(keep, km_pair, 0.0))]
    stack.append(jnp.zeros((LANES - 6 * sub, LANES), F32))
    km_all = jnp.concatenate(stack, axis=0).astype(BF16)
    blk = lax.broadcasted_iota(jnp.int32, (sub, t), 0)
    gates = [_dot_nt(km_all, q_ref[0, i * t:(i + 1) * t, :]) for i in range(nkb)]
    for hh in range(2):
        keep, f = _head_lanes(hh)
        for i in range(nkb):
            rows = slice(i * t, (i + 1) * t)
            q = q_ref[0, rows, :]
            base = 3 * sub * hh
            g = (gates[i][base:base + sub] + gates[i][base + sub:base + 2 * sub]) \
                + gates[i][base + 2 * sub:base + 3 * sub]
            past = blk < i
            g = jnp.where(past, g, NEG_INF)
            rank = jnp.zeros((sub, t), jnp.int32)
            for r in range(i):
                gr = jnp.broadcast_to(g[r:r + 1, :], (sub, t))
                beats = (gr > g) | ((gr == g) & (r < blk))
                rank = rank + beats.astype(jnp.int32)
            drop = jnp.where(past & (rank >= MOBA_TOPK), 1.0, 0.0)
            below = [jnp.zeros((HEAD_DIM, t), F32)] if hh == 0 else []
            above = jnp.zeros((LANES - sub - (HEAD_DIM if hh == 0 else 0), t), F32)
            padded = jnp.concatenate(below + [drop, above], axis=0)
            kf = jnp.where(f == i, NEG_INF, 0.0).astype(BF16)
            qp_scr[hh, rows, :] = jnp.where(keep, q, padded.T.astype(BF16))
            kp_scr[hh, rows, :] = jnp.where(keep, k_ref[0, rows, :], kf)
    _causal_attention(qp_scr, kp_scr, v_ref, o_ref, t)


def _moba(u3, side=None):
    return _pair_attention(_moba_kernel, u3, (), 14, 16, 18, MOBA_HEADS, "moba", side)


def _rms_f32(x):
    x = x.astype(F32)
    return x * lax.rsqrt(jnp.mean(x * x, axis=-1, keepdims=True) + RMS_EPS)


def _outproj_kernel(oa_ref, ob_ref, oc_ref, gain_ref, w_ref, x_ref, h_ref):
    a0, a1, a2 = 0, SWA_WIDTH, SWA_WIDTH + FOX_WIDTH
    ya = (_rms_f32(oa_ref[...]) * gain_ref[:, a0:a1]).astype(BF16)
    yb = (_rms_f32(ob_ref[...]) * gain_ref[:, a1:a2]).astype(BF16)
    yc = (_rms_f32(oc_ref[...]) * gain_ref[:, a2:]).astype(BF16)
    y = (_dot(ya, w_ref[a0:a1, :]) + _dot(yb, w_ref[a1:a2, :])) + _dot(yc, w_ref[a2:, :])
    h_ref[...] = x_ref[...] + y


def _outproj(oa, ob, oc, gain, w_out, x2):
    t = x2.shape[0]
    tm = ROW_TILE
    row = lambda i: (i, 0)
    fixed = lambda i: (0, 0)
    return pl.pallas_call(
        _outproj_kernel,
        grid=(t // tm,),
        in_specs=[
            pl.BlockSpec((tm, SWA_WIDTH), row),
            pl.BlockSpec((tm, FOX_WIDTH), row),
            pl.BlockSpec((tm, MOBA_WIDTH), row),
            pl.BlockSpec((1, MIX_WIDTH), fixed),
            pl.BlockSpec((MIX_WIDTH, D_MODEL), fixed),
            pl.BlockSpec((tm, D_MODEL), row),
        ],
        out_specs=pl.BlockSpec((tm, D_MODEL), row),
        out_shape=jax.ShapeDtypeStruct((t, D_MODEL), F32),
        compiler_params=_params("parallel"),
        name="outproj",
    )(oa, ob, oc, gain, w_out, x2)


def _silu(x):
    return x / (1.0 + jnp.exp(-x))


def _ffn_kernel(h_ref, g_ref, wg_ref, wu_ref, wd_ref, o_ref, acc_scr):
    j = pl.program_id(1)

    @pl.when((pl.program_id(0) == 0) & (j == 0))
    def _():
        acc_scr[...] = jnp.zeros_like(acc_scr)

    x = h_ref[...]
    inv = lax.rsqrt(jnp.mean(x * x, axis=-1, keepdims=True) + RMS_EPS)
    z = (x * inv * g_ref[...]).astype(BF16)
    a = (_silu(_dot(z, wg_ref[...].astype(BF16))) * _dot(z, wu_ref[...].astype(BF16))).astype(BF16)
    total = acc_scr[...] + _dot(a, wd_ref[...].astype(BF16))
    o_ref[...] = x + total
    acc_scr[...] = jnp.where(j == pl.num_programs(1) - 1, 0.0, total)


def _dense_ffn(h2, norm_g, w_gate, w_up, w_down):
    t = h2.shape[0]
    tm, tf = 2 * ROW_TILE, FF_TILE
    return pl.pallas_call(
        _ffn_kernel,
        grid=(t // tm, D_FF // tf),
        in_specs=[
            pl.BlockSpec((tm, D_MODEL), lambda i, j: (i, 0)),
            pl.BlockSpec((1, D_MODEL), lambda i, j: (0, 0)),
            pl.BlockSpec((D_MODEL, tf), lambda i, j: (0, j)),
            pl.BlockSpec((D_MODEL, tf), lambda i, j: (0, j)),
            pl.BlockSpec((tf, D_MODEL), lambda i, j: (j, 0)),
        ],
        out_specs=pl.BlockSpec((tm, D_MODEL), lambda i, j: (i, 0)),
        out_shape=jax.ShapeDtypeStruct((t, D_MODEL), F32),
        scratch_shapes=[pltpu.VMEM((tm, D_MODEL), F32)],
        compiler_params=_params("arbitrary", "arbitrary"),
        name="dense_ffn",
    )(h2, norm_g, w_gate, w_up, w_down)


def _router_kernel(h_ref, g_ref, wr_ref, z_ref, idx_ref, gate_ref, rank_ref, count_ref, count_scr):
    tm = h_ref.shape[0]
    x = h_ref[...]
    inv = lax.rsqrt(jnp.mean(x * x, axis=-1, keepdims=True) + RMS_EPS)
    z = x * inv * g_ref[...]
    for c in range(CHUNKS):
        z_ref[pl.ds(c, tm, stride=CHUNKS), :] = z[:, c * LANES:(c + 1) * LANES]
    zh, zm, _ = _split3(z)
    wh, wm = wr_ref[0], wr_ref[1]
    logits = _dot(zh, wh) + (_dot(zh, wm) + _dot(zm, wh))
    lane = lax.broadcasted_iota(jnp.int32, (tm, LANES), 1)
    logits = jnp.where(lane < N_EXPERTS, logits, -jnp.inf)
    m1 = jnp.max(logits, axis=1, keepdims=True)
    i1 = jnp.min(jnp.where(logits == m1, lane, LANES), axis=1, keepdims=True)
    rest = jnp.where(lane == i1, -jnp.inf, logits)
    m2 = jnp.max(rest, axis=1, keepdims=True)
    i2 = jnp.min(jnp.where(rest == m2, lane, LANES), axis=1, keepdims=True)
    e2 = jnp.exp(m2 - m1)
    g1 = 1.0 / (1.0 + e2)
    idx_ref[...] = jnp.where(lane == 0, i1, i2)
    gate_ref[...] = jnp.where(lane == 0, g1, e2 * g1)

    @pl.when(pl.program_id(0) == 0)
    def _():
        count_scr[...] = jnp.zeros_like(count_scr)

    chosen = ((lane == i1) | (lane == i2)).astype(F32)
    r = lax.broadcasted_iota(jnp.int32, (tm, tm), 0)
    c = lax.broadcasted_iota(jnp.int32, (tm, tm), 1)
    before = jnp.where(c < r, 1.0, 0.0).astype(BF16)
    earlier = _dot(before, chosen.astype(BF16)) + count_scr[...]
    r1 = jnp.sum(jnp.where(lane == i1, earlier, 0.0), axis=1, keepdims=True)
    r2 = jnp.sum(jnp.where(lane == i2, earlier, 0.0), axis=1, keepdims=True)
    rank_ref[...] = jnp.where(lane == 0, r1, r2).astype(jnp.int32)
    count_scr[...] += jnp.sum(chosen, axis=0, keepdims=True)
    count_ref[...] = count_scr[...].astype(jnp.int32)


def _router(h2, norm_g, wr3):
    t = h2.shape[0]
    tm = ROW_TILE
    row = lambda i: (i, 0)
    return pl.pallas_call(
        _router_kernel,
        grid=(t // tm,),
        in_specs=[
            pl.BlockSpec((tm, D_MODEL), row),
            pl.BlockSpec((1, D_MODEL), lambda i: (0, 0)),
            pl.BlockSpec((2, D_MODEL, LANES), lambda i: (0, 0, 0)),
        ],
        out_specs=[pl.BlockSpec((tm * CHUNKS, LANES), row),
                   pl.BlockSpec((tm, LANES), row),
                   pl.BlockSpec((tm, LANES), row),
                   pl.BlockSpec((tm, LANES), row),
                   pl.BlockSpec((1, LANES), lambda i: (0, 0))],
        out_shape=[jax.ShapeDtypeStruct((t * CHUNKS, LANES), F32),
                   jax.ShapeDtypeStruct((t, LANES), jnp.int32),
                   jax.ShapeDtypeStruct((t, LANES), F32),
                   jax.ShapeDtypeStruct((t, LANES), jnp.int32),
                   jax.ShapeDtypeStruct((1, LANES), jnp.int32)],
        scratch_shapes=[pltpu.VMEM((1, LANES), F32)],
        compiler_params=_params("arbitrary"),
        name="router",
    )(h2, norm_g, wr3)


def _row_copy(src, src_row, dst, dst_row, sem):
    return pltpu.make_async_copy(src.at[pl.ds(src_row * CHUNKS, CHUNKS), :],
                                 dst.at[pl.ds(dst_row * CHUNKS, CHUNKS), :], sem)


def _rows_wait(src, dst, rows, sem):
    pltpu.make_async_copy(src.at[pl.ds(0, rows * CHUNKS), :],
                          dst.at[pl.ds(0, rows * CHUNKS), :], sem).wait()


def _dispatch_kernel(dest_ref, fill_ref, z_ref, x_hbm, zero_scr, fsem, sem):
    i = pl.program_id(0)
    blk = MOE_BLOCK
    tile = z_ref.shape[0] // CHUNKS

    @pl.when(i == 0)
    def _():
        zero_scr[...] = jnp.zeros_like(zero_scr)
        for e in range(fill_ref.shape[0]):
            @pl.when(fill_ref[e] >= 0)
            def _():
                start = pl.multiple_of(fill_ref[e] * CHUNKS, blk * CHUNKS)
                pltpu.make_async_copy(zero_scr, x_hbm.at[pl.ds(start, blk * CHUNKS), :], fsem).start()
        for e in range(fill_ref.shape[0]):
            @pl.when(fill_ref[e] >= 0)
            def _():
                _rows_wait(zero_scr, x_hbm, blk, fsem)

    base = i * tile * TOP_K

    tokens_per_trip = MOE_ISSUE_UNROLL // TOP_K

    def trip(it, c):
        for k in range(MOE_ISSUE_UNROLL):
            token = it * tokens_per_trip + k // TOP_K
            slot = dest_ref[base + it * MOE_ISSUE_UNROLL + k]
            _row_copy(z_ref, token, x_hbm, slot, sem).start(priority=k % 2)
        return c
    lax.fori_loop(0, tile // tokens_per_trip, trip, 0)
    for _ in range(TOP_K):
        _rows_wait(z_ref, x_hbm, tile, sem)


def _dispatch(z8, dest, fill, p_rows):
    tile = DISPATCH_TILE
    assert z8.shape[0] % (tile * CHUNKS) == 0
    grid_spec = pltpu.PrefetchScalarGridSpec(
        num_scalar_prefetch=2,
        grid=(z8.shape[0] // (tile * CHUNKS),),
        in_specs=[pl.BlockSpec((tile * CHUNKS, LANES), lambda i, dd, f: (i, 0))],
        out_specs=pl.BlockSpec(memory_space=pl.ANY),
        scratch_shapes=[pltpu.VMEM((MOE_BLOCK * CHUNKS, LANES), F32),
                        pltpu.SemaphoreType.DMA(()), pltpu.SemaphoreType.DMA(())],
    )
    return pl.pallas_call(
        _dispatch_kernel,
        grid_spec=grid_spec,
        out_shape=jax.ShapeDtypeStruct((p_rows * CHUNKS, LANES), F32),
        compiler_params=_params("arbitrary"),
        name="moe_dispatch",
    )(dest, fill, z8)


def _moe_kernel(be_ref, live_ref, bx_ref, x_ref, wg_ref, wu_ref, wd_ref, y_ref, acc):
    i = pl.program_id(0)
    j = pl.program_id(1)
    tm = acc.shape[0]

    @pl.when((i == 0) & (j == 0))
    def _():
        acc[...] = jnp.zeros_like(acc)

    @pl.when((live_ref[i] == 0) & (j == 0))
    def _():
        y_ref[...] = jnp.zeros_like(y_ref)

    @pl.when(live_ref[i] > 0)
    def _():
        x = jnp.concatenate([x_ref[pl.ds(c, tm, stride=CHUNKS), :].astype(BF16)
                             for c in range(CHUNKS)], axis=1)
        a = (_silu(_dot(x, wg_ref[0])) * _dot(x, wu_ref[0])).astype(BF16)
        total = acc[...] + _dot(a, wd_ref[0])
        for c in range(CHUNKS):
            y_ref[pl.ds(c, tm, stride=CHUNKS), :] = total[:, c * LANES:(c + 1) * LANES]
        acc[...] = jnp.where(j == pl.num_programs(1) - 1, 0.0, total)


def _moe_experts(x8, blk_e, blk_live, blk_x, w_gate, w_up, w_down):
    tm, tf = MOE_BLOCK, MOE_FF_TILE
    nblk = blk_e.shape[0]
    nff = D_FF // tf
    ff = lambda i, j, live: jnp.where(live[i] > 0, j, nff - 1)
    grid_spec = pltpu.PrefetchScalarGridSpec(
        num_scalar_prefetch=3,
        grid=(nblk, nff),
        in_specs=[
            pl.BlockSpec((tm * CHUNKS, LANES), lambda i, j, be, lv, bx: (bx[i], 0)),
            pl.BlockSpec((1, D_MODEL, tf), lambda i, j, be, lv, bx: (be[i], 0, ff(i, j, lv))),
            pl.BlockSpec((1, D_MODEL, tf), lambda i, j, be, lv, bx: (be[i], 0, ff(i, j, lv))),
            pl.BlockSpec((1, tf, D_MODEL), lambda i, j, be, lv, bx: (be[i], ff(i, j, lv), 0)),
        ],
        out_specs=pl.BlockSpec((tm * CHUNKS, LANES), lambda i, j, be, lv, bx: (i, 0)),
        scratch_shapes=[pltpu.VMEM((tm, D_MODEL), F32)],
    )
    return pl.pallas_call(
        _moe_kernel,
        grid_spec=grid_spec,
        out_shape=jax.ShapeDtypeStruct(x8.shape, F32),
        compiler_params=_params("arbitrary", "arbitrary"),
        name="moe_experts",
    )(blk_e, blk_live, blk_x, x8, w_gate, w_up, w_down)


def _combine_kernel(dest_ref, h_ref, gate_ref, *rest, final_norm):
    if final_norm:
        g_ref, y_hbm, o_ref, ybuf, sem = rest
    else:
        y_hbm, o_ref, ybuf, sem = rest
    i = pl.program_id(0)
    tm = h_ref.shape[0]
    rows = tm * TOP_K
    stride = TOP_K * CHUNKS

    def pull(tile, slot):
        def trip(it, c):
            for k in range(MOE_ISSUE_UNROLL):
                r = it * MOE_ISSUE_UNROLL + k
                copy = _row_copy(y_hbm, dest_ref[tile * rows + r], ybuf.at[slot], r, sem.at[slot])
                copy.start(priority=k % 2)
            return c
        lax.fori_loop(0, rows // MOE_ISSUE_UNROLL, trip, 0)

    @pl.when(i == 0)
    def _():
        pull(0, 0)

    @pl.when(i + 1 < pl.num_programs(0))
    def _():
        pull(i + 1, (i + 1) % 2)

    slot = i % 2
    yb = ybuf.at[slot]
    _rows_wait(y_hbm, yb, rows, sem.at[slot])
    gates = [gate_ref[:, k:k + 1] for k in range(TOP_K)]
    parts = []
    ss = jnp.zeros((tm, 1), F32)
    for c in range(CHUNKS):
        y = gates[0] * yb[pl.ds(c, tm, stride=stride), :]
        for k in range(1, TOP_K):
            y = y + gates[k] * yb[pl.ds(k * CHUNKS + c, tm, stride=stride), :]
        hc = h_ref[:, c * LANES:(c + 1) * LANES] + y
        parts.append(hc)
        ss = ss + jnp.sum(hc * hc, axis=1, keepdims=True)
    if final_norm:
        inv = lax.rsqrt(ss / D_MODEL + RMS_EPS)
    for c in range(CHUNKS):
        cols = slice(c * LANES, (c + 1) * LANES)
        o_ref[:, cols] = parts[c] * inv * g_ref[:, cols] if final_norm else parts[c]


def _combine(h2, y8, dest, gate, out_g):
    t = h2.shape[0]
    tm = ROW_TILE
    final_norm = out_g is not None
    row = lambda i, d: (i, 0)
    in_specs = [pl.BlockSpec((tm, D_MODEL), row), pl.BlockSpec((tm, LANES), row)]
    args = [h2, gate]
    if final_norm:
        in_specs.append(pl.BlockSpec((1, D_MODEL), lambda i, d: (0, 0)))
        args.append(out_g.reshape(1, D_MODEL))
    grid_spec = pltpu.PrefetchScalarGridSpec(
        num_scalar_prefetch=1,
        grid=(t // tm,),
        in_specs=in_specs + [pl.BlockSpec(memory_space=pl.ANY)],
        out_specs=pl.BlockSpec((tm, D_MODEL), row),
        scratch_shapes=[pltpu.VMEM((2, tm * TOP_K * CHUNKS, LANES), F32),
                        pltpu.SemaphoreType.DMA((2,))],
    )
    return pl.pallas_call(
        functools.partial(_combine_kernel, final_norm=final_norm),
        grid_spec=grid_spec,
        out_shape=jax.ShapeDtypeStruct((t, D_MODEL), F32),
        compiler_params=_params("arbitrary"),
        name="moe_combine",
    )(dest, *args, y8)


def _norm_kernel(h_ref, g_ref, o_ref):
    x = h_ref[...]
    o_ref[...] = x * lax.rsqrt(jnp.mean(x * x, axis=-1, keepdims=True) + RMS_EPS) * g_ref[...]


def _final_norm(h2, norm_g):
    t = h2.shape[0]
    tm = ROW_TILE
    return pl.pallas_call(
        _norm_kernel,
        grid=(t // tm,),
        in_specs=[pl.BlockSpec((tm, D_MODEL), lambda i: (i, 0)),
                  pl.BlockSpec((1, D_MODEL), lambda i: (0, 0))],
        out_specs=pl.BlockSpec((tm, D_MODEL), lambda i: (i, 0)),
        out_shape=jax.ShapeDtypeStruct((t, D_MODEL), F32),
        compiler_params=_params("parallel"),
        name="final_norm",
    )(h2, norm_g)


def _dispatch_plan(top_i, rank, counts, t):
    n = t * TOP_K
    tm = MOE_BLOCK
    padded = (counts + tm - 1) // tm * tm
    pends = jnp.cumsum(padded)
    pstarts = pends - padded
    onehot = (top_i[:, :, None] == jnp.arange(N_EXPERTS)[None, None, :]).astype(jnp.int32)
    dest = (jnp.sum(onehot * pstarts[None, None, :], axis=2) + rank).astype(jnp.int32).reshape(n)
    p_rows = -(-n // tm) * tm + N_EXPERTS * tm
    nblk = p_rows // tm
    blk = jnp.arange(nblk, dtype=jnp.int32)
    blk_e = jnp.minimum(jnp.sum((blk * tm)[:, None] >= pends[None, :], axis=1), N_EXPERTS - 1)
    live = blk * tm < pends[-1]
    last_live = jnp.maximum(pends[-1] // tm - 1, 0)
    blk_e = jnp.where(live, blk_e, blk_e[last_live])
    blk_x = jnp.where(live, blk, 0)
    tail = jnp.where(padded > 0, pends - tm, -1)
    spare = pends[-1] + jnp.arange(N_EXPERTS) * tm
    fill = jnp.concatenate([tail, jnp.where(spare < p_rows, spare, -1)])
    i32 = lambda a: a.astype(jnp.int32)
    return dest, i32(fill), i32(blk_e), i32(live), i32(blk_x), p_rows


def _rope_tables(seq):
    inv = 1.0 / (ROPE_THETA ** (jnp.arange(0, HEAD_DIM, 2, dtype=F32) / HEAD_DIM))
    ang = jnp.arange(seq, dtype=F32)[:, None] * inv[None, :]
    cos = jnp.concatenate([jnp.cos(ang)] * 4, axis=-1)
    sin = jnp.concatenate([jnp.sin(ang)] * 4, axis=-1)
    upper = (jnp.arange(LANES) % HEAD_DIM) >= HEAD_DIM // 2
    sa = jnp.where(upper[None, :], sin, 0.0)
    sb = jnp.where(upper[None, :], 0.0, -sin)
    return cos, sa, sb


def _mixer(h2, b, s, norm_g, w_in, forget_bias, sinks, mix_gain, w_out, tables,
           fox_side=None, moba_side=None):
    f0 = SWA_WIDTH + 2 * SWA_KV_WIDTH + 3 * FOX_WIDTH
    f1 = f0 + FOX_HEADS
    assert f0 + 3 * MOBA_WIDTH == W_BLOCKS * LANES
    wt = jnp.swapaxes(w_in, 0, 1)
    w_all = jnp.concatenate(
        [wt[:f0], wt[f1:], jnp.pad(wt[f0:f1], ((0, LANES - FOX_HEADS), (0, 0)))],
        axis=0).astype(BF16)
    fbias = jnp.pad(forget_bias.astype(F32), (0, LANES - FOX_HEADS)).reshape(1, LANES)
    u, lf = _project(h2, norm_g.reshape(1, D_MODEL), w_all, *tables, fbias, s)
    u3 = u.reshape(b, s, U_WIDTH)
    oa = _swa(u3, sinks.astype(F32))
    ob, fox_cast = _fox(u3, lf.reshape(b, s, LANES), fox_side)
    oc, moba_cast = _moba(u3, moba_side)
    t = b * s
    h = _outproj(oa.reshape(t, SWA_WIDTH), ob.reshape(t, FOX_WIDTH), oc.reshape(t, MOBA_WIDTH),
                 mix_gain.astype(F32).reshape(1, MIX_WIDTH), w_out.astype(BF16), h2)
    return h, fox_cast, moba_cast


def _moe_ffn(h2, norm_g, w_router, wg, wu, wd, out_g):
    t = h2.shape[0]
    wr = jnp.pad(w_router.astype(F32), ((0, 0), (0, LANES - N_EXPERTS)))
    wr2 = jnp.stack(_split3(wr)[:2])
    z8, top, gate, rank, counts = _router(h2, norm_g.reshape(1, D_MODEL), wr2)
    dest, fill, blk_e, blk_live, blk_x, p_rows = _dispatch_plan(
        top[:, :TOP_K], rank[:, :TOP_K], counts[0, :N_EXPERTS], t)
    x8 = _dispatch(z8, dest, fill, p_rows)
    y8 = _moe_experts(x8, blk_e, blk_live, blk_x, wg, wu, wd)
    return _combine(h2, y8, dest, gate, out_g)


def kernel(x, attn_norm, w_in, fox_forget_bias, swa_sinks, mix_gain, w_out, ffn_norm,
           dense_w_gate, dense_w_up, dense_w_down, router_w, moe_w_gate, moe_w_up,
           moe_w_down, final_norm):
    b, s, d = x.shape
    depth = attn_norm.shape[0]
    assert d == D_MODEL and s % MOBA_BLOCK == 0 and s % ROW_TILE == 0
    tables = _rope_tables(s)
    h = x.reshape(b * s, d)
    normed = False
    ne, _, ff = moe_w_gate.shape[1:]
    flat = lambda w: w.reshape(w.shape[0] * w.shape[1], w.shape[2])
    wg = wu = None
    for layer in range(depth):
        j = layer // 2
        routed = layer % 2 == 1
        routed_next = layer + 1 < depth and (layer + 1) % 2 == 1
        fox_side = flat(moe_w_down[j]) if routed else (flat(moe_w_gate[(layer + 1) // 2])
                                                      if routed_next else None)
        moba_side = flat(moe_w_up[(layer + 1) // 2]) if routed_next and not routed else None
        h, fox_cast, moba_cast = _mixer(h, b, s, attn_norm[layer], w_in[layer],
                                        fox_forget_bias[layer], swa_sinks[layer], mix_gain[layer],
                                        w_out[layer], tables, fox_side, moba_side)
        last = layer == depth - 1
        if not routed:
            wg, wu = fox_cast, moba_cast
            h = _dense_ffn(h, ffn_norm[layer].reshape(1, d), dense_w_gate[j], dense_w_up[j],
                           dense_w_down[j])
        else:
            h = _moe_ffn(h, ffn_norm[layer], router_w[j], wg.reshape(ne, d, ff),
                         wu.reshape(ne, d, ff), fox_cast.reshape(ne, ff, d),
                         final_norm if last else None)
            normed = last
    if not normed:
        h = _final_norm(h, final_norm.reshape(1, d))
    return h.reshape(b, s, d)
```

```python
import functools

import jax
import jax.numpy as jnp
from jax import lax
from jax.experimental import pallas as pl
from jax.experimental.pallas import tpu as pltpu

F32 = jnp.float32
BF16 = jnp.bfloat16

D_MODEL = 1024
HEAD_DIM = 64
LANES = 128
SWA_Q_HEADS = 8
SWA_KV_HEADS = 2
SWA_WINDOW = 128
FOX_HEADS = 4
MOBA_HEADS = 4
MOBA_BLOCK = 256
MOBA_TOPK = 3
ROPE_THETA = 10000.0
RMS_EPS = 1e-5
D_FF = 3584
N_EXPERTS = 8
TOP_K = 2
MOE_BLOCK = 512
NEG_INF = -1e30
ATTN_SCALE = HEAD_DIM ** -0.5
LOG2E = 1.4426950408889634
Q_SCALE = ATTN_SCALE * LOG2E

SWA_WIDTH = SWA_Q_HEADS * HEAD_DIM
SWA_KV_WIDTH = SWA_KV_HEADS * HEAD_DIM
FOX_WIDTH = FOX_HEADS * HEAD_DIM
MOBA_WIDTH = MOBA_HEADS * HEAD_DIM
MIX_WIDTH = SWA_WIDTH + FOX_WIDTH + MOBA_WIDTH

W_BLOCKS = 18
W_ALL_WIDTH = (W_BLOCKS + 1) * LANES
ROPE_BLOCKS = (0, 1, 2, 3, 4, 12, 13, 14, 15)
Q_BLOCKS = (0, 1, 2, 3, 6, 7, 12, 13)
SWA_KV_BLOCKS = (4, 5)
U_BLOCKS = W_BLOCKS + len(SWA_KV_BLOCKS)
U_WIDTH = U_BLOCKS * LANES

VMEM_LIMIT = 56 * 1024 * 1024

ROW_TILE = 512
ATT_TILE = 256
SWA_STEP_BLOCKS = 4
FF_TILE = 512
MOE_FF_TILE = 1792
CHUNKS = D_MODEL // LANES
MOE_ISSUE_UNROLL = 8
DISPATCH_TILE = 2048


def _params(*sem):
    return pltpu.CompilerParams(dimension_semantics=sem, vmem_limit_bytes=VMEM_LIMIT)


def _split3(x):
    hi = x.astype(BF16)
    r1 = x - hi.astype(F32)
    mid = r1.astype(BF16)
    lo = (r1 - mid.astype(F32)).astype(BF16)
    return hi, mid, lo


def _dot_nt(a, b):
    return lax.dot_general(a, b, (((1,), (1,)), ((), ())), preferred_element_type=F32)


def _dot(a, b):
    return jnp.dot(a, b, preferred_element_type=F32)


def _side_cast(w2d, steps, index_map):
    rows, cols = w2d.shape
    assert rows % steps == 0
    spec = pl.BlockSpec((rows // steps, cols), index_map)
    return spec, spec, jax.ShapeDtypeStruct((rows, cols), BF16)


def _with_side_cast(kernel_fn, n_in):
    def body(*refs):
        side_in, out_ref, side_out = refs[n_in:n_in + 3]
        side_out[...] = side_in[...].astype(BF16)
        kernel_fn(*refs[:n_in], out_ref, *refs[n_in + 3:])
    return body


def _proj_kernel(x_ref, g_ref, w_ref, cos_ref, sa_ref, sb_ref, fb_ref, u_ref, lf_ref):
    x = x_ref[...]
    inv = lax.rsqrt(jnp.mean(x * x, axis=-1, keepdims=True) + RMS_EPS)
    h = (x * inv * g_ref[...]).astype(BF16)
    cos = cos_ref[...]
    sa = sa_ref[...]
    sb = sb_ref[...]
    low = lax.broadcasted_iota(jnp.int32, (1, LANES), 1) < HEAD_DIM
    dst = 0
    for c in range(W_BLOCKS // 2):
        acc = _dot_nt(h, w_ref[c * 2 * LANES:(c + 1) * 2 * LANES, :])
        for half in range(2):
            blk = 2 * c + half
            a = acc[:, half * LANES:(half + 1) * LANES]
            if blk in ROPE_BLOCKS:
                a = a * cos + pltpu.roll(a, 32, 1) * sa + pltpu.roll(a, 96, 1) * sb
            if blk in Q_BLOCKS:
                a = a * Q_SCALE
            if blk in SWA_KV_BLOCKS:
                swapped = pltpu.roll(a, HEAD_DIM, 1)
                outs = [jnp.where(low, a, swapped), jnp.where(low, swapped, a)]
            else:
                outs = [a]
            for o in outs:
                u_ref[:, dst * LANES:(dst + 1) * LANES] = o.astype(BF16)
                dst += 1
    assert dst == U_BLOCKS
    f = _dot_nt(h, w_ref[W_BLOCKS * LANES:W_ALL_WIDTH, :]) + fb_ref[...]
    lf_ref[...] = jnp.minimum(f, 0.0) - jnp.log(1.0 + jnp.exp(-jnp.abs(f)))


def _project(x2, norm_g, w_all, cos, sa, sb, fbias, seq):
    t = x2.shape[0]
    tm = ROW_TILE
    nseq = seq // tm
    row = lambda i: (i, 0)
    pos = lambda i: (i % nseq, 0)
    fixed = lambda i: (0, 0)
    return pl.pallas_call(
        _proj_kernel,
        grid=(t // tm,),
        in_specs=[
            pl.BlockSpec((tm, D_MODEL), row),
            pl.BlockSpec((1, D_MODEL), fixed),
            pl.BlockSpec((W_ALL_WIDTH, D_MODEL), fixed),
            pl.BlockSpec((tm, LANES), pos),
            pl.BlockSpec((tm, LANES), pos),
            pl.BlockSpec((tm, LANES), pos),
            pl.BlockSpec((1, LANES), fixed),
        ],
        out_specs=[pl.BlockSpec((tm, U_WIDTH), row), pl.BlockSpec((tm, LANES), row)],
        out_shape=[jax.ShapeDtypeStruct((t, U_WIDTH), BF16),
                   jax.ShapeDtypeStruct((t, LANES), F32)],
        compiler_params=_params("parallel"),
        name="proj",
    )(x2, norm_g, w_all, cos, sa, sb, fbias)


def _swa_kernel(sink_ref, q_ref, kc_ref, kp_ref, vc_ref, vp_ref, o_ref):
    n = pl.program_id(1)
    w = SWA_WINDOW
    group = SWA_Q_HEADS // SWA_KV_HEADS
    lane = lax.broadcasted_iota(jnp.int32, (1, LANES), 1)
    low = lane < HEAD_DIM

    k_all = jnp.concatenate([kp_ref[0], kc_ref[0]], axis=0)
    v_all = jnp.concatenate([vp_ref[0], vc_ref[0]], axis=0)
    qi = lax.broadcasted_iota(jnp.int32, (group * w, 2 * w), 0) % w
    kj = lax.broadcasted_iota(jnp.int32, (group * w, 2 * w), 1)
    window = (kj > qi) & (kj <= qi + w)
    pairs = group // 2
    for a in range(SWA_STEP_BLOCKS):
        rows = slice(a * w, (a + 1) * w)
        valid = (window & ((kj >= w) | (n > 0))) if a == 0 else window
        for g in range(SWA_KV_HEADS):
            k2 = k_all[a * w:(a + 2) * w, g * LANES:(g + 1) * LANES]
            v2 = v_all[a * w:(a + 2) * w, g * LANES:(g + 1) * LANES]
            qs = jnp.concatenate(
                [jnp.where(low if half == 0 else ~low,
                           q_ref[0, rows, (pairs * g + jj) * LANES:(pairs * g + jj + 1) * LANES], 0)
                 for jj in range(pairs) for half in range(2)], axis=0)
            s = jnp.where(valid, _dot_nt(qs, k2), NEG_INF)
            ps, inv = [], []
            for j in range(group):
                sj = s[j * w:(j + 1) * w]
                sink = sink_ref[g * group + j] * LOG2E
                m = jnp.maximum(jnp.max(sj, axis=1, keepdims=True), sink)
                p = jnp.exp2(sj - m)
                inv.append(1.0 / (jnp.sum(p, axis=1, keepdims=True) + jnp.exp2(sink - m)))
                ps.append(p.astype(BF16))
            o = _dot(jnp.concatenate(ps, axis=0), v2)
            outs = [o[j * w:(j + 1) * w] * inv[j] for j in range(group)]
            for jj in range(pairs):
                oj = jnp.where(low, outs[2 * jj], outs[2 * jj + 1])
                cols = slice((pairs * g + jj) * LANES, (pairs * g + jj + 1) * LANES)
                o_ref[0, rows, cols] = oj.astype(BF16)


def _swa(u3, sinks):
    b, s, _ = u3.shape
    w = SWA_WINDOW
    nb = SWA_STEP_BLOCKS
    assert s % (nb * w) == 0
    kv = SWA_KV_HEADS * LANES
    cur = lambda unit: (lambda bi, n, sk: (bi, n, unit))
    prev = lambda unit: (lambda bi, n, sk: (bi, jnp.maximum(n * nb - 1, 0), unit))
    grid_spec = pltpu.PrefetchScalarGridSpec(
        num_scalar_prefetch=1,
        grid=(b, s // (nb * w)),
        in_specs=[
            pl.BlockSpec((1, nb * w, SWA_WIDTH), lambda bi, n, sk: (bi, n, 0)),
            pl.BlockSpec((1, nb * w, kv), cur(2)),
            pl.BlockSpec((1, w, kv), prev(2)),
            pl.BlockSpec((1, nb * w, kv), cur(3)),
            pl.BlockSpec((1, w, kv), prev(3)),
        ],
        out_specs=pl.BlockSpec((1, nb * w, SWA_WIDTH), lambda bi, n, sk: (bi, n, 0)),
    )
    return pl.pallas_call(
        _swa_kernel,
        grid_spec=grid_spec,
        out_shape=jax.ShapeDtypeStruct((b, s, SWA_WIDTH), BF16),
        compiler_params=_params("parallel", "parallel"),
        name="swa",
    )(sinks, u3, u3, u3, u3, u3)


def _head_lanes(hh):
    lane = lax.broadcasted_iota(jnp.int32, (1, LANES), 1)
    if hh == 0:
        return lane < HEAD_DIM, lane - HEAD_DIM
    return lane >= HEAD_DIM, lane


def _causal_attention(qp_scr, kp_scr, v_ref, o_ref, t):
    s_len = qp_scr.shape[1]
    lane = lax.broadcasted_iota(jnp.int32, (1, LANES), 1)
    row = lax.broadcasted_iota(jnp.int32, (t, t), 0)
    col = lax.broadcasted_iota(jnp.int32, (t, t), 1)
    for qi in range(s_len // t):
        rows = slice(qi * t, (qi + 1) * t)
        outs = []
        for hh in range(2):
            q = qp_scr[hh, rows, :]
            sd = jnp.where(col <= row, _dot_nt(q, kp_scr[hh, rows, :]), NEG_INF)
            m = jnp.max(sd, axis=1, keepdims=True)
            if qi:
                sp = _dot_nt(q, kp_scr[hh, :qi * t, :])
                m = jnp.maximum(m, jnp.max(sp, axis=1, keepdims=True))
            pd = jnp.exp2(sd - m)
            l = jnp.sum(pd, axis=1, keepdims=True)
            o = _dot(pd.astype(BF16), v_ref[0, rows, :])
            if qi:
                pp = jnp.exp2(sp - m)
                l = l + jnp.sum(pp, axis=1, keepdims=True)
                o = o + _dot(pp.astype(BF16), v_ref[0, :qi * t, :])
            outs.append(o * (1.0 / l))
        o_ref[0, rows, :] = jnp.where(lane < HEAD_DIM, outs[0], outs[1]).astype(BF16)


def _pair_attention(kernel_fn, u3, extra, q_blk, k_blk, v_blk, n_heads, name, side=None):
    b, s, _ = u3.shape
    pairs = n_heads // 2
    pair = lambda blk: (lambda bi, p: (bi, 0, blk + p))
    seq_block = pl.BlockSpec((1, s, LANES), lambda bi, p: (bi, 0, 0))
    in_specs = [pl.BlockSpec((1, s, LANES), pair(q_blk)),
                pl.BlockSpec((1, s, LANES), pair(k_blk)),
                pl.BlockSpec((1, s, LANES), pair(v_blk))] + [seq_block] * len(extra)
    out_specs = [pl.BlockSpec((1, s, LANES), pair(0))]
    out_shape = [jax.ShapeDtypeStruct((b, s, n_heads * HEAD_DIM), BF16)]
    args = [u3, u3, u3, *extra]
    if side is not None:
        kernel_fn = _with_side_cast(kernel_fn, len(args))
        spec_in, spec_out, shape = _side_cast(side, b * pairs, lambda bi, p: (bi * pairs + p, 0))
        in_specs.append(spec_in)
        out_specs.append(spec_out)
        out_shape.append(shape)
        args.append(side)
    outs = pl.pallas_call(
        kernel_fn,
        grid=(b, pairs),
        in_specs=in_specs,
        out_specs=out_specs,
        out_shape=out_shape,
        scratch_shapes=[pltpu.VMEM((2, s, LANES), BF16), pltpu.VMEM((2, s, LANES), BF16)],
        compiler_params=_params("parallel", "parallel"),
        name=name,
    )(*args)
    return tuple(outs) if side is not None else (outs[0], None)


def _fox_kernel(q_ref, k_ref, v_ref, lf_ref, o_ref, qp_scr, kp_scr):
    p = pl.program_id(1)
    s_len = q_ref.shape[1]
    t = ATT_TILE
    lane = lax.broadcasted_iota(jnp.int32, (1, LANES), 1)
    r = lax.broadcasted_iota(jnp.int32, (t, t), 0)
    cc = lax.broadcasted_iota(jnp.int32, (t, t), 1)
    tri = jnp.where(cc <= r, 1.0, 0.0).astype(BF16)
    carry = jnp.zeros((1, LANES), F32)
    for i in range(s_len // t):
        rows = slice(i * t, (i + 1) * t)
        lf = lf_ref[0, rows, :]
        lf0 = jnp.sum(jnp.where(lane == 2 * p, lf, 0.0), axis=1, keepdims=True)
        lf1 = jnp.sum(jnp.where(lane == 2 * p + 1, lf, 0.0), axis=1, keepdims=True)
        hi, mid, lo = _split3(jnp.where(lane < 3, lf0, jnp.where(lane < 6, lf1, 0.0)))
        terms = jnp.where((lane == 0) | (lane == 3), hi,
                          jnp.where((lane == 1) | (lane == 4), mid, lo))
        sums = _dot(tri, terms) + carry
        carry = sums[t - 1:t, :]
        c0 = jnp.sum(jnp.where(lane < 3, sums, 0.0), axis=1, keepdims=True)
        c1 = jnp.sum(jnp.where((lane >= 3) & (lane < 6), sums, 0.0), axis=1, keepdims=True)
        chi, cmid, clo = _split3(jnp.where(lane < HEAD_DIM, c1, c0) * LOG2E)
        for hh in range(2):
            keep, f = _head_lanes(hh)
            ones_q = jnp.where((f >= 3) & (f < 6), 1.0, 0.0).astype(BF16)
            ones_k = jnp.where((f >= 0) & (f < 3), 1.0, 0.0).astype(BF16)
            qf = jnp.where(f == 0, chi, jnp.where(f == 1, cmid, jnp.where(f == 2, clo, ones_q)))
            kf = jnp.where(f == 3, -chi, jnp.where(f == 4, -cmid, jnp.where(f == 5, -clo, ones_k)))
            qp_scr[hh, rows, :] = jnp.where(keep, q_ref[0, rows, :], qf)
            kp_scr[hh, rows, :] = jnp.where(keep, k_ref[0, rows, :], kf)
    _causal_attention(qp_scr, kp_scr, v_ref, o_ref, t)


def _fox(u3, lf3, side=None):
    return _pair_attention(_fox_kernel, u3, (lf3,), 8, 10, 12, FOX_HEADS, "fox", side)


def _moba_kernel(q_ref, k_ref, v_ref, o_ref, qp_scr, kp_scr):
    s_len = q_ref.shape[1]
    t = MOBA_BLOCK
    nkb = s_len // t
    sub = 8
    assert nkb <= sub
    means = [jnp.mean(k_ref[0, j * t:(j + 1) * t, :].astype(F32), axis=0, keepdims=True)
             for j in range(nkb)]
    km_pair = jnp.concatenate(means + [jnp.zeros((sub - nkb, LANES), F32)], axis=0) \
        if nkb < sub else jnp.concatenate(means, axis=0)
    stack = []
    for hh in range(2):
        keep, _ = _head_lanes(hh)
        stack += [part.astype(F32) for part in _split3(jnp.where(keep, km_pair, 0.0))]
    stack.append(jnp.zeros((LANES - 6 * sub, LANES), F32))
    km_all = jnp.concatenate(stack, axis=0).astype(BF16)
    blk = lax.broadcasted_iota(jnp.int32, (sub, t), 0)
    gates = [_dot_nt(km_all, q_ref[0, i * t:(i + 1) * t, :]) for i in range(nkb)]
    for hh in range(2):
        keep, f = _head_lanes(hh)
        for i in range(nkb):
            rows = slice(i * t, (i + 1) * t)
            q = q_ref[0, rows, :]
            base = 3 * sub * hh
            g = (gates[i][base:base + sub] + gates[i][base + sub:base + 2 * sub]) \
                + gates[i][base + 2 * sub:base + 3 * sub]
            past = blk < i
            g = jnp.where(past, g, NEG_INF)
            rank = jnp.zeros((sub, t), jnp.int32)
            for r in range(i):
                gr = jnp.broadcast_to(g[r:r + 1, :], (sub, t))
                beats = (gr > g) | ((gr == g) & (r < blk))
                rank = rank + beats.astype(jnp.int32)
            drop = jnp.where(past & (rank >= MOBA_TOPK), 1.0, 0.0)
            below = [jnp.zeros((HEAD_DIM, t), F32)] if hh == 0 else []
            above = jnp.zeros((LANES - sub - (HEAD_DIM if hh == 0 else 0), t), F32)
            padded = jnp.concatenate(below + [drop, above], axis=0)
            kf = jnp.where(f == i, NEG_INF, 0.0).astype(BF16)
            qp_scr[hh, rows, :] = jnp.where(keep, q, padded.T.astype(BF16))
            kp_scr[hh, rows, :] = jnp.where(keep, k_ref[0, rows, :], kf)
    _causal_attention(qp_scr, kp_scr, v_ref, o_ref, t)


def _moba(u3, side=None):
    return _pair_attention(_moba_kernel, u3, (), 14, 16, 18, MOBA_HEADS, "moba", side)


def _rms_f32(x):
    x = x.astype(F32)
    return x * lax.rsqrt(jnp.mean(x * x, axis=-1, keepdims=True) + RMS_EPS)


def _outproj_kernel(oa_ref, ob_ref, oc_ref, gain_ref, w_ref, x_ref, h_ref):
    a0, a1, a2 = 0, SWA_WIDTH, SWA_WIDTH + FOX_WIDTH
    ya = (_rms_f32(oa_ref[...]) * gain_ref[:, a0:a1]).astype(BF16)
    yb = (_rms_f32(ob_ref[...]) * gain_ref[:, a1:a2]).astype(BF16)
    yc = (_rms_f32(oc_ref[...]) * gain_ref[:, a2:]).astype(BF16)
    y = (_dot(ya, w_ref[a0:a1, :]) + _dot(yb, w_ref[a1:a2, :])) + _dot(yc, w_ref[a2:, :])
    h_ref[...] = x_ref[...] + y


def _outproj(oa, ob, oc, gain, w_out, x2):
    t = x2.shape[0]
    tm = ROW_TILE
    row = lambda i: (i, 0)
    fixed = lambda i: (0, 0)
    return pl.pallas_call(
        _outproj_kernel,
        grid=(t // tm,),
        in_specs=[
            pl.BlockSpec((tm, SWA_WIDTH), row),
            pl.BlockSpec((tm, FOX_WIDTH), row),
            pl.BlockSpec((tm, MOBA_WIDTH), row),
            pl.BlockSpec((1, MIX_WIDTH), fixed),
            pl.BlockSpec((MIX_WIDTH, D_MODEL), fixed),
            pl.BlockSpec((tm, D_MODEL), row),
        ],
        out_specs=pl.BlockSpec((tm, D_MODEL), row),
        out_shape=jax.ShapeDtypeStruct((t, D_MODEL), F32),
        compiler_params=_params("parallel"),
        name="outproj",
    )(oa, ob, oc, gain, w_out, x2)


def _silu(x):
    return x / (1.0 + jnp.exp(-x))


def _ffn_kernel(h_ref, g_ref, wg_ref, wu_ref, wd_ref, o_ref, acc_scr):
    j = pl.program_id(1)

    @pl.when((pl.program_id(0) == 0) & (j == 0))
    def _():
        acc_scr[...] = jnp.zeros_like(acc_scr)

    x = h_ref[...]
    inv = lax.rsqrt(jnp.mean(x * x, axis=-1, keepdims=True) + RMS_EPS)
    z = (x * inv * g_ref[...]).astype(BF16)
    a = (_silu(_dot(z, wg_ref[...].astype(BF16))) * _dot(z, wu_ref[...].astype(BF16))).astype(BF16)
    total = acc_scr[...] + _dot(a, wd_ref[...].astype(BF16))
    o_ref[...] = x + total
    acc_scr[...] = jnp.where(j == pl.num_programs(1) - 1, 0.0, total)


def _dense_ffn(h2, norm_g, w_gate, w_up, w_down):
    t = h2.shape[0]
    tm, tf = 2 * ROW_TILE, FF_TILE
    return pl.pallas_call(
        _ffn_kernel,
        grid=(t // tm, D_FF // tf),
        in_specs=[
            pl.BlockSpec((tm, D_MODEL), lambda i, j: (i, 0)),
            pl.BlockSpec((1, D_MODEL), lambda i, j: (0, 0)),
            pl.BlockSpec((D_MODEL, tf), lambda i, j: (0, j)),
            pl.BlockSpec((D_MODEL, tf), lambda i, j: (0, j)),
            pl.BlockSpec((tf, D_MODEL), lambda i, j: (j, 0)),
        ],
        out_specs=pl.BlockSpec((tm, D_MODEL), lambda i, j: (i, 0)),
        out_shape=jax.ShapeDtypeStruct((t, D_MODEL), F32),
        scratch_shapes=[pltpu.VMEM((tm, D_MODEL), F32)],
        compiler_params=_params("arbitrary", "arbitrary"),
        name="dense_ffn",
    )(h2, norm_g, w_gate, w_up, w_down)


def _router_kernel(h_ref, g_ref, wr_ref, z_ref, idx_ref, gate_ref, rank_ref, count_ref, count_scr):
    tm = h_ref.shape[0]
    x = h_ref[...]
    inv = lax.rsqrt(jnp.mean(x * x, axis=-1, keepdims=True) + RMS_EPS)
    z = x * inv * g_ref[...]
    for c in range(CHUNKS):
        z_ref[pl.ds(c, tm, stride=CHUNKS), :] = z[:, c * LANES:(c + 1) * LANES]
    zh, zm, _ = _split3(z)
    wh, wm = wr_ref[0], wr_ref[1]
    logits = _dot(zh, wh) + (_dot(zh, wm) + _dot(zm, wh))
    lane = lax.broadcasted_iota(jnp.int32, (tm, LANES), 1)
    logits = jnp.where(lane < N_EXPERTS, logits, -jnp.inf)
    m1 = jnp.max(logits, axis=1, keepdims=True)
    i1 = jnp.min(jnp.where(logits == m1, lane, LANES), axis=1, keepdims=True)
    rest = jnp.where(lane == i1, -jnp.inf, logits)
    m2 = jnp.max(rest, axis=1, keepdims=True)
    i2 = jnp.min(jnp.where(rest == m2, lane, LANES), axis=1, keepdims=True)
    e2 = jnp.exp(m2 - m1)
    g1 = 1.0 / (1.0 + e2)
    idx_ref[...] = jnp.where(lane == 0, i1, i2)
    gate_ref[...] = jnp.where(lane == 0, g1, e2 * g1)

    @pl.when(pl.program_id(0) == 0)
    def _():
        count_scr[...] = jnp.zeros_like(count_scr)

    chosen = ((lane == i1) | (lane == i2)).astype(F32)
    r = lax.broadcasted_iota(jnp.int32, (tm, tm), 0)
    c = lax.broadcasted_iota(jnp.int32, (tm, tm), 1)
    before = jnp.where(c < r, 1.0, 0.0).astype(BF16)
    earlier = _dot(before, chosen.astype(BF16)) + count_scr[...]
    r1 = jnp.sum(jnp.where(lane == i1, earlier, 0.0), axis=1, keepdims=True)
    r2 = jnp.sum(jnp.where(lane == i2, earlier, 0.0), axis=1, keepdims=True)
    rank_ref[...] = jnp.where(lane == 0, r1, r2).astype(jnp.int32)
    count_scr[...] += jnp.sum(chosen, axis=0, keepdims=True)
    count_ref[...] = count_scr[...].astype(jnp.int32)


def _router(h2, norm_g, wr3):
    t = h2.shape[0]
    tm = ROW_TILE
    row = lambda i: (i, 0)
    return pl.pallas_call(
        _router_kernel,
        grid=(t // tm,),
        in_specs=[
            pl.BlockSpec((tm, D_MODEL), row),
            pl.BlockSpec((1, D_MODEL), lambda i: (0, 0)),
            pl.BlockSpec((2, D_MODEL, LANES), lambda i: (0, 0, 0)),
        ],
        out_specs=[pl.BlockSpec((tm * CHUNKS, LANES), row),
                   pl.BlockSpec((tm, LANES), row),
                   pl.BlockSpec((tm, LANES), row),
                   pl.BlockSpec((tm, LANES), row),
                   pl.BlockSpec((1, LANES), lambda i: (0, 0))],
        out_shape=[jax.ShapeDtypeStruct((t * CHUNKS, LANES), F32),
                   jax.ShapeDtypeStruct((t, LANES), jnp.int32),
                   jax.ShapeDtypeStruct((t, LANES), F32),
                   jax.ShapeDtypeStruct((t, LANES), jnp.int32),
                   jax.ShapeDtypeStruct((1, LANES), jnp.int32)],
        scratch_shapes=[pltpu.VMEM((1, LANES), F32)],
        compiler_params=_params("arbitrary"),
        name="router",
    )(h2, norm_g, wr3)


def _row_copy(src, src_row, dst, dst_row, sem):
    return pltpu.make_async_copy(src.at[pl.ds(src_row * CHUNKS, CHUNKS), :],
                                 dst.at[pl.ds(dst_row * CHUNKS, CHUNKS), :], sem)


def _rows_wait(src, dst, rows, sem):
    pltpu.make_async_copy(src.at[pl.ds(0, rows * CHUNKS), :],
                          dst.at[pl.ds(0, rows * CHUNKS), :], sem).wait()


def _dispatch_kernel(dest_ref, fill_ref, z_ref, x_hbm, zero_scr, fsem, sem):
    i = pl.program_id(0)
    blk = MOE_BLOCK
    tile = z_ref.shape[0] // CHUNKS

    @pl.when(i == 0)
    def _():
        zero_scr[...] = jnp.zeros_like(zero_scr)
        for e in range(fill_ref.shape[0]):
            @pl.when(fill_ref[e] >= 0)
            def _():
                start = pl.multiple_of(fill_ref[e] * CHUNKS, blk * CHUNKS)
                pltpu.make_async_copy(zero_scr, x_hbm.at[pl.ds(start, blk * CHUNKS), :], fsem).start()
        for e in range(fill_ref.shape[0]):
            @pl.when(fill_ref[e] >= 0)
            def _():
                _rows_wait(zero_scr, x_hbm, blk, fsem)

    base = i * tile * TOP_K

    tokens_per_trip = MOE_ISSUE_UNROLL // TOP_K

    def trip(it, c):
        for k in range(MOE_ISSUE_UNROLL):
            token = it * tokens_per_trip + k // TOP_K
            slot = dest_ref[base + it * MOE_ISSUE_UNROLL + k]
            _row_copy(z_ref, token, x_hbm, slot, sem).start(priority=k % 2)
        return c
    lax.fori_loop(0, tile // tokens_per_trip, trip, 0)
    for _ in range(TOP_K):
        _rows_wait(z_ref, x_hbm, tile, sem)


def _dispatch(z8, dest, fill, p_rows):
    tile = DISPATCH_TILE
    assert z8.shape[0] % (tile * CHUNKS) == 0
    grid_spec = pltpu.PrefetchScalarGridSpec(
        num_scalar_prefetch=2,
        grid=(z8.shape[0] // (tile * CHUNKS),),
        in_specs=[pl.BlockSpec((tile * CHUNKS, LANES), lambda i, dd, f: (i, 0))],
        out_specs=pl.BlockSpec(memory_space=pl.ANY),
        scratch_shapes=[pltpu.VMEM((MOE_BLOCK * CHUNKS, LANES), F32),
                        pltpu.SemaphoreType.DMA(()), pltpu.SemaphoreType.DMA(())],
    )
    return pl.pallas_call(
        _dispatch_kernel,
        grid_spec=grid_spec,
        out_shape=jax.ShapeDtypeStruct((p_rows * CHUNKS, LANES), F32),
        compiler_params=_params("arbitrary"),
        name="moe_dispatch",
    )(dest, fill, z8)


def _moe_kernel(be_ref, live_ref, bx_ref, x_ref, wg_ref, wu_ref, wd_ref, y_ref, acc):
    i = pl.program_id(0)
    j = pl.program_id(1)
    tm = acc.shape[0]

    @pl.when((i == 0) & (j == 0))
    def _():
        acc[...] = jnp.zeros_like(acc)

    @pl.when((live_ref[i] == 0) & (j == 0))
    def _():
        y_ref[...] = jnp.zeros_like(y_ref)

    @pl.when(live_ref[i] > 0)
    def _():
        x = jnp.concatenate([x_ref[pl.ds(c, tm, stride=CHUNKS), :].astype(BF16)
                             for c in range(CHUNKS)], axis=1)
        a = (_silu(_dot(x, wg_ref[0])) * _dot(x, wu_ref[0])).astype(BF16)
        total = acc[...] + _dot(a, wd_ref[0])
        for c in range(CHUNKS):
            y_ref[pl.ds(c, tm, stride=CHUNKS), :] = total[:, c * LANES:(c + 1) * LANES]
        acc[...] = jnp.where(j == pl.num_programs(1) - 1, 0.0, total)


def _moe_experts(x8, blk_e, blk_live, blk_x, w_gate, w_up, w_down):
    tm, tf = MOE_BLOCK, MOE_FF_TILE
    nblk = blk_e.shape[0]
    nff = D_FF // tf
    ff = lambda i, j, live: jnp.where(live[i] > 0, j, nff - 1)
    grid_spec = pltpu.PrefetchScalarGridSpec(
        num_scalar_prefetch=3,
        grid=(nblk, nff),
        in_specs=[
            pl.BlockSpec((tm * CHUNKS, LANES), lambda i, j, be, lv, bx: (bx[i], 0)),
            pl.BlockSpec((1, D_MODEL, tf), lambda i, j, be, lv, bx: (be[i], 0, ff(i, j, lv))),
            pl.BlockSpec((1, D_MODEL, tf), lambda i, j, be, lv, bx: (be[i], 0, ff(i, j, lv))),
            pl.BlockSpec((1, tf, D_MODEL), lambda i, j, be, lv, bx: (be[i], ff(i, j, lv), 0)),
        ],
        out_specs=pl.BlockSpec((tm * CHUNKS, LANES), lambda i, j, be, lv, bx: (i, 0)),
        scratch_shapes=[pltpu.VMEM((tm, D_MODEL), F32)],
    )
    return pl.pallas_call(
        _moe_kernel,
        grid_spec=grid_spec,
        out_shape=jax.ShapeDtypeStruct(x8.shape, F32),
        compiler_params=_params("arbitrary", "arbitrary"),
        name="moe_experts",
    )(blk_e, blk_live, blk_x, x8, w_gate, w_up, w_down)


def _combine_kernel(dest_ref, h_ref, gate_ref, *rest, final_norm):
    if final_norm:
        g_ref, y_hbm, o_ref, yb_even, yb_odd, sem = rest
    else:
        y_hbm, o_ref, yb_even, yb_odd, sem = rest
    i = pl.program_id(0)
    last = pl.num_programs(0) - 1
    tm = h_ref.shape[0]
    rows = tm * TOP_K
    stride = TOP_K * CHUNKS

    @pl.when(i == 0)
    def _():
        def trip(it, c):
            for k in range(MOE_ISSUE_UNROLL):
                r = it * MOE_ISSUE_UNROLL + k
                _row_copy(y_hbm, dest_ref[r], yb_even, r, sem.at[0]).start(priority=k % 2)
            return c
        lax.fori_loop(0, rows // MOE_ISSUE_UNROLL, trip, 0)

    def step(parity, yb, yb_next):
        _rows_wait(y_hbm, yb, rows, sem.at[parity])
        nxt = jnp.minimum(i + 1, last)
        for r in range(rows):
            copy = _row_copy(y_hbm, dest_ref[nxt * rows + r], yb_next, r, sem.at[1 - parity])
            copy.start(priority=r % 2)
        gates = [gate_ref[:, k:k + 1] for k in range(TOP_K)]
        parts = []
        ss = jnp.zeros((tm, 1), F32)
        for c in range(CHUNKS):
            y = gates[0] * yb[pl.ds(c, tm, stride=stride), :]
            for k in range(1, TOP_K):
                y = y + gates[k] * yb[pl.ds(k * CHUNKS + c, tm, stride=stride), :]
            hc = h_ref[:, c * LANES:(c + 1) * LANES] + y
            parts.append(hc)
            ss = ss + jnp.sum(hc * hc, axis=1, keepdims=True)
        if final_norm:
            inv = lax.rsqrt(ss / D_MODEL + RMS_EPS)
        for c in range(CHUNKS):
            cols = slice(c * LANES, (c + 1) * LANES)
            o_ref[:, cols] = parts[c] * inv * g_ref[:, cols] if final_norm else parts[c]

        @pl.when(i == last)
        def _():
            _rows_wait(y_hbm, yb_next, rows, sem.at[1 - parity])

    @pl.when(i % 2 == 0)
    def _():
        step(0, yb_even, yb_odd)

    @pl.when(i % 2 == 1)
    def _():
        step(1, yb_odd, yb_even)


def _combine(h2, y8, dest, gate, out_g):
    t = h2.shape[0]
    tm = ROW_TILE
    final_norm = out_g is not None
    row = lambda i, d: (i, 0)
    in_specs = [pl.BlockSpec((tm, D_MODEL), row), pl.BlockSpec((tm, LANES), row)]
    args = [h2, gate]
    if final_norm:
        in_specs.append(pl.BlockSpec((1, D_MODEL), lambda i, d: (0, 0)))
        args.append(out_g.reshape(1, D_MODEL))
    grid_spec = pltpu.PrefetchScalarGridSpec(
        num_scalar_prefetch=1,
        grid=(t // tm,),
        in_specs=in_specs + [pl.BlockSpec(memory_space=pl.ANY)],
        out_specs=pl.BlockSpec((tm, D_MODEL), row),
        scratch_shapes=[pltpu.VMEM((tm * TOP_K * CHUNKS, LANES), F32),
                        pltpu.VMEM((tm * TOP_K * CHUNKS, LANES), F32),
                        pltpu.SemaphoreType.DMA((2,))],
    )
    return pl.pallas_call(
        functools.partial(_combine_kernel, final_norm=final_norm),
        grid_spec=grid_spec,
        out_shape=jax.ShapeDtypeStruct((t, D_MODEL), F32),
        compiler_params=_params("arbitrary"),
        name="moe_combine",
    )(dest, *args, y8)


def _norm_kernel(h_ref, g_ref, o_ref):
    x = h_ref[...]
    o_ref[...] = x * lax.rsqrt(jnp.mean(x * x, axis=-1, keepdims=True) + RMS_EPS) * g_ref[...]


def _final_norm(h2, norm_g):
    t = h2.shape[0]
    tm = ROW_TILE
    return pl.pallas_call(
        _norm_kernel,
        grid=(t // tm,),
        in_specs=[pl.BlockSpec((tm, D_MODEL), lambda i: (i, 0)),
                  pl.BlockSpec((1, D_MODEL), lambda i: (0, 0))],
        out_specs=pl.BlockSpec((tm, D_MODEL), lambda i: (i, 0)),
        out_shape=jax.ShapeDtypeStruct((t, D_MODEL), F32),
        compiler_params=_params("parallel"),
        name="final_norm",
    )(h2, norm_g)


def _dispatch_plan(top_i, rank, counts, t):
    n = t * TOP_K
    tm = MOE_BLOCK
    padded = (counts + tm - 1) // tm * tm
    pends = jnp.cumsum(padded)
    pstarts = pends - padded
    onehot = (top_i[:, :, None] == jnp.arange(N_EXPERTS)[None, None, :]).astype(jnp.int32)
    dest = (jnp.sum(onehot * pstarts[None, None, :], axis=2) + rank).astype(jnp.int32).reshape(n)
    p_rows = -(-n // tm) * tm + N_EXPERTS * tm
    nblk = p_rows // tm
    blk = jnp.arange(nblk, dtype=jnp.int32)
    blk_e = jnp.minimum(jnp.sum((blk * tm)[:, None] >= pends[None, :], axis=1), N_EXPERTS - 1)
    live = blk * tm < pends[-1]
    last_live = jnp.maximum(pends[-1] // tm - 1, 0)
    blk_e = jnp.where(live, blk_e, blk_e[last_live])
    blk_x = jnp.where(live, blk, 0)
    tail = jnp.where(padded > 0, pends - tm, -1)
    spare = pends[-1] + jnp.arange(N_EXPERTS) * tm
    fill = jnp.concatenate([tail, jnp.where(spare < p_rows, spare, -1)])
    i32 = lambda a: a.astype(jnp.int32)
    return dest, i32(fill), i32(blk_e), i32(live), i32(blk_x), p_rows


def _rope_tables(seq):
    inv = 1.0 / (ROPE_THETA ** (jnp.arange(0, HEAD_DIM, 2, dtype=F32) / HEAD_DIM))
    ang = jnp.arange(seq, dtype=F32)[:, None] * inv[None, :]
    cos = jnp.concatenate([jnp.cos(ang)] * 4, axis=-1)
    sin = jnp.concatenate([jnp.sin(ang)] * 4, axis=-1)
    upper = (jnp.arange(LANES) % HEAD_DIM) >= HEAD_DIM // 2
    sa = jnp.where(upper[None, :], sin, 0.0)
    sb = jnp.where(upper[None, :], 0.0, -sin)
    return cos, sa, sb


def _mixer(h2, b, s, norm_g, w_in, forget_bias, sinks, mix_gain, w_out, tables,
           fox_side=None, moba_side=None):
    f0 = SWA_WIDTH + 2 * SWA_KV_WIDTH + 3 * FOX_WIDTH
    f1 = f0 + FOX_HEADS
    assert f0 + 3 * MOBA_WIDTH == W_BLOCKS * LANES
    wt = jnp.swapaxes(w_in, 0, 1)
    w_all = jnp.concatenate(
        [wt[:f0], wt[f1:], jnp.pad(wt[f0:f1], ((0, LANES - FOX_HEADS), (0, 0)))],
        axis=0).astype(BF16)
    fbias = jnp.pad(forget_bias.astype(F32), (0, LANES - FOX_HEADS)).reshape(1, LANES)
    u, lf = _project(h2, norm_g.reshape(1, D_MODEL), w_all, *tables, fbias, s)
    u3 = u.reshape(b, s, U_WIDTH)
    oa = _swa(u3, sinks.astype(F32))
    ob, fox_cast = _fox(u3, lf.reshape(b, s, LANES), fox_side)
    oc, moba_cast = _moba(u3, moba_side)
    t = b * s
    h = _outproj(oa.reshape(t, SWA_WIDTH), ob.reshape(t, FOX_WIDTH), oc.reshape(t, MOBA_WIDTH),
                 mix_gain.astype(F32).reshape(1, MIX_WIDTH), w_out.astype(BF16), h2)
    return h, fox_cast, moba_cast


def _moe_ffn(h2, norm_g, w_router, wg, wu, wd, out_g):
    t = h2.shape[0]
    wr = jnp.pad(w_router.astype(F32), ((0, 0), (0, LANES - N_EXPERTS)))
    wr2 = jnp.stack(_split3(wr)[:2])
    z8, top, gate, rank, counts = _router(h2, norm_g.reshape(1, D_MODEL), wr2)
    dest, fill, blk_e, blk_live, blk_x, p_rows = _dispatch_plan(
        top[:, :TOP_K], rank[:, :TOP_K], counts[0, :N_EXPERTS], t)
    x8 = _dispatch(z8, dest, fill, p_rows)
    y8 = _moe_experts(x8, blk_e, blk_live, blk_x, wg, wu, wd)
    return _combine(h2, y8, dest, gate, out_g)


def kernel(x, attn_norm, w_in, fox_forget_bias, swa_sinks, mix_gain, w_out, ffn_norm,
           dense_w_gate, dense_w_up, dense_w_down, router_w, moe_w_gate, moe_w_up,
           moe_w_down, final_norm):
    b, s, d = x.shape
    depth = attn_norm.shape[0]
    assert d == D_MODEL and s % MOBA_BLOCK == 0 and s % ROW_TILE == 0
    tables = _rope_tables(s)
    h = x.reshape(b * s, d)
    normed = False
    ne, _, ff = moe_w_gate.shape[1:]
    flat = lambda w: w.reshape(w.shape[0] * w.shape[1], w.shape[2])
    wg = wu = None
    for layer in range(depth):
        j = layer // 2
        routed = layer % 2 == 1
        routed_next = layer + 1 < depth and (layer + 1) % 2 == 1
        fox_side = flat(moe_w_down[j]) if routed else (flat(moe_w_gate[(layer + 1) // 2])
                                                      if routed_next else None)
        moba_side = flat(moe_w_up[(layer + 1) // 2]) if routed_next and not routed else None
        h, fox_cast, moba_cast = _mixer(h, b, s, attn_norm[layer], w_in[layer],
                                        fox_forget_bias[layer], swa_sinks[layer], mix_gain[layer],
                                        w_out[layer], tables, fox_side, moba_side)
        last = layer == depth - 1
        if not routed:
            wg, wu = fox_cast, moba_cast
            h = _dense_ffn(h, ffn_norm[layer].reshape(1, d), dense_w_gate[j], dense_w_up[j],
                           dense_w_down[j])
        else:
            h = _moe_ffn(h, ffn_norm[layer], router_w[j], wg.reshape(ne, d, ff),
                         wu.reshape(ne, d, ff), fox_cast.reshape(ne, ff, d),
                         final_norm if last else None)
            normed = last
    if not normed:
        h = _final_norm(h, final_norm.reshape(1, d))
    return h.reshape(b, s, d)
```

```python
import functools

import jax
import jax.numpy as jnp
from jax import lax
from jax.experimental import pallas as pl
from jax.experimental.pallas import tpu as pltpu

F32 = jnp.float32
BF16 = jnp.bfloat16

D_MODEL = 1024
HEAD_DIM = 64
LANES = 128
SWA_Q_HEADS = 8
SWA_KV_HEADS = 2
SWA_WINDOW = 128
FOX_HEADS = 4
MOBA_HEADS = 4
MOBA_BLOCK = 256
MOBA_TOPK = 3
ROPE_THETA = 10000.0
RMS_EPS = 1e-5
D_FF = 3584
N_EXPERTS = 8
TOP_K = 2
MOE_BLOCK = 512
NEG_INF = -1e30
ATTN_SCALE = HEAD_DIM ** -0.5
LOG2E = 1.4426950408889634
Q_SCALE = ATTN_SCALE * LOG2E

SWA_WIDTH = SWA_Q_HEADS * HEAD_DIM
SWA_KV_WIDTH = SWA_KV_HEADS * HEAD_DIM
FOX_WIDTH = FOX_HEADS * HEAD_DIM
MOBA_WIDTH = MOBA_HEADS * HEAD_DIM
MIX_WIDTH = SWA_WIDTH + FOX_WIDTH + MOBA_WIDTH

W_BLOCKS = 18
W_ALL_WIDTH = (W_BLOCKS + 1) * LANES
ROPE_BLOCKS = (0, 1, 2, 3, 4, 12, 13, 14, 15)
Q_BLOCKS = (0, 1, 2, 3, 6, 7, 12, 13)
SWA_KV_BLOCKS = (4, 5)
U_BLOCKS = W_BLOCKS + len(SWA_KV_BLOCKS)
U_WIDTH = U_BLOCKS * LANES

VMEM_LIMIT = 56 * 1024 * 1024

ROW_TILE = 1024
ROUTE_TILE = 512
ATT_TILE = 256
SWA_STEP_BLOCKS = 8
FF_TILE = 512
MOE_FF_TILE = 1792
CHUNKS = D_MODEL // LANES
MOE_ISSUE_UNROLL = 8
DISPATCH_TILE = 2048


def _params(*sem):
    return pltpu.CompilerParams(dimension_semantics=sem, vmem_limit_bytes=VMEM_LIMIT)


def _split3(x):
    hi = x.astype(BF16)
    r1 = x - hi.astype(F32)
    mid = r1.astype(BF16)
    lo = (r1 - mid.astype(F32)).astype(BF16)
    return hi, mid, lo


def _dot_nt(a, b):
    return lax.dot_general(a, b, (((1,), (1,)), ((), ())), preferred_element_type=F32)


def _dot(a, b):
    return jnp.dot(a, b, preferred_element_type=F32)


def _side_cast(w2d, steps, index_map):
    rows, cols = w2d.shape
    assert rows % steps == 0
    spec = pl.BlockSpec((rows // steps, cols), index_map)
    return spec, spec, jax.ShapeDtypeStruct((rows, cols), BF16)


def _with_side_cast(kernel_fn, n_in):
    def body(*refs):
        side_in, out_ref, side_out = refs[n_in:n_in + 3]
        side_out[...] = side_in[...].astype(BF16)
        kernel_fn(*refs[:n_in], out_ref, *refs[n_in + 3:])
    return body


def _proj_kernel(x_ref, g_ref, w_ref, cos_ref, sa_ref, sb_ref, fb_ref, u_ref, lf_ref):
    x = x_ref[...]
    inv = lax.rsqrt(jnp.mean(x * x, axis=-1, keepdims=True) + RMS_EPS)
    h = (x * inv * g_ref[...]).astype(BF16)
    cos = cos_ref[...]
    sa = sa_ref[...]
    sb = sb_ref[...]
    low = lax.broadcasted_iota(jnp.int32, (1, LANES), 1) < HEAD_DIM
    dst = 0
    for c in range(W_BLOCKS // 2):
        acc = _dot_nt(h, w_ref[c * 2 * LANES:(c + 1) * 2 * LANES, :])
        for half in range(2):
            blk = 2 * c + half
            a = acc[:, half * LANES:(half + 1) * LANES]
            if blk in ROPE_BLOCKS:
                a = a * cos + pltpu.roll(a, 32, 1) * sa + pltpu.roll(a, 96, 1) * sb
            if blk in Q_BLOCKS:
                a = a * Q_SCALE
            if blk in SWA_KV_BLOCKS:
                swapped = pltpu.roll(a, HEAD_DIM, 1)
                outs = [jnp.where(low, a, swapped), jnp.where(low, swapped, a)]
            else:
                outs = [a]
            for o in outs:
                u_ref[:, dst * LANES:(dst + 1) * LANES] = o.astype(BF16)
                dst += 1
    assert dst == U_BLOCKS
    f = _dot_nt(h, w_ref[W_BLOCKS * LANES:W_ALL_WIDTH, :]) + fb_ref[...]
    lf_ref[...] = jnp.minimum(f, 0.0) - jnp.log(1.0 + jnp.exp(-jnp.abs(f)))


def _project(x2, norm_g, w_all, cos, sa, sb, fbias, seq):
    t = x2.shape[0]
    tm = ROW_TILE
    nseq = seq // tm
    row = lambda i: (i, 0)
    pos = lambda i: (i % nseq, 0)
    fixed = lambda i: (0, 0)
    return pl.pallas_call(
        _proj_kernel,
        grid=(t // tm,),
        in_specs=[
            pl.BlockSpec((tm, D_MODEL), row),
            pl.BlockSpec((1, D_MODEL), fixed),
            pl.BlockSpec((W_ALL_WIDTH, D_MODEL), fixed),
            pl.BlockSpec((tm, LANES), pos),
            pl.BlockSpec((tm, LANES), pos),
            pl.BlockSpec((tm, LANES), pos),
            pl.BlockSpec((1, LANES), fixed),
        ],
        out_specs=[pl.BlockSpec((tm, U_WIDTH), row), pl.BlockSpec((tm, LANES), row)],
        out_shape=[jax.ShapeDtypeStruct((t, U_WIDTH), BF16),
                   jax.ShapeDtypeStruct((t, LANES), F32)],
        compiler_params=_params("parallel"),
        name="proj",
    )(x2, norm_g, w_all, cos, sa, sb, fbias)


def _swa_kernel(sink_ref, q_ref, kc_ref, kp_ref, vc_ref, vp_ref, o_ref):
    n = pl.program_id(1)
    w = SWA_WINDOW
    group = SWA_Q_HEADS // SWA_KV_HEADS
    lane = lax.broadcasted_iota(jnp.int32, (1, LANES), 1)
    low = lane < HEAD_DIM

    k_all = jnp.concatenate([kp_ref[0], kc_ref[0]], axis=0)
    v_all = jnp.concatenate([vp_ref[0], vc_ref[0]], axis=0)
    qi = lax.broadcasted_iota(jnp.int32, (group * w, 2 * w), 0) % w
    kj = lax.broadcasted_iota(jnp.int32, (group * w, 2 * w), 1)
    window = (kj > qi) & (kj <= qi + w)
    pairs = group // 2
    for a in range(SWA_STEP_BLOCKS):
        rows = slice(a * w, (a + 1) * w)
        valid = (window & ((kj >= w) | (n > 0))) if a == 0 else window
        for g in range(SWA_KV_HEADS):
            k2 = k_all[a * w:(a + 2) * w, g * LANES:(g + 1) * LANES]
            v2 = v_all[a * w:(a + 2) * w, g * LANES:(g + 1) * LANES]
            qs = jnp.concatenate(
                [jnp.where(low if half == 0 else ~low,
                           q_ref[0, rows, (pairs * g + jj) * LANES:(pairs * g + jj + 1) * LANES], 0)
                 for jj in range(pairs) for half in range(2)], axis=0)
            s = jnp.where(valid, _dot_nt(qs, k2), NEG_INF)
            ps, inv = [], []
            for j in range(group):
                sj = s[j * w:(j + 1) * w]
                sink = sink_ref[g * group + j] * LOG2E
                m = jnp.maximum(jnp.max(sj, axis=1, keepdims=True), sink)
                p = jnp.exp2(sj - m)
                inv.append(1.0 / (jnp.sum(p, axis=1, keepdims=True) + jnp.exp2(sink - m)))
                ps.append(p.astype(BF16))
            o = _dot(jnp.concatenate(ps, axis=0), v2)
            outs = [o[j * w:(j + 1) * w] * inv[j] for j in range(group)]
            for jj in range(pairs):
                oj = jnp.where(low, outs[2 * jj], outs[2 * jj + 1])
                cols = slice((pairs * g + jj) * LANES, (pairs * g + jj + 1) * LANES)
                o_ref[0, rows, cols] = oj.astype(BF16)


def _swa(u3, sinks):
    b, s, _ = u3.shape
    w = SWA_WINDOW
    nb = SWA_STEP_BLOCKS
    assert s % (nb * w) == 0
    kv = SWA_KV_HEADS * LANES
    cur = lambda unit: (lambda bi, n, sk: (bi, n, unit))
    prev = lambda unit: (lambda bi, n, sk: (bi, jnp.maximum(n * nb - 1, 0), unit))
    grid_spec = pltpu.PrefetchScalarGridSpec(
        num_scalar_prefetch=1,
        grid=(b, s // (nb * w)),
        in_specs=[
            pl.BlockSpec((1, nb * w, SWA_WIDTH), lambda bi, n, sk: (bi, n, 0)),
            pl.BlockSpec((1, nb * w, kv), cur(2)),
            pl.BlockSpec((1, w, kv), prev(2)),
            pl.BlockSpec((1, nb * w, kv), cur(3)),
            pl.BlockSpec((1, w, kv), prev(3)),
        ],
        out_specs=pl.BlockSpec((1, nb * w, SWA_WIDTH), lambda bi, n, sk: (bi, n, 0)),
    )
    return pl.pallas_call(
        _swa_kernel,
        grid_spec=grid_spec,
        out_shape=jax.ShapeDtypeStruct((b, s, SWA_WIDTH), BF16),
        compiler_params=_params("parallel", "parallel"),
        name="swa",
    )(sinks, u3, u3, u3, u3, u3)


def _head_lanes(hh):
    lane = lax.broadcasted_iota(jnp.int32, (1, LANES), 1)
    if hh == 0:
        return lane < HEAD_DIM, lane - HEAD_DIM
    return lane >= HEAD_DIM, lane


def _causal_attention(qp_scr, kp_scr, v_ref, o_ref, t):
    s_len = qp_scr.shape[1]
    lane = lax.broadcasted_iota(jnp.int32, (1, LANES), 1)
    row = lax.broadcasted_iota(jnp.int32, (t, t), 0)
    col = lax.broadcasted_iota(jnp.int32, (t, t), 1)
    for qi in range(s_len // t):
        rows = slice(qi * t, (qi + 1) * t)
        outs = []
        for hh in range(2):
            q = qp_scr[hh, rows, :]
            sd = jnp.where(col <= row, _dot_nt(q, kp_scr[hh, rows, :]), NEG_INF)
            m = jnp.max(sd, axis=1, keepdims=True)
            if qi:
                sp = _dot_nt(q, kp_scr[hh, :qi * t, :])
                m = jnp.maximum(m, jnp.max(sp, axis=1, keepdims=True))
            pd = jnp.exp2(sd - m)
            l = jnp.sum(pd, axis=1, keepdims=True)
            o = _dot(pd.astype(BF16), v_ref[0, rows, :])
            if qi:
                pp = jnp.exp2(sp - m)
                l = l + jnp.sum(pp, axis=1, keepdims=True)
                o = o + _dot(pp.astype(BF16), v_ref[0, :qi * t, :])
            outs.append(o * (1.0 / l))
        o_ref[0, rows, :] = jnp.where(lane < HEAD_DIM, outs[0], outs[1]).astype(BF16)


def _pair_attention(kernel_fn, u3, extra, q_blk, k_blk, v_blk, n_heads, name, side=None):
    b, s, _ = u3.shape
    pairs = n_heads // 2
    pair = lambda blk: (lambda bi, p: (bi, 0, blk + p))
    seq_block = pl.BlockSpec((1, s, LANES), lambda bi, p: (bi, 0, 0))
    in_specs = [pl.BlockSpec((1, s, LANES), pair(q_blk)),
                pl.BlockSpec((1, s, LANES), pair(k_blk)),
                pl.BlockSpec((1, s, LANES), pair(v_blk))] + [seq_block] * len(extra)
    out_specs = [pl.BlockSpec((1, s, LANES), pair(0))]
    out_shape = [jax.ShapeDtypeStruct((b, s, n_heads * HEAD_DIM), BF16)]
    args = [u3, u3, u3, *extra]
    if side is not None:
        kernel_fn = _with_side_cast(kernel_fn, len(args))
        spec_in, spec_out, shape = _side_cast(side, b * pairs, lambda bi, p: (bi * pairs + p, 0))
        in_specs.append(spec_in)
        out_specs.append(spec_out)
        out_shape.append(shape)
        args.append(side)
    outs = pl.pallas_call(
        kernel_fn,
        grid=(b, pairs),
        in_specs=in_specs,
        out_specs=out_specs,
        out_shape=out_shape,
        scratch_shapes=[pltpu.VMEM((2, s, LANES), BF16), pltpu.VMEM((2, s, LANES), BF16)],
        compiler_params=_params("parallel", "parallel"),
        name=name,
    )(*args)
    return tuple(outs) if side is not None else (outs[0], None)


def _fox_kernel(q_ref, k_ref, v_ref, lf_ref, o_ref, qp_scr, kp_scr):
    p = pl.program_id(1)
    s_len = q_ref.shape[1]
    t = ATT_TILE
    lane = lax.broadcasted_iota(jnp.int32, (1, LANES), 1)
    r = lax.broadcasted_iota(jnp.int32, (t, t), 0)
    cc = lax.broadcasted_iota(jnp.int32, (t, t), 1)
    tri = jnp.where(cc <= r, 1.0, 0.0).astype(BF16)
    carry = jnp.zeros((1, LANES), F32)
    for i in range(s_len // t):
        rows = slice(i * t, (i + 1) * t)
        lf = lf_ref[0, rows, :]
        lf0 = jnp.sum(jnp.where(lane == 2 * p, lf, 0.0), axis=1, keepdims=True)
        lf1 = jnp.sum(jnp.where(lane == 2 * p + 1, lf, 0.0), axis=1, keepdims=True)
        hi, mid, lo = _split3(jnp.where(lane < 3, lf0, jnp.where(lane < 6, lf1, 0.0)))
        terms = jnp.where((lane == 0) | (lane == 3), hi,
                          jnp.where((lane == 1) | (lane == 4), mid, lo))
        sums = _dot(tri, terms) + carry
        carry = sums[t - 1:t, :]
        c0 = jnp.sum(jnp.where(lane < 3, sums, 0.0), axis=1, keepdims=True)
        c1 = jnp.sum(jnp.where((lane >= 3) & (lane < 6), sums, 0.0), axis=1, keepdims=True)
        chi, cmid, clo = _split3(jnp.where(lane < HEAD_DIM, c1, c0) * LOG2E)
        for hh in range(2):
            keep, f = _head_lanes(hh)
            ones_q = jnp.where((f >= 3) & (f < 6), 1.0, 0.0).astype(BF16)
            ones_k = jnp.where((f >= 0) & (f < 3), 1.0, 0.0).astype(BF16)
            qf = jnp.where(f == 0, chi, jnp.where(f == 1, cmid, jnp.where(f == 2, clo, ones_q)))
            kf = jnp.where(f == 3, -chi, jnp.where(f == 4, -cmid, jnp.where(f == 5, -clo, ones_k)))
            qp_scr[hh, rows, :] = jnp.where(keep, q_ref[0, rows, :], qf)
            kp_scr[hh, rows, :] = jnp.where(keep, k_ref[0, rows, :], kf)
    _causal_attention(qp_scr, kp_scr, v_ref, o_ref, t)


def _fox(u3, lf3, side=None):
    return _pair_attention(_fox_kernel, u3, (lf3,), 8, 10, 12, FOX_HEADS, "fox", side)


def _moba_kernel(q_ref, k_ref, v_ref, o_ref, qp_scr, kp_scr):
    s_len = q_ref.shape[1]
    t = MOBA_BLOCK
    nkb = s_len // t
    sub = 8
    assert nkb <= sub
    means = [jnp.mean(k_ref[0, j * t:(j + 1) * t, :].astype(F32), axis=0, keepdims=True)
             for j in range(nkb)]
    km_pair = jnp.concatenate(means + [jnp.zeros((sub - nkb, LANES), F32)], axis=0) \
        if nkb < sub else jnp.concatenate(means, axis=0)
    stack = []
    for hh in range(2):
        keep, _ = _head_lanes(hh)
        stack += [part.astype(F32) for part in _split3(jnp.where(keep, km_pair, 0.0))]
    stack.append(jnp.zeros((LANES - 6 * sub, LANES), F32))
    km_all = jnp.concatenate(stack, axis=0).astype(BF16)
    blk = lax.broadcasted_iota(jnp.int32, (sub, t), 0)
    gates = [_dot_nt(km_all, q_ref[0, i * t:(i + 1) * t, :]) for i in range(nkb)]
    for hh in range(2):
        keep, f = _head_lanes(hh)
        for i in range(nkb):
            rows = slice(i * t, (i + 1) * t)
            q = q_ref[0, rows, :]
            base = 3 * sub * hh
            g = (gates[i][base:base + sub] + gates[i][base + sub:base + 2 * sub]) \
                + gates[i][base + 2 * sub:base + 3 * sub]
            past = blk < i
            g = jnp.where(past, g, NEG_INF)
            rank = jnp.zeros((sub, t), jnp.int32)
            for r in range(i):
                gr = jnp.broadcast_to(g[r:r + 1, :], (sub, t))
                beats = (gr > g) | ((gr == g) & (r < blk))
                rank = rank + beats.astype(jnp.int32)
            drop = jnp.where(past & (rank >= MOBA_TOPK), 1.0, 0.0)
            below = [jnp.zeros((HEAD_DIM, t), F32)] if hh == 0 else []
            above = jnp.zeros((LANES - sub - (HEAD_DIM if hh == 0 else 0), t), F32)
            padded = jnp.concatenate(below + [drop, above], axis=0)
            kf = jnp.where(f == i, NEG_INF, 0.0).astype(BF16)
            qp_scr[hh, rows, :] = jnp.where(keep, q, padded.T.astype(BF16))
            kp_scr[hh, rows, :] = jnp.where(keep, k_ref[0, rows, :], kf)
    _causal_attention(qp_scr, kp_scr, v_ref, o_ref, t)


def _moba(u3, side=None):
    return _pair_attention(_moba_kernel, u3, (), 14, 16, 18, MOBA_HEADS, "moba", side)


def _rms_f32(x):
    x = x.astype(F32)
    return x * lax.rsqrt(jnp.mean(x * x, axis=-1, keepdims=True) + RMS_EPS)


def _outproj_kernel(oa_ref, ob_ref, oc_ref, gain_ref, w_ref, x_ref, h_ref):
    a0, a1, a2 = 0, SWA_WIDTH, SWA_WIDTH + FOX_WIDTH
    ya = (_rms_f32(oa_ref[...]) * gain_ref[:, a0:a1]).astype(BF16)
    yb = (_rms_f32(ob_ref[...]) * gain_ref[:, a1:a2]).astype(BF16)
    yc = (_rms_f32(oc_ref[...]) * gain_ref[:, a2:]).astype(BF16)
    y = (_dot(ya, w_ref[a0:a1, :]) + _dot(yb, w_ref[a1:a2, :])) + _dot(yc, w_ref[a2:, :])
    h_ref[...] = x_ref[...] + y


def _outproj(oa, ob, oc, gain, w_out, x2):
    t = x2.shape[0]
    tm = ROW_TILE
    row = lambda i: (i, 0)
    fixed = lambda i: (0, 0)
    return pl.pallas_call(
        _outproj_kernel,
        grid=(t // tm,),
        in_specs=[
            pl.BlockSpec((tm, SWA_WIDTH), row),
            pl.BlockSpec((tm, FOX_WIDTH), row),
            pl.BlockSpec((tm, MOBA_WIDTH), row),
            pl.BlockSpec((1, MIX_WIDTH), fixed),
            pl.BlockSpec((MIX_WIDTH, D_MODEL), fixed),
            pl.BlockSpec((tm, D_MODEL), row),
        ],
        out_specs=pl.BlockSpec((tm, D_MODEL), row),
        out_shape=jax.ShapeDtypeStruct((t, D_MODEL), F32),
        compiler_params=_params("parallel"),
        name="outproj",
    )(oa, ob, oc, gain, w_out, x2)


def _silu(x):
    return x / (1.0 + jnp.exp(-x))


def _ffn_kernel(h_ref, g_ref, wg_ref, wu_ref, wd_ref, o_ref, acc_scr):
    j = pl.program_id(1)

    @pl.when((pl.program_id(0) == 0) & (j == 0))
    def _():
        acc_scr[...] = jnp.zeros_like(acc_scr)

    x = h_ref[...]
    inv = lax.rsqrt(jnp.mean(x * x, axis=-1, keepdims=True) + RMS_EPS)
    z = (x * inv * g_ref[...]).astype(BF16)
    a = (_silu(_dot(z, wg_ref[...].astype(BF16))) * _dot(z, wu_ref[...].astype(BF16))).astype(BF16)
    total = acc_scr[...] + _dot(a, wd_ref[...].astype(BF16))
    o_ref[...] = x + total
    acc_scr[...] = jnp.where(j == pl.num_programs(1) - 1, 0.0, total)


def _dense_ffn(h2, norm_g, w_gate, w_up, w_down):
    t = h2.shape[0]
    tm, tf = ROW_TILE, FF_TILE
    return pl.pallas_call(
        _ffn_kernel,
        grid=(t // tm, D_FF // tf),
        in_specs=[
            pl.BlockSpec((tm, D_MODEL), lambda i, j: (i, 0)),
            pl.BlockSpec((1, D_MODEL), lambda i, j: (0, 0)),
            pl.BlockSpec((D_MODEL, tf), lambda i, j: (0, j)),
            pl.BlockSpec((D_MODEL, tf), lambda i, j: (0, j)),
            pl.BlockSpec((tf, D_MODEL), lambda i, j: (j, 0)),
        ],
        out_specs=pl.BlockSpec((tm, D_MODEL), lambda i, j: (i, 0)),
        out_shape=jax.ShapeDtypeStruct((t, D_MODEL), F32),
        scratch_shapes=[pltpu.VMEM((tm, D_MODEL), F32)],
        compiler_params=_params("arbitrary", "arbitrary"),
        name="dense_ffn",
    )(h2, norm_g, w_gate, w_up, w_down)


def _router_kernel(h_ref, g_ref, wr_ref, z_ref, idx_ref, gate_ref, rank_ref, count_ref, count_scr):
    tm = h_ref.shape[0]
    x = h_ref[...]
    inv = lax.rsqrt(jnp.mean(x * x, axis=-1, keepdims=True) + RMS_EPS)
    z = x * inv * g_ref[...]
    for c in range(CHUNKS):
        z_ref[pl.ds(c, tm, stride=CHUNKS), :] = z[:, c * LANES:(c + 1) * LANES]
    zh, zm, _ = _split3(z)
    wh, wm = wr_ref[0], wr_ref[1]
    logits = _dot(zh, wh) + (_dot(zh, wm) + _dot(zm, wh))
    lane = lax.broadcasted_iota(jnp.int32, (tm, LANES), 1)
    logits = jnp.where(lane < N_EXPERTS, logits, -jnp.inf)
    m1 = jnp.max(logits, axis=1, keepdims=True)
    i1 = jnp.min(jnp.where(logits == m1, lane, LANES), axis=1, keepdims=True)
    rest = jnp.where(lane == i1, -jnp.inf, logits)
    m2 = jnp.max(rest, axis=1, keepdims=True)
    i2 = jnp.min(jnp.where(rest == m2, lane, LANES), axis=1, keepdims=True)
    e2 = jnp.exp(m2 - m1)
    g1 = 1.0 / (1.0 + e2)
    idx_ref[...] = jnp.where(lane == 0, i1, i2)
    gate_ref[...] = jnp.where(lane == 0, g1, e2 * g1)

    @pl.when(pl.program_id(0) == 0)
    def _():
        count_scr[...] = jnp.zeros_like(count_scr)

    chosen = ((lane == i1) | (lane == i2)).astype(F32)
    r = lax.broadcasted_iota(jnp.int32, (tm, tm), 0)
    c = lax.broadcasted_iota(jnp.int32, (tm, tm), 1)
    before = jnp.where(c < r, 1.0, 0.0).astype(BF16)
    earlier = _dot(before, chosen.astype(BF16)) + count_scr[...]
    r1 = jnp.sum(jnp.where(lane == i1, earlier, 0.0), axis=1, keepdims=True)
    r2 = jnp.sum(jnp.where(lane == i2, earlier, 0.0), axis=1, keepdims=True)
    rank_ref[...] = jnp.where(lane == 0, r1, r2).astype(jnp.int32)
    count_scr[...] += jnp.sum(chosen, axis=0, keepdims=True)
    count_ref[...] = count_scr[...].astype(jnp.int32)


def _router(h2, norm_g, wr3):
    t = h2.shape[0]
    tm = ROUTE_TILE
    row = lambda i: (i, 0)
    return pl.pallas_call(
        _router_kernel,
        grid=(t // tm,),
        in_specs=[
            pl.BlockSpec((tm, D_MODEL), row),
            pl.BlockSpec((1, D_MODEL), lambda i: (0, 0)),
            pl.BlockSpec((2, D_MODEL, LANES), lambda i: (0, 0, 0)),
        ],
        out_specs=[pl.BlockSpec((tm * CHUNKS, LANES), row),
                   pl.BlockSpec((tm, LANES), row),
                   pl.BlockSpec((tm, LANES), row),
                   pl.BlockSpec((tm, LANES), row),
                   pl.BlockSpec((1, LANES), lambda i: (0, 0))],
        out_shape=[jax.ShapeDtypeStruct((t * CHUNKS, LANES), F32),
                   jax.ShapeDtypeStruct((t, LANES), jnp.int32),
                   jax.ShapeDtypeStruct((t, LANES), F32),
                   jax.ShapeDtypeStruct((t, LANES), jnp.int32),
                   jax.ShapeDtypeStruct((1, LANES), jnp.int32)],
        scratch_shapes=[pltpu.VMEM((1, LANES), F32)],
        compiler_params=_params("arbitrary"),
        name="router",
    )(h2, norm_g, wr3)


def _row_copy(src, src_row, dst, dst_row, sem):
    return pltpu.make_async_copy(src.at[pl.ds(src_row * CHUNKS, CHUNKS), :],
                                 dst.at[pl.ds(dst_row * CHUNKS, CHUNKS), :], sem)


def _rows_wait(src, dst, rows, sem):
    pltpu.make_async_copy(src.at[pl.ds(0, rows * CHUNKS), :],
                          dst.at[pl.ds(0, rows * CHUNKS), :], sem).wait()


def _dispatch_kernel(dest_ref, fill_ref, z_ref, x_hbm, zero_scr, fsem, sem):
    i = pl.program_id(0)
    blk = MOE_BLOCK
    tile = z_ref.shape[0] // CHUNKS

    @pl.when(i == 0)
    def _():
        zero_scr[...] = jnp.zeros_like(zero_scr)
        for e in range(fill_ref.shape[0]):
            @pl.when(fill_ref[e] >= 0)
            def _():
                start = pl.multiple_of(fill_ref[e] * CHUNKS, blk * CHUNKS)
                pltpu.make_async_copy(zero_scr, x_hbm.at[pl.ds(start, blk * CHUNKS), :], fsem).start()
        for e in range(fill_ref.shape[0]):
            @pl.when(fill_ref[e] >= 0)
            def _():
                _rows_wait(zero_scr, x_hbm, blk, fsem)

    base = i * tile * TOP_K

    tokens_per_trip = MOE_ISSUE_UNROLL // TOP_K

    def trip(it, c):
        for k in range(MOE_ISSUE_UNROLL):
            token = it * tokens_per_trip + k // TOP_K
            slot = dest_ref[base + it * MOE_ISSUE_UNROLL + k]
            _row_copy(z_ref, token, x_hbm, slot, sem).start(priority=k % 2)
        return c
    lax.fori_loop(0, tile // tokens_per_trip, trip, 0)
    for _ in range(TOP_K):
        _rows_wait(z_ref, x_hbm, tile, sem)


def _dispatch(z8, dest, fill, p_rows):
    tile = DISPATCH_TILE
    assert z8.shape[0] % (tile * CHUNKS) == 0
    grid_spec = pltpu.PrefetchScalarGridSpec(
        num_scalar_prefetch=2,
        grid=(z8.shape[0] // (tile * CHUNKS),),
        in_specs=[pl.BlockSpec((tile * CHUNKS, LANES), lambda i, dd, f: (i, 0))],
        out_specs=pl.BlockSpec(memory_space=pl.ANY),
        scratch_shapes=[pltpu.VMEM((MOE_BLOCK * CHUNKS, LANES), F32),
                        pltpu.SemaphoreType.DMA(()), pltpu.SemaphoreType.DMA(())],
    )
    return pl.pallas_call(
        _dispatch_kernel,
        grid_spec=grid_spec,
        out_shape=jax.ShapeDtypeStruct((p_rows * CHUNKS, LANES), F32),
        compiler_params=_params("arbitrary"),
        name="moe_dispatch",
    )(dest, fill, z8)


def _moe_kernel(be_ref, live_ref, bx_ref, x_ref, wg_ref, wu_ref, wd_ref, y_ref, acc):
    i = pl.program_id(0)
    j = pl.program_id(1)
    tm = acc.shape[0]

    @pl.when((i == 0) & (j == 0))
    def _():
        acc[...] = jnp.zeros_like(acc)

    @pl.when((live_ref[i] == 0) & (j == 0))
    def _():
        y_ref[...] = jnp.zeros_like(y_ref)

    @pl.when(live_ref[i] > 0)
    def _():
        x = jnp.concatenate([x_ref[pl.ds(c, tm, stride=CHUNKS), :].astype(BF16)
                             for c in range(CHUNKS)], axis=1)
        a = (_silu(_dot(x, wg_ref[0])) * _dot(x, wu_ref[0])).astype(BF16)
        total = acc[...] + _dot(a, wd_ref[0])
        for c in range(CHUNKS):
            y_ref[pl.ds(c, tm, stride=CHUNKS), :] = total[:, c * LANES:(c + 1) * LANES]
        acc[...] = jnp.where(j == pl.num_programs(1) - 1, 0.0, total)


def _moe_experts(x8, blk_e, blk_live, blk_x, w_gate, w_up, w_down):
    tm, tf = MOE_BLOCK, MOE_FF_TILE
    nblk = blk_e.shape[0]
    nff = D_FF // tf
    ff = lambda i, j, live: jnp.where(live[i] > 0, j, nff - 1)
    grid_spec = pltpu.PrefetchScalarGridSpec(
        num_scalar_prefetch=3,
        grid=(nblk, nff),
        in_specs=[
            pl.BlockSpec((tm * CHUNKS, LANES), lambda i, j, be, lv, bx: (bx[i], 0)),
            pl.BlockSpec((1, D_MODEL, tf), lambda i, j, be, lv, bx: (be[i], 0, ff(i, j, lv))),
            pl.BlockSpec((1, D_MODEL, tf), lambda i, j, be, lv, bx: (be[i], 0, ff(i, j, lv))),
            pl.BlockSpec((1, tf, D_MODEL), lambda i, j, be, lv, bx: (be[i], ff(i, j, lv), 0)),
        ],
        out_specs=pl.BlockSpec((tm * CHUNKS, LANES), lambda i, j, be, lv, bx: (i, 0)),
        scratch_shapes=[pltpu.VMEM((tm, D_MODEL), F32)],
    )
    return pl.pallas_call(
        _moe_kernel,
        grid_spec=grid_spec,
        out_shape=jax.ShapeDtypeStruct(x8.shape, F32),
        compiler_params=_params("arbitrary", "arbitrary"),
        name="moe_experts",
    )(blk_e, blk_live, blk_x, x8, w_gate, w_up, w_down)


def _combine_kernel(dest_ref, h_ref, gate_ref, *rest, final_norm):
    if final_norm:
        g_ref, y_hbm, o_ref, yb_even, yb_odd, sem = rest
    else:
        y_hbm, o_ref, yb_even, yb_odd, sem = rest
    i = pl.program_id(0)
    last = pl.num_programs(0) - 1
    tm = h_ref.shape[0]
    rows = tm * TOP_K
    stride = TOP_K * CHUNKS

    @pl.when(i == 0)
    def _():
        def trip(it, c):
            for k in range(MOE_ISSUE_UNROLL):
                r = it * MOE_ISSUE_UNROLL + k
                _row_copy(y_hbm, dest_ref[r], yb_even, r, sem.at[0]).start(priority=k % 2)
            return c
        lax.fori_loop(0, rows // MOE_ISSUE_UNROLL, trip, 0)

    def step(parity, yb, yb_next):
        _rows_wait(y_hbm, yb, rows, sem.at[parity])
        nxt = jnp.minimum(i + 1, last)
        for r in range(rows):
            copy = _row_copy(y_hbm, dest_ref[nxt * rows + r], yb_next, r, sem.at[1 - parity])
            copy.start(priority=r % 2)
        gates = [gate_ref[:, k:k + 1] for k in range(TOP_K)]
        parts = []
        ss = jnp.zeros((tm, 1), F32)
        for c in range(CHUNKS):
            y = gates[0] * yb[pl.ds(c, tm, stride=stride), :]
            for k in range(1, TOP_K):
                y = y + gates[k] * yb[pl.ds(k * CHUNKS + c, tm, stride=stride), :]
            hc = h_ref[:, c * LANES:(c + 1) * LANES] + y
            parts.append(hc)
            ss = ss + jnp.sum(hc * hc, axis=1, keepdims=True)
        if final_norm:
            inv = lax.rsqrt(ss / D_MODEL + RMS_EPS)
        for c in range(CHUNKS):
            cols = slice(c * LANES, (c + 1) * LANES)
            o_ref[:, cols] = parts[c] * inv * g_ref[:, cols] if final_norm else parts[c]

        @pl.when(i == last)
        def _():
            _rows_wait(y_hbm, yb_next, rows, sem.at[1 - parity])

    @pl.when(i % 2 == 0)
    def _():
        step(0, yb_even, yb_odd)

    @pl.when(i % 2 == 1)
    def _():
        step(1, yb_odd, yb_even)


def _combine(h2, y8, dest, gate, out_g):
    t = h2.shape[0]
    tm = ROUTE_TILE
    final_norm = out_g is not None
    row = lambda i, d: (i, 0)
    in_specs = [pl.BlockSpec((tm, D_MODEL), row), pl.BlockSpec((tm, LANES), row)]
    args = [h2, gate]
    if final_norm:
        in_specs.append(pl.BlockSpec((1, D_MODEL), lambda i, d: (0, 0)))
        args.append(out_g.reshape(1, D_MODEL))
    grid_spec = pltpu.PrefetchScalarGridSpec(
        num_scalar_prefetch=1,
        grid=(t // tm,),
        in_specs=in_specs + [pl.BlockSpec(memory_space=pl.ANY)],
        out_specs=pl.BlockSpec((tm, D_MODEL), row),
        scratch_shapes=[pltpu.VMEM((tm * TOP_K * CHUNKS, LANES), F32),
                        pltpu.VMEM((tm * TOP_K * CHUNKS, LANES), F32),
                        pltpu.SemaphoreType.DMA((2,))],
    )
    return pl.pallas_call(
        functools.partial(_combine_kernel, final_norm=final_norm),
        grid_spec=grid_spec,
        out_shape=jax.ShapeDtypeStruct((t, D_MODEL), F32),
        compiler_params=_params("arbitrary"),
        name="moe_combine",
    )(dest, *args, y8)


def _norm_kernel(h_ref, g_ref, o_ref):
    x = h_ref[...]
    o_ref[...] = x * lax.rsqrt(jnp.mean(x * x, axis=-1, keepdims=True) + RMS_EPS) * g_ref[...]


def _final_norm(h2, norm_g):
    t = h2.shape[0]
    tm = ROUTE_TILE
    return pl.pallas_call(
        _norm_kernel,
        grid=(t // tm,),
        in_specs=[pl.BlockSpec((tm, D_MODEL), lambda i: (i, 0)),
                  pl.BlockSpec((1, D_MODEL), lambda i: (0, 0))],
        out_specs=pl.BlockSpec((tm, D_MODEL), lambda i: (i, 0)),
        out_shape=jax.ShapeDtypeStruct((t, D_MODEL), F32),
        compiler_params=_params("parallel"),
        name="final_norm",
    )(h2, norm_g)


def _dispatch_plan(top_i, rank, counts, t):
    n = t * TOP_K
    tm = MOE_BLOCK
    padded = (counts + tm - 1) // tm * tm
    pends = jnp.cumsum(padded)
    pstarts = pends - padded
    onehot = (top_i[:, :, None] == jnp.arange(N_EXPERTS)[None, None, :]).astype(jnp.int32)
    dest = (jnp.sum(onehot * pstarts[None, None, :], axis=2) + rank).astype(jnp.int32).reshape(n)
    p_rows = -(-n // tm) * tm + N_EXPERTS * tm
    nblk = p_rows // tm
    blk = jnp.arange(nblk, dtype=jnp.int32)
    blk_e = jnp.minimum(jnp.sum((blk * tm)[:, None] >= pends[None, :], axis=1), N_EXPERTS - 1)
    live = blk * tm < pends[-1]
    last_live = jnp.maximum(pends[-1] // tm - 1, 0)
    blk_e = jnp.where(live, blk_e, blk_e[last_live])
    blk_x = jnp.where(live, blk, 0)
    tail = jnp.where(padded > 0, pends - tm, -1)
    spare = pends[-1] + jnp.arange(N_EXPERTS) * tm
    fill = jnp.concatenate([tail, jnp.where(spare < p_rows, spare, -1)])
    i32 = lambda a: a.astype(jnp.int32)
    return dest, i32(fill), i32(blk_e), i32(live), i32(blk_x), p_rows


def _rope_tables(seq):
    inv = 1.0 / (ROPE_THETA ** (jnp.arange(0, HEAD_DIM, 2, dtype=F32) / HEAD_DIM))
    ang = jnp.arange(seq, dtype=F32)[:, None] * inv[None, :]
    cos = jnp.concatenate([jnp.cos(ang)] * 4, axis=-1)
    sin = jnp.concatenate([jnp.sin(ang)] * 4, axis=-1)
    upper = (jnp.arange(LANES) % HEAD_DIM) >= HEAD_DIM // 2
    sa = jnp.where(upper[None, :], sin, 0.0)
    sb = jnp.where(upper[None, :], 0.0, -sin)
    return cos, sa, sb


def _mixer(h2, b, s, norm_g, w_in, forget_bias, sinks, mix_gain, w_out, tables,
           fox_side=None, moba_side=None):
    f0 = SWA_WIDTH + 2 * SWA_KV_WIDTH + 3 * FOX_WIDTH
    f1 = f0 + FOX_HEADS
    assert f0 + 3 * MOBA_WIDTH == W_BLOCKS * LANES
    wt = jnp.swapaxes(w_in, 0, 1)
    w_all = jnp.concatenate(
        [wt[:f0], wt[f1:], jnp.pad(wt[f0:f1], ((0, LANES - FOX_HEADS), (0, 0)))],
        axis=0).astype(BF16)
    fbias = jnp.pad(forget_bias.astype(F32), (0, LANES - FOX_HEADS)).reshape(1, LANES)
    u, lf = _project(h2, norm_g.reshape(1, D_MODEL), w_all, *tables, fbias, s)
    u3 = u.reshape(b, s, U_WIDTH)
    oa = _swa(u3, sinks.astype(F32))
    ob, fox_cast = _fox(u3, lf.reshape(b, s, LANES), fox_side)
    oc, moba_cast = _moba(u3, moba_side)
    t = b * s
    h = _outproj(oa.reshape(t, SWA_WIDTH), ob.reshape(t, FOX_WIDTH), oc.reshape(t, MOBA_WIDTH),
                 mix_gain.astype(F32).reshape(1, MIX_WIDTH), w_out.astype(BF16), h2)
    return h, fox_cast, moba_cast


def _moe_ffn(h2, norm_g, w_router, wg, wu, wd, out_g):
    t = h2.shape[0]
    wr = jnp.pad(w_router.astype(F32), ((0, 0), (0, LANES - N_EXPERTS)))
    wr2 = jnp.stack(_split3(wr)[:2])
    z8, top, gate, rank, counts = _router(h2, norm_g.reshape(1, D_MODEL), wr2)
    dest, fill, blk_e, blk_live, blk_x, p_rows = _dispatch_plan(
        top[:, :TOP_K], rank[:, :TOP_K], counts[0, :N_EXPERTS], t)
    x8 = _dispatch(z8, dest, fill, p_rows)
    y8 = _moe_experts(x8, blk_e, blk_live, blk_x, wg, wu, wd)
    return _combine(h2, y8, dest, gate, out_g)


def kernel(x, attn_norm, w_in, fox_forget_bias, swa_sinks, mix_gain, w_out, ffn_norm,
           dense_w_gate, dense_w_up, dense_w_down, router_w, moe_w_gate, moe_w_up,
           moe_w_down, final_norm):
    b, s, d = x.shape
    depth = attn_norm.shape[0]
    assert d == D_MODEL and s % MOBA_BLOCK == 0 and s % ROW_TILE == 0
    tables = _rope_tables(s)
    h = x.reshape(b * s, d)
    normed = False
    ne, _, ff = moe_w_gate.shape[1:]
    flat = lambda w: w.reshape(w.shape[0] * w.shape[1], w.shape[2])
    wg = wu = None
    for layer in range(depth):
        j = layer // 2
        routed = layer % 2 == 1
        routed_next = layer + 1 < depth and (layer + 1) % 2 == 1
        fox_side = flat(moe_w_down[j]) if routed else (flat(moe_w_gate[(layer + 1) // 2])
                                                      if routed_next else None)
        moba_side = flat(moe_w_up[(layer + 1) // 2]) if routed_next and not routed else None
        h, fox_cast, moba_cast = _mixer(h, b, s, attn_norm[layer], w_in[layer],
                                        fox_forget_bias[layer], swa_sinks[layer], mix_gain[layer],
                                        w_out[layer], tables, fox_side, moba_side)
        last = layer == depth - 1
        if not routed:
            wg, wu = fox_cast, moba_cast
            h = _dense_ffn(h, ffn_norm[layer].reshape(1, d), dense_w_gate[j], dense_w_up[j],
                           dense_w_down[j])
        else:
            h = _moe_ffn(h, ffn_norm[layer], router_w[j], wg.reshape(ne, d, ff),
                         wu.reshape(ne, d, ff), fox_cast.reshape(ne, ff, d),
                         final_norm if last else None)
            normed = last
    if not normed:
        h = _final_norm(h, final_norm.reshape(1, d))
    return h.reshape(b, s, d)
```

```python
import functools

import jax
import jax.numpy as jnp
from jax import lax
from jax.experimental import pallas as pl
from jax.experimental.pallas import tpu as pltpu

F32 = jnp.float32
BF16 = jnp.bfloat16

D_MODEL = 1024
HEAD_DIM = 64
LANES = 128
SWA_Q_HEADS = 8
SWA_KV_HEADS = 2
SWA_WINDOW = 128
FOX_HEADS = 4
MOBA_HEADS = 4
MOBA_BLOCK = 256
MOBA_TOPK = 3
ROPE_THETA = 10000.0
RMS_EPS = 1e-5
D_FF = 3584
N_EXPERTS = 8
TOP_K = 2
MOE_BLOCK = 512
NEG_INF = -1e30
ATTN_SCALE = HEAD_DIM ** -0.5
LOG2E = 1.4426950408889634
Q_SCALE = ATTN_SCALE * LOG2E

SWA_WIDTH = SWA_Q_HEADS * HEAD_DIM
SWA_KV_WIDTH = SWA_KV_HEADS * HEAD_DIM
FOX_WIDTH = FOX_HEADS * HEAD_DIM
MOBA_WIDTH = MOBA_HEADS * HEAD_DIM
MIX_WIDTH = SWA_WIDTH + FOX_WIDTH + MOBA_WIDTH

W_BLOCKS = 18
W_ALL_WIDTH = (W_BLOCKS + 1) * LANES
ROPE_BLOCKS = (0, 1, 2, 3, 4, 12, 13, 14, 15)
Q_BLOCKS = (0, 1, 2, 3, 6, 7, 12, 13)
SWA_KV_BLOCKS = (4, 5)
U_BLOCKS = W_BLOCKS + len(SWA_KV_BLOCKS)
U_WIDTH = U_BLOCKS * LANES

VMEM_LIMIT = 56 * 1024 * 1024

ROW_TILE = 1024
ROUTE_TILE = 512
ATT_TILE = 256
SWA_STEP_BLOCKS = 16
FF_TILE = 512
MOE_FF_TILE = 1792
CHUNKS = D_MODEL // LANES
MOE_ISSUE_UNROLL = 8
DISPATCH_TILE = 2048


def _params(*sem):
    return pltpu.CompilerParams(dimension_semantics=sem, vmem_limit_bytes=VMEM_LIMIT)


def _split3(x):
    hi = x.astype(BF16)
    r1 = x - hi.astype(F32)
    mid = r1.astype(BF16)
    lo = (r1 - mid.astype(F32)).astype(BF16)
    return hi, mid, lo


def _dot_nt(a, b):
    return lax.dot_general(a, b, (((1,), (1,)), ((), ())), preferred_element_type=F32)


def _dot(a, b):
    return jnp.dot(a, b, preferred_element_type=F32)


def _side_cast(w2d, steps, index_map):
    rows, cols = w2d.shape
    assert rows % steps == 0
    spec = pl.BlockSpec((rows // steps, cols), index_map)
    return spec, spec, jax.ShapeDtypeStruct((rows, cols), BF16)


def _with_side_cast(kernel_fn, n_in):
    def body(*refs):
        side_in, out_ref, side_out = refs[n_in:n_in + 3]
        side_out[...] = side_in[...].astype(BF16)
        kernel_fn(*refs[:n_in], out_ref, *refs[n_in + 3:])
    return body


def _proj_kernel(x_ref, g_ref, w_ref, cos_ref, sa_ref, sb_ref, fb_ref, u_ref, lf_ref):
    x = x_ref[...]
    inv = lax.rsqrt(jnp.mean(x * x, axis=-1, keepdims=True) + RMS_EPS)
    h = (x * inv * g_ref[...]).astype(BF16)
    cos = cos_ref[...]
    sa = sa_ref[...]
    sb = sb_ref[...]
    low = lax.broadcasted_iota(jnp.int32, (1, LANES), 1) < HEAD_DIM
    dst = 0
    for c in range(W_BLOCKS // 2):
        acc = _dot_nt(h, w_ref[c * 2 * LANES:(c + 1) * 2 * LANES, :])
        for half in range(2):
            blk = 2 * c + half
            a = acc[:, half * LANES:(half + 1) * LANES]
            if blk in ROPE_BLOCKS:
                a = a * cos + pltpu.roll(a, 32, 1) * sa + pltpu.roll(a, 96, 1) * sb
            if blk in Q_BLOCKS:
                a = a * Q_SCALE
            if blk in SWA_KV_BLOCKS:
                swapped = pltpu.roll(a, HEAD_DIM, 1)
                outs = [jnp.where(low, a, swapped), jnp.where(low, swapped, a)]
            else:
                outs = [a]
            for o in outs:
                u_ref[:, dst * LANES:(dst + 1) * LANES] = o.astype(BF16)
                dst += 1
    assert dst == U_BLOCKS
    f = _dot_nt(h, w_ref[W_BLOCKS * LANES:W_ALL_WIDTH, :]) + fb_ref[...]
    lf_ref[...] = jnp.minimum(f, 0.0) - jnp.log(1.0 + jnp.exp(-jnp.abs(f)))


def _project(x2, norm_g, w_all, cos, sa, sb, fbias, seq):
    t = x2.shape[0]
    tm = ROW_TILE
    nseq = seq // tm
    row = lambda i: (i, 0)
    pos = lambda i: (i % nseq, 0)
    fixed = lambda i: (0, 0)
    return pl.pallas_call(
        _proj_kernel,
        grid=(t // tm,),
        in_specs=[
            pl.BlockSpec((tm, D_MODEL), row),
            pl.BlockSpec((1, D_MODEL), fixed),
            pl.BlockSpec((W_ALL_WIDTH, D_MODEL), fixed),
            pl.BlockSpec((tm, LANES), pos),
            pl.BlockSpec((tm, LANES), pos),
            pl.BlockSpec((tm, LANES), pos),
            pl.BlockSpec((1, LANES), fixed),
        ],
        out_specs=[pl.BlockSpec((tm, U_WIDTH), row), pl.BlockSpec((tm, LANES), row)],
        out_shape=[jax.ShapeDtypeStruct((t, U_WIDTH), BF16),
                   jax.ShapeDtypeStruct((t, LANES), F32)],
        compiler_params=_params("parallel"),
        name="proj",
    )(x2, norm_g, w_all, cos, sa, sb, fbias)


def _swa_kernel(sink_ref, q_ref, kc_ref, kp_ref, vc_ref, vp_ref, o_ref):
    n = pl.program_id(1)
    w = SWA_WINDOW
    group = SWA_Q_HEADS // SWA_KV_HEADS
    lane = lax.broadcasted_iota(jnp.int32, (1, LANES), 1)
    low = lane < HEAD_DIM

    k_all = jnp.concatenate([kp_ref[0], kc_ref[0]], axis=0)
    v_all = jnp.concatenate([vp_ref[0], vc_ref[0]], axis=0)
    qi = lax.broadcasted_iota(jnp.int32, (group * w, 2 * w), 0) % w
    kj = lax.broadcasted_iota(jnp.int32, (group * w, 2 * w), 1)
    window = (kj > qi) & (kj <= qi + w)
    pairs = group // 2
    for a in range(SWA_STEP_BLOCKS):
        rows = slice(a * w, (a + 1) * w)
        valid = (window & ((kj >= w) | (n > 0))) if a == 0 else window
        for g in range(SWA_KV_HEADS):
            k2 = k_all[a * w:(a + 2) * w, g * LANES:(g + 1) * LANES]
            v2 = v_all[a * w:(a + 2) * w, g * LANES:(g + 1) * LANES]
            qs = jnp.concatenate(
                [jnp.where(low if half == 0 else ~low,
                           q_ref[0, rows, (pairs * g + jj) * LANES:(pairs * g + jj + 1) * LANES], 0)
                 for jj in range(pairs) for half in range(2)], axis=0)
            s = jnp.where(valid, _dot_nt(qs, k2), NEG_INF)
            ps, inv = [], []
            for j in range(group):
                sj = s[j * w:(j + 1) * w]
                sink = sink_ref[g * group + j] * LOG2E
                m = jnp.maximum(jnp.max(sj, axis=1, keepdims=True), sink)
                p = jnp.exp2(sj - m)
                inv.append(1.0 / (jnp.sum(p, axis=1, keepdims=True) + jnp.exp2(sink - m)))
                ps.append(p.astype(BF16))
            o = _dot(jnp.concatenate(ps, axis=0), v2)
            outs = [o[j * w:(j + 1) * w] * inv[j] for j in range(group)]
            for jj in range(pairs):
                oj = jnp.where(low, outs[2 * jj], outs[2 * jj + 1])
                cols = slice((pairs * g + jj) * LANES, (pairs * g + jj + 1) * LANES)
                o_ref[0, rows, cols] = oj.astype(BF16)


def _swa(u3, sinks):
    b, s, _ = u3.shape
    w = SWA_WINDOW
    nb = SWA_STEP_BLOCKS
    assert s % (nb * w) == 0
    kv = SWA_KV_HEADS * LANES
    cur = lambda unit: (lambda bi, n, sk: (bi, n, unit))
    prev = lambda unit: (lambda bi, n, sk: (bi, jnp.maximum(n * nb - 1, 0), unit))
    grid_spec = pltpu.PrefetchScalarGridSpec(
        num_scalar_prefetch=1,
        grid=(b, s // (nb * w)),
        in_specs=[
            pl.BlockSpec((1, nb * w, SWA_WIDTH), lambda bi, n, sk: (bi, n, 0)),
            pl.BlockSpec((1, nb * w, kv), cur(2)),
            pl.BlockSpec((1, w, kv), prev(2)),
            pl.BlockSpec((1, nb * w, kv), cur(3)),
            pl.BlockSpec((1, w, kv), prev(3)),
        ],
        out_specs=pl.BlockSpec((1, nb * w, SWA_WIDTH), lambda bi, n, sk: (bi, n, 0)),
    )
    return pl.pallas_call(
        _swa_kernel,
        grid_spec=grid_spec,
        out_shape=jax.ShapeDtypeStruct((b, s, SWA_WIDTH), BF16),
        compiler_params=_params("parallel", "parallel"),
        name="swa",
    )(sinks, u3, u3, u3, u3, u3)


def _head_lanes(hh):
    lane = lax.broadcasted_iota(jnp.int32, (1, LANES), 1)
    if hh == 0:
        return lane < HEAD_DIM, lane - HEAD_DIM
    return lane >= HEAD_DIM, lane


def _causal_attention(qp_scr, kp_scr, v_ref, o_ref, t):
    s_len = qp_scr.shape[1]
    lane = lax.broadcasted_iota(jnp.int32, (1, LANES), 1)
    row = lax.broadcasted_iota(jnp.int32, (t, t), 0)
    col = lax.broadcasted_iota(jnp.int32, (t, t), 1)
    for qi in range(s_len // t):
        rows = slice(qi * t, (qi + 1) * t)
        outs = []
        for hh in range(2):
            q = qp_scr[hh, rows, :]
            sd = jnp.where(col <= row, _dot_nt(q, kp_scr[hh, rows, :]), NEG_INF)
            m = jnp.max(sd, axis=1, keepdims=True)
            if qi:
                sp = _dot_nt(q, kp_scr[hh, :qi * t, :])
                m = jnp.maximum(m, jnp.max(sp, axis=1, keepdims=True))
            pd = jnp.exp2(sd - m)
            l = jnp.sum(pd, axis=1, keepdims=True)
            o = _dot(pd.astype(BF16), v_ref[0, rows, :])
            if qi:
                pp = jnp.exp2(sp - m)
                l = l + jnp.sum(pp, axis=1, keepdims=True)
                o = o + _dot(pp.astype(BF16), v_ref[0, :qi * t, :])
            outs.append(o * (1.0 / l))
        o_ref[0, rows, :] = jnp.where(lane < HEAD_DIM, outs[0], outs[1]).astype(BF16)


def _pair_attention(kernel_fn, u3, extra, q_blk, k_blk, v_blk, n_heads, name, side=None):
    b, s, _ = u3.shape
    pairs = n_heads // 2
    pair = lambda blk: (lambda bi, p: (bi, 0, blk + p))
    seq_block = pl.BlockSpec((1, s, LANES), lambda bi, p: (bi, 0, 0))
    in_specs = [pl.BlockSpec((1, s, LANES), pair(q_blk)),
                pl.BlockSpec((1, s, LANES), pair(k_blk)),
                pl.BlockSpec((1, s, LANES), pair(v_blk))] + [seq_block] * len(extra)
    out_specs = [pl.BlockSpec((1, s, LANES), pair(0))]
    out_shape = [jax.ShapeDtypeStruct((b, s, n_heads * HEAD_DIM), BF16)]
    args = [u3, u3, u3, *extra]
    if side is not None:
        kernel_fn = _with_side_cast(kernel_fn, len(args))
        spec_in, spec_out, shape = _side_cast(side, b * pairs, lambda bi, p: (bi * pairs + p, 0))
        in_specs.append(spec_in)
        out_specs.append(spec_out)
        out_shape.append(shape)
        args.append(side)
    outs = pl.pallas_call(
        kernel_fn,
        grid=(b, pairs),
        in_specs=in_specs,
        out_specs=out_specs,
        out_shape=out_shape,
        scratch_shapes=[pltpu.VMEM((2, s, LANES), BF16), pltpu.VMEM((2, s, LANES), BF16)],
        compiler_params=_params("parallel", "parallel"),
        name=name,
    )(*args)
    return tuple(outs) if side is not None else (outs[0], None)


def _fox_kernel(q_ref, k_ref, v_ref, lf_ref, o_ref, qp_scr, kp_scr):
    p = pl.program_id(1)
    s_len = q_ref.shape[1]
    t = ATT_TILE
    lane = lax.broadcasted_iota(jnp.int32, (1, LANES), 1)
    r = lax.broadcasted_iota(jnp.int32, (t, t), 0)
    cc = lax.broadcasted_iota(jnp.int32, (t, t), 1)
    tri = jnp.where(cc <= r, 1.0, 0.0).astype(BF16)
    carry = jnp.zeros((1, LANES), F32)
    for i in range(s_len // t):
        rows = slice(i * t, (i + 1) * t)
        lf = lf_ref[0, rows, :]
        lf0 = jnp.sum(jnp.where(lane == 2 * p, lf, 0.0), axis=1, keepdims=True)
        lf1 = jnp.sum(jnp.where(lane == 2 * p + 1, lf, 0.0), axis=1, keepdims=True)
        hi, mid, lo = _split3(jnp.where(lane < 3, lf0, jnp.where(lane < 6, lf1, 0.0)))
        terms = jnp.where((lane == 0) | (lane == 3), hi,
                          jnp.where((lane == 1) | (lane == 4), mid, lo))
        sums = _dot(tri, terms) + carry
        carry = sums[t - 1:t, :]
        c0 = jnp.sum(jnp.where(lane < 3, sums, 0.0), axis=1, keepdims=True)
        c1 = jnp.sum(jnp.where((lane >= 3) & (lane < 6), sums, 0.0), axis=1, keepdims=True)
        chi, cmid, clo = _split3(jnp.where(lane < HEAD_DIM, c1, c0) * LOG2E)
        for hh in range(2):
            keep, f = _head_lanes(hh)
            ones_q = jnp.where((f >= 3) & (f < 6), 1.0, 0.0).astype(BF16)
            ones_k = jnp.where((f >= 0) & (f < 3), 1.0, 0.0).astype(BF16)
            qf = jnp.where(f == 0, chi, jnp.where(f == 1, cmid, jnp.where(f == 2, clo, ones_q)))
            kf = jnp.where(f == 3, -chi, jnp.where(f == 4, -cmid, jnp.where(f == 5, -clo, ones_k)))
            qp_scr[hh, rows, :] = jnp.where(keep, q_ref[0, rows, :], qf)
            kp_scr[hh, rows, :] = jnp.where(keep, k_ref[0, rows, :], kf)
    _causal_attention(qp_scr, kp_scr, v_ref, o_ref, t)


def _fox(u3, lf3, side=None):
    return _pair_attention(_fox_kernel, u3, (lf3,), 8, 10, 12, FOX_HEADS, "fox", side)


def _moba_kernel(q_ref, k_ref, v_ref, o_ref, qp_scr, kp_scr):
    s_len = q_ref.shape[1]
    t = MOBA_BLOCK
    nkb = s_len // t
    sub = 8
    assert nkb <= sub
    means = [jnp.mean(k_ref[0, j * t:(j + 1) * t, :].astype(F32), axis=0, keepdims=True)
             for j in range(nkb)]
    km_pair = jnp.concatenate(means + [jnp.zeros((sub - nkb, LANES), F32)], axis=0) \
        if nkb < sub else jnp.concatenate(means, axis=0)
    stack = []
    for hh in range(2):
        keep, _ = _head_lanes(hh)
        stack += [part.astype(F32) for part in _split3(jnp.where(keep, km_pair, 0.0))]
    stack.append(jnp.zeros((LANES - 6 * sub, LANES), F32))
    km_all = jnp.concatenate(stack, axis=0).astype(BF16)
    blk = lax.broadcasted_iota(jnp.int32, (sub, t), 0)
    gates = [_dot_nt(km_all, q_ref[0, i * t:(i + 1) * t, :]) for i in range(nkb)]
    for hh in range(2):
        keep, f = _head_lanes(hh)
        for i in range(nkb):
            rows = slice(i * t, (i + 1) * t)
            q = q_ref[0, rows, :]
            base = 3 * sub * hh
            g = (gates[i][base:base + sub] + gates[i][base + sub:base + 2 * sub]) \
                + gates[i][base + 2 * sub:base + 3 * sub]
            past = blk < i
            g = jnp.where(past, g, NEG_INF)
            rank = jnp.zeros((sub, t), jnp.int32)
            for r in range(i):
                gr = jnp.broadcast_to(g[r:r + 1, :], (sub, t))
                beats = (gr > g) | ((gr == g) & (r < blk))
                rank = rank + beats.astype(jnp.int32)
            drop = jnp.where(past & (rank >= MOBA_TOPK), 1.0, 0.0)
            below = [jnp.zeros((HEAD_DIM, t), F32)] if hh == 0 else []
            above = jnp.zeros((LANES - sub - (HEAD_DIM if hh == 0 else 0), t), F32)
            padded = jnp.concatenate(below + [drop, above], axis=0)
            kf = jnp.where(f == i, NEG_INF, 0.0).astype(BF16)
            qp_scr[hh, rows, :] = jnp.where(keep, q, padded.T.astype(BF16))
            kp_scr[hh, rows, :] = jnp.where(keep, k_ref[0, rows, :], kf)
    _causal_attention(qp_scr, kp_scr, v_ref, o_ref, t)


def _moba(u3, side=None):
    return _pair_attention(_moba_kernel, u3, (), 14, 16, 18, MOBA_HEADS, "moba", side)


def _rms_f32(x):
    x = x.astype(F32)
    return x * lax.rsqrt(jnp.mean(x * x, axis=-1, keepdims=True) + RMS_EPS)


def _outproj_kernel(oa_ref, ob_ref, oc_ref, gain_ref, w_ref, x_ref, h_ref):
    a0, a1, a2 = 0, SWA_WIDTH, SWA_WIDTH + FOX_WIDTH
    ya = (_rms_f32(oa_ref[...]) * gain_ref[:, a0:a1]).astype(BF16)
    yb = (_rms_f32(ob_ref[...]) * gain_ref[:, a1:a2]).astype(BF16)
    yc = (_rms_f32(oc_ref[...]) * gain_ref[:, a2:]).astype(BF16)
    y = (_dot(ya, w_ref[a0:a1, :]) + _dot(yb, w_ref[a1:a2, :])) + _dot(yc, w_ref[a2:, :])
    h_ref[...] = x_ref[...] + y


def _outproj(oa, ob, oc, gain, w_out, x2):
    t = x2.shape[0]
    tm = ROW_TILE
    row = lambda i: (i, 0)
    fixed = lambda i: (0, 0)
    return pl.pallas_call(
        _outproj_kernel,
        grid=(t // tm,),
        in_specs=[
            pl.BlockSpec((tm, SWA_WIDTH), row),
            pl.BlockSpec((tm, FOX_WIDTH), row),
            pl.BlockSpec((tm, MOBA_WIDTH), row),
            pl.BlockSpec((1, MIX_WIDTH), fixed),
            pl.BlockSpec((MIX_WIDTH, D_MODEL), fixed),
            pl.BlockSpec((tm, D_MODEL), row),
        ],
        out_specs=pl.BlockSpec((tm, D_MODEL), row),
        out_shape=jax.ShapeDtypeStruct((t, D_MODEL), F32),
        compiler_params=_params("parallel"),
        name="outproj",
    )(oa, ob, oc, gain, w_out, x2)


def _silu(x):
    return x / (1.0 + jnp.exp(-x))


def _ffn_kernel(h_ref, g_ref, wg_ref, wu_ref, wd_ref, o_ref, acc_scr):
    j = pl.program_id(1)

    @pl.when((pl.program_id(0) == 0) & (j == 0))
    def _():
        acc_scr[...] = jnp.zeros_like(acc_scr)

    x = h_ref[...]
    inv = lax.rsqrt(jnp.mean(x * x, axis=-1, keepdims=True) + RMS_EPS)
    z = (x * inv * g_ref[...]).astype(BF16)
    a = (_silu(_dot(z, wg_ref[...].astype(BF16))) * _dot(z, wu_ref[...].astype(BF16))).astype(BF16)
    total = acc_scr[...] + _dot(a, wd_ref[...].astype(BF16))
    o_ref[...] = x + total
    acc_scr[...] = jnp.where(j == pl.num_programs(1) - 1, 0.0, total)


def _dense_ffn(h2, norm_g, w_gate, w_up, w_down):
    t = h2.shape[0]
    tm, tf = ROW_TILE, FF_TILE
    return pl.pallas_call(
        _ffn_kernel,
        grid=(t // tm, D_FF // tf),
        in_specs=[
            pl.BlockSpec((tm, D_MODEL), lambda i, j: (i, 0)),
            pl.BlockSpec((1, D_MODEL), lambda i, j: (0, 0)),
            pl.BlockSpec((D_MODEL, tf), lambda i, j: (0, j)),
            pl.BlockSpec((D_MODEL, tf), lambda i, j: (0, j)),
            pl.BlockSpec((tf, D_MODEL), lambda i, j: (j, 0)),
        ],
        out_specs=pl.BlockSpec((tm, D_MODEL), lambda i, j: (i, 0)),
        out_shape=jax.ShapeDtypeStruct((t, D_MODEL), F32),
        scratch_shapes=[pltpu.VMEM((tm, D_MODEL), F32)],
        compiler_params=_params("arbitrary", "arbitrary"),
        name="dense_ffn",
    )(h2, norm_g, w_gate, w_up, w_down)


def _router_kernel(h_ref, g_ref, wr_ref, z_ref, idx_ref, gate_ref, rank_ref, count_ref, count_scr):
    tm = h_ref.shape[0]
    x = h_ref[...]
    inv = lax.rsqrt(jnp.mean(x * x, axis=-1, keepdims=True) + RMS_EPS)
    z = x * inv * g_ref[...]
    for c in range(CHUNKS):
        z_ref[pl.ds(c, tm, stride=CHUNKS), :] = z[:, c * LANES:(c + 1) * LANES]
    zh, zm, _ = _split3(z)
    wh, wm = wr_ref[0], wr_ref[1]
    logits = _dot(zh, wh) + (_dot(zh, wm) + _dot(zm, wh))
    lane = lax.broadcasted_iota(jnp.int32, (tm, LANES), 1)
    logits = jnp.where(lane < N_EXPERTS, logits, -jnp.inf)
    m1 = jnp.max(logits, axis=1, keepdims=True)
    i1 = jnp.min(jnp.where(logits == m1, lane, LANES), axis=1, keepdims=True)
    rest = jnp.where(lane == i1, -jnp.inf, logits)
    m2 = jnp.max(rest, axis=1, keepdims=True)
    i2 = jnp.min(jnp.where(rest == m2, lane, LANES), axis=1, keepdims=True)
    e2 = jnp.exp(m2 - m1)
    g1 = 1.0 / (1.0 + e2)
    idx_ref[...] = jnp.where(lane == 0, i1, i2)
    gate_ref[...] = jnp.where(lane == 0, g1, e2 * g1)

    @pl.when(pl.program_id(0) == 0)
    def _():
        count_scr[...] = jnp.zeros_like(count_scr)

    chosen = ((lane == i1) | (lane == i2)).astype(F32)
    r = lax.broadcasted_iota(jnp.int32, (tm, tm), 0)
    c = lax.broadcasted_iota(jnp.int32, (tm, tm), 1)
    before = jnp.where(c < r, 1.0, 0.0).astype(BF16)
    earlier = _dot(before, chosen.astype(BF16)) + count_scr[...]
    r1 = jnp.sum(jnp.where(lane == i1, earlier, 0.0), axis=1, keepdims=True)
    r2 = jnp.sum(jnp.where(lane == i2, earlier, 0.0), axis=1, keepdims=True)
    rank_ref[...] = jnp.where(lane == 0, r1, r2).astype(jnp.int32)
    count_scr[...] += jnp.sum(chosen, axis=0, keepdims=True)
    count_ref[...] = count_scr[...].astype(jnp.int32)


def _router(h2, norm_g, wr3):
    t = h2.shape[0]
    tm = ROUTE_TILE
    row = lambda i: (i, 0)
    return pl.pallas_call(
        _router_kernel,
        grid=(t // tm,),
        in_specs=[
            pl.BlockSpec((tm, D_MODEL), row),
            pl.BlockSpec((1, D_MODEL), lambda i: (0, 0)),
            pl.BlockSpec((2, D_MODEL, LANES), lambda i: (0, 0, 0)),
        ],
        out_specs=[pl.BlockSpec((tm * CHUNKS, LANES), row),
                   pl.BlockSpec((tm, LANES), row),
                   pl.BlockSpec((tm, LANES), row),
                   pl.BlockSpec((tm, LANES), row),
                   pl.BlockSpec((1, LANES), lambda i: (0, 0))],
        out_shape=[jax.ShapeDtypeStruct((t * CHUNKS, LANES), F32),
                   jax.ShapeDtypeStruct((t, LANES), jnp.int32),
                   jax.ShapeDtypeStruct((t, LANES), F32),
                   jax.ShapeDtypeStruct((t, LANES), jnp.int32),
                   jax.ShapeDtypeStruct((1, LANES), jnp.int32)],
        scratch_shapes=[pltpu.VMEM((1, LANES), F32)],
        compiler_params=_params("arbitrary"),
        name="router",
    )(h2, norm_g, wr3)


def _row_copy(src, src_row, dst, dst_row, sem):
    return pltpu.make_async_copy(src.at[pl.ds(src_row * CHUNKS, CHUNKS), :],
                                 dst.at[pl.ds(dst_row * CHUNKS, CHUNKS), :], sem)


def _rows_wait(src, dst, rows, sem):
    pltpu.make_async_copy(src.at[pl.ds(0, rows * CHUNKS), :],
                          dst.at[pl.ds(0, rows * CHUNKS), :], sem).wait()


def _dispatch_kernel(dest_ref, fill_ref, z_ref, x_hbm, zero_scr, fsem, sem):
    i = pl.program_id(0)
    blk = MOE_BLOCK
    tile = z_ref.shape[0] // CHUNKS

    @pl.when(i == 0)
    def _():
        zero_scr[...] = jnp.zeros_like(zero_scr)
        for e in range(fill_ref.shape[0]):
            @pl.when(fill_ref[e] >= 0)
            def _():
                start = pl.multiple_of(fill_ref[e] * CHUNKS, blk * CHUNKS)
                pltpu.make_async_copy(zero_scr, x_hbm.at[pl.ds(start, blk * CHUNKS), :], fsem).start()
        for e in range(fill_ref.shape[0]):
            @pl.when(fill_ref[e] >= 0)
            def _():
                _rows_wait(zero_scr, x_hbm, blk, fsem)

    base = i * tile * TOP_K

    tokens_per_trip = MOE_ISSUE_UNROLL // TOP_K

    def trip(it, c):
        for k in range(MOE_ISSUE_UNROLL):
            token = it * tokens_per_trip + k // TOP_K
            slot = dest_ref[base + it * MOE_ISSUE_UNROLL + k]
            _row_copy(z_ref, token, x_hbm, slot, sem).start(priority=k % 2)
        return c
    lax.fori_loop(0, tile // tokens_per_trip, trip, 0)
    for _ in range(TOP_K):
        _rows_wait(z_ref, x_hbm, tile, sem)


def _dispatch(z8, dest, fill, p_rows):
    tile = DISPATCH_TILE
    assert z8.shape[0] % (tile * CHUNKS) == 0
    grid_spec = pltpu.PrefetchScalarGridSpec(
        num_scalar_prefetch=2,
        grid=(z8.shape[0] // (tile * CHUNKS),),
        in_specs=[pl.BlockSpec((tile * CHUNKS, LANES), lambda i, dd, f: (i, 0))],
        out_specs=pl.BlockSpec(memory_space=pl.ANY),
        scratch_shapes=[pltpu.VMEM((MOE_BLOCK * CHUNKS, LANES), F32),
                        pltpu.SemaphoreType.DMA(()), pltpu.SemaphoreType.DMA(())],
    )
    return pl.pallas_call(
        _dispatch_kernel,
        grid_spec=grid_spec,
        out_shape=jax.ShapeDtypeStruct((p_rows * CHUNKS, LANES), F32),
        compiler_params=_params("arbitrary"),
        name="moe_dispatch",
    )(dest, fill, z8)


def _moe_kernel(be_ref, live_ref, bx_ref, x_ref, wg_ref, wu_ref, wd_ref, y_ref, acc):
    i = pl.program_id(0)
    j = pl.program_id(1)
    tm = acc.shape[0]

    @pl.when((i == 0) & (j == 0))
    def _():
        acc[...] = jnp.zeros_like(acc)

    @pl.when((live_ref[i] == 0) & (j == 0))
    def _():
        y_ref[...] = jnp.zeros_like(y_ref)

    @pl.when(live_ref[i] > 0)
    def _():
        x = jnp.concatenate([x_ref[pl.ds(c, tm, stride=CHUNKS), :].astype(BF16)
                             for c in range(CHUNKS)], axis=1)
        a = (_silu(_dot(x, wg_ref[0])) * _dot(x, wu_ref[0])).astype(BF16)
        total = acc[...] + _dot(a, wd_ref[0])
        for c in range(CHUNKS):
            y_ref[pl.ds(c, tm, stride=CHUNKS), :] = total[:, c * LANES:(c + 1) * LANES]
        acc[...] = jnp.where(j == pl.num_programs(1) - 1, 0.0, total)


def _moe_experts(x8, blk_e, blk_live, blk_x, w_gate, w_up, w_down):
    tm, tf = MOE_BLOCK, MOE_FF_TILE
    nblk = blk_e.shape[0]
    nff = D_FF // tf
    ff = lambda i, j, live: jnp.where(live[i] > 0, j, nff - 1)
    grid_spec = pltpu.PrefetchScalarGridSpec(
        num_scalar_prefetch=3,
        grid=(nblk, nff),
        in_specs=[
            pl.BlockSpec((tm * CHUNKS, LANES), lambda i, j, be, lv, bx: (bx[i], 0)),
            pl.BlockSpec((1, D_MODEL, tf), lambda i, j, be, lv, bx: (be[i], 0, ff(i, j, lv))),
            pl.BlockSpec((1, D_MODEL, tf), lambda i, j, be, lv, bx: (be[i], 0, ff(i, j, lv))),
            pl.BlockSpec((1, tf, D_MODEL), lambda i, j, be, lv, bx: (be[i], ff(i, j, lv), 0)),
        ],
        out_specs=pl.BlockSpec((tm * CHUNKS, LANES), lambda i, j, be, lv, bx: (i, 0)),
        scratch_shapes=[pltpu.VMEM((tm, D_MODEL), F32)],
    )
    return pl.pallas_call(
        _moe_kernel,
        grid_spec=grid_spec,
        out_shape=jax.ShapeDtypeStruct(x8.shape, F32),
        compiler_params=_params("arbitrary", "arbitrary"),
        name="moe_experts",
    )(blk_e, blk_live, blk_x, x8, w_gate, w_up, w_down)


def _combine_kernel(dest_ref, h_ref, gate_ref, *rest, final_norm):
    if final_norm:
        g_ref, y_hbm, o_ref, yb_even, yb_odd, sem = rest
    else:
        y_hbm, o_ref, yb_even, yb_odd, sem = rest
    i = pl.program_id(0)
    last = pl.num_programs(0) - 1
    tm = h_ref.shape[0]
    rows = tm * TOP_K
    stride = TOP_K * CHUNKS

    @pl.when(i == 0)
    def _():
        def trip(it, c):
            for k in range(MOE_ISSUE_UNROLL):
                r = it * MOE_ISSUE_UNROLL + k
                _row_copy(y_hbm, dest_ref[r], yb_even, r, sem.at[0]).start(priority=k % 2)
            return c
        lax.fori_loop(0, rows // MOE_ISSUE_UNROLL, trip, 0)

    def step(parity, yb, yb_next):
        _rows_wait(y_hbm, yb, rows, sem.at[parity])
        nxt = jnp.minimum(i + 1, last)
        for r in range(rows):
            copy = _row_copy(y_hbm, dest_ref[nxt * rows + r], yb_next, r, sem.at[1 - parity])
            copy.start(priority=r % 2)
        gates = [gate_ref[:, k:k + 1] for k in range(TOP_K)]
        parts = []
        ss = jnp.zeros((tm, 1), F32)
        for c in range(CHUNKS):
            y = gates[0] * yb[pl.ds(c, tm, stride=stride), :]
            for k in range(1, TOP_K):
                y = y + gates[k] * yb[pl.ds(k * CHUNKS + c, tm, stride=stride), :]
            hc = h_ref[:, c * LANES:(c + 1) * LANES] + y
            parts.append(hc)
            ss = ss + jnp.sum(hc * hc, axis=1, keepdims=True)
        if final_norm:
            inv = lax.rsqrt(ss / D_MODEL + RMS_EPS)
        for c in range(CHUNKS):
            cols = slice(c * LANES, (c + 1) * LANES)
            o_ref[:, cols] = parts[c] * inv * g_ref[:, cols] if final_norm else parts[c]

        @pl.when(i == last)
        def _():
            _rows_wait(y_hbm, yb_next, rows, sem.at[1 - parity])

    @pl.when(i % 2 == 0)
    def _():
        step(0, yb_even, yb_odd)

    @pl.when(i % 2 == 1)
    def _():
        step(1, yb_odd, yb_even)


def _combine(h2, y8, dest, gate, out_g):
    t = h2.shape[0]
    tm = ROUTE_TILE
    final_norm = out_g is not None
    row = lambda i, d: (i, 0)
    in_specs = [pl.BlockSpec((tm, D_MODEL), row), pl.BlockSpec((tm, LANES), row)]
    args = [h2, gate]
    if final_norm:
        in_specs.append(pl.BlockSpec((1, D_MODEL), lambda i, d: (0, 0)))
        args.append(out_g.reshape(1, D_MODEL))
    grid_spec = pltpu.PrefetchScalarGridSpec(
        num_scalar_prefetch=1,
        grid=(t // tm,),
        in_specs=in_specs + [pl.BlockSpec(memory_space=pl.ANY)],
        out_specs=pl.BlockSpec((tm, D_MODEL), row),
        scratch_shapes=[pltpu.VMEM((tm * TOP_K * CHUNKS, LANES), F32),
                        pltpu.VMEM((tm * TOP_K * CHUNKS, LANES), F32),
                        pltpu.SemaphoreType.DMA((2,))],
    )
    return pl.pallas_call(
        functools.partial(_combine_kernel, final_norm=final_norm),
        grid_spec=grid_spec,
        out_shape=jax.ShapeDtypeStruct((t, D_MODEL), F32),
        compiler_params=_params("arbitrary"),
        name="moe_combine",
    )(dest, *args, y8)


def _norm_kernel(h_ref, g_ref, o_ref):
    x = h_ref[...]
    o_ref[...] = x * lax.rsqrt(jnp.mean(x * x, axis=-1, keepdims=True) + RMS_EPS) * g_ref[...]


def _final_norm(h2, norm_g):
    t = h2.shape[0]
    tm = ROUTE_TILE
    return pl.pallas_call(
        _norm_kernel,
        grid=(t // tm,),
        in_specs=[pl.BlockSpec((tm, D_MODEL), lambda i: (i, 0)),
                  pl.BlockSpec((1, D_MODEL), lambda i: (0, 0))],
        out_specs=pl.BlockSpec((tm, D_MODEL), lambda i: (i, 0)),
        out_shape=jax.ShapeDtypeStruct((t, D_MODEL), F32),
        compiler_params=_params("parallel"),
        name="final_norm",
    )(h2, norm_g)


def _dispatch_plan(top_i, rank, counts, t):
    n = t * TOP_K
    tm = MOE_BLOCK
    padded = (counts + tm - 1) // tm * tm
    pends = jnp.cumsum(padded)
    pstarts = pends - padded
    onehot = (top_i[:, :, None] == jnp.arange(N_EXPERTS)[None, None, :]).astype(jnp.int32)
    dest = (jnp.sum(onehot * pstarts[None, None, :], axis=2) + rank).astype(jnp.int32).reshape(n)
    p_rows = -(-n // tm) * tm + N_EXPERTS * tm
    nblk = p_rows // tm
    blk = jnp.arange(nblk, dtype=jnp.int32)
    blk_e = jnp.minimum(jnp.sum((blk * tm)[:, None] >= pends[None, :], axis=1), N_EXPERTS - 1)
    live = blk * tm < pends[-1]
    last_live = jnp.maximum(pends[-1] // tm - 1, 0)
    blk_e = jnp.where(live, blk_e, blk_e[last_live])
    blk_x = jnp.where(live, blk, 0)
    tail = jnp.where(padded > 0, pends - tm, -1)
    spare = pends[-1] + jnp.arange(N_EXPERTS) * tm
    fill = jnp.concatenate([tail, jnp.where(spare < p_rows, spare, -1)])
    i32 = lambda a: a.astype(jnp.int32)
    return dest, i32(fill), i32(blk_e), i32(live), i32(blk_x), p_rows


def _rope_tables(seq):
    inv = 1.0 / (ROPE_THETA ** (jnp.arange(0, HEAD_DIM, 2, dtype=F32) / HEAD_DIM))
    ang = jnp.arange(seq, dtype=F32)[:, None] * inv[None, :]
    cos = jnp.concatenate([jnp.cos(ang)] * 4, axis=-1)
    sin = jnp.concatenate([jnp.sin(ang)] * 4, axis=-1)
    upper = (jnp.arange(LANES) % HEAD_DIM) >= HEAD_DIM // 2
    sa = jnp.where(upper[None, :], sin, 0.0)
    sb = jnp.where(upper[None, :], 0.0, -sin)
    return cos, sa, sb


def _mixer(h2, b, s, norm_g, w_in, forget_bias, sinks, mix_gain, w_out, tables,
           fox_side=None, moba_side=None):
    f0 = SWA_WIDTH + 2 * SWA_KV_WIDTH + 3 * FOX_WIDTH
    f1 = f0 + FOX_HEADS
    assert f0 + 3 * MOBA_WIDTH == W_BLOCKS * LANES
    wt = jnp.swapaxes(w_in, 0, 1)
    w_all = jnp.concatenate(
        [wt[:f0], wt[f1:], jnp.pad(wt[f0:f1], ((0, LANES - FOX_HEADS), (0, 0)))],
        axis=0).astype(BF16)
    fbias = jnp.pad(forget_bias.astype(F32), (0, LANES - FOX_HEADS)).reshape(1, LANES)
    u, lf = _project(h2, norm_g.reshape(1, D_MODEL), w_all, *tables, fbias, s)
    u3 = u.reshape(b, s, U_WIDTH)
    oa = _swa(u3, sinks.astype(F32))
    ob, fox_cast = _fox(u3, lf.reshape(b, s, LANES), fox_side)
    oc, moba_cast = _moba(u3, moba_side)
    t = b * s
    h = _outproj(oa.reshape(t, SWA_WIDTH), ob.reshape(t, FOX_WIDTH), oc.reshape(t, MOBA_WIDTH),
                 mix_gain.astype(F32).reshape(1, MIX_WIDTH), w_out.astype(BF16), h2)
    return h, fox_cast, moba_cast


def _moe_ffn(h2, norm_g, w_router, wg, wu, wd, out_g):
    t = h2.shape[0]
    wr = jnp.pad(w_router.astype(F32), ((0, 0), (0, LANES - N_EXPERTS)))
    wr2 = jnp.stack(_split3(wr)[:2])
    z8, top, gate, rank, counts = _router(h2, norm_g.reshape(1, D_MODEL), wr2)
    dest, fill, blk_e, blk_live, blk_x, p_rows = _dispatch_plan(
        top[:, :TOP_K], rank[:, :TOP_K], counts[0, :N_EXPERTS], t)
    x8 = _dispatch(z8, dest, fill, p_rows)
    y8 = _moe_experts(x8, blk_e, blk_live, blk_x, wg, wu, wd)
    return _combine(h2, y8, dest, gate, out_g)


def kernel(x, attn_norm, w_in, fox_forget_bias, swa_sinks, mix_gain, w_out, ffn_norm,
           dense_w_gate, dense_w_up, dense_w_down, router_w, moe_w_gate, moe_w_up,
           moe_w_down, final_norm):
    b, s, d = x.shape
    depth = attn_norm.shape[0]
    assert d == D_MODEL and s % MOBA_BLOCK == 0 and s % ROW_TILE == 0
    tables = _rope_tables(s)
    h = x.reshape(b * s, d)
    normed = False
    ne, _, ff = moe_w_gate.shape[1:]
    flat = lambda w: w.reshape(w.shape[0] * w.shape[1], w.shape[2])
    wg = wu = None
    for layer in range(depth):
        j = layer // 2
        routed = layer % 2 == 1
        routed_next = layer + 1 < depth and (layer + 1) % 2 == 1
        fox_side = flat(moe_w_down[j]) if routed else (flat(moe_w_gate[(layer + 1) // 2])
                                                      if routed_next else None)
        moba_side = flat(moe_w_up[(layer + 1) // 2]) if routed_next and not routed else None
        h, fox_cast, moba_cast = _mixer(h, b, s, attn_norm[layer], w_in[layer],
                                        fox_forget_bias[layer], swa_sinks[layer], mix_gain[layer],
                                        w_out[layer], tables, fox_side, moba_side)
        last = layer == depth - 1
        if not routed:
            wg, wu = fox_cast, moba_cast
            h = _dense_ffn(h, ffn_norm[layer].reshape(1, d), dense_w_gate[j], dense_w_up[j],
                           dense_w_down[j])
        else:
            h = _moe_ffn(h, ffn_norm[layer], router_w[j], wg.reshape(ne, d, ff),
                         wu.reshape(ne, d, ff), fox_cast.reshape(ne, ff, d),
                         final_norm if last else None)
            normed = last
    if not normed:
        h = _final_norm(h, final_norm.reshape(1, d))
    return h.reshape(b, s, d)
```

```python
import functools

import jax
import jax.numpy as jnp
from jax import lax
from jax.experimental import pallas as pl
from jax.experimental.pallas import tpu as pltpu

F32 = jnp.float32
BF16 = jnp.bfloat16

D_MODEL = 1024
HEAD_DIM = 64
LANES = 128
SWA_Q_HEADS = 8
SWA_KV_HEADS = 2
SWA_WINDOW = 128
FOX_HEADS = 4
MOBA_HEADS = 4
MOBA_BLOCK = 256
MOBA_TOPK = 3
ROPE_THETA = 10000.0
RMS_EPS = 1e-5
D_FF = 3584
N_EXPERTS = 8
TOP_K = 2
MOE_BLOCK = 512
NEG_INF = -1e30
ATTN_SCALE = HEAD_DIM ** -0.5
LOG2E = 1.4426950408889634
Q_SCALE = ATTN_SCALE * LOG2E

SWA_WIDTH = SWA_Q_HEADS * HEAD_DIM
SWA_KV_WIDTH = SWA_KV_HEADS * HEAD_DIM
FOX_WIDTH = FOX_HEADS * HEAD_DIM
MOBA_WIDTH = MOBA_HEADS * HEAD_DIM
MIX_WIDTH = SWA_WIDTH + FOX_WIDTH + MOBA_WIDTH

W_BLOCKS = 18
W_ALL_WIDTH = (W_BLOCKS + 1) * LANES
ROPE_BLOCKS = (0, 1, 2, 3, 4, 12, 13, 14, 15)
Q_BLOCKS = (0, 1, 2, 3, 6, 7, 12, 13)
SWA_KV_BLOCKS = (4, 5)
U_BLOCKS = W_BLOCKS + len(SWA_KV_BLOCKS)
U_WIDTH = U_BLOCKS * LANES

VMEM_LIMIT = 56 * 1024 * 1024

ROW_TILE = 1024
ROUTE_TILE = 512
ATT_TILE = 256
SWA_STEP_BLOCKS = 16
FF_TILE = 512
MOE_FF_TILE = 1792
CHUNKS = D_MODEL // LANES
MOE_ISSUE_UNROLL = 8
DISPATCH_TILE = 2048


def _params(*sem):
    return pltpu.CompilerParams(dimension_semantics=sem, vmem_limit_bytes=VMEM_LIMIT)


def _split3(x):
    hi = x.astype(BF16)
    r1 = x - hi.astype(F32)
    mid = r1.astype(BF16)
    lo = (r1 - mid.astype(F32)).astype(BF16)
    return hi, mid, lo


def _dot_nt(a, b):
    return lax.dot_general(a, b, (((1,), (1,)), ((), ())), preferred_element_type=F32)


def _dot(a, b):
    return jnp.dot(a, b, preferred_element_type=F32)


def _side_cast(w2d, steps, index_map):
    rows, cols = w2d.shape
    assert rows % steps == 0
    spec = pl.BlockSpec((rows // steps, cols), index_map)
    return spec, spec, jax.ShapeDtypeStruct((rows, cols), BF16)


def _with_side_cast(kernel_fn, n_in):
    def body(*refs):
        side_in, out_ref, side_out = refs[n_in:n_in + 3]
        side_out[...] = side_in[...].astype(BF16)
        kernel_fn(*refs[:n_in], out_ref, *refs[n_in + 3:])
    return body


def _proj_kernel(x_ref, g_ref, w_ref, cos_ref, sa_ref, sb_ref, fb_ref, u_ref, lf_ref):
    x = x_ref[...]
    inv = lax.rsqrt(jnp.mean(x * x, axis=-1, keepdims=True) + RMS_EPS)
    h = (x * inv * g_ref[...]).astype(BF16)
    cos = cos_ref[...]
    sa = sa_ref[...]
    sb = sb_ref[...]
    low = lax.broadcasted_iota(jnp.int32, (1, LANES), 1) < HEAD_DIM
    dst = 0
    for c in range(W_BLOCKS // 2):
        acc = _dot_nt(h, w_ref[c * 2 * LANES:(c + 1) * 2 * LANES, :])
        for half in range(2):
            blk = 2 * c + half
            a = acc[:, half * LANES:(half + 1) * LANES]
            if blk in ROPE_BLOCKS:
                a = a * cos + pltpu.roll(a, 32, 1) * sa + pltpu.roll(a, 96, 1) * sb
            if blk in Q_BLOCKS:
                a = a * Q_SCALE
            if blk in SWA_KV_BLOCKS:
                swapped = pltpu.roll(a, HEAD_DIM, 1)
                outs = [jnp.where(low, a, swapped), jnp.where(low, swapped, a)]
            else:
                outs = [a]
            for o in outs:
                u_ref[:, dst * LANES:(dst + 1) * LANES] = o.astype(BF16)
                dst += 1
    assert dst == U_BLOCKS
    f = _dot_nt(h, w_ref[W_BLOCKS * LANES:W_ALL_WIDTH, :]) + fb_ref[...]
    lf_ref[...] = jnp.minimum(f, 0.0) - jnp.log(1.0 + jnp.exp(-jnp.abs(f)))


def _project(x2, norm_g, w_all, cos, sa, sb, fbias, seq):
    t = x2.shape[0]
    tm = ROW_TILE
    nseq = seq // tm
    row = lambda i: (i, 0)
    pos = lambda i: (i % nseq, 0)
    fixed = lambda i: (0, 0)
    return pl.pallas_call(
        _proj_kernel,
        grid=(t // tm,),
        in_specs=[
            pl.BlockSpec((tm, D_MODEL), row),
            pl.BlockSpec((1, D_MODEL), fixed),
            pl.BlockSpec((W_ALL_WIDTH, D_MODEL), fixed),
            pl.BlockSpec((tm, LANES), pos),
            pl.BlockSpec((tm, LANES), pos),
            pl.BlockSpec((tm, LANES), pos),
            pl.BlockSpec((1, LANES), fixed),
        ],
        out_specs=[pl.BlockSpec((tm, U_WIDTH), row), pl.BlockSpec((tm, LANES), row)],
        out_shape=[jax.ShapeDtypeStruct((t, U_WIDTH), BF16),
                   jax.ShapeDtypeStruct((t, LANES), F32)],
        compiler_params=_params("parallel"),
        name="proj",
    )(x2, norm_g, w_all, cos, sa, sb, fbias)


def _swa_kernel(sink_ref, q_ref, kc_ref, kp_ref, vc_ref, vp_ref, o_ref):
    n = pl.program_id(1)
    w = SWA_WINDOW
    group = SWA_Q_HEADS // SWA_KV_HEADS
    lane = lax.broadcasted_iota(jnp.int32, (1, LANES), 1)
    low = lane < HEAD_DIM

    k_all = jnp.concatenate([kp_ref[0], kc_ref[0]], axis=0)
    v_all = jnp.concatenate([vp_ref[0], vc_ref[0]], axis=0)
    qi = lax.broadcasted_iota(jnp.int32, (group * w, 2 * w), 0) % w
    kj = lax.broadcasted_iota(jnp.int32, (group * w, 2 * w), 1)
    window = (kj > qi) & (kj <= qi + w)
    pairs = group // 2
    for a in range(SWA_STEP_BLOCKS):
        rows = slice(a * w, (a + 1) * w)
        valid = (window & ((kj >= w) | (n > 0))) if a == 0 else window
        for g in range(SWA_KV_HEADS):
            k2 = k_all[a * w:(a + 2) * w, g * LANES:(g + 1) * LANES]
            v2 = v_all[a * w:(a + 2) * w, g * LANES:(g + 1) * LANES]
            qs = jnp.concatenate(
                [jnp.where(low if half == 0 else ~low,
                           q_ref[0, rows, (pairs * g + jj) * LANES:(pairs * g + jj + 1) * LANES], 0)
                 for jj in range(pairs) for half in range(2)], axis=0)
            s = jnp.where(valid, _dot_nt(qs, k2), NEG_INF)
            ps, inv = [], []
            for j in range(group):
                sj = s[j * w:(j + 1) * w]
                sink = sink_ref[g * group + j] * LOG2E
                m = jnp.maximum(jnp.max(sj, axis=1, keepdims=True), sink)
                p = jnp.exp2(sj - m)
                inv.append(1.0 / (jnp.sum(p, axis=1, keepdims=True) + jnp.exp2(sink - m)))
                ps.append(p.astype(BF16))
            o = _dot(jnp.concatenate(ps, axis=0), v2)
            outs = [o[j * w:(j + 1) * w] * inv[j] for j in range(group)]
            for jj in range(pairs):
                oj = jnp.where(low, outs[2 * jj], outs[2 * jj + 1])
                cols = slice((pairs * g + jj) * LANES, (pairs * g + jj + 1) * LANES)
                o_ref[0, rows, cols] = oj.astype(BF16)


def _swa(u3, sinks):
    b, s, _ = u3.shape
    w = SWA_WINDOW
    nb = SWA_STEP_BLOCKS
    assert s % (nb * w) == 0
    kv = SWA_KV_HEADS * LANES
    cur = lambda unit: (lambda bi, n, sk: (bi, n, unit))
    prev = lambda unit: (lambda bi, n, sk: (bi, jnp.maximum(n * nb - 1, 0), unit))
    grid_spec = pltpu.PrefetchScalarGridSpec(
        num_scalar_prefetch=1,
        grid=(b, s // (nb * w)),
        in_specs=[
            pl.BlockSpec((1, nb * w, SWA_WIDTH), lambda bi, n, sk: (bi, n, 0)),
            pl.BlockSpec((1, nb * w, kv), cur(2)),
            pl.BlockSpec((1, w, kv), prev(2)),
            pl.BlockSpec((1, nb * w, kv), cur(3)),
            pl.BlockSpec((1, w, kv), prev(3)),
        ],
        out_specs=pl.BlockSpec((1, nb * w, SWA_WIDTH), lambda bi, n, sk: (bi, n, 0)),
    )
    return pl.pallas_call(
        _swa_kernel,
        grid_spec=grid_spec,
        out_shape=jax.ShapeDtypeStruct((b, s, SWA_WIDTH), BF16),
        compiler_params=_params("parallel", "parallel"),
        name="swa",
    )(sinks, u3, u3, u3, u3, u3)


def _head_lanes(hh):
    lane = lax.broadcasted_iota(jnp.int32, (1, LANES), 1)
    if hh == 0:
        return lane < HEAD_DIM, lane - HEAD_DIM
    return lane >= HEAD_DIM, lane


def _causal_attention(qp_scr, kp_scr, v_ref, o_ref, t):
    s_len = qp_scr.shape[1]
    lane = lax.broadcasted_iota(jnp.int32, (1, LANES), 1)
    row = lax.broadcasted_iota(jnp.int32, (t, t), 0)
    col = lax.broadcasted_iota(jnp.int32, (t, t), 1)
    for qi in range(s_len // t):
        rows = slice(qi * t, (qi + 1) * t)
        outs = []
        for hh in range(2):
            q = qp_scr[hh, rows, :]
            n = (qi + 1) * t
            s = _dot_nt(q, kp_scr[hh, :n, :])
            own = jnp.where(col <= row, s[:, qi * t:], NEG_INF)
            s = jnp.concatenate([s[:, :qi * t], own], axis=1) if qi else own
            m = jnp.max(s, axis=1, keepdims=True)
            p = jnp.exp2(s - m)
            l = jnp.sum(p, axis=1, keepdims=True)
            outs.append(_dot(p.astype(BF16), v_ref[0, :n, :]) * (1.0 / l))
        o_ref[0, rows, :] = jnp.where(lane < HEAD_DIM, outs[0], outs[1]).astype(BF16)


def _pair_attention(kernel_fn, u3, extra, q_blk, k_blk, v_blk, n_heads, name, side=None):
    b, s, _ = u3.shape
    pairs = n_heads // 2
    pair = lambda blk: (lambda bi, p: (bi, 0, blk + p))
    seq_block = pl.BlockSpec((1, s, LANES), lambda bi, p: (bi, 0, 0))
    in_specs = [pl.BlockSpec((1, s, LANES), pair(q_blk)),
                pl.BlockSpec((1, s, LANES), pair(k_blk)),
                pl.BlockSpec((1, s, LANES), pair(v_blk))] + [seq_block] * len(extra)
    out_specs = [pl.BlockSpec((1, s, LANES), pair(0))]
    out_shape = [jax.ShapeDtypeStruct((b, s, n_heads * HEAD_DIM), BF16)]
    args = [u3, u3, u3, *extra]
    if side is not None:
        kernel_fn = _with_side_cast(kernel_fn, len(args))
        spec_in, spec_out, shape = _side_cast(side, b * pairs, lambda bi, p: (bi * pairs + p, 0))
        in_specs.append(spec_in)
        out_specs.append(spec_out)
        out_shape.append(shape)
        args.append(side)
    outs = pl.pallas_call(
        kernel_fn,
        grid=(b, pairs),
        in_specs=in_specs,
        out_specs=out_specs,
        out_shape=out_shape,
        scratch_shapes=[pltpu.VMEM((2, s, LANES), BF16), pltpu.VMEM((2, s, LANES), BF16)],
        compiler_params=_params("parallel", "parallel"),
        name=name,
    )(*args)
    return tuple(outs) if side is not None else (outs[0], None)


def _fox_kernel(q_ref, k_ref, v_ref, lf_ref, o_ref, qp_scr, kp_scr):
    p = pl.program_id(1)
    s_len = q_ref.shape[1]
    t = ATT_TILE
    lane = lax.broadcasted_iota(jnp.int32, (1, LANES), 1)
    r = lax.broadcasted_iota(jnp.int32, (t, t), 0)
    cc = lax.broadcasted_iota(jnp.int32, (t, t), 1)
    tri = jnp.where(cc <= r, 1.0, 0.0).astype(BF16)
    carry = jnp.zeros((1, LANES), F32)
    for i in range(s_len // t):
        rows = slice(i * t, (i + 1) * t)
        lf = lf_ref[0, rows, :]
        lf0 = jnp.sum(jnp.where(lane == 2 * p, lf, 0.0), axis=1, keepdims=True)
        lf1 = jnp.sum(jnp.where(lane == 2 * p + 1, lf, 0.0), axis=1, keepdims=True)
        hi, mid, lo = _split3(jnp.where(lane < 3, lf0, jnp.where(lane < 6, lf1, 0.0)))
        terms = jnp.where((lane == 0) | (lane == 3), hi,
                          jnp.where((lane == 1) | (lane == 4), mid, lo))
        sums = _dot(tri, terms) + carry
        carry = sums[t - 1:t, :]
        c0 = jnp.sum(jnp.where(lane < 3, sums, 0.0), axis=1, keepdims=True)
        c1 = jnp.sum(jnp.where((lane >= 3) & (lane < 6), sums, 0.0), axis=1, keepdims=True)
        chi, cmid, clo = _split3(jnp.where(lane < HEAD_DIM, c1, c0) * LOG2E)
        for hh in range(2):
            keep, f = _head_lanes(hh)
            ones_q = jnp.where((f >= 3) & (f < 6), 1.0, 0.0).astype(BF16)
            ones_k = jnp.where((f >= 0) & (f < 3), 1.0, 0.0).astype(BF16)
            qf = jnp.where(f == 0, chi, jnp.where(f == 1, cmid, jnp.where(f == 2, clo, ones_q)))
            kf = jnp.where(f == 3, -chi, jnp.where(f == 4, -cmid, jnp.where(f == 5, -clo, ones_k)))
            qp_scr[hh, rows, :] = jnp.where(keep, q_ref[0, rows, :], qf)
            kp_scr[hh, rows, :] = jnp.where(keep, k_ref[0, rows, :], kf)
    _causal_attention(qp_scr, kp_scr, v_ref, o_ref, t)


def _fox(u3, lf3, side=None):
    return _pair_attention(_fox_kernel, u3, (lf3,), 8, 10, 12, FOX_HEADS, "fox", side)


def _moba_kernel(q_ref, k_ref, v_ref, o_ref, qp_scr, kp_scr):
    s_len = q_ref.shape[1]
    t = MOBA_BLOCK
    nkb = s_len // t
    sub = 8
    assert nkb <= sub
    means = [jnp.mean(k_ref[0, j * t:(j + 1) * t, :].astype(F32), axis=0, keepdims=True)
             for j in range(nkb)]
    km_pair = jnp.concatenate(means + [jnp.zeros((sub - nkb, LANES), F32)], axis=0) \
        if nkb < sub else jnp.concatenate(means, axis=0)
    stack = []
    for hh in range(2):
        keep, _ = _head_lanes(hh)
        stack += [part.astype(F32) for part in _split3(jnp.where(keep, km_pair, 0.0))]
    stack.append(jnp.zeros((LANES - 6 * sub, LANES), F32))
    km_all = jnp.concatenate(stack, axis=0).astype(BF16)
    blk = lax.broadcasted_iota(jnp.int32, (sub, t), 0)
    gates = [_dot_nt(km_all, q_ref[0, i * t:(i + 1) * t, :]) for i in range(nkb)]
    for hh in range(2):
        keep, f = _head_lanes(hh)
        for i in range(nkb):
            rows = slice(i * t, (i + 1) * t)
            q = q_ref[0, rows, :]
            base = 3 * sub * hh
            g = (gates[i][base:base + sub] + gates[i][base + sub:base + 2 * sub]) \
                + gates[i][base + 2 * sub:base + 3 * sub]
            past = blk < i
            g = jnp.where(past, g, NEG_INF)
            rank = jnp.zeros((sub, t), jnp.int32)
            for r in range(i):
                gr = jnp.broadcast_to(g[r:r + 1, :], (sub, t))
                beats = (gr > g) | ((gr == g) & (r < blk))
                rank = rank + beats.astype(jnp.int32)
            drop = jnp.where(past & (rank >= MOBA_TOPK), 1.0, 0.0)
            below = [jnp.zeros((HEAD_DIM, t), F32)] if hh == 0 else []
            above = jnp.zeros((LANES - sub - (HEAD_DIM if hh == 0 else 0), t), F32)
            padded = jnp.concatenate(below + [drop, above], axis=0)
            kf = jnp.where(f == i, NEG_INF, 0.0).astype(BF16)
            qp_scr[hh, rows, :] = jnp.where(keep, q, padded.T.astype(BF16))
            kp_scr[hh, rows, :] = jnp.where(keep, k_ref[0, rows, :], kf)
    _causal_attention(qp_scr, kp_scr, v_ref, o_ref, t)


def _moba(u3, side=None):
    return _pair_attention(_moba_kernel, u3, (), 14, 16, 18, MOBA_HEADS, "moba", side)


def _rms_f32(x):
    x = x.astype(F32)
    return x * lax.rsqrt(jnp.mean(x * x, axis=-1, keepdims=True) + RMS_EPS)


def _outproj_kernel(oa_ref, ob_ref, oc_ref, gain_ref, w_ref, x_ref, h_ref):
    a0, a1, a2 = 0, SWA_WIDTH, SWA_WIDTH + FOX_WIDTH
    ya = (_rms_f32(oa_ref[...]) * gain_ref[:, a0:a1]).astype(BF16)
    yb = (_rms_f32(ob_ref[...]) * gain_ref[:, a1:a2]).astype(BF16)
    yc = (_rms_f32(oc_ref[...]) * gain_ref[:, a2:]).astype(BF16)
    y = (_dot(ya, w_ref[a0:a1, :]) + _dot(yb, w_ref[a1:a2, :])) + _dot(yc, w_ref[a2:, :])
    h_ref[...] = x_ref[...] + y


def _outproj(oa, ob, oc, gain, w_out, x2):
    t = x2.shape[0]
    tm = ROW_TILE
    row = lambda i: (i, 0)
    fixed = lambda i: (0, 0)
    return pl.pallas_call(
        _outproj_kernel,
        grid=(t // tm,),
        in_specs=[
            pl.BlockSpec((tm, SWA_WIDTH), row),
            pl.BlockSpec((tm, FOX_WIDTH), row),
            pl.BlockSpec((tm, MOBA_WIDTH), row),
            pl.BlockSpec((1, MIX_WIDTH), fixed),
            pl.BlockSpec((MIX_WIDTH, D_MODEL), fixed),
            pl.BlockSpec((tm, D_MODEL), row),
        ],
        out_specs=pl.BlockSpec((tm, D_MODEL), row),
        out_shape=jax.ShapeDtypeStruct((t, D_MODEL), F32),
        compiler_params=_params("parallel"),
        name="outproj",
    )(oa, ob, oc, gain, w_out, x2)


def _silu(x):
    return x / (1.0 + jnp.exp(-x))


def _ffn_kernel(h_ref, g_ref, wg_ref, wu_ref, wd_ref, o_ref, acc_scr):
    j = pl.program_id(1)

    @pl.when((pl.program_id(0) == 0) & (j == 0))
    def _():
        acc_scr[...] = jnp.zeros_like(acc_scr)

    x = h_ref[...]
    inv = lax.rsqrt(jnp.mean(x * x, axis=-1, keepdims=True) + RMS_EPS)
    z = (x * inv * g_ref[...]).astype(BF16)
    a = (_silu(_dot(z, wg_ref[...].astype(BF16))) * _dot(z, wu_ref[...].astype(BF16))).astype(BF16)
    total = acc_scr[...] + _dot(a, wd_ref[...].astype(BF16))
    o_ref[...] = x + total
    acc_scr[...] = jnp.where(j == pl.num_programs(1) - 1, 0.0, total)


def _dense_ffn(h2, norm_g, w_gate, w_up, w_down):
    t = h2.shape[0]
    tm, tf = ROW_TILE, FF_TILE
    return pl.pallas_call(
        _ffn_kernel,
        grid=(t // tm, D_FF // tf),
        in_specs=[
            pl.BlockSpec((tm, D_MODEL), lambda i, j: (i, 0)),
            pl.BlockSpec((1, D_MODEL), lambda i, j: (0, 0)),
            pl.BlockSpec((D_MODEL, tf), lambda i, j: (0, j)),
            pl.BlockSpec((D_MODEL, tf), lambda i, j: (0, j)),
            pl.BlockSpec((tf, D_MODEL), lambda i, j: (j, 0)),
        ],
        out_specs=pl.BlockSpec((tm, D_MODEL), lambda i, j: (i, 0)),
        out_shape=jax.ShapeDtypeStruct((t, D_MODEL), F32),
        scratch_shapes=[pltpu.VMEM((tm, D_MODEL), F32)],
        compiler_params=_params("arbitrary", "arbitrary"),
        name="dense_ffn",
    )(h2, norm_g, w_gate, w_up, w_down)


def _router_kernel(h_ref, g_ref, wr_ref, z_ref, idx_ref, gate_ref, rank_ref, count_ref, count_scr):
    tm = h_ref.shape[0]
    x = h_ref[...]
    inv = lax.rsqrt(jnp.mean(x * x, axis=-1, keepdims=True) + RMS_EPS)
    z = x * inv * g_ref[...]
    for c in range(CHUNKS):
        z_ref[pl.ds(c, tm, stride=CHUNKS), :] = z[:, c * LANES:(c + 1) * LANES]
    zh, zm, _ = _split3(z)
    wh, wm = wr_ref[0], wr_ref[1]
    logits = _dot(zh, wh) + (_dot(zh, wm) + _dot(zm, wh))
    lane = lax.broadcasted_iota(jnp.int32, (tm, LANES), 1)
    logits = jnp.where(lane < N_EXPERTS, logits, -jnp.inf)
    m1 = jnp.max(logits, axis=1, keepdims=True)
    i1 = jnp.min(jnp.where(logits == m1, lane, LANES), axis=1, keepdims=True)
    rest = jnp.where(lane == i1, -jnp.inf, logits)
    m2 = jnp.max(rest, axis=1, keepdims=True)
    i2 = jnp.min(jnp.where(rest == m2, lane, LANES), axis=1, keepdims=True)
    e2 = jnp.exp(m2 - m1)
    g1 = 1.0 / (1.0 + e2)
    idx_ref[...] = jnp.where(lane == 0, i1, i2)
    gate_ref[...] = jnp.where(lane == 0, g1, e2 * g1)

    @pl.when(pl.program_id(0) == 0)
    def _():
        count_scr[...] = jnp.zeros_like(count_scr)

    chosen = ((lane == i1) | (lane == i2)).astype(F32)
    r = lax.broadcasted_iota(jnp.int32, (tm, tm), 0)
    c = lax.broadcasted_iota(jnp.int32, (tm, tm), 1)
    before = jnp.where(c < r, 1.0, 0.0).astype(BF16)
    earlier = _dot(before, chosen.astype(BF16)) + count_scr[...]
    r1 = jnp.sum(jnp.where(lane == i1, earlier, 0.0), axis=1, keepdims=True)
    r2 = jnp.sum(jnp.where(lane == i2, earlier, 0.0), axis=1, keepdims=True)
    rank_ref[...] = jnp.where(lane == 0, r1, r2).astype(jnp.int32)
    count_scr[...] += jnp.sum(chosen, axis=0, keepdims=True)
    count_ref[...] = count_scr[...].astype(jnp.int32)


def _router(h2, norm_g, wr3):
    t = h2.shape[0]
    tm = ROUTE_TILE
    row = lambda i: (i, 0)
    return pl.pallas_call(
        _router_kernel,
        grid=(t // tm,),
        in_specs=[
            pl.BlockSpec((tm, D_MODEL), row),
            pl.BlockSpec((1, D_MODEL), lambda i: (0, 0)),
            pl.BlockSpec((2, D_MODEL, LANES), lambda i: (0, 0, 0)),
        ],
        out_specs=[pl.BlockSpec((tm * CHUNKS, LANES), row),
                   pl.BlockSpec((tm, LANES), row),
                   pl.BlockSpec((tm, LANES), row),
                   pl.BlockSpec((tm, LANES), row),
                   pl.BlockSpec((1, LANES), lambda i: (0, 0))],
        out_shape=[jax.ShapeDtypeStruct((t * CHUNKS, LANES), F32),
                   jax.ShapeDtypeStruct((t, LANES), jnp.int32),
                   jax.ShapeDtypeStruct((t, LANES), F32),
                   jax.ShapeDtypeStruct((t, LANES), jnp.int32),
                   jax.ShapeDtypeStruct((1, LANES), jnp.int32)],
        scratch_shapes=[pltpu.VMEM((1, LANES), F32)],
        compiler_params=_params("arbitrary"),
        name="router",
    )(h2, norm_g, wr3)


def _row_copy(src, src_row, dst, dst_row, sem):
    return pltpu.make_async_copy(src.at[pl.ds(src_row * CHUNKS, CHUNKS), :],
                                 dst.at[pl.ds(dst_row * CHUNKS, CHUNKS), :], sem)


def _rows_wait(src, dst, rows, sem):
    pltpu.make_async_copy(src.at[pl.ds(0, rows * CHUNKS), :],
                          dst.at[pl.ds(0, rows * CHUNKS), :], sem).wait()


def _dispatch_kernel(dest_ref, fill_ref, z_ref, x_hbm, zero_scr, fsem, sem):
    i = pl.program_id(0)
    blk = MOE_BLOCK
    tile = z_ref.shape[0] // CHUNKS

    @pl.when(i == 0)
    def _():
        zero_scr[...] = jnp.zeros_like(zero_scr)
        for e in range(fill_ref.shape[0]):
            @pl.when(fill_ref[e] >= 0)
            def _():
                start = pl.multiple_of(fill_ref[e] * CHUNKS, blk * CHUNKS)
                pltpu.make_async_copy(zero_scr, x_hbm.at[pl.ds(start, blk * CHUNKS), :], fsem).start()
        for e in range(fill_ref.shape[0]):
            @pl.when(fill_ref[e] >= 0)
            def _():
                _rows_wait(zero_scr, x_hbm, blk, fsem)

    base = i * tile * TOP_K

    tokens_per_trip = MOE_ISSUE_UNROLL // TOP_K

    def trip(it, c):
        for k in range(MOE_ISSUE_UNROLL):
            token = it * tokens_per_trip + k // TOP_K
            slot = dest_ref[base + it * MOE_ISSUE_UNROLL + k]
            _row_copy(z_ref, token, x_hbm, slot, sem).start(priority=k % 2)
        return c
    lax.fori_loop(0, tile // tokens_per_trip, trip, 0)
    for _ in range(TOP_K):
        _rows_wait(z_ref, x_hbm, tile, sem)


def _dispatch(z8, dest, fill, p_rows):
    tile = DISPATCH_TILE
    assert z8.shape[0] % (tile * CHUNKS) == 0
    grid_spec = pltpu.PrefetchScalarGridSpec(
        num_scalar_prefetch=2,
        grid=(z8.shape[0] // (tile * CHUNKS),),
        in_specs=[pl.BlockSpec((tile * CHUNKS, LANES), lambda i, dd, f: (i, 0))],
        out_specs=pl.BlockSpec(memory_space=pl.ANY),
        scratch_shapes=[pltpu.VMEM((MOE_BLOCK * CHUNKS, LANES), F32),
                        pltpu.SemaphoreType.DMA(()), pltpu.SemaphoreType.DMA(())],
    )
    return pl.pallas_call(
        _dispatch_kernel,
        grid_spec=grid_spec,
        out_shape=jax.ShapeDtypeStruct((p_rows * CHUNKS, LANES), F32),
        compiler_params=_params("arbitrary"),
        name="moe_dispatch",
    )(dest, fill, z8)


def _moe_kernel(be_ref, live_ref, bx_ref, x_ref, wg_ref, wu_ref, wd_ref, y_ref, acc):
    i = pl.program_id(0)
    j = pl.program_id(1)
    tm = acc.shape[0]

    @pl.when((i == 0) & (j == 0))
    def _():
        acc[...] = jnp.zeros_like(acc)

    @pl.when((live_ref[i] == 0) & (j == 0))
    def _():
        y_ref[...] = jnp.zeros_like(y_ref)

    @pl.when(live_ref[i] > 0)
    def _():
        x = jnp.concatenate([x_ref[pl.ds(c, tm, stride=CHUNKS), :].astype(BF16)
                             for c in range(CHUNKS)], axis=1)
        a = (_silu(_dot(x, wg_ref[0])) * _dot(x, wu_ref[0])).astype(BF16)
        total = acc[...] + _dot(a, wd_ref[0])
        for c in range(CHUNKS):
            y_ref[pl.ds(c, tm, stride=CHUNKS), :] = total[:, c * LANES:(c + 1) * LANES]
        acc[...] = jnp.where(j == pl.num_programs(1) - 1, 0.0, total)


def _moe_experts(x8, blk_e, blk_live, blk_x, w_gate, w_up, w_down):
    tm, tf = MOE_BLOCK, MOE_FF_TILE
    nblk = blk_e.shape[0]
    nff = D_FF // tf
    ff = lambda i, j, live: jnp.where(live[i] > 0, j, nff - 1)
    grid_spec = pltpu.PrefetchScalarGridSpec(
        num_scalar_prefetch=3,
        grid=(nblk, nff),
        in_specs=[
            pl.BlockSpec((tm * CHUNKS, LANES), lambda i, j, be, lv, bx: (bx[i], 0)),
            pl.BlockSpec((1, D_MODEL, tf), lambda i, j, be, lv, bx: (be[i], 0, ff(i, j, lv))),
            pl.BlockSpec((1, D_MODEL, tf), lambda i, j, be, lv, bx: (be[i], 0, ff(i, j, lv))),
            pl.BlockSpec((1, tf, D_MODEL), lambda i, j, be, lv, bx: (be[i], ff(i, j, lv), 0)),
        ],
        out_specs=pl.BlockSpec((tm * CHUNKS, LANES), lambda i, j, be, lv, bx: (i, 0)),
        scratch_shapes=[pltpu.VMEM((tm, D_MODEL), F32)],
    )
    return pl.pallas_call(
        _moe_kernel,
        grid_spec=grid_spec,
        out_shape=jax.ShapeDtypeStruct(x8.shape, F32),
        compiler_params=_params("arbitrary", "arbitrary"),
        name="moe_experts",
    )(blk_e, blk_live, blk_x, x8, w_gate, w_up, w_down)


def _combine_kernel(dest_ref, h_ref, gate_ref, *rest, final_norm):
    if final_norm:
        g_ref, y_hbm, o_ref, yb_even, yb_odd, sem = rest
    else:
        y_hbm, o_ref, yb_even, yb_odd, sem = rest
    i = pl.program_id(0)
    last = pl.num_programs(0) - 1
    tm = h_ref.shape[0]
    rows = tm * TOP_K
    stride = TOP_K * CHUNKS

    @pl.when(i == 0)
    def _():
        def trip(it, c):
            for k in range(MOE_ISSUE_UNROLL):
                r = it * MOE_ISSUE_UNROLL + k
                _row_copy(y_hbm, dest_ref[r], yb_even, r, sem.at[0]).start(priority=k % 2)
            return c
        lax.fori_loop(0, rows // MOE_ISSUE_UNROLL, trip, 0)

    def step(parity, yb, yb_next):
        _rows_wait(y_hbm, yb, rows, sem.at[parity])
        nxt = jnp.minimum(i + 1, last)
        for r in range(rows):
            copy = _row_copy(y_hbm, dest_ref[nxt * rows + r], yb_next, r, sem.at[1 - parity])
            copy.start(priority=r % 2)
        gates = [gate_ref[:, k:k + 1] for k in range(TOP_K)]
        parts = []
        ss = jnp.zeros((tm, 1), F32)
        for c in range(CHUNKS):
            y = gates[0] * yb[pl.ds(c, tm, stride=stride), :]
            for k in range(1, TOP_K):
                y = y + gates[k] * yb[pl.ds(k * CHUNKS + c, tm, stride=stride), :]
            hc = h_ref[:, c * LANES:(c + 1) * LANES] + y
            parts.append(hc)
            ss = ss + jnp.sum(hc * hc, axis=1, keepdims=True)
        if final_norm:
            inv = lax.rsqrt(ss / D_MODEL + RMS_EPS)
        for c in range(CHUNKS):
            cols = slice(c * LANES, (c + 1) * LANES)
            o_ref[:, cols] = parts[c] * inv * g_ref[:, cols] if final_norm else parts[c]

        @pl.when(i == last)
        def _():
            _rows_wait(y_hbm, yb_next, rows, sem.at[1 - parity])

    @pl.when(i % 2 == 0)
    def _():
        step(0, yb_even, yb_odd)

    @pl.when(i % 2 == 1)
    def _():
        step(1, yb_odd, yb_even)


def _combine(h2, y8, dest, gate, out_g):
    t = h2.shape[0]
    tm = ROUTE_TILE
    final_norm = out_g is not None
    row = lambda i, d: (i, 0)
    in_specs = [pl.BlockSpec((tm, D_MODEL), row), pl.BlockSpec((tm, LANES), row)]
    args = [h2, gate]
    if final_norm:
        in_specs.append(pl.BlockSpec((1, D_MODEL), lambda i, d: (0, 0)))
        args.append(out_g.reshape(1, D_MODEL))
    grid_spec = pltpu.PrefetchScalarGridSpec(
        num_scalar_prefetch=1,
        grid=(t // tm,),
        in_specs=in_specs + [pl.BlockSpec(memory_space=pl.ANY)],
        out_specs=pl.BlockSpec((tm, D_MODEL), row),
        scratch_shapes=[pltpu.VMEM((tm * TOP_K * CHUNKS, LANES), F32),
                        pltpu.VMEM((tm * TOP_K * CHUNKS, LANES), F32),
                        pltpu.SemaphoreType.DMA((2,))],
    )
    return pl.pallas_call(
        functools.partial(_combine_kernel, final_norm=final_norm),
        grid_spec=grid_spec,
        out_shape=jax.ShapeDtypeStruct((t, D_MODEL), F32),
        compiler_params=_params("arbitrary"),
        name="moe_combine",
    )(dest, *args, y8)


def _norm_kernel(h_ref, g_ref, o_ref):
    x = h_ref[...]
    o_ref[...] = x * lax.rsqrt(jnp.mean(x * x, axis=-1, keepdims=True) + RMS_EPS) * g_ref[...]


def _final_norm(h2, norm_g):
    t = h2.shape[0]
    tm = ROUTE_TILE
    return pl.pallas_call(
        _norm_kernel,
        grid=(t // tm,),
        in_specs=[pl.BlockSpec((tm, D_MODEL), lambda i: (i, 0)),
                  pl.BlockSpec((1, D_MODEL), lambda i: (0, 0))],
        out_specs=pl.BlockSpec((tm, D_MODEL), lambda i: (i, 0)),
        out_shape=jax.ShapeDtypeStruct((t, D_MODEL), F32),
        compiler_params=_params("parallel"),
        name="final_norm",
    )(h2, norm_g)


def _dispatch_plan(top_i, rank, counts, t):
    n = t * TOP_K
    tm = MOE_BLOCK
    padded = (counts + tm - 1) // tm * tm
    pends = jnp.cumsum(padded)
    pstarts = pends - padded
    onehot = (top_i[:, :, None] == jnp.arange(N_EXPERTS)[None, None, :]).astype(jnp.int32)
    dest = (jnp.sum(onehot * pstarts[None, None, :], axis=2) + rank).astype(jnp.int32).reshape(n)
    p_rows = -(-n // tm) * tm + N_EXPERTS * tm
    nblk = p_rows // tm
    blk = jnp.arange(nblk, dtype=jnp.int32)
    blk_e = jnp.minimum(jnp.sum((blk * tm)[:, None] >= pends[None, :], axis=1), N_EXPERTS - 1)
    live = blk * tm < pends[-1]
    last_live = jnp.maximum(pends[-1] // tm - 1, 0)
    blk_e = jnp.where(live, blk_e, blk_e[last_live])
    blk_x = jnp.where(live, blk, 0)
    tail = jnp.where(padded > 0, pends - tm, -1)
    spare = pends[-1] + jnp.arange(N_EXPERTS) * tm
    fill = jnp.concatenate([tail, jnp.where(spare < p_rows, spare, -1)])
    i32 = lambda a: a.astype(jnp.int32)
    return dest, i32(fill), i32(blk_e), i32(live), i32(blk_x), p_rows


def _rope_tables(seq):
    inv = 1.0 / (ROPE_THETA ** (jnp.arange(0, HEAD_DIM, 2, dtype=F32) / HEAD_DIM))
    ang = jnp.arange(seq, dtype=F32)[:, None] * inv[None, :]
    cos = jnp.concatenate([jnp.cos(ang)] * 4, axis=-1)
    sin = jnp.concatenate([jnp.sin(ang)] * 4, axis=-1)
    upper = (jnp.arange(LANES) % HEAD_DIM) >= HEAD_DIM // 2
    sa = jnp.where(upper[None, :], sin, 0.0)
    sb = jnp.where(upper[None, :], 0.0, -sin)
    return cos, sa, sb


def _mixer(h2, b, s, norm_g, w_in, forget_bias, sinks, mix_gain, w_out, tables,
           fox_side=None, moba_side=None):
    f0 = SWA_WIDTH + 2 * SWA_KV_WIDTH + 3 * FOX_WIDTH
    f1 = f0 + FOX_HEADS
    assert f0 + 3 * MOBA_WIDTH == W_BLOCKS * LANES
    wt = jnp.swapaxes(w_in, 0, 1)
    w_all = jnp.concatenate(
        [wt[:f0], wt[f1:], jnp.pad(wt[f0:f1], ((0, LANES - FOX_HEADS), (0, 0)))],
        axis=0).astype(BF16)
    fbias = jnp.pad(forget_bias.astype(F32), (0, LANES - FOX_HEADS)).reshape(1, LANES)
    u, lf = _project(h2, norm_g.reshape(1, D_MODEL), w_all, *tables, fbias, s)
    u3 = u.reshape(b, s, U_WIDTH)
    oa = _swa(u3, sinks.astype(F32))
    ob, fox_cast = _fox(u3, lf.reshape(b, s, LANES), fox_side)
    oc, moba_cast = _moba(u3, moba_side)
    t = b * s
    h = _outproj(oa.reshape(t, SWA_WIDTH), ob.reshape(t, FOX_WIDTH), oc.reshape(t, MOBA_WIDTH),
                 mix_gain.astype(F32).reshape(1, MIX_WIDTH), w_out.astype(BF16), h2)
    return h, fox_cast, moba_cast


def _moe_ffn(h2, norm_g, w_router, wg, wu, wd, out_g):
    t = h2.shape[0]
    wr = jnp.pad(w_router.astype(F32), ((0, 0), (0, LANES - N_EXPERTS)))
    wr2 = jnp.stack(_split3(wr)[:2])
    z8, top, gate, rank, counts = _router(h2, norm_g.reshape(1, D_MODEL), wr2)
    dest, fill, blk_e, blk_live, blk_x, p_rows = _dispatch_plan(
        top[:, :TOP_K], rank[:, :TOP_K], counts[0, :N_EXPERTS], t)
    x8 = _dispatch(z8, dest, fill, p_rows)
    y8 = _moe_experts(x8, blk_e, blk_live, blk_x, wg, wu, wd)
    return _combine(h2, y8, dest, gate, out_g)


def kernel(x, attn_norm, w_in, fox_forget_bias, swa_sinks, mix_gain, w_out, ffn_norm,
           dense_w_gate, dense_w_up, dense_w_down, router_w, moe_w_gate, moe_w_up,
           moe_w_down, final_norm):
    b, s, d = x.shape
    depth = attn_norm.shape[0]
    assert d == D_MODEL and s % MOBA_BLOCK == 0 and s % ROW_TILE == 0
    tables = _rope_tables(s)
    h = x.reshape(b * s, d)
    normed = False
    ne, _, ff = moe_w_gate.shape[1:]
    flat = lambda w: w.reshape(w.shape[0] * w.shape[1], w.shape[2])
    wg = wu = None
    for layer in range(depth):
        j = layer // 2
        routed = layer % 2 == 1
        routed_next = layer + 1 < depth and (layer + 1) % 2 == 1
        fox_side = flat(moe_w_down[j]) if routed else (flat(moe_w_gate[(layer + 1) // 2])
                                                      if routed_next else None)
        moba_side = flat(moe_w_up[(layer + 1) // 2]) if routed_next and not routed else None
        h, fox_cast, moba_cast = _mixer(h, b, s, attn_norm[layer], w_in[layer],
                                        fox_forget_bias[layer], swa_sinks[layer], mix_gain[layer],
                                        w_out[layer], tables, fox_side, moba_side)
        last = layer == depth - 1
        if not routed:
            wg, wu = fox_cast, moba_cast
            h = _dense_ffn(h, ffn_norm[layer].reshape(1, d), dense_w_gate[j], dense_w_up[j],
                           dense_w_down[j])
        else:
            h = _moe_ffn(h, ffn_norm[layer], router_w[j], wg.reshape(ne, d, ff),
                         wu.reshape(ne, d, ff), fox_cast.reshape(ne, ff, d),
                         final_norm if last else None)
            normed = last
    if not normed:
        h = _final_norm(h, final_norm.reshape(1, d))
    return h.reshape(b, s, d)
```

```python
import functools

import jax
import jax.numpy as jnp
from jax import lax
from jax.experimental import pallas as pl
from jax.experimental.pallas import tpu as pltpu

F32 = jnp.float32
BF16 = jnp.bfloat16

D_MODEL = 1024
HEAD_DIM = 64
LANES = 128
SWA_Q_HEADS = 8
SWA_KV_HEADS = 2
SWA_WINDOW = 128
FOX_HEADS = 4
MOBA_HEADS = 4
MOBA_BLOCK = 256
MOBA_TOPK = 3
ROPE_THETA = 10000.0
RMS_EPS = 1e-5
D_FF = 3584
N_EXPERTS = 8
TOP_K = 2
MOE_BLOCK = 512
NEG_INF = -1e30
ATTN_SCALE = HEAD_DIM ** -0.5
LOG2E = 1.4426950408889634
Q_SCALE = ATTN_SCALE * LOG2E

SWA_WIDTH = SWA_Q_HEADS * HEAD_DIM
SWA_KV_WIDTH = SWA_KV_HEADS * HEAD_DIM
FOX_WIDTH = FOX_HEADS * HEAD_DIM
MOBA_WIDTH = MOBA_HEADS * HEAD_DIM
MIX_WIDTH = SWA_WIDTH + FOX_WIDTH + MOBA_WIDTH

W_BLOCKS = 18
W_ALL_WIDTH = (W_BLOCKS + 1) * LANES
ROPE_BLOCKS = (0, 1, 2, 3, 4, 12, 13, 14, 15)
Q_BLOCKS = (0, 1, 2, 3, 6, 7, 12, 13)
SWA_KV_BLOCKS = (4, 5)
U_BLOCKS = W_BLOCKS + len(SWA_KV_BLOCKS)
U_WIDTH = U_BLOCKS * LANES

VMEM_LIMIT = 56 * 1024 * 1024

ROW_TILE = 1024
ROUTE_TILE = 512
RESIDUAL_RING = 3
ATT_TILE = 256
SWA_STEP_BLOCKS = 16
FF_TILE = 512
MOE_FF_TILE = 1792
CHUNKS = D_MODEL // LANES
MOE_ISSUE_UNROLL = 8
DISPATCH_TILE = 2048


def _params(*sem):
    return pltpu.CompilerParams(dimension_semantics=sem, vmem_limit_bytes=VMEM_LIMIT)


def _split3(x):
    hi = x.astype(BF16)
    r1 = x - hi.astype(F32)
    mid = r1.astype(BF16)
    lo = (r1 - mid.astype(F32)).astype(BF16)
    return hi, mid, lo


def _dot_nt(a, b):
    return lax.dot_general(a, b, (((1,), (1,)), ((), ())), preferred_element_type=F32)


def _dot(a, b):
    return jnp.dot(a, b, preferred_element_type=F32)


def _side_cast(w2d, steps, index_map):
    rows, cols = w2d.shape
    assert rows % steps == 0
    spec = pl.BlockSpec((rows // steps, cols), index_map)
    return spec, spec, jax.ShapeDtypeStruct((rows, cols), BF16)


def _with_side_cast(kernel_fn, n_in):
    def body(*refs):
        side_in, out_ref, side_out = refs[n_in:n_in + 3]
        side_out[...] = side_in[...].astype(BF16)
        kernel_fn(*refs[:n_in], out_ref, *refs[n_in + 3:])
    return body


def _proj_kernel(x_ref, g_ref, w_ref, cos_ref, sa_ref, sb_ref, fb_ref, u_ref, lf_ref):
    x = x_ref[...]
    inv = lax.rsqrt(jnp.mean(x * x, axis=-1, keepdims=True) + RMS_EPS)
    h = (x * inv * g_ref[...]).astype(BF16)
    cos = cos_ref[...]
    sa = sa_ref[...]
    sb = sb_ref[...]
    low = lax.broadcasted_iota(jnp.int32, (1, LANES), 1) < HEAD_DIM
    dst = 0
    for c in range(W_BLOCKS // 2):
        acc = _dot_nt(h, w_ref[c * 2 * LANES:(c + 1) * 2 * LANES, :])
        for half in range(2):
            blk = 2 * c + half
            a = acc[:, half * LANES:(half + 1) * LANES]
            if blk in ROPE_BLOCKS:
                a = a * cos + pltpu.roll(a, 32, 1) * sa + pltpu.roll(a, 96, 1) * sb
            if blk in Q_BLOCKS:
                a = a * Q_SCALE
            if blk in SWA_KV_BLOCKS:
                swapped = pltpu.roll(a, HEAD_DIM, 1)
                outs = [jnp.where(low, a, swapped), jnp.where(low, swapped, a)]
            else:
                outs = [a]
            for o in outs:
                u_ref[:, dst * LANES:(dst + 1) * LANES] = o.astype(BF16)
                dst += 1
    assert dst == U_BLOCKS
    f = _dot_nt(h, w_ref[W_BLOCKS * LANES:W_ALL_WIDTH, :]) + fb_ref[...]
    lf_ref[...] = jnp.minimum(f, 0.0) - jnp.log(1.0 + jnp.exp(-jnp.abs(f)))


def _project(x2, norm_g, w_all, cos, sa, sb, fbias, seq):
    t = x2.shape[0]
    tm = ROW_TILE
    nseq = seq // tm
    row = lambda i: (i, 0)
    pos = lambda i: (i % nseq, 0)
    fixed = lambda i: (0, 0)
    return pl.pallas_call(
        _proj_kernel,
        grid=(t // tm,),
        in_specs=[
            pl.BlockSpec((tm, D_MODEL), row),
            pl.BlockSpec((1, D_MODEL), fixed),
            pl.BlockSpec((W_ALL_WIDTH, D_MODEL), fixed),
            pl.BlockSpec((tm, LANES), pos),
            pl.BlockSpec((tm, LANES), pos),
            pl.BlockSpec((tm, LANES), pos),
            pl.BlockSpec((1, LANES), fixed),
        ],
        out_specs=[pl.BlockSpec((tm, U_WIDTH), row), pl.BlockSpec((tm, LANES), row)],
        out_shape=[jax.ShapeDtypeStruct((t, U_WIDTH), BF16),
                   jax.ShapeDtypeStruct((t, LANES), F32)],
        compiler_params=_params("parallel"),
        name="proj",
    )(x2, norm_g, w_all, cos, sa, sb, fbias)


def _swa_kernel(sink_ref, q_ref, kc_ref, kp_ref, vc_ref, vp_ref, o_ref):
    n = pl.program_id(1)
    w = SWA_WINDOW
    group = SWA_Q_HEADS // SWA_KV_HEADS
    lane = lax.broadcasted_iota(jnp.int32, (1, LANES), 1)
    low = lane < HEAD_DIM

    k_all = jnp.concatenate([kp_ref[0], kc_ref[0]], axis=0)
    v_all = jnp.concatenate([vp_ref[0], vc_ref[0]], axis=0)
    qi = lax.broadcasted_iota(jnp.int32, (group * w, 2 * w), 0) % w
    kj = lax.broadcasted_iota(jnp.int32, (group * w, 2 * w), 1)
    window = (kj > qi) & (kj <= qi + w)
    pairs = group // 2
    for a in range(SWA_STEP_BLOCKS):
        rows = slice(a * w, (a + 1) * w)
        valid = (window & ((kj >= w) | (n > 0))) if a == 0 else window
        for g in range(SWA_KV_HEADS):
            k2 = k_all[a * w:(a + 2) * w, g * LANES:(g + 1) * LANES]
            v2 = v_all[a * w:(a + 2) * w, g * LANES:(g + 1) * LANES]
            qs = jnp.concatenate(
                [jnp.where(low if half == 0 else ~low,
                           q_ref[0, rows, (pairs * g + jj) * LANES:(pairs * g + jj + 1) * LANES], 0)
                 for jj in range(pairs) for half in range(2)], axis=0)
            s = jnp.where(valid, _dot_nt(qs, k2), NEG_INF)
            ps, inv = [], []
            for j in range(group):
                sj = s[j * w:(j + 1) * w]
                sink = sink_ref[g * group + j] * LOG2E
                m = jnp.maximum(jnp.max(sj, axis=1, keepdims=True), sink)
                p = jnp.exp2(sj - m)
                inv.append(1.0 / (jnp.sum(p, axis=1, keepdims=True) + jnp.exp2(sink - m)))
                ps.append(p.astype(BF16))
            o = _dot(jnp.concatenate(ps, axis=0), v2)
            outs = [o[j * w:(j + 1) * w] * inv[j] for j in range(group)]
            for jj in range(pairs):
                oj = jnp.where(low, outs[2 * jj], outs[2 * jj + 1])
                cols = slice((pairs * g + jj) * LANES, (pairs * g + jj + 1) * LANES)
                o_ref[0, rows, cols] = oj.astype(BF16)


def _swa(u3, sinks):
    b, s, _ = u3.shape
    w = SWA_WINDOW
    nb = SWA_STEP_BLOCKS
    assert s % (nb * w) == 0
    kv = SWA_KV_HEADS * LANES
    cur = lambda unit: (lambda bi, n, sk: (bi, n, unit))
    prev = lambda unit: (lambda bi, n, sk: (bi, jnp.maximum(n * nb - 1, 0), unit))
    grid_spec = pltpu.PrefetchScalarGridSpec(
        num_scalar_prefetch=1,
        grid=(b, s // (nb * w)),
        in_specs=[
            pl.BlockSpec((1, nb * w, SWA_WIDTH), lambda bi, n, sk: (bi, n, 0)),
            pl.BlockSpec((1, nb * w, kv), cur(2)),
            pl.BlockSpec((1, w, kv), prev(2)),
            pl.BlockSpec((1, nb * w, kv), cur(3)),
            pl.BlockSpec((1, w, kv), prev(3)),
        ],
        out_specs=pl.BlockSpec((1, nb * w, SWA_WIDTH), lambda bi, n, sk: (bi, n, 0)),
    )
    return pl.pallas_call(
        _swa_kernel,
        grid_spec=grid_spec,
        out_shape=jax.ShapeDtypeStruct((b, s, SWA_WIDTH), BF16),
        compiler_params=_params("parallel", "parallel"),
        name="swa",
    )(sinks, u3, u3, u3, u3, u3)


def _head_lanes(hh):
    lane = lax.broadcasted_iota(jnp.int32, (1, LANES), 1)
    if hh == 0:
        return lane < HEAD_DIM, lane - HEAD_DIM
    return lane >= HEAD_DIM, lane


def _causal_attention(qp_scr, kp_scr, v_ref, o_ref, t):
    s_len = qp_scr.shape[1]
    lane = lax.broadcasted_iota(jnp.int32, (1, LANES), 1)
    row = lax.broadcasted_iota(jnp.int32, (t, t), 0)
    col = lax.broadcasted_iota(jnp.int32, (t, t), 1)
    for qi in range(s_len // t):
        rows = slice(qi * t, (qi + 1) * t)
        outs = []
        for hh in range(2):
            q = qp_scr[hh, rows, :]
            n = (qi + 1) * t
            s = _dot_nt(q, kp_scr[hh, :n, :])
            own = jnp.where(col <= row, s[:, qi * t:], NEG_INF)
            s = jnp.concatenate([s[:, :qi * t], own], axis=1) if qi else own
            m = jnp.max(s, axis=1, keepdims=True)
            p = jnp.exp2(s - m)
            l = jnp.sum(p, axis=1, keepdims=True)
            outs.append(_dot(p.astype(BF16), v_ref[0, :n, :]) * (1.0 / l))
        o_ref[0, rows, :] = jnp.where(lane < HEAD_DIM, outs[0], outs[1]).astype(BF16)


def _pair_attention(kernel_fn, u3, extra, q_blk, k_blk, v_blk, n_heads, name, side=None):
    b, s, _ = u3.shape
    pairs = n_heads // 2
    pair = lambda blk: (lambda bi, p: (bi, 0, blk + p))
    seq_block = pl.BlockSpec((1, s, LANES), lambda bi, p: (bi, 0, 0))
    in_specs = [pl.BlockSpec((1, s, LANES), pair(q_blk)),
                pl.BlockSpec((1, s, LANES), pair(k_blk)),
                pl.BlockSpec((1, s, LANES), pair(v_blk))] + [seq_block] * len(extra)
    out_specs = [pl.BlockSpec((1, s, LANES), pair(0))]
    out_shape = [jax.ShapeDtypeStruct((b, s, n_heads * HEAD_DIM), BF16)]
    args = [u3, u3, u3, *extra]
    if side is not None:
        kernel_fn = _with_side_cast(kernel_fn, len(args))
        spec_in, spec_out, shape = _side_cast(side, b * pairs, lambda bi, p: (bi * pairs + p, 0))
        in_specs.append(spec_in)
        out_specs.append(spec_out)
        out_shape.append(shape)
        args.append(side)
    outs = pl.pallas_call(
        kernel_fn,
        grid=(b, pairs),
        in_specs=in_specs,
        out_specs=out_specs,
        out_shape=out_shape,
        scratch_shapes=[pltpu.VMEM((2, s, LANES), BF16), pltpu.VMEM((2, s, LANES), BF16)],
        compiler_params=_params("parallel", "parallel"),
        name=name,
    )(*args)
    return tuple(outs) if side is not None else (outs[0], None)


def _fox_kernel(q_ref, k_ref, v_ref, lf_ref, o_ref, qp_scr, kp_scr):
    p = pl.program_id(1)
    s_len = q_ref.shape[1]
    t = ATT_TILE
    lane = lax.broadcasted_iota(jnp.int32, (1, LANES), 1)
    r = lax.broadcasted_iota(jnp.int32, (t, t), 0)
    cc = lax.broadcasted_iota(jnp.int32, (t, t), 1)
    tri = jnp.where(cc <= r, 1.0, 0.0).astype(BF16)
    carry = jnp.zeros((1, LANES), F32)
    for i in range(s_len // t):
        rows = slice(i * t, (i + 1) * t)
        lf = lf_ref[0, rows, :]
        lf0 = jnp.sum(jnp.where(lane == 2 * p, lf, 0.0), axis=1, keepdims=True)
        lf1 = jnp.sum(jnp.where(lane == 2 * p + 1, lf, 0.0), axis=1, keepdims=True)
        hi, mid, lo = _split3(jnp.where(lane < 3, lf0, jnp.where(lane < 6, lf1, 0.0)))
        terms = jnp.where((lane == 0) | (lane == 3), hi,
                          jnp.where((lane == 1) | (lane == 4), mid, lo))
        sums = _dot(tri, terms) + carry
        carry = sums[t - 1:t, :]
        c0 = jnp.sum(jnp.where(lane < 3, sums, 0.0), axis=1, keepdims=True)
        c1 = jnp.sum(jnp.where((lane >= 3) & (lane < 6), sums, 0.0), axis=1, keepdims=True)
        chi, cmid, clo = _split3(jnp.where(lane < HEAD_DIM, c1, c0) * LOG2E)
        for hh in range(2):
            keep, f = _head_lanes(hh)
            ones_q = jnp.where((f >= 3) & (f < 6), 1.0, 0.0).astype(BF16)
            ones_k = jnp.where((f >= 0) & (f < 3), 1.0, 0.0).astype(BF16)
            qf = jnp.where(f == 0, chi, jnp.where(f == 1, cmid, jnp.where(f == 2, clo, ones_q)))
            kf = jnp.where(f == 3, -chi, jnp.where(f == 4, -cmid, jnp.where(f == 5, -clo, ones_k)))
            qp_scr[hh, rows, :] = jnp.where(keep, q_ref[0, rows, :], qf)
            kp_scr[hh, rows, :] = jnp.where(keep, k_ref[0, rows, :], kf)
    _causal_attention(qp_scr, kp_scr, v_ref, o_ref, t)


def _fox(u3, lf3, side=None):
    return _pair_attention(_fox_kernel, u3, (lf3,), 8, 10, 12, FOX_HEADS, "fox", side)


def _moba_kernel(q_ref, k_ref, v_ref, o_ref, qp_scr, kp_scr):
    s_len = q_ref.shape[1]
    t = MOBA_BLOCK
    nkb = s_len // t
    sub = 8
    assert nkb <= sub
    means = [jnp.mean(k_ref[0, j * t:(j + 1) * t, :].astype(F32), axis=0, keepdims=True)
             for j in range(nkb)]
    km_pair = jnp.concatenate(means + [jnp.zeros((sub - nkb, LANES), F32)], axis=0) \
        if nkb < sub else jnp.concatenate(means, axis=0)
    stack = []
    for hh in range(2):
        keep, _ = _head_lanes(hh)
        stack += [part.astype(F32) for part in _split3(jnp.where(keep, km_pair, 0.0))]
    stack.append(jnp.zeros((LANES - 6 * sub, LANES), F32))
    km_all = jnp.concatenate(stack, axis=0).astype(BF16)
    blk = lax.broadcasted_iota(jnp.int32, (sub, t), 0)
    gates = [_dot_nt(km_all, q_ref[0, i * t:(i + 1) * t, :]) for i in range(nkb)]
    for hh in range(2):
        keep, f = _head_lanes(hh)
        for i in range(nkb):
            rows = slice(i * t, (i + 1) * t)
            q = q_ref[0, rows, :]
            base = 3 * sub * hh
            g = (gates[i][base:base + sub] + gates[i][base + sub:base + 2 * sub]) \
                + gates[i][base + 2 * sub:base + 3 * sub]
            past = blk < i
            g = jnp.where(past, g, NEG_INF)
            rank = jnp.zeros((sub, t), jnp.int32)
            for r in range(i):
                gr = jnp.broadcast_to(g[r:r + 1, :], (sub, t))
                beats = (gr > g) | ((gr == g) & (r < blk))
                rank = rank + beats.astype(jnp.int32)
            drop = jnp.where(past & (rank >= MOBA_TOPK), 1.0, 0.0)
            below = [jnp.zeros((HEAD_DIM, t), F32)] if hh == 0 else []
            above = jnp.zeros((LANES - sub - (HEAD_DIM if hh == 0 else 0), t), F32)
            padded = jnp.concatenate(below + [drop, above], axis=0)
            kf = jnp.where(f == i, NEG_INF, 0.0).astype(BF16)
            qp_scr[hh, rows, :] = jnp.where(keep, q, padded.T.astype(BF16))
            kp_scr[hh, rows, :] = jnp.where(keep, k_ref[0, rows, :], kf)
    _causal_attention(qp_scr, kp_scr, v_ref, o_ref, t)


def _moba(u3, side=None):
    return _pair_attention(_moba_kernel, u3, (), 14, 16, 18, MOBA_HEADS, "moba", side)


def _rms_f32(x):
    x = x.astype(F32)
    return x * lax.rsqrt(jnp.mean(x * x, axis=-1, keepdims=True) + RMS_EPS)


def _outproj_kernel(oa_ref, ob_ref, oc_ref, gain_ref, w_ref, x_hbm, h_ref, ring, sem):
    i = pl.program_id(0)
    n = pl.num_programs(0)
    tm = h_ref.shape[0]
    ahead = RESIDUAL_RING - 1

    def fetch(tile):
        slot = tile % RESIDUAL_RING
        rows = pl.ds(pl.multiple_of(tile * tm, tm), tm)
        return pltpu.make_async_copy(x_hbm.at[rows, :], ring.at[slot], sem.at[slot])

    @pl.when(i == 0)
    def _():
        for tile in range(ahead):
            fetch(tile).start()

    @pl.when(i + ahead < n)
    def _():
        fetch(i + ahead).start()

    a0, a1, a2 = 0, SWA_WIDTH, SWA_WIDTH + FOX_WIDTH
    ya = (_rms_f32(oa_ref[...]) * gain_ref[:, a0:a1]).astype(BF16)
    yb = (_rms_f32(ob_ref[...]) * gain_ref[:, a1:a2]).astype(BF16)
    yc = (_rms_f32(oc_ref[...]) * gain_ref[:, a2:]).astype(BF16)
    y = (_dot(ya, w_ref[a0:a1, :]) + _dot(yb, w_ref[a1:a2, :])) + _dot(yc, w_ref[a2:, :])
    fetch(i).wait()
    h_ref[...] = ring[i % RESIDUAL_RING] + y


def _outproj(oa, ob, oc, gain, w_out, x2):
    t = x2.shape[0]
    tm = ROW_TILE
    assert t // tm >= RESIDUAL_RING - 1
    row = lambda i: (i, 0)
    fixed = lambda i: (0, 0)
    return pl.pallas_call(
        _outproj_kernel,
        grid=(t // tm,),
        in_specs=[
            pl.BlockSpec((tm, SWA_WIDTH), row),
            pl.BlockSpec((tm, FOX_WIDTH), row),
            pl.BlockSpec((tm, MOBA_WIDTH), row),
            pl.BlockSpec((1, MIX_WIDTH), fixed),
            pl.BlockSpec((MIX_WIDTH, D_MODEL), fixed),
            pl.BlockSpec(memory_space=pl.ANY),
        ],
        out_specs=pl.BlockSpec((tm, D_MODEL), row),
        out_shape=jax.ShapeDtypeStruct((t, D_MODEL), F32),
        scratch_shapes=[pltpu.VMEM((RESIDUAL_RING, tm, D_MODEL), F32),
                        pltpu.SemaphoreType.DMA((RESIDUAL_RING,))],
        compiler_params=_params("arbitrary"),
        name="outproj",
    )(oa, ob, oc, gain, w_out, x2)


def _silu(x):
    return x / (1.0 + jnp.exp(-x))


def _ffn_kernel(h_ref, g_ref, wg_ref, wu_ref, wd_ref, o_ref, acc_scr):
    j = pl.program_id(1)

    @pl.when((pl.program_id(0) == 0) & (j == 0))
    def _():
        acc_scr[...] = jnp.zeros_like(acc_scr)

    x = h_ref[...]
    inv = lax.rsqrt(jnp.mean(x * x, axis=-1, keepdims=True) + RMS_EPS)
    z = (x * inv * g_ref[...]).astype(BF16)
    a = (_silu(_dot(z, wg_ref[...].astype(BF16))) * _dot(z, wu_ref[...].astype(BF16))).astype(BF16)
    total = acc_scr[...] + _dot(a, wd_ref[...].astype(BF16))
    o_ref[...] = x + total
    acc_scr[...] = jnp.where(j == pl.num_programs(1) - 1, 0.0, total)


def _dense_ffn(h2, norm_g, w_gate, w_up, w_down):
    t = h2.shape[0]
    tm, tf = ROW_TILE, FF_TILE
    return pl.pallas_call(
        _ffn_kernel,
        grid=(t // tm, D_FF // tf),
        in_specs=[
            pl.BlockSpec((tm, D_MODEL), lambda i, j: (i, 0)),
            pl.BlockSpec((1, D_MODEL), lambda i, j: (0, 0)),
            pl.BlockSpec((D_MODEL, tf), lambda i, j: (0, j)),
            pl.BlockSpec((D_MODEL, tf), lambda i, j: (0, j)),
            pl.BlockSpec((tf, D_MODEL), lambda i, j: (j, 0)),
        ],
        out_specs=pl.BlockSpec((tm, D_MODEL), lambda i, j: (i, 0)),
        out_shape=jax.ShapeDtypeStruct((t, D_MODEL), F32),
        scratch_shapes=[pltpu.VMEM((tm, D_MODEL), F32)],
        compiler_params=_params("arbitrary", "arbitrary"),
        name="dense_ffn",
    )(h2, norm_g, w_gate, w_up, w_down)


def _router_kernel(h_ref, g_ref, wr_ref, z_ref, idx_ref, gate_ref, rank_ref, count_ref, count_scr):
    tm = h_ref.shape[0]
    x = h_ref[...]
    inv = lax.rsqrt(jnp.mean(x * x, axis=-1, keepdims=True) + RMS_EPS)
    z = x * inv * g_ref[...]
    for c in range(CHUNKS):
        z_ref[pl.ds(c, tm, stride=CHUNKS), :] = z[:, c * LANES:(c + 1) * LANES]
    zh, zm, _ = _split3(z)
    wh, wm = wr_ref[0], wr_ref[1]
    logits = _dot(zh, wh) + (_dot(zh, wm) + _dot(zm, wh))
    lane = lax.broadcasted_iota(jnp.int32, (tm, LANES), 1)
    logits = jnp.where(lane < N_EXPERTS, logits, -jnp.inf)
    m1 = jnp.max(logits, axis=1, keepdims=True)
    i1 = jnp.min(jnp.where(logits == m1, lane, LANES), axis=1, keepdims=True)
    rest = jnp.where(lane == i1, -jnp.inf, logits)
    m2 = jnp.max(rest, axis=1, keepdims=True)
    i2 = jnp.min(jnp.where(rest == m2, lane, LANES), axis=1, keepdims=True)
    e2 = jnp.exp(m2 - m1)
    g1 = 1.0 / (1.0 + e2)
    idx_ref[...] = jnp.where(lane == 0, i1, i2)
    gate_ref[...] = jnp.where(lane == 0, g1, e2 * g1)

    @pl.when(pl.program_id(0) == 0)
    def _():
        count_scr[...] = jnp.zeros_like(count_scr)

    chosen = ((lane == i1) | (lane == i2)).astype(F32)
    r = lax.broadcasted_iota(jnp.int32, (tm, tm), 0)
    c = lax.broadcasted_iota(jnp.int32, (tm, tm), 1)
    before = jnp.where(c < r, 1.0, 0.0).astype(BF16)
    earlier = _dot(before, chosen.astype(BF16)) + count_scr[...]
    r1 = jnp.sum(jnp.where(lane == i1, earlier, 0.0), axis=1, keepdims=True)
    r2 = jnp.sum(jnp.where(lane == i2, earlier, 0.0), axis=1, keepdims=True)
    rank_ref[...] = jnp.where(lane == 0, r1, r2).astype(jnp.int32)
    count_scr[...] += jnp.sum(chosen, axis=0, keepdims=True)
    count_ref[...] = count_scr[...].astype(jnp.int32)


def _router(h2, norm_g, wr3):
    t = h2.shape[0]
    tm = ROUTE_TILE
    row = lambda i: (i, 0)
    return pl.pallas_call(
        _router_kernel,
        grid=(t // tm,),
        in_specs=[
            pl.BlockSpec((tm, D_MODEL), row),
            pl.BlockSpec((1, D_MODEL), lambda i: (0, 0)),
            pl.BlockSpec((2, D_MODEL, LANES), lambda i: (0, 0, 0)),
        ],
        out_specs=[pl.BlockSpec((tm * CHUNKS, LANES), row),
                   pl.BlockSpec((tm, LANES), row),
                   pl.BlockSpec((tm, LANES), row),
                   pl.BlockSpec((tm, LANES), row),
                   pl.BlockSpec((1, LANES), lambda i: (0, 0))],
        out_shape=[jax.ShapeDtypeStruct((t * CHUNKS, LANES), F32),
                   jax.ShapeDtypeStruct((t, LANES), jnp.int32),
                   jax.ShapeDtypeStruct((t, LANES), F32),
                   jax.ShapeDtypeStruct((t, LANES), jnp.int32),
                   jax.ShapeDtypeStruct((1, LANES), jnp.int32)],
        scratch_shapes=[pltpu.VMEM((1, LANES), F32)],
        compiler_params=_params("arbitrary"),
        name="router",
    )(h2, norm_g, wr3)


def _row_copy(src, src_row, dst, dst_row, sem):
    return pltpu.make_async_copy(src.at[pl.ds(src_row * CHUNKS, CHUNKS), :],
                                 dst.at[pl.ds(dst_row * CHUNKS, CHUNKS), :], sem)


def _rows_wait(src, dst, rows, sem):
    pltpu.make_async_copy(src.at[pl.ds(0, rows * CHUNKS), :],
                          dst.at[pl.ds(0, rows * CHUNKS), :], sem).wait()


def _dispatch_kernel(dest_ref, fill_ref, z_ref, x_hbm, zero_scr, fsem, sem):
    i = pl.program_id(0)
    blk = MOE_BLOCK
    tile = z_ref.shape[0] // CHUNKS

    @pl.when(i == 0)
    def _():
        zero_scr[...] = jnp.zeros_like(zero_scr)
        for e in range(fill_ref.shape[0]):
            @pl.when(fill_ref[e] >= 0)
            def _():
                start = pl.multiple_of(fill_ref[e] * CHUNKS, blk * CHUNKS)
                pltpu.make_async_copy(zero_scr, x_hbm.at[pl.ds(start, blk * CHUNKS), :], fsem).start()
        for e in range(fill_ref.shape[0]):
            @pl.when(fill_ref[e] >= 0)
            def _():
                _rows_wait(zero_scr, x_hbm, blk, fsem)

    base = i * tile * TOP_K

    tokens_per_trip = MOE_ISSUE_UNROLL // TOP_K

    def trip(it, c):
        for k in range(MOE_ISSUE_UNROLL):
            token = it * tokens_per_trip + k // TOP_K
            slot = dest_ref[base + it * MOE_ISSUE_UNROLL + k]
            _row_copy(z_ref, token, x_hbm, slot, sem).start(priority=k % 2)
        return c
    lax.fori_loop(0, tile // tokens_per_trip, trip, 0)
    for _ in range(TOP_K):
        _rows_wait(z_ref, x_hbm, tile, sem)


def _dispatch(z8, dest, fill, p_rows):
    tile = DISPATCH_TILE
    assert z8.shape[0] % (tile * CHUNKS) == 0
    grid_spec = pltpu.PrefetchScalarGridSpec(
        num_scalar_prefetch=2,
        grid=(z8.shape[0] // (tile * CHUNKS),),
        in_specs=[pl.BlockSpec((tile * CHUNKS, LANES), lambda i, dd, f: (i, 0))],
        out_specs=pl.BlockSpec(memory_space=pl.ANY),
        scratch_shapes=[pltpu.VMEM((MOE_BLOCK * CHUNKS, LANES), F32),
                        pltpu.SemaphoreType.DMA(()), pltpu.SemaphoreType.DMA(())],
    )
    return pl.pallas_call(
        _dispatch_kernel,
        grid_spec=grid_spec,
        out_shape=jax.ShapeDtypeStruct((p_rows * CHUNKS, LANES), F32),
        compiler_params=_params("arbitrary"),
        name="moe_dispatch",
    )(dest, fill, z8)


def _moe_kernel(be_ref, live_ref, bx_ref, x_ref, wg_ref, wu_ref, wd_ref, y_ref, acc):
    i = pl.program_id(0)
    j = pl.program_id(1)
    tm = acc.shape[0]

    @pl.when((i == 0) & (j == 0))
    def _():
        acc[...] = jnp.zeros_like(acc)

    @pl.when((live_ref[i] == 0) & (j == 0))
    def _():
        y_ref[...] = jnp.zeros_like(y_ref)

    @pl.when(live_ref[i] > 0)
    def _():
        x = jnp.concatenate([x_ref[pl.ds(c, tm, stride=CHUNKS), :].astype(BF16)
                             for c in range(CHUNKS)], axis=1)
        a = (_silu(_dot(x, wg_ref[0])) * _dot(x, wu_ref[0])).astype(BF16)
        total = acc[...] + _dot(a, wd_ref[0])
        for c in range(CHUNKS):
            y_ref[pl.ds(c, tm, stride=CHUNKS), :] = total[:, c * LANES:(c + 1) * LANES]
        acc[...] = jnp.where(j == pl.num_programs(1) - 1, 0.0, total)


def _moe_experts(x8, blk_e, blk_live, blk_x, w_gate, w_up, w_down):
    tm, tf = MOE_BLOCK, MOE_FF_TILE
    nblk = blk_e.shape[0]
    nff = D_FF // tf
    ff = lambda i, j, live: jnp.where(live[i] > 0, j, nff - 1)
    grid_spec = pltpu.PrefetchScalarGridSpec(
        num_scalar_prefetch=3,
        grid=(nblk, nff),
        in_specs=[
            pl.BlockSpec((tm * CHUNKS, LANES), lambda i, j, be, lv, bx: (bx[i], 0)),
            pl.BlockSpec((1, D_MODEL, tf), lambda i, j, be, lv, bx: (be[i], 0, ff(i, j, lv))),
            pl.BlockSpec((1, D_MODEL, tf), lambda i, j, be, lv, bx: (be[i], 0, ff(i, j, lv))),
            pl.BlockSpec((1, tf, D_MODEL), lambda i, j, be, lv, bx: (be[i], ff(i, j, lv), 0)),
        ],
        out_specs=pl.BlockSpec((tm * CHUNKS, LANES), lambda i, j, be, lv, bx: (i, 0)),
        scratch_shapes=[pltpu.VMEM((tm, D_MODEL), F32)],
    )
    return pl.pallas_call(
        _moe_kernel,
        grid_spec=grid_spec,
        out_shape=jax.ShapeDtypeStruct(x8.shape, F32),
        compiler_params=_params("arbitrary", "arbitrary"),
        name="moe_experts",
    )(blk_e, blk_live, blk_x, x8, w_gate, w_up, w_down)


def _combine_kernel(dest_ref, h_ref, gate_ref, *rest, final_norm):
    if final_norm:
        g_ref, y_hbm, o_ref, yb_even, yb_odd, sem = rest
    else:
        y_hbm, o_ref, yb_even, yb_odd, sem = rest
    i = pl.program_id(0)
    last = pl.num_programs(0) - 1
    tm = h_ref.shape[0]
    rows = tm * TOP_K
    stride = TOP_K * CHUNKS

    @pl.when(i == 0)
    def _():
        def trip(it, c):
            for k in range(MOE_ISSUE_UNROLL):
                r = it * MOE_ISSUE_UNROLL + k
                _row_copy(y_hbm, dest_ref[r], yb_even, r, sem.at[0]).start(priority=k % 2)
            return c
        lax.fori_loop(0, rows // MOE_ISSUE_UNROLL, trip, 0)

    def step(parity, yb, yb_next):
        _rows_wait(y_hbm, yb, rows, sem.at[parity])
        nxt = jnp.minimum(i + 1, last)
        for r in range(rows):
            copy = _row_copy(y_hbm, dest_ref[nxt * rows + r], yb_next, r, sem.at[1 - parity])
            copy.start(priority=r % 2)
        gates = [gate_ref[:, k:k + 1] for k in range(TOP_K)]
        parts = []
        ss = jnp.zeros((tm, 1), F32)
        for c in range(CHUNKS):
            y = gates[0] * yb[pl.ds(c, tm, stride=stride), :]
            for k in range(1, TOP_K):
                y = y + gates[k] * yb[pl.ds(k * CHUNKS + c, tm, stride=stride), :]
            hc = h_ref[:, c * LANES:(c + 1) * LANES] + y
            parts.append(hc)
            ss = ss + jnp.sum(hc * hc, axis=1, keepdims=True)
        if final_norm:
            inv = lax.rsqrt(ss / D_MODEL + RMS_EPS)
        for c in range(CHUNKS):
            cols = slice(c * LANES, (c + 1) * LANES)
            o_ref[:, cols] = parts[c] * inv * g_ref[:, cols] if final_norm else parts[c]

        @pl.when(i == last)
        def _():
            _rows_wait(y_hbm, yb_next, rows, sem.at[1 - parity])

    @pl.when(i % 2 == 0)
    def _():
        step(0, yb_even, yb_odd)

    @pl.when(i % 2 == 1)
    def _():
        step(1, yb_odd, yb_even)


def _combine(h2, y8, dest, gate, out_g):
    t = h2.shape[0]
    tm = ROUTE_TILE
    final_norm = out_g is not None
    row = lambda i, d: (i, 0)
    in_specs = [pl.BlockSpec((tm, D_MODEL), row), pl.BlockSpec((tm, LANES), row)]
    args = [h2, gate]
    if final_norm:
        in_specs.append(pl.BlockSpec((1, D_MODEL), lambda i, d: (0, 0)))
        args.append(out_g.reshape(1, D_MODEL))
    grid_spec = pltpu.PrefetchScalarGridSpec(
        num_scalar_prefetch=1,
        grid=(t // tm,),
        in_specs=in_specs + [pl.BlockSpec(memory_space=pl.ANY)],
        out_specs=pl.BlockSpec((tm, D_MODEL), row),
        scratch_shapes=[pltpu.VMEM((tm * TOP_K * CHUNKS, LANES), F32),
                        pltpu.VMEM((tm * TOP_K * CHUNKS, LANES), F32),
                        pltpu.SemaphoreType.DMA((2,))],
    )
    return pl.pallas_call(
        functools.partial(_combine_kernel, final_norm=final_norm),
        grid_spec=grid_spec,
        out_shape=jax.ShapeDtypeStruct((t, D_MODEL), F32),
        compiler_params=_params("arbitrary"),
        name="moe_combine",
    )(dest, *args, y8)


def _norm_kernel(h_ref, g_ref, o_ref):
    x = h_ref[...]
    o_ref[...] = x * lax.rsqrt(jnp.mean(x * x, axis=-1, keepdims=True) + RMS_EPS) * g_ref[...]


def _final_norm(h2, norm_g):
    t = h2.shape[0]
    tm = ROUTE_TILE
    return pl.pallas_call(
        _norm_kernel,
        grid=(t // tm,),
        in_specs=[pl.BlockSpec((tm, D_MODEL), lambda i: (i, 0)),
                  pl.BlockSpec((1, D_MODEL), lambda i: (0, 0))],
        out_specs=pl.BlockSpec((tm, D_MODEL), lambda i: (i, 0)),
        out_shape=jax.ShapeDtypeStruct((t, D_MODEL), F32),
        compiler_params=_params("parallel"),
        name="final_norm",
    )(h2, norm_g)


def _dispatch_plan(top_i, rank, counts, t):
    n = t * TOP_K
    tm = MOE_BLOCK
    padded = (counts + tm - 1) // tm * tm
    pends = jnp.cumsum(padded)
    pstarts = pends - padded
    onehot = (top_i[:, :, None] == jnp.arange(N_EXPERTS)[None, None, :]).astype(jnp.int32)
    dest = (jnp.sum(onehot * pstarts[None, None, :], axis=2) + rank).astype(jnp.int32).reshape(n)
    p_rows = -(-n // tm) * tm + N_EXPERTS * tm
    nblk = p_rows // tm
    blk = jnp.arange(nblk, dtype=jnp.int32)
    blk_e = jnp.minimum(jnp.sum((blk * tm)[:, None] >= pends[None, :], axis=1), N_EXPERTS - 1)
    live = blk * tm < pends[-1]
    last_live = jnp.maximum(pends[-1] // tm - 1, 0)
    blk_e = jnp.where(live, blk_e, blk_e[last_live])
    blk_x = jnp.where(live, blk, 0)
    tail = jnp.where(padded > 0, pends - tm, -1)
    spare = pends[-1] + jnp.arange(N_EXPERTS) * tm
    fill = jnp.concatenate([tail, jnp.where(spare < p_rows, spare, -1)])
    i32 = lambda a: a.astype(jnp.int32)
    return dest, i32(fill), i32(blk_e), i32(live), i32(blk_x), p_rows


def _rope_tables(seq):
    inv = 1.0 / (ROPE_THETA ** (jnp.arange(0, HEAD_DIM, 2, dtype=F32) / HEAD_DIM))
    ang = jnp.arange(seq, dtype=F32)[:, None] * inv[None, :]
    cos = jnp.concatenate([jnp.cos(ang)] * 4, axis=-1)
    sin = jnp.concatenate([jnp.sin(ang)] * 4, axis=-1)
    upper = (jnp.arange(LANES) % HEAD_DIM) >= HEAD_DIM // 2
    sa = jnp.where(upper[None, :], sin, 0.0)
    sb = jnp.where(upper[None, :], 0.0, -sin)
    return cos, sa, sb


def _mixer(h2, b, s, norm_g, w_in, forget_bias, sinks, mix_gain, w_out, tables,
           fox_side=None, moba_side=None):
    f0 = SWA_WIDTH + 2 * SWA_KV_WIDTH + 3 * FOX_WIDTH
    f1 = f0 + FOX_HEADS
    assert f0 + 3 * MOBA_WIDTH == W_BLOCKS * LANES
    wt = jnp.swapaxes(w_in, 0, 1)
    w_all = jnp.concatenate(
        [wt[:f0], wt[f1:], jnp.pad(wt[f0:f1], ((0, LANES - FOX_HEADS), (0, 0)))],
        axis=0).astype(BF16)
    fbias = jnp.pad(forget_bias.astype(F32), (0, LANES - FOX_HEADS)).reshape(1, LANES)
    u, lf = _project(h2, norm_g.reshape(1, D_MODEL), w_all, *tables, fbias, s)
    u3 = u.reshape(b, s, U_WIDTH)
    oa = _swa(u3, sinks.astype(F32))
    ob, fox_cast = _fox(u3, lf.reshape(b, s, LANES), fox_side)
    oc, moba_cast = _moba(u3, moba_side)
    t = b * s
    h = _outproj(oa.reshape(t, SWA_WIDTH), ob.reshape(t, FOX_WIDTH), oc.reshape(t, MOBA_WIDTH),
                 mix_gain.astype(F32).reshape(1, MIX_WIDTH), w_out.astype(BF16), h2)
    return h, fox_cast, moba_cast


def _moe_ffn(h2, norm_g, w_router, wg, wu, wd, out_g):
    t = h2.shape[0]
    wr = jnp.pad(w_router.astype(F32), ((0, 0), (0, LANES - N_EXPERTS)))
    wr2 = jnp.stack(_split3(wr)[:2])
    z8, top, gate, rank, counts = _router(h2, norm_g.reshape(1, D_MODEL), wr2)
    dest, fill, blk_e, blk_live, blk_x, p_rows = _dispatch_plan(
        top[:, :TOP_K], rank[:, :TOP_K], counts[0, :N_EXPERTS], t)
    x8 = _dispatch(z8, dest, fill, p_rows)
    y8 = _moe_experts(x8, blk_e, blk_live, blk_x, wg, wu, wd)
    return _combine(h2, y8, dest, gate, out_g)


def kernel(x, attn_norm, w_in, fox_forget_bias, swa_sinks, mix_gain, w_out, ffn_norm,
           dense_w_gate, dense_w_up, dense_w_down, router_w, moe_w_gate, moe_w_up,
           moe_w_down, final_norm):
    b, s, d = x.shape
    depth = attn_norm.shape[0]
    assert d == D_MODEL and s % MOBA_BLOCK == 0 and s % ROW_TILE == 0
    tables = _rope_tables(s)
    h = x.reshape(b * s, d)
    normed = False
    ne, _, ff = moe_w_gate.shape[1:]
    flat = lambda w: w.reshape(w.shape[0] * w.shape[1], w.shape[2])
    wg = wu = None
    for layer in range(depth):
        j = layer // 2
        routed = layer % 2 == 1
        routed_next = layer + 1 < depth and (layer + 1) % 2 == 1
        fox_side = flat(moe_w_down[j]) if routed else (flat(moe_w_gate[(layer + 1) // 2])
                                                      if routed_next else None)
        moba_side = flat(moe_w_up[(layer + 1) // 2]) if routed_next and not routed else None
        h, fox_cast, moba_cast = _mixer(h, b, s, attn_norm[layer], w_in[layer],
                                        fox_forget_bias[layer], swa_sinks[layer], mix_gain[layer],
                                        w_out[layer], tables, fox_side, moba_side)
        last = layer == depth - 1
        if not routed:
            wg, wu = fox_cast, moba_cast
            h = _dense_ffn(h, ffn_norm[layer].reshape(1, d), dense_w_gate[j], dense_w_up[j],
                           dense_w_down[j])
        else:
            h = _moe_ffn(h, ffn_norm[layer], router_w[j], wg.reshape(ne, d, ff),
                         wu.reshape(ne, d, ff), fox_cast.reshape(ne, ff, d),
                         final_norm if last else None)
            normed = last
    if not normed:
        h = _final_norm(h, final_norm.reshape(1, d))
    return h.reshape(b, s, d)
```
